```python
import jax
import jax.numpy as jnp
from jax import lax
import numpy as np

D_MODEL = 1024
BATCH = 2
SEQ = 16384
DEPTH = 4

EPS = 1e-6
N_BRANCH = 3
MLA_HEADS = 8
MLA_Q_LORA = 256
MLA_KV_LORA = 128
MLA_NOPE = 64
MLA_ROPE = 32
MLA_V = 64
ROPE_THETA = 10000.0
Q_BLOCK = 128
MLSTM_HEADS = 4
MLSTM_QK = 64
MLSTM_V = 128
MLSTM_CHUNK = 64
CONV_K = 4
DIL_HEADS = 8
DIL_HEAD_DIM = 64
DIL_PATTERNS = ((128, 1), (512, 4), (2048, 16))
DIL_WIDTH = DIL_HEADS * DIL_HEAD_DIM
N_GROUPS = 4
EXPERTS_PER_GROUP = 4
N_EXPERTS = N_GROUPS * EXPERTS_PER_GROUP
TOP_K_FINE = 2
EXPERT_FF = 256
IN_SIZES = (MLA_Q_LORA, MLA_KV_LORA, MLA_ROPE,
            2 * MLSTM_HEADS * MLSTM_QK, MLSTM_HEADS * MLSTM_V, MLSTM_HEADS * MLSTM_V, MLSTM_HEADS, MLSTM_HEADS,
            DIL_WIDTH, DIL_WIDTH, DIL_WIDTH,
            N_BRANCH * D_MODEL)
IN_WIDTH = sum(IN_SIZES)

kernel_name = 'hybrid_mla_mlstm_dilated_hmoe_block'


def rmsnorm(x, gain):
    xf = x.astype(jnp.float32)
    y = xf * lax.rsqrt(jnp.mean(xf * xf, axis=-1, keepdims=True) + EPS)
    return (y * gain.astype(jnp.float32)).astype(x.dtype)


def rope(x, pos):
    half = x.shape[-1] // 2
    inv = ROPE_THETA ** (-jnp.arange(half, dtype=jnp.float32) / half)
    ang = pos.astype(jnp.float32)[:, None] * inv[None, :]
    cos = jnp.cos(ang)[None, :, None, :]
    sin = jnp.sin(ang)[None, :, None, :]
    xf = x.astype(jnp.float32)
    x1, x2 = xf[..., :half], xf[..., half:]
    return jnp.concatenate([x1 * cos - x2 * sin, x1 * sin + x2 * cos], axis=-1).astype(x.dtype)


def causal_conv(x, w, b):
    C = x.shape[-1]
    y = lax.conv_general_dilated(x, w[:, None, :], window_strides=(1,), padding=((CONV_K - 1, 0),),
                                 dimension_numbers=('NWC', 'WIO', 'NWC'), feature_group_count=C)
    return y + b


def blocked_causal_attention(q, k, v):
    B, S, H, Dq = q.shape
    Dv = v.shape[-1]
    nq = S // Q_BLOCK
    scale = Dq ** -0.5
    qb = q.reshape(B, nq, Q_BLOCK, H, Dq).transpose(1, 0, 2, 3, 4)
    kpos = jnp.arange(S)

    def one_block(args):
        qi, j = args
        s = jnp.einsum('bqhd,bkhd->bhqk', qi, k).astype(jnp.float32) * scale
        qpos = j * Q_BLOCK + jnp.arange(Q_BLOCK)
        s = jnp.where(kpos[None, :] <= qpos[:, None], s, -jnp.inf)
        p = jax.nn.softmax(s, axis=-1)
        return jnp.einsum('bhqk,bkhd->bqhd', p.astype(v.dtype), v)

    o = lax.map(one_block, (qb, jnp.arange(nq)))
    return o.transpose(1, 0, 2, 3, 4).reshape(B, S, H, Dv)


def mla_branch(cq_raw, ckv_raw, kr, q_norm, kv_norm, w_uq, w_ukv, q_gain, k_gain):
    B, S, _ = cq_raw.shape
    cq = rmsnorm(cq_raw, q_norm)
    ckv = rmsnorm(ckv_raw, kv_norm)
    q = jnp.einsum('bsr,rn->bsn', cq, w_uq).reshape(B, S, MLA_HEADS, MLA_NOPE + MLA_ROPE)
    kv = jnp.einsum('bsr,rn->bsn', ckv, w_ukv).reshape(B, S, MLA_HEADS, MLA_NOPE + MLA_V)
    k_nope, v = kv[..., :MLA_NOPE], kv[..., MLA_NOPE:]
    k = jnp.concatenate([k_nope, jnp.broadcast_to(kr[:, :, None, :], (B, S, MLA_HEADS, MLA_ROPE))], axis=-1)
    q = rmsnorm(q, q_gain)
    k = rmsnorm(k, k_gain)
    pos = jnp.arange(S)
    q = jnp.concatenate([q[..., :MLA_NOPE], rope(q[..., MLA_NOPE:], pos)], axis=-1)
    k = jnp.concatenate([k[..., :MLA_NOPE], rope(k[..., MLA_NOPE:], pos)], axis=-1)
    o = blocked_causal_attention(q, k, v)
    return o.reshape(B, S, MLA_HEADS * MLA_V)


def mlstm_chunkwise(q, k, v, ig, fg):
    out_dtype = v.dtype
    B, S, H, dk = q.shape
    dv = v.shape[-1]
    L = MLSTM_CHUNK
    nc = S // L
    q = q.astype(jnp.float32)
    k = k.astype(jnp.float32) * (dk ** -0.5)
    v = v.astype(jnp.float32)
    ig = ig.astype(jnp.float32)
    logf = jax.nn.log_sigmoid(fg.astype(jnp.float32))

    def to_chunks(t):
        t = t.reshape((B, nc, L) + t.shape[2:])
        return jnp.swapaxes(jnp.moveaxis(t, 1, 0), 2, 3)

    causal = jnp.tril(jnp.ones((L, L), dtype=bool))

    def step(carry, xs):
        C, n, m = carry
        qc, kc, vc, ic, lf = xs
        b = jnp.cumsum(lf, axis=-1)
        dlog = jnp.where(causal, b[..., :, None] - b[..., None, :] + ic[..., None, :], -jnp.inf)
        inter = b + m[..., None]
        m_t = jnp.maximum(inter, jnp.max(dlog, axis=-1))
        sc = jnp.einsum('bhtd,bhsd->bhts', qc, kc) * jnp.exp(dlog - m_t[..., None])
        decay = jnp.exp(inter - m_t)
        num = decay[..., None] * jnp.einsum('bhtd,bhdv->bhtv', qc, C) + jnp.einsum('bhts,bhsv->bhtv', sc, vc)
        den = decay * jnp.einsum('bhtd,bhd->bht', qc, n) + jnp.sum(sc, axis=-1)
        h = num / jnp.maximum(jnp.abs(den), jnp.exp(-m_t))[..., None]
        b_end = b[..., -1]
        wlog = b_end[..., None] - b + ic
        m_new = jnp.maximum(b_end + m, jnp.max(wlog, axis=-1))
        carry_decay = jnp.exp(b_end + m - m_new)
        w = jnp.exp(wlog - m_new[..., None])
        C_new = carry_decay[..., None, None] * C + jnp.einsum('bhs,bhsd,bhsv->bhdv', w, kc, vc)
        n_new = carry_decay[..., None] * n + jnp.einsum('bhs,bhsd->bhd', w, kc)
        return (C_new, n_new, m_new), h

    init = (jnp.zeros((B, H, dk, dv), jnp.float32), jnp.zeros((B, H, dk), jnp.float32), jnp.zeros((B, H), jnp.float32))
    _, hs = lax.scan(step, init, (to_chunks(q), to_chunks(k), to_chunks(v), to_chunks(ig), to_chunks(logf)))
    hs = jnp.moveaxis(jnp.swapaxes(hs, 2, 3), 0, 1).reshape(B, S, H, dv)
    return hs.astype(out_dtype)


def mlstm_branch(qk_raw, v_raw, o_raw, i_raw, f_raw, conv_w, conv_b, b_i, b_f, head_gain):
    B, S, _ = qk_raw.shape
    qk = jax.nn.silu(causal_conv(qk_raw, conv_w, conv_b))
    q = qk[..., :MLSTM_HEADS * MLSTM_QK].reshape(B, S, MLSTM_HEADS, MLSTM_QK)
    k = qk[..., MLSTM_HEADS * MLSTM_QK:].reshape(B, S, MLSTM_HEADS, MLSTM_QK)
    v = v_raw.reshape(B, S, MLSTM_HEADS, MLSTM_V)
    h = mlstm_chunkwise(q, k, v, i_raw + b_i, f_raw + b_f)
    h = rmsnorm(h, head_gain.reshape(MLSTM_HEADS, MLSTM_V))
    return h.reshape(B, S, MLSTM_HEADS * MLSTM_V) * jax.nn.sigmoid(o_raw)


def dilated_window_attention(q, k, v, window, dilation):
    B, S, H, Dh = q.shape
    blk = window // dilation
    span = blk * dilation
    s_pad = -(-S // span) * span
    nb = s_pad // span

    def fold(t):
        t = jnp.pad(t, ((0, 0), (0, s_pad - S), (0, 0), (0, 0)))
        return t.reshape(B, nb, blk, dilation, H, Dh)

    def with_prev(t):
        prev = jnp.concatenate([jnp.zeros_like(t[:, :1]), t[:, :-1]], axis=1)
        return jnp.concatenate([prev, t], axis=2)

    qb = fold(q)
    kk = with_prev(fold(k))
    vv = with_prev(fold(v))
    s = jnp.einsum('bnqrhd,bnkrhd->bnrhqk', qb, kk).astype(jnp.float32) * (Dh ** -0.5)
    iq = jnp.arange(blk)[:, None]
    ik = jnp.arange(2 * blk)[None, :]
    dist = iq + blk - ik
    in_band = (dist >= 0) & (dist <= blk)
    before_start = (jnp.arange(nb) == 0)[:, None, None] & (ik < blk)[None]
    mask = in_band[None] & ~before_start
    s = jnp.where(mask[None, :, None, None], s, -jnp.inf)
    m = jnp.max(s, axis=-1, keepdims=True)
    p = jnp.exp(s - m)
    den = jnp.sum(p, axis=-1)
    o = jnp.einsum('bnrhqk,bnkrhd->bnqrhd', p, vv.astype(jnp.float32))
    o = o / jnp.transpose(den, (0, 1, 4, 2, 3))[..., None]
    lse = jnp.transpose(m[..., 0] + jnp.log(den), (0, 1, 4, 2, 3))
    o = o.reshape(B, s_pad, H, Dh)[:, :S]
    lse = lse.reshape(B, s_pad, H)[:, :S]
    return o, lse


def dilated_branch(q_raw, k_raw, v_raw, q_gain, k_gain):
    B, S, _ = q_raw.shape
    q = rmsnorm(q_raw.reshape(B, S, DIL_HEADS, DIL_HEAD_DIM), q_gain)
    k = rmsnorm(k_raw.reshape(B, S, DIL_HEADS, DIL_HEAD_DIM), k_gain)
    v = v_raw.reshape(B, S, DIL_HEADS, DIL_HEAD_DIM)
    outs, lses = [], []
    for window, dilation in DIL_PATTERNS:
        o, lse = dilated_window_attention(q, k, v, window, dilation)
        outs.append(o)
        lses.append(lse)
    w = jax.nn.softmax(jnp.stack(lses, axis=-1), axis=-1)
    o = jnp.einsum('bshp,bshpd->bshd', w, jnp.stack(outs, axis=-2))
    return o.astype(v_raw.dtype).reshape(B, S, DIL_WIDTH)


def token_mixer(u, w_in, mla_q_norm, mla_kv_norm, mla_w_uq, mla_w_ukv, mla_q_gain, mla_k_gain,
                mlstm_conv_w, mlstm_conv_b, mlstm_b_i, mlstm_b_f, mlstm_head_gain, dil_q_gain, dil_k_gain,
                w_branch_a, w_branch_b, w_branch_c, w_out):
    B, S, D = u.shape
    proj = jnp.einsum('bsd,dn->bsn', u, w_in)
    points = np.cumsum(IN_SIZES)[:-1].tolist()
    (cq, ckv, kr, m_qk, m_v, m_o, m_i, m_f, d_q, d_k, d_v, gates) = jnp.split(proj, points, axis=-1)
    ya = mla_branch(cq, ckv, kr, mla_q_norm, mla_kv_norm, mla_w_uq, mla_w_ukv, mla_q_gain, mla_k_gain)
    yb = mlstm_branch(m_qk, m_v, m_o, m_i, m_f, mlstm_conv_w, mlstm_conv_b, mlstm_b_i, mlstm_b_f, mlstm_head_gain)
    yc = dilated_branch(d_q, d_k, d_v, dil_q_gain, dil_k_gain)
    g = jax.nn.sigmoid(gates).reshape(B, S, N_BRANCH, D)
    merged = (g[:, :, 0] * jnp.einsum('bsn,nd->bsd', ya, w_branch_a)
              + g[:, :, 1] * jnp.einsum('bsn,nd->bsd', yb, w_branch_b)
              + g[:, :, 2] * jnp.einsum('bsn,nd->bsd', yc, w_branch_c))
    return jnp.einsum('bsd,de->bse', merged, w_out)


def hier_moe(h, w_rg, b_rg, w_re, b_re, w_gate, w_up, w_down):
    B, S, D = h.shape
    lg = jnp.einsum('bsd,dg->bsg', h, w_rg).astype(jnp.float32) + b_rg.astype(jnp.float32)
    p_group = jax.nn.softmax(lg, axis=-1)
    g_idx = jnp.argmax(lg, axis=-1)
    p_sel = jnp.take_along_axis(p_group, g_idx[..., None], axis=-1)[..., 0]
    le = (jnp.einsum('bsd,de->bse', h, w_re).astype(jnp.float32) + b_re.astype(jnp.float32))
    le = le.reshape(B, S, N_GROUPS, EXPERTS_PER_GROUP)
    le = jnp.take_along_axis(le, g_idx[..., None, None], axis=2)[:, :, 0]
    top_v, top_i = lax.top_k(le, TOP_K_FINE)
    w2 = jax.nn.softmax(top_v, axis=-1) * p_sel[..., None]
    expert_id = g_idx[..., None] * EXPERTS_PER_GROUP + top_i
    combine = jnp.einsum('bsk,bske->bse', w2, jax.nn.one_hot(expert_id, N_EXPERTS, dtype=jnp.float32))
    hid = jax.nn.silu(jnp.einsum('bsd,edf->bsef', h, w_gate)) * jnp.einsum('bsd,edf->bsef', h, w_up)
    hid = hid * combine.astype(h.dtype)[..., None]
    return jnp.einsum('bsef,efd->bsd', hid, w_down)


def setup_inputs(seed: int = 0) -> dict:
    key = jax.random.key(seed)
    ks = iter(jax.random.split(key, 40))
    L, D = DEPTH, D_MODEL

    def nrm(shape, scale):
        return scale * jax.random.normal(next(ks), shape, jnp.float32)

    def gain(shape):
        return 1.0 + 0.02 * jax.random.normal(next(ks), shape, jnp.float32)

    return {
        'x': nrm((BATCH, SEQ, D), 1.0),
        'c': nrm((BATCH, D), 1.0),
        'w_ada': nrm((L, D, 6 * D), 0.5 * D ** -0.5),
        'b_ada': nrm((L, 6 * D), 0.02),
        'attn_norm': gain((L, D)),
        'w_in': nrm((L, D, IN_WIDTH), D ** -0.5),
        'mla_q_norm': gain((L, MLA_Q_LORA)),
        'mla_kv_norm': gain((L, MLA_KV_LORA)),
        'mla_w_uq': nrm((L, MLA_Q_LORA, MLA_HEADS * (MLA_NOPE + MLA_ROPE)), MLA_Q_LORA ** -0.5),
        'mla_w_ukv': nrm((L, MLA_KV_LORA, MLA_HEADS * (MLA_NOPE + MLA_V)), MLA_KV_LORA ** -0.5),
        'mla_q_gain': gain((L, MLA_NOPE + MLA_ROPE)),
        'mla_k_gain': gain((L, MLA_NOPE + MLA_ROPE)),
        'mlstm_conv_w': nrm((L, CONV_K, 2 * MLSTM_HEADS * MLSTM_QK), CONV_K ** -0.5),
        'mlstm_conv_b': nrm((L, 2 * MLSTM_HEADS * MLSTM_QK), 0.02),
        'mlstm_b_i': nrm((L, MLSTM_HEADS), 0.1),
        'mlstm_b_f': 3.0 + 3.0 * jax.random.uniform(next(ks), (L, MLSTM_HEADS), jnp.float32),
        'mlstm_head_gain': gain((L, MLSTM_HEADS * MLSTM_V)),
        'dil_q_gain': gain((L, DIL_HEAD_DIM)),
        'dil_k_gain': gain((L, DIL_HEAD_DIM)),
        'w_branch_a': nrm((L, MLA_HEADS * MLA_V, D), (MLA_HEADS * MLA_V) ** -0.5),
        'w_branch_b': nrm((L, MLSTM_HEADS * MLSTM_V, D), (MLSTM_HEADS * MLSTM_V) ** -0.5),
        'w_branch_c': nrm((L, DIL_WIDTH, D), DIL_WIDTH ** -0.5),
        'w_out': nrm((L, D, D), D ** -0.5),
        'ffn_norm': gain((L, D)),
        'w_router_group': nrm((L, D, N_GROUPS), D ** -0.5),
        'b_router_group': nrm((L, N_GROUPS), 0.01),
        'w_router_expert': nrm((L, D, N_EXPERTS), D ** -0.5),
        'b_router_expert': nrm((L, N_EXPERTS), 0.01),
        'w_exp_gate': nrm((L, N_EXPERTS, D, EXPERT_FF), D ** -0.5),
        'w_exp_up': nrm((L, N_EXPERTS, D, EXPERT_FF), D ** -0.5),
        'w_exp_down': nrm((L, N_EXPERTS, EXPERT_FF, D), EXPERT_FF ** -0.5),
    }


def reference(x, c, w_ada, b_ada, attn_norm, w_in, mla_q_norm, mla_kv_norm, mla_w_uq, mla_w_ukv,
              mla_q_gain, mla_k_gain, mlstm_conv_w, mlstm_conv_b, mlstm_b_i, mlstm_b_f, mlstm_head_gain,
              dil_q_gain, dil_k_gain, w_branch_a, w_branch_b, w_branch_c, w_out, ffn_norm,
              w_router_group, b_router_group, w_router_expert, b_router_expert,
              w_exp_gate, w_exp_up, w_exp_down):
    B, S, D = x.shape
    c_act = jax.nn.silu(c)
    for l in range(DEPTH):
        mod = (jnp.einsum('bd,dm->bm', c_act, w_ada[l]) + b_ada[l]).reshape(B, 6, D)[:, :, None, :]
        sh_a, sc_a, g_a, sh_f, sc_f, g_f = (mod[:, i] for i in range(6))
        u = rmsnorm(x, attn_norm[l]) * (1.0 + sc_a) + sh_a
        y = token_mixer(u, w_in[l], mla_q_norm[l], mla_kv_norm[l], mla_w_uq[l], mla_w_ukv[l],
                        mla_q_gain[l], mla_k_gain[l], mlstm_conv_w[l], mlstm_conv_b[l], mlstm_b_i[l],
                        mlstm_b_f[l], mlstm_head_gain[l], dil_q_gain[l], dil_k_gain[l],
                        w_branch_a[l], w_branch_b[l], w_branch_c[l], w_out[l])
        x = x + g_a * y
        hf = rmsnorm(x, ffn_norm[l]) * (1.0 + sc_f) + sh_f
        x = x + g_f * hier_moe(hf, w_router_group[l], b_router_group[l], w_router_expert[l],
                               b_router_expert[l], w_exp_gate[l], w_exp_up[l], w_exp_down[l])
    return x
```

```python
import functools

import jax
import jax.numpy as jnp
import numpy as np
from jax import lax
from jax.experimental import pallas as pl
from jax.experimental.pallas import tpu as pltpu

F32 = jnp.float32
BF16 = jnp.bfloat16
HIGHEST = lax.Precision.HIGHEST
NEG_INF = float("-inf")

EPS = 1e-6
LANES = 128
VMEM_BYTES_V7X = 64 * 1024 * 1024

D_MODEL = 1024
MLA_HEADS = 8
MLA_Q_LORA = 256
MLA_KV_LORA = 128
MLA_NOPE = 64
MLA_ROPE = 32
MLA_V = 64
MLA_QK = MLA_NOPE + MLA_ROPE
ROPE_THETA = 10000.0
MLSTM_HEADS = 4
MLSTM_QK = 64
MLSTM_V = 128
CONV_K = 4
DIL_HEADS = 8
DIL_HEAD_DIM = 64
DIL_PATTERNS = ((128, 1), (512, 4), (2048, 16))
DIL_WIDTH = DIL_HEADS * DIL_HEAD_DIM
N_GROUPS = 4
EXPERTS_PER_GROUP = 4
N_EXPERTS = N_GROUPS * EXPERTS_PER_GROUP
EXPERT_FF = 256
N_BRANCH = 3
IN_SIZES = (MLA_Q_LORA, MLA_KV_LORA, MLA_ROPE,
            2 * MLSTM_HEADS * MLSTM_QK, MLSTM_HEADS * MLSTM_V, MLSTM_HEADS * MLSTM_V, MLSTM_HEADS, MLSTM_HEADS,
            DIL_WIDTH, DIL_WIDTH, DIL_WIDTH, N_BRANCH * D_MODEL)

COL_A = 0
COL_MQK = 512
COL_MV = 1024
COL_MO = 1536
COL_MIF = 2048
COL_DQ = 2176
COL_DK = 3200
COL_DV = 4224
COL_G = 4736
COL_END = 7808

FLASH_TILE = 512
DIL_MAX_WINDOW = max(w for w, _ in DIL_PATTERNS)


def _cparams(sem, vmem_mb):
    return pltpu.CompilerParams(dimension_semantics=sem, vmem_limit_bytes=vmem_mb * 1024 * 1024)


def _sigmoid(x):
    return 1.0 / (1.0 + jnp.exp(-x))


def _const_spec(shape, index_map):
    return pl.BlockSpec(shape, index_map, pipeline_mode=pl.Buffered(1))


def _adaln_kernel(c_ref, w_ref, b_ref, o_ref):
    c = c_ref[...]
    ca = c * _sigmoid(c)
    o_ref[0] = jnp.dot(ca, w_ref[0], preferred_element_type=F32, precision=HIGHEST) + b_ref[0]


def _adaln(c, w_ada, b_ada):
    L, D, D6 = w_ada.shape
    B = c.shape[0]
    rows = 8
    cp = jnp.zeros((rows, D), F32).at[:B].set(c)
    out = pl.pallas_call(
        _adaln_kernel,
        grid=(L, D6 // D),
        in_specs=[pl.BlockSpec((rows, D), lambda l, j: (0, 0)),
                  pl.BlockSpec((1, D, D), lambda l, j: (l, 0, j)),
                  pl.BlockSpec((1, 1, D), lambda l, j: (l, 0, j))],
        out_specs=pl.BlockSpec((1, rows, D), lambda l, j: (l, 0, j)),
        out_shape=jax.ShapeDtypeStruct((L, rows, D6), F32),
        compiler_params=_cparams(("parallel", "parallel"), 32),
        name="adaln",
    )(cp, w_ada, b_ada.reshape(L, 1, D6))
    return out[:, :B].reshape(L, B, D6 // D, 1, D)


def _inproj_kernel(x_ref, gain_ref, sc_ref, sh_ref, w_ref, dqg_ref, dkg_ref,
                   oa_ref, mqk_ref, mv_ref, mo_ref, mif_ref, qd_ref, kd_ref, vd_ref, g_ref):
    x = x_ref[...]
    ms = jnp.mean(x * x, axis=-1, keepdims=True)
    y = x * lax.rsqrt(ms + EPS) * gain_ref[0]
    u = (y * (1.0 + sc_ref[0]) + sh_ref[0]).astype(BF16)

    def proj(a, b):
        return jnp.dot(u, w_ref[0, :, a:b], preferred_element_type=F32)

    oa_ref[...] = proj(COL_A, COL_MQK)
    mqk_ref[...] = proj(COL_MQK, COL_MV)
    mv_ref[...] = proj(COL_MV, COL_MO).astype(BF16)
    mo_ref[...] = proj(COL_MO, COL_MIF)
    mif_ref[...] = proj(COL_MIF, COL_DQ)

    def head_norm(z, g):
        for h in range(DIL_HEADS):
            zh = z[:, h * LANES:(h + 1) * LANES]
            r = lax.rsqrt(jnp.sum(zh * zh, axis=-1, keepdims=True) / DIL_HEAD_DIM + EPS)
            yield h, zh * r * g

    dq = proj(COL_DQ, COL_DK)
    for h, zn in head_norm(dq, dqg_ref[0]):
        qd_ref[:, h * LANES:(h + 1) * LANES] = (zn * (DIL_HEAD_DIM ** -0.5)).astype(BF16)
    dk = proj(COL_DK, COL_DV)
    for h, zn in head_norm(dk, dkg_ref[0]):
        kd_ref[:, h * LANES:(h + 1) * LANES] = zn.astype(BF16)
    vd_ref[...] = proj(COL_DV, COL_G).astype(BF16)
    for c in range(N_BRANCH):
        g_ref[:, c * D_MODEL:(c + 1) * D_MODEL] = proj(COL_G + c * D_MODEL, COL_G + (c + 1) * D_MODEL)


def _inproj(x2, S, layer, gain, sc, sh, w_pad, dqg, dkg, tm=256):
    N, D = x2.shape
    per_b = S // tm
    row = lambda i: (i, 0)
    bsel = lambda i: (i // per_b, 0, 0)
    outs = [(512, F32), (512, F32), (512, BF16), (512, F32), (LANES, F32),
            (1024, BF16), (1024, BF16), (512, BF16), (3072, F32)]
    return pl.pallas_call(
        _inproj_kernel,
        grid=(N // tm,),
        in_specs=[pl.BlockSpec((tm, D), row),
                  _const_spec((1, 1, D), lambda i: (layer, 0, 0)),
                  pl.BlockSpec((1, 1, D), bsel),
                  pl.BlockSpec((1, 1, D), bsel),
                  _const_spec((1, D, COL_END), lambda i: (layer, 0, 0)),
                  _const_spec((1, 1, LANES), lambda i: (layer, 0, 0)),
                  _const_spec((1, 1, LANES), lambda i: (layer, 0, 0))],
        out_specs=[pl.BlockSpec((tm, w), row) for w, _ in outs],
        out_shape=[jax.ShapeDtypeStruct((N, w), dt) for w, dt in outs],
        compiler_params=_cparams(("parallel",), 48),
        name="inproj",
    )(x2, gain, sc, sh, w_pad, dqg, dkg)


def _mla_prep_kernel(a_ref, qn_ref, kvn_ref, wuq_ref, wkv_ref, qg_ref, kg_ref, cos_ref, sa_ref, sb_ref,
                     q_ref, k_ref, v_ref):
    a = a_ref[...]
    cq = a[:, :MLA_Q_LORA]
    ckv = a[:, MLA_Q_LORA:MLA_Q_LORA + MLA_KV_LORA]
    kr_tile = a[:, MLA_Q_LORA + MLA_KV_LORA:]

    def rms(z, g):
        return z * lax.rsqrt(jnp.mean(z * z, axis=-1, keepdims=True) + EPS) * g

    cqn = rms(cq, qn_ref[0]).astype(BF16)
    ckvn = rms(ckv, kvn_ref[0]).astype(BF16)
    qr = jnp.dot(cqn, wuq_ref[0], preferred_element_type=F32)
    kvr = jnp.dot(ckvn, wkv_ref[0], preferred_element_type=F32)
    v_ref[...] = kvr[:, MLA_HEADS * LANES:].astype(BF16)
    kr_at_rope = pltpu.roll(kr_tile, MLA_NOPE, 1)
    cosf, sa, sb = cos_ref[...], sa_ref[...], sb_ref[...]
    half = MLA_ROPE // 2

    def norm_rope(z, g):
        r = lax.rsqrt(jnp.sum(z * z, axis=-1, keepdims=True) / MLA_QK + EPS)
        zn = z * r * g
        return zn * cosf + pltpu.roll(zn, LANES - half, 1) * sa + pltpu.roll(zn, half, 1) * sb

    qg, kg = qg_ref[0], kg_ref[0]
    for h in range(MLA_HEADS):
        sl = slice(h * LANES, (h + 1) * LANES)
        q_ref[:, sl] = (norm_rope(qr[:, sl], qg) * (MLA_QK ** -0.5)).astype(BF16)
        k_ref[:, sl] = norm_rope(kvr[:, sl] + kr_at_rope, kg).astype(BF16)


def _rope_tables(S):
    half = MLA_ROPE // 2
    inv = ROPE_THETA ** (-jnp.arange(half, dtype=F32) / half)
    ang = jnp.arange(S).astype(F32)[:, None] * inv[None, :]
    cos, sin = jnp.cos(ang), jnp.sin(ang)
    one = jnp.ones((S, MLA_NOPE), F32)
    z = lambda n: jnp.zeros((S, n), F32)
    tail = LANES - MLA_QK
    cosf = jnp.concatenate([one, cos, cos, z(tail)], axis=1)
    sa = jnp.concatenate([z(MLA_NOPE), -sin, z(half), z(tail)], axis=1)
    sb = jnp.concatenate([z(MLA_NOPE), z(half), sin, z(tail)], axis=1)
    return cosf, sa, sb


def _mla_prep(oa, S, layer, qn, kvn, wuq, wkv, qg, kg, tables, tm=512):
    N = oa.shape[0]
    per_b = S // tm
    row = lambda i: (i, 0)
    pos = lambda i: (i % per_b, 0)
    HW = MLA_HEADS * LANES
    return pl.pallas_call(
        _mla_prep_kernel,
        grid=(N // tm,),
        in_specs=[pl.BlockSpec((tm, 512), row),
                  _const_spec((1, 1, MLA_Q_LORA), lambda i: (layer, 0, 0)),
                  _const_spec((1, 1, MLA_KV_LORA), lambda i: (layer, 0, 0)),
                  _const_spec((1, MLA_Q_LORA, HW), lambda i: (layer, 0, 0)),
                  _const_spec((1, MLA_KV_LORA, HW + MLA_HEADS * MLA_V), lambda i: (layer, 0, 0)),
                  _const_spec((1, 1, LANES), lambda i: (layer, 0, 0)),
                  _const_spec((1, 1, LANES), lambda i: (layer, 0, 0)),
                  pl.BlockSpec((tm, LANES), pos),
                  pl.BlockSpec((tm, LANES), pos),
                  pl.BlockSpec((tm, LANES), pos)],
        out_specs=[pl.BlockSpec((tm, HW), row), pl.BlockSpec((tm, HW), row),
                   pl.BlockSpec((tm, MLA_HEADS * MLA_V), row)],
        out_shape=[jax.ShapeDtypeStruct((N, HW), BF16), jax.ShapeDtypeStruct((N, HW), BF16),
                   jax.ShapeDtypeStruct((N, MLA_HEADS * MLA_V), BF16)],
        compiler_params=_cparams(("parallel",), 32),
        name="mla_prep",
    )(oa, qn, kvn, wuq, wkv, qg, kg, *tables)


def _flash_kernel(qi_ref, kj_ref, q_ref, k_ref, v_ref, *rest, band, T, W):
    if band:
        cm_ref, o_ref, m_ref, l_ref, acc_ref = rest
    else:
        o_ref, m_ref, l_ref, acc_ref = rest
    p_id = pl.program_id(2)
    qi = qi_ref[p_id]
    kj = kj_ref[p_id]
    if band:
        first = kj == qi
        last = kj == jnp.maximum(qi - W, 0)
    else:
        first = kj == 0
        last = kj == qi

    @pl.when(first)
    def _():
        m_ref[...] = jnp.full(m_ref.shape, NEG_INF, F32)
        l_ref[...] = jnp.zeros(l_ref.shape, F32)
        acc_ref[...] = jnp.zeros(acc_ref.shape, F32)

    half = LANES // 2
    lane = lax.broadcasted_iota(jnp.int32, (T, LANES), 1)
    lo = lane < half

    def step(masked):
        v = v_ref[0]
        zero = jnp.zeros_like(v)
        v_heads = (jnp.where(lo, v, zero), jnp.where(lo, zero, v))
        if band:
            cw = cm_ref[qi - kj]
            valid = cw > 0.0
        elif masked:
            valid = (lax.broadcasted_iota(jnp.int32, (T, T), 1) <= lax.broadcasted_iota(jnp.int32, (T, T), 0))
        alphas, pvs = [], []
        for hh in range(2):
            q = q_ref[0, :, hh * LANES:(hh + 1) * LANES]
            k = k_ref[0, :, hh * LANES:(hh + 1) * LANES]
            s = lax.dot_general(q, k, (((1,), (1,)), ((), ())), preferred_element_type=F32)
            if band or masked:
                s = jnp.where(valid, s, NEG_INF)
            m_prev = m_ref[hh]
            m_new = jnp.maximum(m_prev, jnp.max(s, axis=1, keepdims=True))
            alpha = jnp.exp(m_prev - m_new)
            p = jnp.exp(s - m_new)
            if band:
                p = p * cw
            l_ref[hh] = alpha * l_ref[hh] + jnp.sum(p, axis=1, keepdims=True)
            m_ref[hh] = m_new
            alphas.append(alpha)
            pvs.append(jnp.dot(p.astype(BF16), v_heads[hh], preferred_element_type=F32))
        acc_ref[...] = acc_ref[...] * jnp.where(lo, alphas[0], alphas[1]) + pvs[0] + pvs[1]

    if band:
        step(True)
    else:
        pl.when(kj == qi)(lambda: step(True))
        pl.when(kj < qi)(lambda: step(False))

    @pl.when(last)
    def _():
        o_ref[0] = (acc_ref[...] / jnp.where(lo, l_ref[0], l_ref[1])).astype(o_ref.dtype)


def _band_weights(T, W):
    d = (jnp.arange(W + 1)[:, None, None] * T + jnp.arange(T)[None, :, None] - jnp.arange(T)[None, None, :])
    cw = jnp.zeros(d.shape, F32)
    for window, dil in DIL_PATTERNS:
        cw = cw + ((d >= 0) & (d <= window) & (d % dil == 0)).astype(F32)
    return cw


def _flash(q, k, v, band, T=FLASH_TILE):
    B, S, HW = q.shape
    H = HW // LANES
    nq = S // T
    W = DIL_MAX_WINDOW // T if band else 0
    if band:
        pairs = [(i, i - j) for i in range(nq) for j in range(min(W, i) + 1)]
    else:
        pairs = [(i, j) for i in range(nq) for j in range(i + 1)]
    qi = jnp.asarray(np.array([p[0] for p in pairs], np.int32))
    kj = jnp.asarray(np.array([p[1] for p in pairs], np.int32))
    in_specs = [pl.BlockSpec((1, T, 2 * LANES), lambda b, h, p, qi, kj: (b, qi[p], h)),
                pl.BlockSpec((1, T, 2 * LANES), lambda b, h, p, qi, kj: (b, kj[p], h)),
                pl.BlockSpec((1, T, LANES), lambda b, h, p, qi, kj: (b, kj[p], h))]
    args = [q, k, v]
    if band:
        in_specs.append(_const_spec((W + 1, T, T), lambda b, h, p, qi, kj: (0, 0, 0)))
        args.append(_band_weights(T, W))
    return pl.pallas_call(
        functools.partial(_flash_kernel, band=band, T=T, W=W),
        grid_spec=pltpu.PrefetchScalarGridSpec(
            num_scalar_prefetch=2,
            grid=(B, H // 2, len(pairs)),
            in_specs=in_specs,
            out_specs=pl.BlockSpec((1, T, LANES), lambda b, h, p, qi, kj: (b, qi[p], h)),
            scratch_shapes=[pltpu.VMEM((2, T, 1), F32), pltpu.VMEM((2, T, 1), F32), pltpu.VMEM((T, LANES), F32)]),
        out_shape=jax.ShapeDtypeStruct((B, S, H * (LANES // 2)), BF16),
        compiler_params=_cparams(("parallel", "parallel", "arbitrary"), 40),
        name="flash_band" if band else "flash_causal",
    )(qi, kj, *args)


def _mlstm_kernel(qk_ref, v_ref, o_ref, if_ref, cw_ref, cb_ref, gb_ref, hg_ref, y_ref,
                  xs_ref, c_ref, n_ref, m_ref, *, Lc):
    H, DK, DV = MLSTM_HEADS, MLSTM_QK, MLSTM_V
    HALO = 8

    @pl.when(pl.program_id(1) == 0)
    def _():
        xs_ref[0:HALO, :] = jnp.zeros((HALO, 2 * H * DK), F32)
        c_ref[...] = jnp.zeros(c_ref.shape, F32)
        n_ref[...] = jnp.zeros(n_ref.shape, F32)
        m_ref[...] = jnp.zeros(m_ref.shape, F32)

    x = qk_ref[...]
    xs_ref[HALO:HALO + Lc, :] = x
    y = cb_ref[0]
    for j in range(CONV_K):
        y = y + cw_ref[0, j:j + 1, :] * xs_ref[pl.ds(HALO - (CONV_K - 1) + j, Lc), :]
    xs_ref[0:HALO, :] = x[Lc - HALO:, :]
    qk = y * _sigmoid(y)

    g = if_ref[...] + gb_ref[0]
    logf = jnp.minimum(g, 0.0) - jnp.log(1.0 + jnp.exp(-jnp.abs(g)))
    row = lax.broadcasted_iota(jnp.int32, (Lc, Lc), 0)
    col = lax.broadcasted_iota(jnp.int32, (Lc, Lc), 1)
    causal = col <= row
    tri = jnp.where(causal, 1.0, 0.0).astype(F32)
    bcum = jnp.dot(tri, logf, preferred_element_type=F32, precision=HIGHEST)
    g_t = g.T
    bcum_t = bcum.T

    for h in range(H):
        q = qk[:, h * DK:(h + 1) * DK]
        k = qk[:, (H + h) * DK:(H + h + 1) * DK] * (DK ** -0.5)
        v = v_ref[:, h * DV:(h + 1) * DV]
        bt = bcum[:, H + h:H + h + 1]
        ig_c = g[:, h:h + 1]
        bs = bcum_t[H + h:H + h + 1, :]
        ig_r = g_t[h:h + 1, :]
        m_prev = m_ref[h:h + 1, 0:1]
        dlog = jnp.where(causal, bt - bs + ig_r, NEG_INF)
        inter = bt + m_prev
        m_t = jnp.maximum(inter, jnp.max(dlog, axis=1, keepdims=True))
        qb = q.astype(BF16)
        s = lax.dot_general(qb, k.astype(BF16), (((1,), (1,)), ((), ())), preferred_element_type=F32)
        sc = s * jnp.exp(dlog - m_t)
        decay = jnp.exp(inter - m_t)
        c_h = c_ref[h]
        n_h = n_ref[h:h + 1, :]
        num = (decay * jnp.dot(qb, c_h.astype(BF16), preferred_element_type=F32)
               + jnp.dot(sc.astype(BF16), v, preferred_element_type=F32))
        den = decay * jnp.sum(q * n_h, axis=1, keepdims=True) + jnp.sum(sc, axis=1, keepdims=True)
        hh = num / jnp.maximum(jnp.abs(den), jnp.exp(-m_t))
        hn = hh * lax.rsqrt(jnp.mean(hh * hh, axis=-1, keepdims=True) + EPS) * hg_ref[0, :, h * DV:(h + 1) * DV]
        y_ref[:, h * DV:(h + 1) * DV] = (hn * _sigmoid(o_ref[:, h * DV:(h + 1) * DV])).astype(y_ref.dtype)

        b_end = bt[Lc - 1:Lc, :]
        wlog = b_end - bt + ig_c
        m_new = jnp.maximum(b_end + m_prev, jnp.max(wlog, axis=0, keepdims=True))
        cd = jnp.exp(b_end + m_prev - m_new)
        kw = k * jnp.exp(wlog - m_new)
        c_ref[h] = cd * c_h + lax.dot_general(kw.astype(BF16), v, (((0,), (0,)), ((), ())),
                                              preferred_element_type=F32)
        n_ref[h:h + 1, :] = cd * n_h + jnp.sum(kw, axis=0, keepdims=True)
        m_ref[h:h + 1, :] = jnp.broadcast_to(m_new, (1, LANES))


def _mlstm(mqk, mv, mo, mif, B, S, layer, conv_w, conv_b, gate_b, head_gain, Lc=256):
    H, DK, DV = MLSTM_HEADS, MLSTM_QK, MLSTM_V
    nc = S // Lc
    row = lambda b, c: (b * nc + c, 0)
    return pl.pallas_call(
        functools.partial(_mlstm_kernel, Lc=Lc),
        grid=(B, nc),
        in_specs=[pl.BlockSpec((Lc, 2 * H * DK), row),
                  pl.BlockSpec((Lc, H * DV), row),
                  pl.BlockSpec((Lc, H * DV), row),
                  pl.BlockSpec((Lc, LANES), row),
                  _const_spec((1, CONV_K, 2 * H * DK), lambda b, c: (layer, 0, 0)),
                  _const_spec((1, 1, 2 * H * DK), lambda b, c: (layer, 0, 0)),
                  _const_spec((1, 1, LANES), lambda b, c: (layer, 0, 0)),
                  _const_spec((1, 1, H * DV), lambda b, c: (layer, 0, 0))],
        out_specs=pl.BlockSpec((Lc, H * DV), row),
        out_shape=jax.ShapeDtypeStruct((B * S, H * DV), BF16),
        scratch_shapes=[pltpu.VMEM((Lc + 8, 2 * H * DK), F32), pltpu.VMEM((H, DK, DV), F32),
                        pltpu.VMEM((8, DK), F32), pltpu.VMEM((8, LANES), F32)],
        compiler_params=_cparams(("arbitrary", "arbitrary"), 32),
        name="mlstm",
    )(mqk, mv, mo, mif, conv_w, conv_b, gate_b, head_gain)


def _merge_kernel(ya_ref, yb_ref, yc_ref, g_ref, x_ref, ga_ref, wa_ref, wb_ref, wc_ref, wo_ref, out_ref):
    def branch(y_ref, w_ref, c):
        gate = _sigmoid(g_ref[:, c * D_MODEL:(c + 1) * D_MODEL])
        return gate * jnp.dot(y_ref[...], w_ref[0], preferred_element_type=F32)

    merged = branch(ya_ref, wa_ref, 0) + branch(yb_ref, wb_ref, 1) + branch(yc_ref, wc_ref, 2)
    y = jnp.dot(merged.astype(BF16), wo_ref[0], preferred_element_type=F32)
    out_ref[...] = x_ref[...] + ga_ref[0] * y


def _merge(ya, yb, yc, gates, x2, S, layer, ga, wa, wb, wc, wo, tm=512):
    N, D = x2.shape
    per_b = S // tm
    row = lambda i: (i, 0)
    wsel = lambda i: (layer, 0, 0)
    return pl.pallas_call(
        _merge_kernel,
        grid=(N // tm,),
        in_specs=[pl.BlockSpec((tm, 512), row), pl.BlockSpec((tm, 512), row), pl.BlockSpec((tm, 512), row),
                  pl.BlockSpec((tm, N_BRANCH * D), row), pl.BlockSpec((tm, D), row),
                  pl.BlockSpec((1, 1, D), lambda i: (i // per_b, 0, 0)),
                  _const_spec((1, 512, D), wsel), _const_spec((1, 512, D), wsel), _const_spec((1, 512, D), wsel),
                  _const_spec((1, D, D), wsel)],
        out_specs=pl.BlockSpec((tm, D), row),
        out_shape=jax.ShapeDtypeStruct((N, D), F32),
        compiler_params=_cparams(("parallel",), 48),
        name="merge",
    )(ya, yb, yc, gates, x2, ga, wa, wb, wc, wo)


def _ffn_kernel(x_ref, gain_ref, sc_ref, sh_ref, gf_ref, wr_ref, br_ref, wg_ref, wu_ref, wd_ref, out_ref):
    x = x_ref[...]
    tm = x.shape[0]
    ms = jnp.mean(x * x, axis=-1, keepdims=True)
    hf = x * lax.rsqrt(ms + EPS) * gain_ref[0] * (1.0 + sc_ref[0]) + sh_ref[0]

    logits = jnp.dot(hf, wr_ref[0], preferred_element_type=F32, precision=HIGHEST) + br_ref[0]
    lane_i = lax.broadcasted_iota(jnp.int32, (tm, LANES), 1)
    lane = lane_i.astype(F32)
    big = float(LANES)

    def first_argmax(z):
        zmax = jnp.max(z, axis=1, keepdims=True)
        return zmax, jnp.min(jnp.where(z == zmax, lane, big), axis=1, keepdims=True)

    lg = jnp.where(lane_i < N_GROUPS, logits, NEG_INF)
    gmax, g_idx = first_argmax(lg)
    p_sel = 1.0 / jnp.sum(jnp.exp(lg - gmax), axis=1, keepdims=True)
    lane_grp = ((lane_i - N_GROUPS) // EXPERTS_PER_GROUP).astype(F32)
    in_grp = (lane_i >= N_GROUPS) & (lane_i < N_GROUPS + N_EXPERTS) & (lane_grp == g_idx)
    le = jnp.where(in_grp, logits, NEG_INF)
    v1, i1 = first_argmax(le)
    le2 = jnp.where(lane == i1, NEG_INF, le)
    v2, i2 = first_argmax(le2)
    e2 = jnp.exp(v2 - v1)
    den = 1.0 + e2
    comb = jnp.where(lane == i1, (1.0 / den) * p_sel, 0.0) + jnp.where(lane == i2, (e2 / den) * p_sel, 0.0)

    hb = hf.astype(BF16)
    acc = jnp.zeros((tm, D_MODEL), F32)
    for e in range(N_EXPERTS):
        gate = jnp.dot(hb, wg_ref[0, e], preferred_element_type=F32)
        up = jnp.dot(hb, wu_ref[0, e], preferred_element_type=F32)
        hid = gate * _sigmoid(gate) * up * comb[:, N_GROUPS + e:N_GROUPS + e + 1]
        acc = acc + jnp.dot(hid.astype(BF16), wd_ref[0, e], preferred_element_type=F32)
    out_ref[...] = x + gf_ref[0] * acc


def _ffn(x2, S, layer, gain, sc, sh, gf, wr, br, wg, wu, wd, tm=512):
    N, D = x2.shape
    per_b = S // tm
    row = lambda i: (i, 0)
    bsel = lambda i: (i // per_b, 0, 0)
    wsel = lambda i: (layer, 0, 0, 0)
    return pl.pallas_call(
        _ffn_kernel,
        grid=(N // tm,),
        in_specs=[pl.BlockSpec((tm, D), row),
                  _const_spec((1, 1, D), lambda i: (layer, 0, 0)),
                  pl.BlockSpec((1, 1, D), bsel), pl.BlockSpec((1, 1, D), bsel), pl.BlockSpec((1, 1, D), bsel),
                  _const_spec((1, D, LANES), lambda i: (layer, 0, 0)),
                  _const_spec((1, 1, LANES), lambda i: (layer, 0, 0)),
                  _const_spec((1, N_EXPERTS, D, EXPERT_FF), wsel),
                  _const_spec((1, N_EXPERTS, D, EXPERT_FF), wsel),
                  _const_spec((1, N_EXPERTS, EXPERT_FF, D), wsel)],
        out_specs=pl.BlockSpec((tm, D), row),
        out_shape=jax.ShapeDtypeStruct((N, D), F32),
        compiler_params=_cparams(("parallel",), 56),
        name="ffn",
    )(x2, gain, sc, sh, gf, wr, br, wg, wu, wd)


def _pad_last(a, n):
    return jnp.pad(a, [(0, 0)] * (a.ndim - 1) + [(0, n - a.shape[-1])])


def _head_tiles(a, heads, dim):
    a = a.reshape(a.shape[:-1] + (heads, dim))
    return _pad_last(a, LANES).reshape(a.shape[:-2] + (heads * LANES,))


def _prep_w_in(w_in):
    pts = np.cumsum(IN_SIZES)[:-1].tolist()
    cq, ckv, kr, m_qk, m_v, m_o, m_i, m_f, d_q, d_k, d_v, gates = jnp.split(w_in, pts, axis=-1)
    seg_a = _pad_last(jnp.concatenate([cq, ckv, kr], axis=-1), 512)
    seg_if = _pad_last(jnp.concatenate([m_i, m_f], axis=-1), LANES)
    w = jnp.concatenate([seg_a, m_qk, m_v, m_o, seg_if,
                         _head_tiles(d_q, DIL_HEADS, DIL_HEAD_DIM), _head_tiles(d_k, DIL_HEADS, DIL_HEAD_DIM),
                         d_v, gates], axis=-1)
    assert w.shape[-1] == COL_END
    return w.astype(BF16)


def kernel(x, c, w_ada, b_ada, attn_norm, w_in, mla_q_norm, mla_kv_norm, mla_w_uq, mla_w_ukv, mla_q_gain, mla_k_gain, mlstm_conv_w, mlstm_conv_b, mlstm_b_i, mlstm_b_f, mlstm_head_gain, dil_q_gain, dil_k_gain, w_branch_a, w_branch_b, w_branch_c, w_out, ffn_norm, w_router_group, b_router_group, w_router_expert, b_router_expert, w_exp_gate, w_exp_up, w_exp_down):
    B, S, D = x.shape
    L = w_ada.shape[0]
    N = B * S
    assert D == D_MODEL and S % DIL_MAX_WINDOW == 0

    mod = _adaln(c, w_ada, b_ada)

    w_pad = _prep_w_in(w_in)
    wuq = _head_tiles(mla_w_uq, MLA_HEADS, MLA_QK).astype(BF16)
    ukv = mla_w_ukv.reshape(L, MLA_KV_LORA, MLA_HEADS, MLA_NOPE + MLA_V)
    wkv = jnp.concatenate([_pad_last(ukv[..., :MLA_NOPE], LANES).reshape(L, MLA_KV_LORA, MLA_HEADS * LANES),
                           ukv[..., MLA_NOPE:].reshape(L, MLA_KV_LORA, MLA_HEADS * MLA_V)], axis=-1).astype(BF16)
    rows = lambda a: a[:, None, :]
    qg = rows(_pad_last(mla_q_gain, LANES))
    kg = rows(_pad_last(mla_k_gain, LANES))
    dqg = rows(_pad_last(dil_q_gain, LANES))
    dkg = rows(_pad_last(dil_k_gain, LANES))
    gate_b = rows(_pad_last(jnp.concatenate([mlstm_b_i, mlstm_b_f], axis=-1), LANES))
    tables = _rope_tables(S)
    wa, wb, wc, wo = (w.astype(BF16) for w in (w_branch_a, w_branch_b, w_branch_c, w_out))
    wr = _pad_last(jnp.concatenate([w_router_group, w_router_expert], axis=-1), LANES)
    br = rows(_pad_last(jnp.concatenate([b_router_group, b_router_expert], axis=-1), LANES))
    wg, wu, wd = (w.astype(BF16) for w in (w_exp_gate, w_exp_up, w_exp_down))
    attn_norm, ffn_norm, mla_q_norm, mla_kv_norm, mlstm_conv_b, mlstm_head_gain = (
        rows(a) for a in (attn_norm, ffn_norm, mla_q_norm, mla_kv_norm, mlstm_conv_b, mlstm_head_gain))

    x2 = x.reshape(N, D)
    for l in range(L):
        sh_a, sc_a, g_a, sh_f, sc_f, g_f = (mod[l, :, i] for i in range(6))
        oa, mqk, mv, mo, mif, qd, kd, vd, gates = _inproj(x2, S, l, attn_norm, sc_a, sh_a, w_pad, dqg, dkg)
        q, k, v = _mla_prep(oa, S, l, mla_q_norm, mla_kv_norm, wuq, wkv, qg, kg, tables)
        ya = _flash(q.reshape(B, S, -1), k.reshape(B, S, -1), v.reshape(B, S, -1), band=False)
        yb = _mlstm(mqk, mv, mo, mif, B, S, l, mlstm_conv_w, mlstm_conv_b, gate_b, mlstm_head_gain)
        yc = _flash(qd.reshape(B, S, -1), kd.reshape(B, S, -1), vd.reshape(B, S, -1), band=True)
        x2 = _merge(ya.reshape(N, -1), yb, yc.reshape(N, -1), gates, x2, S, l, g_a, wa, wb, wc, wo)
        x2 = _ffn(x2, S, l, ffn_norm, sc_f, sh_f, g_f, wr, br, wg, wu, wd)
    return x2.reshape(B, S, D)
```

```python
import functools

import jax
import jax.numpy as jnp
import numpy as np
from jax import lax
from jax.experimental import pallas as pl
from jax.experimental.pallas import tpu as pltpu

F32 = jnp.float32
BF16 = jnp.bfloat16
HIGHEST = lax.Precision.HIGHEST
NEG_INF = float("-inf")

EPS = 1e-6
LANES = 128
VMEM_BYTES_V7X = 64 * 1024 * 1024

D_MODEL = 1024
MLA_HEADS = 8
MLA_Q_LORA = 256
MLA_KV_LORA = 128
MLA_NOPE = 64
MLA_ROPE = 32
MLA_V = 64
MLA_QK = MLA_NOPE + MLA_ROPE
ROPE_THETA = 10000.0
MLSTM_HEADS = 4
MLSTM_QK = 64
MLSTM_V = 128
CONV_K = 4
DIL_HEADS = 8
DIL_HEAD_DIM = 64
DIL_PATTERNS = ((128, 1), (512, 4), (2048, 16))
DIL_WIDTH = DIL_HEADS * DIL_HEAD_DIM
N_GROUPS = 4
EXPERTS_PER_GROUP = 4
N_EXPERTS = N_GROUPS * EXPERTS_PER_GROUP
EXPERT_FF = 256
N_BRANCH = 3
IN_SIZES = (MLA_Q_LORA, MLA_KV_LORA, MLA_ROPE,
            2 * MLSTM_HEADS * MLSTM_QK, MLSTM_HEADS * MLSTM_V, MLSTM_HEADS * MLSTM_V, MLSTM_HEADS, MLSTM_HEADS,
            DIL_WIDTH, DIL_WIDTH, DIL_WIDTH, N_BRANCH * D_MODEL)

COL_A = 0
COL_MQK = 512
COL_MV = 1024
COL_MO = 1536
COL_MIF = 2048
COL_DQ = 2176
COL_DK = 3200
COL_DV = 4224
COL_G = 5248
COL_END = 8320

FLASH_TILE = 512
DIL_MAX_WINDOW = max(w for w, _ in DIL_PATTERNS)
V_DIM = 64
V_ROWS = 80
LOG2E = 1.4426950408889634
assert MLA_V == V_DIM and DIL_HEAD_DIM == V_DIM


def _v_tile_t(v_tile):
    one_row = (lax.broadcasted_iota(jnp.int32, (1, LANES), 1) == V_DIM).astype(F32)
    return (v_tile + one_row).T[:V_ROWS, :].astype(BF16)


def _cparams(sem, vmem_mb):
    return pltpu.CompilerParams(dimension_semantics=sem, vmem_limit_bytes=vmem_mb * 1024 * 1024)


def _sigmoid(x):
    return 1.0 / (1.0 + jnp.exp(-x))


def _const_spec(shape, index_map):
    return pl.BlockSpec(shape, index_map, pipeline_mode=pl.Buffered(1))


def _adaln_kernel(c_ref, w_ref, b_ref, o_ref):
    c = c_ref[...]
    ca = c * _sigmoid(c)
    o_ref[0] = jnp.dot(ca, w_ref[0], preferred_element_type=F32, precision=HIGHEST) + b_ref[0]


def _adaln(c, w_ada, b_ada):
    L, D, D6 = w_ada.shape
    B = c.shape[0]
    rows = 8
    cp = jnp.zeros((rows, D), F32).at[:B].set(c)
    out = pl.pallas_call(
        _adaln_kernel,
        grid=(L, D6 // D),
        in_specs=[pl.BlockSpec((rows, D), lambda l, j: (0, 0)),
                  pl.BlockSpec((1, D, D), lambda l, j: (l, 0, j)),
                  pl.BlockSpec((1, 1, D), lambda l, j: (l, 0, j))],
        out_specs=pl.BlockSpec((1, rows, D), lambda l, j: (l, 0, j)),
        out_shape=jax.ShapeDtypeStruct((L, rows, D6), F32),
        compiler_params=_cparams(("parallel", "parallel"), 32),
        name="adaln",
    )(cp, w_ada, b_ada.reshape(L, 1, D6))
    return out[:, :B].reshape(L, B, D6 // D, 1, D)


def _inproj_kernel(x_ref, gain_ref, sc_ref, sh_ref, w_ref, dqg_ref, dkg_ref,
                   oa_ref, mqk_ref, mv_ref, mo_ref, mif_ref, qd_ref, kd_ref, vd_ref, g_ref):
    x = x_ref[...]
    ms = jnp.mean(x * x, axis=-1, keepdims=True)
    y = x * lax.rsqrt(ms + EPS) * gain_ref[0]
    u = (y * (1.0 + sc_ref[0]) + sh_ref[0]).astype(BF16)

    def proj(a, b):
        return jnp.dot(u, w_ref[0, :, a:b], preferred_element_type=F32)

    oa_ref[...] = proj(COL_A, COL_MQK)
    mqk_ref[...] = proj(COL_MQK, COL_MV)
    mv_ref[...] = proj(COL_MV, COL_MO).astype(BF16)
    mo_ref[...] = proj(COL_MO, COL_MIF)
    mif_ref[...] = proj(COL_MIF, COL_DQ)

    def head_norm(z, g):
        for h in range(DIL_HEADS):
            zh = z[:, h * LANES:(h + 1) * LANES]
            r = lax.rsqrt(jnp.sum(zh * zh, axis=-1, keepdims=True) / DIL_HEAD_DIM + EPS)
            yield h, zh * r * g

    dq = proj(COL_DQ, COL_DK)
    for h, zn in head_norm(dq, dqg_ref[0]):
        qd_ref[:, h * LANES:(h + 1) * LANES] = (zn * (DIL_HEAD_DIM ** -0.5 * LOG2E)).astype(BF16)
    dk = proj(COL_DK, COL_DV)
    for h, zn in head_norm(dk, dkg_ref[0]):
        kd_ref[:, h * LANES:(h + 1) * LANES] = zn.astype(BF16)
    dv = proj(COL_DV, COL_G)
    for h in range(DIL_HEADS):
        vd_ref[0, h] = _v_tile_t(dv[:, h * LANES:(h + 1) * LANES])
    for c in range(N_BRANCH):
        g_ref[:, c * D_MODEL:(c + 1) * D_MODEL] = proj(COL_G + c * D_MODEL, COL_G + (c + 1) * D_MODEL)


def _inproj(x2, S, layer, gain, sc, sh, w_pad, dqg, dkg, tm=256):
    N, D = x2.shape
    per_b = S // tm
    row = lambda i: (i, 0)
    bsel = lambda i: (i // per_b, 0, 0)
    B = N // S
    outs = [(512, F32), (512, F32), (512, BF16), (512, F32), (LANES, F32),
            (1024, BF16), (1024, BF16), None, (3072, F32)]
    vt_spec = pl.BlockSpec((1, DIL_HEADS, V_ROWS, tm), lambda i: (i // per_b, 0, 0, i % per_b))
    vt_shape = jax.ShapeDtypeStruct((B, DIL_HEADS, V_ROWS, S), BF16)
    return pl.pallas_call(
        _inproj_kernel,
        grid=(N // tm,),
        in_specs=[pl.BlockSpec((tm, D), row),
                  _const_spec((1, 1, D), lambda i: (layer, 0, 0)),
                  pl.BlockSpec((1, 1, D), bsel),
                  pl.BlockSpec((1, 1, D), bsel),
                  _const_spec((1, D, COL_END), lambda i: (layer, 0, 0)),
                  _const_spec((1, 1, LANES), lambda i: (layer, 0, 0)),
                  _const_spec((1, 1, LANES), lambda i: (layer, 0, 0))],
        out_specs=[vt_spec if o is None else pl.BlockSpec((tm, o[0]), row) for o in outs],
        out_shape=[vt_shape if o is None else jax.ShapeDtypeStruct((N, o[0]), o[1]) for o in outs],
        compiler_params=_cparams(("parallel",), 52),
        name="inproj",
    )(x2, gain, sc, sh, w_pad, dqg, dkg)


def _mla_prep_kernel(a_ref, qn_ref, kvn_ref, wuq_ref, wkv_ref, qg_ref, kg_ref, cos_ref, sa_ref, sb_ref,
                     q_ref, k_ref, vt_ref):
    HW = MLA_HEADS * LANES
    a = a_ref[...]
    cq = a[:, :MLA_Q_LORA]
    ckv = a[:, MLA_Q_LORA:MLA_Q_LORA + MLA_KV_LORA]
    kr_tile = a[:, MLA_Q_LORA + MLA_KV_LORA:]

    def rms(z, g):
        return z * lax.rsqrt(jnp.mean(z * z, axis=-1, keepdims=True) + EPS) * g

    cqn = rms(cq, qn_ref[0]).astype(BF16)
    ckvn = rms(ckv, kvn_ref[0]).astype(BF16)
    qr = jnp.dot(cqn, wuq_ref[0], preferred_element_type=F32)
    kvr = jnp.dot(ckvn, wkv_ref[0], preferred_element_type=F32)
    kr_at_rope = pltpu.roll(kr_tile, MLA_NOPE, 1)
    cosf, sa, sb = cos_ref[...], sa_ref[...], sb_ref[...]
    half = MLA_ROPE // 2

    def norm_rope(z, g):
        r = lax.rsqrt(jnp.sum(z * z, axis=-1, keepdims=True) / MLA_QK + EPS)
        zn = z * r * g
        return zn * cosf + pltpu.roll(zn, LANES - half, 1) * sa + pltpu.roll(zn, half, 1) * sb

    qg, kg = qg_ref[0], kg_ref[0]
    for h in range(MLA_HEADS):
        sl = slice(h * LANES, (h + 1) * LANES)
        q_ref[:, sl] = (norm_rope(qr[:, sl], qg) * (MLA_QK ** -0.5 * LOG2E)).astype(BF16)
        k_ref[:, sl] = norm_rope(kvr[:, sl] + kr_at_rope, kg).astype(BF16)
        vt_ref[0, h] = _v_tile_t(kvr[:, HW + h * LANES:HW + (h + 1) * LANES])


def _rope_tables(S):
    half = MLA_ROPE // 2
    inv = ROPE_THETA ** (-jnp.arange(half, dtype=F32) / half)
    ang = jnp.arange(S).astype(F32)[:, None] * inv[None, :]
    cos, sin = jnp.cos(ang), jnp.sin(ang)
    one = jnp.ones((S, MLA_NOPE), F32)
    z = lambda n: jnp.zeros((S, n), F32)
    tail = LANES - MLA_QK
    cosf = jnp.concatenate([one, cos, cos, z(tail)], axis=1)
    sa = jnp.concatenate([z(MLA_NOPE), -sin, z(half), z(tail)], axis=1)
    sb = jnp.concatenate([z(MLA_NOPE), z(half), sin, z(tail)], axis=1)
    return cosf, sa, sb


def _mla_prep(oa, S, layer, qn, kvn, wuq, wkv, qg, kg, tables, tm=512):
    N = oa.shape[0]
    B = N // S
    per_b = S // tm
    row = lambda i: (i, 0)
    pos = lambda i: (i % per_b, 0)
    HW = MLA_HEADS * LANES
    return pl.pallas_call(
        _mla_prep_kernel,
        grid=(N // tm,),
        in_specs=[pl.BlockSpec((tm, 512), row),
                  _const_spec((1, 1, MLA_Q_LORA), lambda i: (layer, 0, 0)),
                  _const_spec((1, 1, MLA_KV_LORA), lambda i: (layer, 0, 0)),
                  _const_spec((1, MLA_Q_LORA, HW), lambda i: (layer, 0, 0)),
                  _const_spec((1, MLA_KV_LORA, 2 * HW), lambda i: (layer, 0, 0)),
                  _const_spec((1, 1, LANES), lambda i: (layer, 0, 0)),
                  _const_spec((1, 1, LANES), lambda i: (layer, 0, 0)),
                  pl.BlockSpec((tm, LANES), pos),
                  pl.BlockSpec((tm, LANES), pos),
                  pl.BlockSpec((tm, LANES), pos)],
        out_specs=[pl.BlockSpec((tm, HW), row), pl.BlockSpec((tm, HW), row),
                   pl.BlockSpec((1, MLA_HEADS, V_ROWS, tm), lambda i: (i // per_b, 0, 0, i % per_b))],
        out_shape=[jax.ShapeDtypeStruct((N, HW), BF16), jax.ShapeDtypeStruct((N, HW), BF16),
                   jax.ShapeDtypeStruct((B, MLA_HEADS, V_ROWS, S), BF16)],
        compiler_params=_cparams(("parallel",), 32),
        name="mla_prep",
    )(oa, qn, kvn, wuq, wkv, qg, kg, *tables)


def _flash_kernel(qi_ref, kj_ref, qn_ref, kn_ref, q0_ref, k0_ref, vt_ref, *rest, band, T, W):
    if band:
        cw_ref, o_ref, m_ref, acc_ref, s_ref = rest
    else:
        o_ref, m_ref, acc_ref, s_ref = rest
    p_id = pl.program_id(2)
    qi = qi_ref[p_id]
    kj = kj_ref[p_id]
    if band:
        first = kj == qi
        last = kj == jnp.maximum(qi - W, 0)
    else:
        first = kj == 0
        last = kj == qi

    def scores(q_ref, k_ref, hh):
        q = q_ref[0, :, hh * LANES:(hh + 1) * LANES]
        k = k_ref[0, :, hh * LANES:(hh + 1) * LANES]
        return lax.dot_general(k, q, (((1,), (1,)), ((), ())), preferred_element_type=F32)

    @pl.when(p_id == 0)
    def _():
        for hh in range(2):
            s_ref[0, hh] = scores(q0_ref, k0_ref, hh)

    @pl.when(first)
    def _():
        m_ref[...] = jnp.full(m_ref.shape, NEG_INF, F32)
        acc_ref[...] = jnp.zeros(acc_ref.shape, F32)

    def step(masked, slot):
        if band:
            cw = cw_ref[qi - kj]
            valid = cw > 0.0
        elif masked:
            valid = lax.broadcasted_iota(jnp.int32, (T, T), 0) <= lax.broadcasted_iota(jnp.int32, (T, T), 1)
        for hh in range(2):
            s_ref[1 - slot, hh] = scores(qn_ref, kn_ref, hh)
            st = s_ref[slot, hh]
            if band or masked:
                st = jnp.where(valid, st, NEG_INF)
            m_prev = m_ref[hh]
            m_new = jnp.maximum(m_prev, jnp.max(st, axis=0, keepdims=True))
            alpha = jnp.exp2(m_prev - m_new)
            pt = jnp.exp2(st - m_new)
            if band:
                pt = pt * cw
            acc_ref[hh] = acc_ref[hh] * alpha + jnp.dot(vt_ref[0, hh], pt.astype(BF16), preferred_element_type=F32)
            m_ref[hh] = m_new

    parity = p_id % 2
    for slot in range(2):
        here = parity == slot
        if band:
            pl.when(here)(functools.partial(step, True, slot))
        else:
            pl.when(jnp.logical_and(here, last))(functools.partial(step, True, slot))
            pl.when(jnp.logical_and(here, jnp.logical_not(last)))(functools.partial(step, False, slot))

    @pl.when(last)
    def _():
        outs = []
        for hh in range(2):
            a = acc_ref[hh]
            outs.append(a[0:V_DIM, :] / a[V_DIM:V_DIM + 1, :])
        o_ref[0] = jnp.concatenate(outs, axis=0).T.astype(o_ref.dtype)


def _band_weights(T, W):
    d = (jnp.arange(W + 1)[:, None, None] * T + jnp.arange(T)[None, None, :] - jnp.arange(T)[None, :, None])
    cw = jnp.zeros(d.shape, F32)
    for window, dil in DIL_PATTERNS:
        cw = cw + ((d >= 0) & (d <= window) & (d % dil == 0)).astype(F32)
    return cw


def _flash(q, k, vt, band, T=FLASH_TILE):
    B, S, HW = q.shape
    H = HW // LANES
    nq = S // T
    W = DIL_MAX_WINDOW // T if band else 0
    if band:
        pairs = [(i, i - j) for i in range(nq) for j in range(min(W, i) + 1)]
    else:
        pairs = [(i, j) for i in range(nq) for j in range(i + 1)]
    n = len(pairs)
    qi = jnp.asarray(np.array([p[0] for p in pairs], np.int32))
    kj = jnp.asarray(np.array([p[1] for p in pairs], np.int32))
    nxt = lambda p: jnp.minimum(p + 1, n - 1)
    in_specs = [pl.BlockSpec((1, T, 2 * LANES), lambda b, h, p, qi, kj: (b, qi[nxt(p)], h)),
                pl.BlockSpec((1, T, 2 * LANES), lambda b, h, p, qi, kj: (b, kj[nxt(p)], h)),
                pl.BlockSpec((1, T, 2 * LANES), lambda b, h, p, qi, kj: (b, qi[0], h)),
                pl.BlockSpec((1, T, 2 * LANES), lambda b, h, p, qi, kj: (b, kj[0], h)),
                pl.BlockSpec((1, 2, V_ROWS, T), lambda b, h, p, qi, kj: (b, h, 0, kj[p]))]
    args = [q, k, q, k, vt]
    if band:
        in_specs.append(_const_spec((W + 1, T, T), lambda b, h, p, qi, kj: (0, 0, 0)))
        args.append(_band_weights(T, W))
    return pl.pallas_call(
        functools.partial(_flash_kernel, band=band, T=T, W=W),
        grid_spec=pltpu.PrefetchScalarGridSpec(
            num_scalar_prefetch=2,
            grid=(B, H // 2, n),
            in_specs=in_specs,
            out_specs=pl.BlockSpec((1, T, LANES), lambda b, h, p, qi, kj: (b, qi[p], h)),
            scratch_shapes=[pltpu.VMEM((2, 1, T), F32), pltpu.VMEM((2, V_ROWS, T), F32),
                            pltpu.VMEM((2, 2, T, T), F32)]),
        out_shape=jax.ShapeDtypeStruct((B, S, H * V_DIM), BF16),
        compiler_params=_cparams(("arbitrary", "arbitrary", "arbitrary"), 40),
        name="flash_band" if band else "flash_causal",
    )(qi, kj, *args)


def _mlstm_kernel(qk_ref, v_ref, o_ref, if_ref, cw_ref, cb_ref, gb_ref, hg_ref, y_ref,
                  xs_ref, c_ref, n_ref, m_ref, *, Lc):
    H, DK, DV = MLSTM_HEADS, MLSTM_QK, MLSTM_V
    HALO = 8

    @pl.when(pl.program_id(1) == 0)
    def _():
        xs_ref[0:HALO, :] = jnp.zeros((HALO, 2 * H * DK), F32)
        c_ref[...] = jnp.zeros(c_ref.shape, F32)
        n_ref[...] = jnp.zeros(n_ref.shape, F32)
        m_ref[...] = jnp.zeros(m_ref.shape, F32)

    x = qk_ref[...]
    xs_ref[HALO:HALO + Lc, :] = x
    y = cb_ref[0]
    for j in range(CONV_K):
        y = y + cw_ref[0, j:j + 1, :] * xs_ref[pl.ds(HALO - (CONV_K - 1) + j, Lc), :]
    xs_ref[0:HALO, :] = x[Lc - HALO:, :]
    qk = y * _sigmoid(y)

    g = if_ref[...] + gb_ref[0]
    logf = jnp.minimum(g, 0.0) - jnp.log(1.0 + jnp.exp(-jnp.abs(g)))
    row = lax.broadcasted_iota(jnp.int32, (Lc, Lc), 0)
    col = lax.broadcasted_iota(jnp.int32, (Lc, Lc), 1)
    causal = col <= row
    tri = jnp.where(causal, 1.0, 0.0).astype(F32)
    bcum = jnp.dot(tri, logf, preferred_element_type=F32, precision=HIGHEST)
    g_t = g.T
    bcum_t = bcum.T

    for h in range(H):
        q = qk[:, h * DK:(h + 1) * DK]
        k = qk[:, (H + h) * DK:(H + h + 1) * DK] * (DK ** -0.5)
        v = v_ref[:, h * DV:(h + 1) * DV]
        bt = bcum[:, H + h:H + h + 1]
        ig_c = g[:, h:h + 1]
        bs = bcum_t[H + h:H + h + 1, :]
        ig_r = g_t[h:h + 1, :]
        m_prev = m_ref[h:h + 1, 0:1]
        dlog = jnp.where(causal, bt - bs + ig_r, NEG_INF)
        inter = bt + m_prev
        m_t = jnp.maximum(inter, jnp.max(dlog, axis=1, keepdims=True))
        qb = q.astype(BF16)
        s = lax.dot_general(qb, k.astype(BF16), (((1,), (1,)), ((), ())), preferred_element_type=F32)
        sc = s * jnp.exp(dlog - m_t)
        decay = jnp.exp(inter - m_t)
        c_h = c_ref[h]
        n_h = n_ref[h:h + 1, :]
        num = (decay * jnp.dot(qb, c_h.astype(BF16), preferred_element_type=F32)
               + jnp.dot(sc.astype(BF16), v, preferred_element_type=F32))
        den = decay * jnp.sum(q * n_h, axis=1, keepdims=True) + jnp.sum(sc, axis=1, keepdims=True)
        hh = num / jnp.maximum(jnp.abs(den), jnp.exp(-m_t))
        hn = hh * lax.rsqrt(jnp.mean(hh * hh, axis=-1, keepdims=True) + EPS) * hg_ref[0, :, h * DV:(h + 1) * DV]
        y_ref[:, h * DV:(h + 1) * DV] = (hn * _sigmoid(o_ref[:, h * DV:(h + 1) * DV])).astype(y_ref.dtype)

        b_end = bt[Lc - 1:Lc, :]
        wlog = b_end - bt + ig_c
        m_new = jnp.maximum(b_end + m_prev, jnp.max(wlog, axis=0, keepdims=True))
        cd = jnp.exp(b_end + m_prev - m_new)
        kw = k * jnp.exp(wlog - m_new)
        c_ref[h] = cd * c_h + lax.dot_general(kw.astype(BF16), v, (((0,), (0,)), ((), ())),
                                              preferred_element_type=F32)
        n_ref[h:h + 1, :] = cd * n_h + jnp.sum(kw, axis=0, keepdims=True)
        m_ref[h:h + 1, :] = jnp.broadcast_to(m_new, (1, LANES))


def _mlstm(mqk, mv, mo, mif, B, S, layer, conv_w, conv_b, gate_b, head_gain, Lc=256):
    H, DK, DV = MLSTM_HEADS, MLSTM_QK, MLSTM_V
    nc = S // Lc
    row = lambda b, c: (b * nc + c, 0)
    return pl.pallas_call(
        functools.partial(_mlstm_kernel, Lc=Lc),
        grid=(B, nc),
        in_specs=[pl.BlockSpec((Lc, 2 * H * DK), row),
                  pl.BlockSpec((Lc, H * DV), row),
                  pl.BlockSpec((Lc, H * DV), row),
                  pl.BlockSpec((Lc, LANES), row),
                  _const_spec((1, CONV_K, 2 * H * DK), lambda b, c: (layer, 0, 0)),
                  _const_spec((1, 1, 2 * H * DK), lambda b, c: (layer, 0, 0)),
                  _const_spec((1, 1, LANES), lambda b, c: (layer, 0, 0)),
                  _const_spec((1, 1, H * DV), lambda b, c: (layer, 0, 0))],
        out_specs=pl.BlockSpec((Lc, H * DV), row),
        out_shape=jax.ShapeDtypeStruct((B * S, H * DV), BF16),
        scratch_shapes=[pltpu.VMEM((Lc + 8, 2 * H * DK), F32), pltpu.VMEM((H, DK, DV), F32),
                        pltpu.VMEM((8, DK), F32), pltpu.VMEM((8, LANES), F32)],
        compiler_params=_cparams(("arbitrary", "arbitrary"), 32),
        name="mlstm",
    )(mqk, mv, mo, mif, conv_w, conv_b, gate_b, head_gain)


def _merge_kernel(ya_ref, yb_ref, yc_ref, g_ref, x_ref, ga_ref, wa_ref, wb_ref, wc_ref, wo_ref, out_ref):
    def branch(y_ref, w_ref, c):
        gate = _sigmoid(g_ref[:, c * D_MODEL:(c + 1) * D_MODEL])
        return gate * jnp.dot(y_ref[...], w_ref[0], preferred_element_type=F32)

    merged = branch(ya_ref, wa_ref, 0) + branch(yb_ref, wb_ref, 1) + branch(yc_ref, wc_ref, 2)
    y = jnp.dot(merged.astype(BF16), wo_ref[0], preferred_element_type=F32)
    out_ref[...] = x_ref[...] + ga_ref[0] * y


def _merge(ya, yb, yc, gates, x2, S, layer, ga, wa, wb, wc, wo, tm=512):
    N, D = x2.shape
    per_b = S // tm
    row = lambda i: (i, 0)
    wsel = lambda i: (layer, 0, 0)
    return pl.pallas_call(
        _merge_kernel,
        grid=(N // tm,),
        in_specs=[pl.BlockSpec((tm, 512), row), pl.BlockSpec((tm, 512), row), pl.BlockSpec((tm, 512), row),
                  pl.BlockSpec((tm, N_BRANCH * D), row), pl.BlockSpec((tm, D), row),
                  pl.BlockSpec((1, 1, D), lambda i: (i // per_b, 0, 0)),
                  _const_spec((1, 512, D), wsel), _const_spec((1, 512, D), wsel), _const_spec((1, 512, D), wsel),
                  _const_spec((1, D, D), wsel)],
        out_specs=pl.BlockSpec((tm, D), row),
        out_shape=jax.ShapeDtypeStruct((N, D), F32),
        compiler_params=_cparams(("parallel",), 48),
        name="merge",
    )(ya, yb, yc, gates, x2, ga, wa, wb, wc, wo)


def _ffn_kernel(x_ref, gain_ref, sc_ref, sh_ref, gf_ref, wr_ref, br_ref, wg_ref, wu_ref, wd_ref, out_ref):
    x = x_ref[...]
    tm = x.shape[0]
    ms = jnp.mean(x * x, axis=-1, keepdims=True)
    hf = x * lax.rsqrt(ms + EPS) * gain_ref[0] * (1.0 + sc_ref[0]) + sh_ref[0]

    logits = jnp.dot(hf, wr_ref[0], preferred_element_type=F32, precision=HIGHEST) + br_ref[0]
    lane_i = lax.broadcasted_iota(jnp.int32, (tm, LANES), 1)
    lane = lane_i.astype(F32)
    big = float(LANES)

    def first_argmax(z):
        zmax = jnp.max(z, axis=1, keepdims=True)
        return zmax, jnp.min(jnp.where(z == zmax, lane, big), axis=1, keepdims=True)

    lg = jnp.where(lane_i < N_GROUPS, logits, NEG_INF)
    gmax, g_idx = first_argmax(lg)
    p_sel = 1.0 / jnp.sum(jnp.exp(lg - gmax), axis=1, keepdims=True)
    lane_grp = ((lane_i - N_GROUPS) // EXPERTS_PER_GROUP).astype(F32)
    in_grp = (lane_i >= N_GROUPS) & (lane_i < N_GROUPS + N_EXPERTS) & (lane_grp == g_idx)
    le = jnp.where(in_grp, logits, NEG_INF)
    v1, i1 = first_argmax(le)
    le2 = jnp.where(lane == i1, NEG_INF, le)
    v2, i2 = first_argmax(le2)
    e2 = jnp.exp(v2 - v1)
    den = 1.0 + e2
    comb = jnp.where(lane == i1, (1.0 / den) * p_sel, 0.0) + jnp.where(lane == i2, (e2 / den) * p_sel, 0.0)

    hb = hf.astype(BF16)
    acc = jnp.zeros((tm, D_MODEL), F32)
    for e in range(N_EXPERTS):
        gate = jnp.dot(hb, wg_ref[0, e], preferred_element_type=F32)
        up = jnp.dot(hb, wu_ref[0, e], preferred_element_type=F32)
        hid = gate * _sigmoid(gate) * up * comb[:, N_GROUPS + e:N_GROUPS + e + 1]
        acc = acc + jnp.dot(hid.astype(BF16), wd_ref[0, e], preferred_element_type=F32)
    out_ref[...] = x + gf_ref[0] * acc


def _ffn(x2, S, layer, gain, sc, sh, gf, wr, br, wg, wu, wd, tm=512):
    N, D = x2.shape
    per_b = S // tm
    row = lambda i: (i, 0)
    bsel = lambda i: (i // per_b, 0, 0)
    wsel = lambda i: (layer, 0, 0, 0)
    return pl.pallas_call(
        _ffn_kernel,
        grid=(N // tm,),
        in_specs=[pl.BlockSpec((tm, D), row),
                  _const_spec((1, 1, D), lambda i: (layer, 0, 0)),
                  pl.BlockSpec((1, 1, D), bsel), pl.BlockSpec((1, 1, D), bsel), pl.BlockSpec((1, 1, D), bsel),
                  _const_spec((1, D, LANES), lambda i: (layer, 0, 0)),
                  _const_spec((1, 1, LANES), lambda i: (layer, 0, 0)),
                  _const_spec((1, N_EXPERTS, D, EXPERT_FF), wsel),
                  _const_spec((1, N_EXPERTS, D, EXPERT_FF), wsel),
                  _const_spec((1, N_EXPERTS, EXPERT_FF, D), wsel)],
        out_specs=pl.BlockSpec((tm, D), row),
        out_shape=jax.ShapeDtypeStruct((N, D), F32),
        compiler_params=_cparams(("parallel",), 56),
        name="ffn",
    )(x2, gain, sc, sh, gf, wr, br, wg, wu, wd)


def _pad_last(a, n):
    return jnp.pad(a, [(0, 0)] * (a.ndim - 1) + [(0, n - a.shape[-1])])


def _head_tiles(a, heads, dim):
    a = a.reshape(a.shape[:-1] + (heads, dim))
    return _pad_last(a, LANES).reshape(a.shape[:-2] + (heads * LANES,))


def _prep_w_in(w_in):
    pts = np.cumsum(IN_SIZES)[:-1].tolist()
    cq, ckv, kr, m_qk, m_v, m_o, m_i, m_f, d_q, d_k, d_v, gates = jnp.split(w_in, pts, axis=-1)
    seg_a = _pad_last(jnp.concatenate([cq, ckv, kr], axis=-1), 512)
    seg_if = _pad_last(jnp.concatenate([m_i, m_f], axis=-1), LANES)
    w = jnp.concatenate([seg_a, m_qk, m_v, m_o, seg_if,
                         _head_tiles(d_q, DIL_HEADS, DIL_HEAD_DIM), _head_tiles(d_k, DIL_HEADS, DIL_HEAD_DIM),
                         _head_tiles(d_v, DIL_HEADS, DIL_HEAD_DIM), gates], axis=-1)
    assert w.shape[-1] == COL_END
    return w.astype(BF16)


def kernel(x, c, w_ada, b_ada, attn_norm, w_in, mla_q_norm, mla_kv_norm, mla_w_uq, mla_w_ukv, mla_q_gain, mla_k_gain, mlstm_conv_w, mlstm_conv_b, mlstm_b_i, mlstm_b_f, mlstm_head_gain, dil_q_gain, dil_k_gain, w_branch_a, w_branch_b, w_branch_c, w_out, ffn_norm, w_router_group, b_router_group, w_router_expert, b_router_expert, w_exp_gate, w_exp_up, w_exp_down):
    B, S, D = x.shape
    L = w_ada.shape[0]
    N = B * S
    assert D == D_MODEL and S % DIL_MAX_WINDOW == 0

    mod = _adaln(c, w_ada, b_ada)

    w_pad = _prep_w_in(w_in)
    wuq = _head_tiles(mla_w_uq, MLA_HEADS, MLA_QK).astype(BF16)
    ukv = mla_w_ukv.reshape(L, MLA_KV_LORA, MLA_HEADS, MLA_NOPE + MLA_V)
    wkv = jnp.concatenate([_pad_last(ukv[..., :MLA_NOPE], LANES).reshape(L, MLA_KV_LORA, MLA_HEADS * LANES),
                           _pad_last(ukv[..., MLA_NOPE:], LANES).reshape(L, MLA_KV_LORA, MLA_HEADS * LANES)],
                          axis=-1).astype(BF16)
    rows = lambda a: a[:, None, :]
    qg = rows(_pad_last(mla_q_gain, LANES))
    kg = rows(_pad_last(mla_k_gain, LANES))
    dqg = rows(_pad_last(dil_q_gain, LANES))
    dkg = rows(_pad_last(dil_k_gain, LANES))
    gate_b = rows(_pad_last(jnp.concatenate([mlstm_b_i, mlstm_b_f], axis=-1), LANES))
    tables = _rope_tables(S)
    wa, wb, wc, wo = (w.astype(BF16) for w in (w_branch_a, w_branch_b, w_branch_c, w_out))
    wr = _pad_last(jnp.concatenate([w_router_group, w_router_expert], axis=-1), LANES)
    br = rows(_pad_last(jnp.concatenate([b_router_group, b_router_expert], axis=-1), LANES))
    wg, wu, wd = (w.astype(BF16) for w in (w_exp_gate, w_exp_up, w_exp_down))
    attn_norm, ffn_norm, mla_q_norm, mla_kv_norm, mlstm_conv_b, mlstm_head_gain = (
        rows(a) for a in (attn_norm, ffn_norm, mla_q_norm, mla_kv_norm, mlstm_conv_b, mlstm_head_gain))

    x2 = x.reshape(N, D)
    for l in range(L):
        sh_a, sc_a, g_a, sh_f, sc_f, g_f = (mod[l, :, i] for i in range(6))
        oa, mqk, mv, mo, mif, qd, kd, vd, gates = _inproj(x2, S, l, attn_norm, sc_a, sh_a, w_pad, dqg, dkg)
        q, k, vt = _mla_prep(oa, S, l, mla_q_norm, mla_kv_norm, wuq, wkv, qg, kg, tables)
        ya = _flash(q.reshape(B, S, -1), k.reshape(B, S, -1), vt, band=False)
        yb = _mlstm(mqk, mv, mo, mif, B, S, l, mlstm_conv_w, mlstm_conv_b, gate_b, mlstm_head_gain)
        yc = _flash(qd.reshape(B, S, -1), kd.reshape(B, S, -1), vd, band=True)
        x2 = _merge(ya.reshape(N, -1), yb, yc.reshape(N, -1), gates, x2, S, l, g_a, wa, wb, wc, wo)
        x2 = _ffn(x2, S, l, ffn_norm, sc_f, sh_f, g_f, wr, br, wg, wu, wd)
    return x2.reshape(B, S, D)
```

```python
import functools

import jax
import jax.numpy as jnp
import numpy as np
from jax import lax
from jax.experimental import pallas as pl
from jax.experimental.pallas import tpu as pltpu

F32 = jnp.float32
BF16 = jnp.bfloat16
HIGHEST = lax.Precision.HIGHEST
NEG_INF = float("-inf")

EPS = 1e-6
LANES = 128
VMEM_BYTES_V7X = 64 * 1024 * 1024

D_MODEL = 1024
MLA_HEADS = 8
MLA_Q_LORA = 256
MLA_KV_LORA = 128
MLA_NOPE = 64
MLA_ROPE = 32
MLA_V = 64
MLA_QK = MLA_NOPE + MLA_ROPE
ROPE_THETA = 10000.0
MLSTM_HEADS = 4
MLSTM_QK = 64
MLSTM_V = 128
CONV_K = 4
DIL_HEADS = 8
DIL_HEAD_DIM = 64
DIL_PATTERNS = ((128, 1), (512, 4), (2048, 16))
DIL_WIDTH = DIL_HEADS * DIL_HEAD_DIM
N_GROUPS = 4
EXPERTS_PER_GROUP = 4
N_EXPERTS = N_GROUPS * EXPERTS_PER_GROUP
EXPERT_FF = 256
N_BRANCH = 3
IN_SIZES = (MLA_Q_LORA, MLA_KV_LORA, MLA_ROPE,
            2 * MLSTM_HEADS * MLSTM_QK, MLSTM_HEADS * MLSTM_V, MLSTM_HEADS * MLSTM_V, MLSTM_HEADS, MLSTM_HEADS,
            DIL_WIDTH, DIL_WIDTH, DIL_WIDTH, N_BRANCH * D_MODEL)

COL_A = 0
COL_MQK = 512
COL_MV = 1024
COL_MO = 1536
COL_MIF = 2048
COL_DQ = 2176
COL_DK = 3200
COL_DV = 4224
COL_G = 5248
COL_END = 8320

FLASH_TILE = 1024
DIL_MAX_WINDOW = max(w for w, _ in DIL_PATTERNS)
V_DIM = 64
V_ROWS = 80
LOG2E = 1.4426950408889634
assert MLA_V == V_DIM and DIL_HEAD_DIM == V_DIM


def _v_tile_t(v_tile):
    one_row = (lax.broadcasted_iota(jnp.int32, (1, LANES), 1) == V_DIM).astype(F32)
    return (v_tile + one_row).T[:V_ROWS, :].astype(BF16)


def _cparams(sem, vmem_mb):
    return pltpu.CompilerParams(dimension_semantics=sem, vmem_limit_bytes=vmem_mb * 1024 * 1024)


def _sigmoid(x):
    return 1.0 / (1.0 + jnp.exp(-x))


def _const_spec(shape, index_map):
    return pl.BlockSpec(shape, index_map, pipeline_mode=pl.Buffered(1))


def _adaln_kernel(c_ref, w_ref, b_ref, o_ref):
    c = c_ref[...]
    ca = c * _sigmoid(c)
    o_ref[0] = jnp.dot(ca, w_ref[0], preferred_element_type=F32, precision=HIGHEST) + b_ref[0]


def _adaln(c, w_ada, b_ada):
    L, D, D6 = w_ada.shape
    B = c.shape[0]
    rows = 8
    cp = jnp.zeros((rows, D), F32).at[:B].set(c)
    out = pl.pallas_call(
        _adaln_kernel,
        grid=(L, D6 // D),
        in_specs=[pl.BlockSpec((rows, D), lambda l, j: (0, 0)),
                  pl.BlockSpec((1, D, D), lambda l, j: (l, 0, j)),
                  pl.BlockSpec((1, 1, D), lambda l, j: (l, 0, j))],
        out_specs=pl.BlockSpec((1, rows, D), lambda l, j: (l, 0, j)),
        out_shape=jax.ShapeDtypeStruct((L, rows, D6), F32),
        compiler_params=_cparams(("parallel", "parallel"), 32),
        name="adaln",
    )(cp, w_ada, b_ada.reshape(L, 1, D6))
    return out[:, :B].reshape(L, B, D6 // D, 1, D)


def _inproj_kernel(x_ref, gain_ref, sc_ref, sh_ref, w_ref, dqg_ref, dkg_ref,
                   oa_ref, mqk_ref, mv_ref, mo_ref, mif_ref, qd_ref, kd_ref, vd_ref, g_ref):
    x = x_ref[...]
    ms = jnp.mean(x * x, axis=-1, keepdims=True)
    y = x * lax.rsqrt(ms + EPS) * gain_ref[0]
    u = (y * (1.0 + sc_ref[0]) + sh_ref[0]).astype(BF16)

    def proj(a, b):
        return jnp.dot(u, w_ref[0, :, a:b], preferred_element_type=F32)

    oa_ref[...] = proj(COL_A, COL_MQK)
    mqk_ref[...] = proj(COL_MQK, COL_MV)
    mv_ref[...] = proj(COL_MV, COL_MO).astype(BF16)
    mo_ref[...] = proj(COL_MO, COL_MIF)
    mif_ref[...] = proj(COL_MIF, COL_DQ)

    def head_norm(z, g):
        for h in range(DIL_HEADS):
            zh = z[:, h * LANES:(h + 1) * LANES]
            r = lax.rsqrt(jnp.sum(zh * zh, axis=-1, keepdims=True) / DIL_HEAD_DIM + EPS)
            yield h, zh * r * g

    dq = proj(COL_DQ, COL_DK)
    for h, zn in head_norm(dq, dqg_ref[0]):
        qd_ref[0, h // 2, :, (h % 2) * LANES:(h % 2 + 1) * LANES] = (zn * (DIL_HEAD_DIM ** -0.5 * LOG2E)).astype(BF16)
    dk = proj(COL_DK, COL_DV)
    for h, zn in head_norm(dk, dkg_ref[0]):
        kd_ref[0, h // 2, :, (h % 2) * LANES:(h % 2 + 1) * LANES] = zn.astype(BF16)
    dv = proj(COL_DV, COL_G)
    for h in range(DIL_HEADS):
        vd_ref[0, h] = _v_tile_t(dv[:, h * LANES:(h + 1) * LANES])
    for c in range(N_BRANCH):
        g_ref[:, c * D_MODEL:(c + 1) * D_MODEL] = proj(COL_G + c * D_MODEL, COL_G + (c + 1) * D_MODEL)


def _inproj(x2, S, layer, gain, sc, sh, w_pad, dqg, dkg, tm=256):
    N, D = x2.shape
    per_b = S // tm
    row = lambda i: (i, 0)
    bsel = lambda i: (i // per_b, 0, 0)
    B = N // S
    outs = [(512, F32), (512, F32), (512, BF16), (512, F32), (LANES, F32),
            "qk", "qk", "vt", (3072, F32)]
    head_major = lambda i: (i // per_b, 0, i % per_b, 0)
    special_spec = {"qk": pl.BlockSpec((1, DIL_HEADS // 2, tm, 2 * LANES), head_major),
                    "vt": pl.BlockSpec((1, DIL_HEADS, V_ROWS, tm), lambda i: (i // per_b, 0, 0, i % per_b))}
    special_shape = {"qk": jax.ShapeDtypeStruct((B, DIL_HEADS // 2, S, 2 * LANES), BF16),
                     "vt": jax.ShapeDtypeStruct((B, DIL_HEADS, V_ROWS, S), BF16)}
    return pl.pallas_call(
        _inproj_kernel,
        grid=(N // tm,),
        in_specs=[pl.BlockSpec((tm, D), row),
                  _const_spec((1, 1, D), lambda i: (layer, 0, 0)),
                  pl.BlockSpec((1, 1, D), bsel),
                  pl.BlockSpec((1, 1, D), bsel),
                  _const_spec((1, D, COL_END), lambda i: (layer, 0, 0)),
                  _const_spec((1, 1, LANES), lambda i: (layer, 0, 0)),
                  _const_spec((1, 1, LANES), lambda i: (layer, 0, 0))],
        out_specs=[special_spec[o] if isinstance(o, str) else pl.BlockSpec((tm, o[0]), row) for o in outs],
        out_shape=[special_shape[o] if isinstance(o, str) else jax.ShapeDtypeStruct((N, o[0]), o[1]) for o in outs],
        compiler_params=_cparams(("parallel",), 52),
        name="inproj",
    )(x2, gain, sc, sh, w_pad, dqg, dkg)


def _mla_prep_kernel(a_ref, qn_ref, kvn_ref, wuq_ref, wkv_ref, qg_ref, kg_ref, cos_ref, sa_ref, sb_ref,
                     q_ref, k_ref, vt_ref):
    HW = MLA_HEADS * LANES
    a = a_ref[...]
    cq = a[:, :MLA_Q_LORA]
    ckv = a[:, MLA_Q_LORA:MLA_Q_LORA + MLA_KV_LORA]
    kr_tile = a[:, MLA_Q_LORA + MLA_KV_LORA:]

    def rms(z, g):
        return z * lax.rsqrt(jnp.mean(z * z, axis=-1, keepdims=True) + EPS) * g

    cqn = rms(cq, qn_ref[0]).astype(BF16)
    ckvn = rms(ckv, kvn_ref[0]).astype(BF16)
    qr = jnp.dot(cqn, wuq_ref[0], preferred_element_type=F32)
    kvr = jnp.dot(ckvn, wkv_ref[0], preferred_element_type=F32)
    kr_at_rope = pltpu.roll(kr_tile, MLA_NOPE, 1)
    cosf, sa, sb = cos_ref[...], sa_ref[...], sb_ref[...]
    half = MLA_ROPE // 2

    def norm_rope(z, g):
        r = lax.rsqrt(jnp.sum(z * z, axis=-1, keepdims=True) / MLA_QK + EPS)
        zn = z * r * g
        return zn * cosf + pltpu.roll(zn, LANES - half, 1) * sa + pltpu.roll(zn, half, 1) * sb

    qg, kg = qg_ref[0], kg_ref[0]
    for h in range(MLA_HEADS):
        sl = slice(h * LANES, (h + 1) * LANES)
        pair_sl = slice((h % 2) * LANES, (h % 2 + 1) * LANES)
        q_ref[0, h // 2, :, pair_sl] = (norm_rope(qr[:, sl], qg) * (MLA_QK ** -0.5 * LOG2E)).astype(BF16)
        k_ref[0, h // 2, :, pair_sl] = norm_rope(kvr[:, sl] + kr_at_rope, kg).astype(BF16)
        vt_ref[0, h] = _v_tile_t(kvr[:, HW + h * LANES:HW + (h + 1) * LANES])


def _rope_tables(S):
    half = MLA_ROPE // 2
    inv = ROPE_THETA ** (-jnp.arange(half, dtype=F32) / half)
    ang = jnp.arange(S).astype(F32)[:, None] * inv[None, :]
    cos, sin = jnp.cos(ang), jnp.sin(ang)
    one = jnp.ones((S, MLA_NOPE), F32)
    z = lambda n: jnp.zeros((S, n), F32)
    tail = LANES - MLA_QK
    cosf = jnp.concatenate([one, cos, cos, z(tail)], axis=1)
    sa = jnp.concatenate([z(MLA_NOPE), -sin, z(half), z(tail)], axis=1)
    sb = jnp.concatenate([z(MLA_NOPE), z(half), sin, z(tail)], axis=1)
    return cosf, sa, sb


def _mla_prep(oa, S, layer, qn, kvn, wuq, wkv, qg, kg, tables, tm=512):
    N = oa.shape[0]
    B = N // S
    per_b = S // tm
    row = lambda i: (i, 0)
    pos = lambda i: (i % per_b, 0)
    HW = MLA_HEADS * LANES
    return pl.pallas_call(
        _mla_prep_kernel,
        grid=(N // tm,),
        in_specs=[pl.BlockSpec((tm, 512), row),
                  _const_spec((1, 1, MLA_Q_LORA), lambda i: (layer, 0, 0)),
                  _const_spec((1, 1, MLA_KV_LORA), lambda i: (layer, 0, 0)),
                  _const_spec((1, MLA_Q_LORA, HW), lambda i: (layer, 0, 0)),
                  _const_spec((1, MLA_KV_LORA, 2 * HW), lambda i: (layer, 0, 0)),
                  _const_spec((1, 1, LANES), lambda i: (layer, 0, 0)),
                  _const_spec((1, 1, LANES), lambda i: (layer, 0, 0)),
                  pl.BlockSpec((tm, LANES), pos),
                  pl.BlockSpec((tm, LANES), pos),
                  pl.BlockSpec((tm, LANES), pos)],
        out_specs=[pl.BlockSpec((1, MLA_HEADS // 2, tm, 2 * LANES), lambda i: (i // per_b, 0, i % per_b, 0)),
                   pl.BlockSpec((1, MLA_HEADS // 2, tm, 2 * LANES), lambda i: (i // per_b, 0, i % per_b, 0)),
                   pl.BlockSpec((1, MLA_HEADS, V_ROWS, tm), lambda i: (i // per_b, 0, 0, i % per_b))],
        out_shape=[jax.ShapeDtypeStruct((B, MLA_HEADS // 2, S, 2 * LANES), BF16),
                   jax.ShapeDtypeStruct((B, MLA_HEADS // 2, S, 2 * LANES), BF16),
                   jax.ShapeDtypeStruct((B, MLA_HEADS, V_ROWS, S), BF16)],
        compiler_params=_cparams(("parallel",), 32),
        name="mla_prep",
    )(oa, qn, kvn, wuq, wkv, qg, kg, *tables)


def _flash_kernel(qi_ref, kj_ref, qn_ref, kn_ref, q0_ref, k0_ref, vt_ref, *rest, band, T, W):
    if band:
        bias_ref, o_ref, m_ref, acc_ref, s_ref = rest
    else:
        o_ref, m_ref, acc_ref, s_ref = rest
    p_id = pl.program_id(2)
    qi = qi_ref[p_id]
    kj = kj_ref[p_id]
    if band:
        first = kj == qi
        last = kj == jnp.maximum(qi - W, 0)
    else:
        first = kj == 0
        last = kj == qi

    def scores(q_ref, k_ref, hh):
        q = q_ref[0, 0, :, hh * LANES:(hh + 1) * LANES]
        k = k_ref[0, 0, :, hh * LANES:(hh + 1) * LANES]
        return lax.dot_general(k, q, (((1,), (1,)), ((), ())), preferred_element_type=F32)

    @pl.when(p_id == 0)
    def _():
        for hh in range(2):
            s_ref[0, hh] = scores(q0_ref, k0_ref, hh)

    @pl.when(first)
    def _():
        m_ref[...] = jnp.full(m_ref.shape, NEG_INF, F32)
        acc_ref[...] = jnp.zeros(acc_ref.shape, F32)

    def step(masked, slot):
        if masked and not band:
            valid = lax.broadcasted_iota(jnp.int32, (T, T), 0) <= lax.broadcasted_iota(jnp.int32, (T, T), 1)
        for hh in range(2):
            s_ref[1 - slot, hh] = scores(qn_ref, kn_ref, hh)
            st = s_ref[slot, hh]
            if band:
                st = st + bias_ref[qi - kj]
            elif masked:
                st = jnp.where(valid, st, NEG_INF)
            m_prev = m_ref[hh]
            m_new = jnp.maximum(m_prev, jnp.max(st, axis=0, keepdims=True))
            alpha = jnp.exp2(m_prev - m_new)
            pt = jnp.exp2(st - m_new).astype(BF16)
            acc_ref[hh] = acc_ref[hh] * alpha + jnp.dot(vt_ref[0, hh], pt, preferred_element_type=F32)
            m_ref[hh] = m_new

    parity = p_id % 2
    for slot in range(2):
        here = parity == slot
        if band:
            pl.when(here)(functools.partial(step, True, slot))
        else:
            pl.when(jnp.logical_and(here, last))(functools.partial(step, True, slot))
            pl.when(jnp.logical_and(here, jnp.logical_not(last)))(functools.partial(step, False, slot))

    @pl.when(last)
    def _():
        outs = []
        for hh in range(2):
            a = acc_ref[hh]
            outs.append(a[0:V_DIM, :] / a[V_DIM:V_DIM + 1, :])
        o_ref[0] = jnp.concatenate(outs, axis=0).T.astype(o_ref.dtype)


def _band_bias(T, W):
    d = (jnp.arange(W + 1)[:, None, None] * T + jnp.arange(T)[None, None, :] - jnp.arange(T)[None, :, None])
    cw = jnp.zeros(d.shape, F32)
    for window, dil in DIL_PATTERNS:
        cw = cw + ((d >= 0) & (d <= window) & (d % dil == 0)).astype(F32)
    return jnp.log2(cw)


def _flash(q, k, vt, band, T=FLASH_TILE):
    B, HP, S, _ = q.shape
    H = 2 * HP
    nq = S // T
    W = DIL_MAX_WINDOW // T if band else 0
    if band:
        pairs = [(i, i - j) for i in range(nq) for j in range(min(W, i) + 1)]
    else:
        pairs = [(i, j) for i in range(nq) for j in range(i + 1)]
    n = len(pairs)
    qi = jnp.asarray(np.array([p[0] for p in pairs], np.int32))
    kj = jnp.asarray(np.array([p[1] for p in pairs], np.int32))
    nxt = lambda p: jnp.minimum(p + 1, n - 1)
    in_specs = [pl.BlockSpec((1, 1, T, 2 * LANES), lambda b, h, p, qi, kj: (b, h, qi[nxt(p)], 0)),
                pl.BlockSpec((1, 1, T, 2 * LANES), lambda b, h, p, qi, kj: (b, h, kj[nxt(p)], 0)),
                pl.BlockSpec((1, 1, T, 2 * LANES), lambda b, h, p, qi, kj: (b, h, qi[0], 0)),
                pl.BlockSpec((1, 1, T, 2 * LANES), lambda b, h, p, qi, kj: (b, h, kj[0], 0)),
                pl.BlockSpec((1, 2, V_ROWS, T), lambda b, h, p, qi, kj: (b, h, 0, kj[p]))]
    args = [q, k, q, k, vt]
    if band:
        in_specs.append(_const_spec((W + 1, T, T), lambda b, h, p, qi, kj: (0, 0, 0)))
        args.append(_band_bias(T, W))
    return pl.pallas_call(
        functools.partial(_flash_kernel, band=band, T=T, W=W),
        grid_spec=pltpu.PrefetchScalarGridSpec(
            num_scalar_prefetch=2,
            grid=(B, H // 2, n),
            in_specs=in_specs,
            out_specs=pl.BlockSpec((1, T, LANES), lambda b, h, p, qi, kj: (b, qi[p], h)),
            scratch_shapes=[pltpu.VMEM((2, 1, T), F32), pltpu.VMEM((2, V_ROWS, T), F32),
                            pltpu.VMEM((2, 2, T, T), F32)]),
        out_shape=jax.ShapeDtypeStruct((B, S, H * V_DIM), BF16),
        compiler_params=_cparams(("arbitrary", "arbitrary", "arbitrary"), 52),
        name="flash_band" if band else "flash_causal",
    )(qi, kj, *args)


def _mlstm_kernel(qk_ref, v_ref, o_ref, if_ref, cw_ref, cb_ref, gb_ref, hg_ref, y_ref,
                  xs_ref, c_ref, n_ref, m_ref, *, Lc):
    H, DK, DV = MLSTM_HEADS, MLSTM_QK, MLSTM_V
    HALO = 8

    @pl.when(pl.program_id(1) == 0)
    def _():
        xs_ref[0:HALO, :] = jnp.zeros((HALO, 2 * H * DK), F32)
        c_ref[...] = jnp.zeros(c_ref.shape, F32)
        n_ref[...] = jnp.zeros(n_ref.shape, F32)
        m_ref[...] = jnp.zeros(m_ref.shape, F32)

    x = qk_ref[...]
    xs_ref[HALO:HALO + Lc, :] = x
    y = cb_ref[0]
    for j in range(CONV_K):
        y = y + cw_ref[0, j:j + 1, :] * xs_ref[pl.ds(HALO - (CONV_K - 1) + j, Lc), :]
    xs_ref[0:HALO, :] = x[Lc - HALO:, :]
    qk = y * _sigmoid(y)

    g = if_ref[...] + gb_ref[0]
    logf = jnp.minimum(g, 0.0) - jnp.log(1.0 + jnp.exp(-jnp.abs(g)))
    row = lax.broadcasted_iota(jnp.int32, (Lc, Lc), 0)
    col = lax.broadcasted_iota(jnp.int32, (Lc, Lc), 1)
    causal = col <= row
    tri = jnp.where(causal, 1.0, 0.0).astype(F32)
    bcum = jnp.dot(tri, logf, preferred_element_type=F32, precision=HIGHEST)
    g_t = g.T
    bcum_t = bcum.T

    for h in range(H):
        q = qk[:, h * DK:(h + 1) * DK]
        k = qk[:, (H + h) * DK:(H + h + 1) * DK] * (DK ** -0.5)
        v = v_ref[:, h * DV:(h + 1) * DV]
        bt = bcum[:, H + h:H + h + 1]
        ig_c = g[:, h:h + 1]
        bs = bcum_t[H + h:H + h + 1, :]
        ig_r = g_t[h:h + 1, :]
        m_prev = m_ref[h:h + 1, 0:1]
        dlog = jnp.where(causal, bt - bs + ig_r, NEG_INF)
        inter = bt + m_prev
        m_t = jnp.maximum(inter, jnp.max(dlog, axis=1, keepdims=True))
        qb = q.astype(BF16)
        s = lax.dot_general(qb, k.astype(BF16), (((1,), (1,)), ((), ())), preferred_element_type=F32)
        sc = s * jnp.exp(dlog - m_t)
        decay = jnp.exp(inter - m_t)
        c_h = c_ref[h]
        n_h = n_ref[h:h + 1, :]
        num = (decay * jnp.dot(qb, c_h.astype(BF16), preferred_element_type=F32)
               + jnp.dot(sc.astype(BF16), v, preferred_element_type=F32))
        den = decay * jnp.sum(q * n_h, axis=1, keepdims=True) + jnp.sum(sc, axis=1, keepdims=True)
        hh = num / jnp.maximum(jnp.abs(den), jnp.exp(-m_t))
        hn = hh * lax.rsqrt(jnp.mean(hh * hh, axis=-1, keepdims=True) + EPS) * hg_ref[0, :, h * DV:(h + 1) * DV]
        y_ref[:, h * DV:(h + 1) * DV] = (hn * _sigmoid(o_ref[:, h * DV:(h + 1) * DV])).astype(y_ref.dtype)

        b_end = bt[Lc - 1:Lc, :]
        wlog = b_end - bt + ig_c
        m_new = jnp.maximum(b_end + m_prev, jnp.max(wlog, axis=0, keepdims=True))
        cd = jnp.exp(b_end + m_prev - m_new)
        kw = k * jnp.exp(wlog - m_new)
        c_ref[h] = cd * c_h + lax.dot_general(kw.astype(BF16), v, (((0,), (0,)), ((), ())),
                                              preferred_element_type=F32)
        n_ref[h:h + 1, :] = cd * n_h + jnp.sum(kw, axis=0, keepdims=True)
        m_ref[h:h + 1, :] = jnp.broadcast_to(m_new, (1, LANES))


def _mlstm(mqk, mv, mo, mif, B, S, layer, conv_w, conv_b, gate_b, head_gain, Lc=256):
    H, DK, DV = MLSTM_HEADS, MLSTM_QK, MLSTM_V
    nc = S // Lc
    row = lambda b, c: (b * nc + c, 0)
    return pl.pallas_call(
        functools.partial(_mlstm_kernel, Lc=Lc),
        grid=(B, nc),
        in_specs=[pl.BlockSpec((Lc, 2 * H * DK), row),
                  pl.BlockSpec((Lc, H * DV), row),
                  pl.BlockSpec((Lc, H * DV), row),
                  pl.BlockSpec((Lc, LANES), row),
                  _const_spec((1, CONV_K, 2 * H * DK), lambda b, c: (layer, 0, 0)),
                  _const_spec((1, 1, 2 * H * DK), lambda b, c: (layer, 0, 0)),
                  _const_spec((1, 1, LANES), lambda b, c: (layer, 0, 0)),
                  _const_spec((1, 1, H * DV), lambda b, c: (layer, 0, 0))],
        out_specs=pl.BlockSpec((Lc, H * DV), row),
        out_shape=jax.ShapeDtypeStruct((B * S, H * DV), BF16),
        scratch_shapes=[pltpu.VMEM((Lc + 8, 2 * H * DK), F32), pltpu.VMEM((H, DK, DV), F32),
                        pltpu.VMEM((8, DK), F32), pltpu.VMEM((8, LANES), F32)],
        compiler_params=_cparams(("arbitrary", "arbitrary"), 32),
        name="mlstm",
    )(mqk, mv, mo, mif, conv_w, conv_b, gate_b, head_gain)


def _merge_kernel(ya_ref, yb_ref, yc_ref, g_ref, x_ref, ga_ref, wa_ref, wb_ref, wc_ref, wo_ref, out_ref):
    def branch(y_ref, w_ref, c):
        gate = _sigmoid(g_ref[:, c * D_MODEL:(c + 1) * D_MODEL])
        return gate * jnp.dot(y_ref[...], w_ref[0], preferred_element_type=F32)

    merged = branch(ya_ref, wa_ref, 0) + branch(yb_ref, wb_ref, 1) + branch(yc_ref, wc_ref, 2)
    y = jnp.dot(merged.astype(BF16), wo_ref[0], preferred_element_type=F32)
    out_ref[...] = x_ref[...] + ga_ref[0] * y


def _merge(ya, yb, yc, gates, x2, S, layer, ga, wa, wb, wc, wo, tm=512):
    N, D = x2.shape
    per_b = S // tm
    row = lambda i: (i, 0)
    wsel = lambda i: (layer, 0, 0)
    return pl.pallas_call(
        _merge_kernel,
        grid=(N // tm,),
        in_specs=[pl.BlockSpec((tm, 512), row), pl.BlockSpec((tm, 512), row), pl.BlockSpec((tm, 512), row),
                  pl.BlockSpec((tm, N_BRANCH * D), row), pl.BlockSpec((tm, D), row),
                  pl.BlockSpec((1, 1, D), lambda i: (i // per_b, 0, 0)),
                  _const_spec((1, 512, D), wsel), _const_spec((1, 512, D), wsel), _const_spec((1, 512, D), wsel),
                  _const_spec((1, D, D), wsel)],
        out_specs=pl.BlockSpec((tm, D), row),
        out_shape=jax.ShapeDtypeStruct((N, D), F32),
        compiler_params=_cparams(("parallel",), 48),
        name="merge",
    )(ya, yb, yc, gates, x2, ga, wa, wb, wc, wo)


def _ffn_kernel(x_ref, gain_ref, sc_ref, sh_ref, gf_ref, wr_ref, br_ref, wg_ref, wu_ref, wd_ref, out_ref):
    x = x_ref[...]
    tm = x.shape[0]
    ms = jnp.mean(x * x, axis=-1, keepdims=True)
    hf = x * lax.rsqrt(ms + EPS) * gain_ref[0] * (1.0 + sc_ref[0]) + sh_ref[0]

    logits = jnp.dot(hf, wr_ref[0], preferred_element_type=F32, precision=HIGHEST) + br_ref[0]
    lane_i = lax.broadcasted_iota(jnp.int32, (tm, LANES), 1)
    lane = lane_i.astype(F32)
    big = float(LANES)

    def first_argmax(z):
        zmax = jnp.max(z, axis=1, keepdims=True)
        return zmax, jnp.min(jnp.where(z == zmax, lane, big), axis=1, keepdims=True)

    lg = jnp.where(lane_i < N_GROUPS, logits, NEG_INF)
    gmax, g_idx = first_argmax(lg)
    p_sel = 1.0 / jnp.sum(jnp.exp(lg - gmax), axis=1, keepdims=True)
    lane_grp = ((lane_i - N_GROUPS) // EXPERTS_PER_GROUP).astype(F32)
    in_grp = (lane_i >= N_GROUPS) & (lane_i < N_GROUPS + N_EXPERTS) & (lane_grp == g_idx)
    le = jnp.where(in_grp, logits, NEG_INF)
    v1, i1 = first_argmax(le)
    le2 = jnp.where(lane == i1, NEG_INF, le)
    v2, i2 = first_argmax(le2)
    e2 = jnp.exp(v2 - v1)
    den = 1.0 + e2
    comb = jnp.where(lane == i1, (1.0 / den) * p_sel, 0.0) + jnp.where(lane == i2, (e2 / den) * p_sel, 0.0)

    hb = hf.astype(BF16)
    acc = jnp.zeros((tm, D_MODEL), F32)
    for e in range(N_EXPERTS):
        gate = jnp.dot(hb, wg_ref[0, e], preferred_element_type=F32)
        up = jnp.dot(hb, wu_ref[0, e], preferred_element_type=F32)
        hid = gate * _sigmoid(gate) * up * comb[:, N_GROUPS + e:N_GROUPS + e + 1]
        acc = acc + jnp.dot(hid.astype(BF16), wd_ref[0, e], preferred_element_type=F32)
    out_ref[...] = x + gf_ref[0] * acc


def _ffn(x2, S, layer, gain, sc, sh, gf, wr, br, wg, wu, wd, tm=512):
    N, D = x2.shape
    per_b = S // tm
    row = lambda i: (i, 0)
    bsel = lambda i: (i // per_b, 0, 0)
    wsel = lambda i: (layer, 0, 0, 0)
    return pl.pallas_call(
        _ffn_kernel,
        grid=(N // tm,),
        in_specs=[pl.BlockSpec((tm, D), row),
                  _const_spec((1, 1, D), lambda i: (layer, 0, 0)),
                  pl.BlockSpec((1, 1, D), bsel), pl.BlockSpec((1, 1, D), bsel), pl.BlockSpec((1, 1, D), bsel),
                  _const_spec((1, D, LANES), lambda i: (layer, 0, 0)),
                  _const_spec((1, 1, LANES), lambda i: (layer, 0, 0)),
                  _const_spec((1, N_EXPERTS, D, EXPERT_FF), wsel),
                  _const_spec((1, N_EXPERTS, D, EXPERT_FF), wsel),
                  _const_spec((1, N_EXPERTS, EXPERT_FF, D), wsel)],
        out_specs=pl.BlockSpec((tm, D), row),
        out_shape=jax.ShapeDtypeStruct((N, D), F32),
        compiler_params=_cparams(("parallel",), 56),
        name="ffn",
    )(x2, gain, sc, sh, gf, wr, br, wg, wu, wd)


def _pad_last(a, n):
    return jnp.pad(a, [(0, 0)] * (a.ndim - 1) + [(0, n - a.shape[-1])])


def _head_tiles(a, heads, dim):
    a = a.reshape(a.shape[:-1] + (heads, dim))
    return _pad_last(a, LANES).reshape(a.shape[:-2] + (heads * LANES,))


def _prep_w_in(w_in):
    pts = np.cumsum(IN_SIZES)[:-1].tolist()
    cq, ckv, kr, m_qk, m_v, m_o, m_i, m_f, d_q, d_k, d_v, gates = jnp.split(w_in, pts, axis=-1)
    seg_a = _pad_last(jnp.concatenate([cq, ckv, kr], axis=-1), 512)
    seg_if = _pad_last(jnp.concatenate([m_i, m_f], axis=-1), LANES)
    w = jnp.concatenate([seg_a, m_qk, m_v, m_o, seg_if,
                         _head_tiles(d_q, DIL_HEADS, DIL_HEAD_DIM), _head_tiles(d_k, DIL_HEADS, DIL_HEAD_DIM),
                         _head_tiles(d_v, DIL_HEADS, DIL_HEAD_DIM), gates], axis=-1)
    assert w.shape[-1] == COL_END
    return w.astype(BF16)


def kernel(x, c, w_ada, b_ada, attn_norm, w_in, mla_q_norm, mla_kv_norm, mla_w_uq, mla_w_ukv, mla_q_gain, mla_k_gain, mlstm_conv_w, mlstm_conv_b, mlstm_b_i, mlstm_b_f, mlstm_head_gain, dil_q_gain, dil_k_gain, w_branch_a, w_branch_b, w_branch_c, w_out, ffn_norm, w_router_group, b_router_group, w_router_expert, b_router_expert, w_exp_gate, w_exp_up, w_exp_down):
    B, S, D = x.shape
    L = w_ada.shape[0]
    N = B * S
    assert D == D_MODEL and S % DIL_MAX_WINDOW == 0

    mod = _adaln(c, w_ada, b_ada)

    w_pad = _prep_w_in(w_in)
    wuq = _head_tiles(mla_w_uq, MLA_HEADS, MLA_QK).astype(BF16)
    ukv = mla_w_ukv.reshape(L, MLA_KV_LORA, MLA_HEADS, MLA_NOPE + MLA_V)
    wkv = jnp.concatenate([_pad_last(ukv[..., :MLA_NOPE], LANES).reshape(L, MLA_KV_LORA, MLA_HEADS * LANES),
                           _pad_last(ukv[..., MLA_NOPE:], LANES).reshape(L, MLA_KV_LORA, MLA_HEADS * LANES)],
                          axis=-1).astype(BF16)
    rows = lambda a: a[:, None, :]
    qg = rows(_pad_last(mla_q_gain, LANES))
    kg = rows(_pad_last(mla_k_gain, LANES))
    dqg = rows(_pad_last(dil_q_gain, LANES))
    dkg = rows(_pad_last(dil_k_gain, LANES))
    gate_b = rows(_pad_last(jnp.concatenate([mlstm_b_i, mlstm_b_f], axis=-1), LANES))
    tables = _rope_tables(S)
    wa, wb, wc, wo = (w.astype(BF16) for w in (w_branch_a, w_branch_b, w_branch_c, w_out))
    wr = _pad_last(jnp.concatenate([w_router_group, w_router_expert], axis=-1), LANES)
    br = rows(_pad_last(jnp.concatenate([b_router_group, b_router_expert], axis=-1), LANES))
    wg, wu, wd = (w.astype(BF16) for w in (w_exp_gate, w_exp_up, w_exp_down))
    attn_norm, ffn_norm, mla_q_norm, mla_kv_norm, mlstm_conv_b, mlstm_head_gain = (
        rows(a) for a in (attn_norm, ffn_norm, mla_q_norm, mla_kv_norm, mlstm_conv_b, mlstm_head_gain))

    x2 = x.reshape(N, D)
    for l in range(L):
        sh_a, sc_a, g_a, sh_f, sc_f, g_f = (mod[l, :, i] for i in range(6))
        oa, mqk, mv, mo, mif, qd, kd, vd, gates = _inproj(x2, S, l, attn_norm, sc_a, sh_a, w_pad, dqg, dkg)
        q, k, vt = _mla_prep(oa, S, l, mla_q_norm, mla_kv_norm, wuq, wkv, qg, kg, tables)
        ya = _flash(q, k, vt, band=False)
        yb = _mlstm(mqk, mv, mo, mif, B, S, l, mlstm_conv_w, mlstm_conv_b, gate_b, mlstm_head_gain)
        yc = _flash(qd, kd, vd, band=True)
        x2 = _merge(ya.reshape(N, -1), yb, yc.reshape(N, -1), gates, x2, S, l, g_a, wa, wb, wc, wo)
        x2 = _ffn(x2, S, l, ffn_norm, sc_f, sh_f, g_f, wr, br, wg, wu, wd)
    return x2.reshape(B, S, D)
```

```python
import functools

import jax
import jax.numpy as jnp
import numpy as np
from jax import lax
from jax.experimental import pallas as pl
from jax.experimental.pallas import tpu as pltpu

F32 = jnp.float32
BF16 = jnp.bfloat16
HIGHEST = lax.Precision.HIGHEST
NEG_INF = float("-inf")

EPS = 1e-6
LANES = 128
VMEM_BYTES_V7X = 64 * 1024 * 1024

D_MODEL = 1024
MLA_HEADS = 8
MLA_Q_LORA = 256
MLA_KV_LORA = 128
MLA_NOPE = 64
MLA_ROPE = 32
MLA_V = 64
MLA_QK = MLA_NOPE + MLA_ROPE
ROPE_THETA = 10000.0
MLSTM_HEADS = 4
MLSTM_QK = 64
MLSTM_V = 128
CONV_K = 4
DIL_HEADS = 8
DIL_HEAD_DIM = 64
DIL_PATTERNS = ((128, 1), (512, 4), (2048, 16))
DIL_WIDTH = DIL_HEADS * DIL_HEAD_DIM
N_GROUPS = 4
EXPERTS_PER_GROUP = 4
N_EXPERTS = N_GROUPS * EXPERTS_PER_GROUP
EXPERT_FF = 256
N_BRANCH = 3
IN_SIZES = (MLA_Q_LORA, MLA_KV_LORA, MLA_ROPE,
            2 * MLSTM_HEADS * MLSTM_QK, MLSTM_HEADS * MLSTM_V, MLSTM_HEADS * MLSTM_V, MLSTM_HEADS, MLSTM_HEADS,
            DIL_WIDTH, DIL_WIDTH, DIL_WIDTH, N_BRANCH * D_MODEL)

COL_A = 0
COL_MQK = 512
COL_MV = 1024
COL_MO = 1536
COL_MIF = 2048
COL_DQ = 2176
COL_DK = 3200
COL_DV = 4224
COL_G = 4736
COL_END = 7808

FLASH_TILE = 1024
DIL_MAX_WINDOW = max(w for w, _ in DIL_PATTERNS)
V_DIM = 64
V_ROWS = 80
LOG2E = 1.4426950408889634
assert MLA_V == V_DIM and DIL_HEAD_DIM == V_DIM


def _store_v_tiles_t(v, vt_ref):
    tm = v.shape[0]
    v_t = v.astype(BF16).T
    fill = (lax.broadcasted_iota(jnp.int32, (V_ROWS - V_DIM, tm), 0) == 0).astype(BF16)
    for h in range(v.shape[1] // V_DIM):
        vt_ref[0, h, 0:V_DIM, :] = v_t[h * V_DIM:(h + 1) * V_DIM, :]
        vt_ref[0, h, V_DIM:V_ROWS, :] = fill


def _cparams(sem, vmem_mb):
    return pltpu.CompilerParams(dimension_semantics=sem, vmem_limit_bytes=vmem_mb * 1024 * 1024)


def _sigmoid(x):
    return 1.0 / (1.0 + jnp.exp(-x))


def _const_spec(shape, index_map):
    return pl.BlockSpec(shape, index_map, pipeline_mode=pl.Buffered(1))


def _adaln_kernel(c_ref, w_ref, b_ref, o_ref):
    c = c_ref[...]
    ca = c * _sigmoid(c)
    o_ref[0] = jnp.dot(ca, w_ref[0], preferred_element_type=F32, precision=HIGHEST) + b_ref[0]


def _adaln(c, w_ada, b_ada):
    L, D, D6 = w_ada.shape
    B = c.shape[0]
    rows = 8
    cp = jnp.zeros((rows, D), F32).at[:B].set(c)
    out = pl.pallas_call(
        _adaln_kernel,
        grid=(L, D6 // D),
        in_specs=[pl.BlockSpec((rows, D), lambda l, j: (0, 0)),
                  pl.BlockSpec((1, D, D), lambda l, j: (l, 0, j)),
                  pl.BlockSpec((1, 1, D), lambda l, j: (l, 0, j))],
        out_specs=pl.BlockSpec((1, rows, D), lambda l, j: (l, 0, j)),
        out_shape=jax.ShapeDtypeStruct((L, rows, D6), F32),
        compiler_params=_cparams(("parallel", "parallel"), 32),
        name="adaln",
    )(cp, w_ada, b_ada.reshape(L, 1, D6))
    return out[:, :B].reshape(L, B, D6 // D, 1, D)


def _inproj_kernel(x_ref, gain_ref, sc_ref, sh_ref, w_ref, dqg_ref, dkg_ref,
                   oa_ref, mqk_ref, mv_ref, mo_ref, mif_ref, qd_ref, kd_ref, vd_ref, g_ref):
    x = x_ref[...]
    ms = jnp.mean(x * x, axis=-1, keepdims=True)
    y = x * lax.rsqrt(ms + EPS) * gain_ref[0]
    u = (y * (1.0 + sc_ref[0]) + sh_ref[0]).astype(BF16)

    def proj(a, b):
        return jnp.dot(u, w_ref[0, :, a:b], preferred_element_type=F32)

    oa_ref[...] = proj(COL_A, COL_MQK)
    mqk_ref[...] = proj(COL_MQK, COL_MV)
    mv_ref[...] = proj(COL_MV, COL_MO).astype(BF16)
    mo_ref[...] = proj(COL_MO, COL_MIF).astype(BF16)
    mif_ref[...] = proj(COL_MIF, COL_DQ)

    def head_norm(z, g):
        for h in range(DIL_HEADS):
            zh = z[:, h * LANES:(h + 1) * LANES]
            r = lax.rsqrt(jnp.sum(zh * zh, axis=-1, keepdims=True) / DIL_HEAD_DIM + EPS)
            yield h, zh * r * g

    dq = proj(COL_DQ, COL_DK)
    for h, zn in head_norm(dq, dqg_ref[0]):
        qd_ref[0, h // 2, :, (h % 2) * LANES:(h % 2 + 1) * LANES] = (zn * (DIL_HEAD_DIM ** -0.5 * LOG2E)).astype(BF16)
    dk = proj(COL_DK, COL_DV)
    for h, zn in head_norm(dk, dkg_ref[0]):
        kd_ref[0, h // 2, :, (h % 2) * LANES:(h % 2 + 1) * LANES] = zn.astype(BF16)
    _store_v_tiles_t(proj(COL_DV, COL_G), vd_ref)
    for c in range(N_BRANCH):
        g_ref[:, c * D_MODEL:(c + 1) * D_MODEL] = proj(COL_G + c * D_MODEL, COL_G + (c + 1) * D_MODEL).astype(BF16)


def _inproj(x2, S, layer, gain, sc, sh, w_pad, dqg, dkg, tm=256):
    N, D = x2.shape
    per_b = S // tm
    row = lambda i: (i, 0)
    bsel = lambda i: (i // per_b, 0, 0)
    B = N // S
    outs = [(512, F32), (512, F32), (512, BF16), (512, BF16), (LANES, F32),
            "qk", "qk", "vt", (3072, BF16)]
    head_major = lambda i: (i // per_b, 0, i % per_b, 0)
    special_spec = {"qk": pl.BlockSpec((1, DIL_HEADS // 2, tm, 2 * LANES), head_major),
                    "vt": pl.BlockSpec((1, DIL_HEADS, V_ROWS, tm), lambda i: (i // per_b, 0, 0, i % per_b))}
    special_shape = {"qk": jax.ShapeDtypeStruct((B, DIL_HEADS // 2, S, 2 * LANES), BF16),
                     "vt": jax.ShapeDtypeStruct((B, DIL_HEADS, V_ROWS, S), BF16)}
    return pl.pallas_call(
        _inproj_kernel,
        grid=(N // tm,),
        in_specs=[pl.BlockSpec((tm, D), row),
                  _const_spec((1, 1, D), lambda i: (layer, 0, 0)),
                  pl.BlockSpec((1, 1, D), bsel),
                  pl.BlockSpec((1, 1, D), bsel),
                  _const_spec((1, D, COL_END), lambda i: (layer, 0, 0)),
                  _const_spec((1, 1, LANES), lambda i: (layer, 0, 0)),
                  _const_spec((1, 1, LANES), lambda i: (layer, 0, 0))],
        out_specs=[special_spec[o] if isinstance(o, str) else pl.BlockSpec((tm, o[0]), row) for o in outs],
        out_shape=[special_shape[o] if isinstance(o, str) else jax.ShapeDtypeStruct((N, o[0]), o[1]) for o in outs],
        compiler_params=_cparams(("parallel",), 52),
        name="inproj",
    )(x2, gain, sc, sh, w_pad, dqg, dkg)


def _mla_prep_kernel(a_ref, qn_ref, kvn_ref, wuq_ref, wkv_ref, qg_ref, kg_ref, cos_ref, sa_ref, sb_ref,
                     q_ref, k_ref, vt_ref):
    HW = MLA_HEADS * LANES
    a = a_ref[...]
    cq = a[:, :MLA_Q_LORA]
    ckv = a[:, MLA_Q_LORA:MLA_Q_LORA + MLA_KV_LORA]
    kr_tile = a[:, MLA_Q_LORA + MLA_KV_LORA:]

    def rms(z, g):
        return z * lax.rsqrt(jnp.mean(z * z, axis=-1, keepdims=True) + EPS) * g

    cqn = rms(cq, qn_ref[0]).astype(BF16)
    ckvn = rms(ckv, kvn_ref[0]).astype(BF16)
    qr = jnp.dot(cqn, wuq_ref[0], preferred_element_type=F32)
    kvr = jnp.dot(ckvn, wkv_ref[0], preferred_element_type=F32)
    kr_at_rope = pltpu.roll(kr_tile, MLA_NOPE, 1)
    cosf, sa, sb = cos_ref[...], sa_ref[...], sb_ref[...]
    half = MLA_ROPE // 2

    def norm_rope(z, g):
        r = lax.rsqrt(jnp.sum(z * z, axis=-1, keepdims=True) / MLA_QK + EPS)
        zn = z * r * g
        return zn * cosf + pltpu.roll(zn, LANES - half, 1) * sa + pltpu.roll(zn, half, 1) * sb

    _store_v_tiles_t(kvr[:, HW:], vt_ref)
    qg, kg = qg_ref[0], kg_ref[0]
    for h in range(MLA_HEADS):
        sl = slice(h * LANES, (h + 1) * LANES)
        pair_sl = slice((h % 2) * LANES, (h % 2 + 1) * LANES)
        q_ref[0, h // 2, :, pair_sl] = (norm_rope(qr[:, sl], qg) * (MLA_QK ** -0.5 * LOG2E)).astype(BF16)
        k_ref[0, h // 2, :, pair_sl] = norm_rope(kvr[:, sl] + kr_at_rope, kg).astype(BF16)


def _rope_tables(S):
    half = MLA_ROPE // 2
    inv = ROPE_THETA ** (-jnp.arange(half, dtype=F32) / half)
    ang = jnp.arange(S).astype(F32)[:, None] * inv[None, :]
    cos, sin = jnp.cos(ang), jnp.sin(ang)
    one = jnp.ones((S, MLA_NOPE), F32)
    z = lambda n: jnp.zeros((S, n), F32)
    tail = LANES - MLA_QK
    cosf = jnp.concatenate([one, cos, cos, z(tail)], axis=1)
    sa = jnp.concatenate([z(MLA_NOPE), -sin, z(half), z(tail)], axis=1)
    sb = jnp.concatenate([z(MLA_NOPE), z(half), sin, z(tail)], axis=1)
    return cosf, sa, sb


def _mla_prep(oa, S, layer, qn, kvn, wuq, wkv, qg, kg, tables, tm=512):
    N = oa.shape[0]
    B = N // S
    per_b = S // tm
    row = lambda i: (i, 0)
    pos = lambda i: (i % per_b, 0)
    HW = MLA_HEADS * LANES
    return pl.pallas_call(
        _mla_prep_kernel,
        grid=(N // tm,),
        in_specs=[pl.BlockSpec((tm, 512), row),
                  _const_spec((1, 1, MLA_Q_LORA), lambda i: (layer, 0, 0)),
                  _const_spec((1, 1, MLA_KV_LORA), lambda i: (layer, 0, 0)),
                  _const_spec((1, MLA_Q_LORA, HW), lambda i: (layer, 0, 0)),
                  _const_spec((1, MLA_KV_LORA, HW + MLA_HEADS * MLA_V), lambda i: (layer, 0, 0)),
                  _const_spec((1, 1, LANES), lambda i: (layer, 0, 0)),
                  _const_spec((1, 1, LANES), lambda i: (layer, 0, 0)),
                  pl.BlockSpec((tm, LANES), pos),
                  pl.BlockSpec((tm, LANES), pos),
                  pl.BlockSpec((tm, LANES), pos)],
        out_specs=[pl.BlockSpec((1, MLA_HEADS // 2, tm, 2 * LANES), lambda i: (i // per_b, 0, i % per_b, 0)),
                   pl.BlockSpec((1, MLA_HEADS // 2, tm, 2 * LANES), lambda i: (i // per_b, 0, i % per_b, 0)),
                   pl.BlockSpec((1, MLA_HEADS, V_ROWS, tm), lambda i: (i // per_b, 0, 0, i % per_b))],
        out_shape=[jax.ShapeDtypeStruct((B, MLA_HEADS // 2, S, 2 * LANES), BF16),
                   jax.ShapeDtypeStruct((B, MLA_HEADS // 2, S, 2 * LANES), BF16),
                   jax.ShapeDtypeStruct((B, MLA_HEADS, V_ROWS, S), BF16)],
        compiler_params=_cparams(("parallel",), 32),
        name="mla_prep",
    )(oa, qn, kvn, wuq, wkv, qg, kg, *tables)


def _flash_kernel(qi_ref, kj_ref, qn_ref, kn_ref, q0_ref, k0_ref, vt_ref, *rest, band, T, W):
    if band:
        bias_ref, o_ref, m_ref, acc_ref, s_ref = rest
    else:
        o_ref, m_ref, acc_ref, s_ref = rest
    p_id = pl.program_id(2)
    qi = qi_ref[p_id]
    kj = kj_ref[p_id]
    if band:
        first = kj == qi
        last = kj == jnp.maximum(qi - W, 0)
    else:
        first = kj == 0
        last = kj == qi

    def scores(q_ref, k_ref, hh):
        q = q_ref[0, 0, :, hh * LANES:(hh + 1) * LANES]
        k = k_ref[0, 0, :, hh * LANES:(hh + 1) * LANES]
        return lax.dot_general(k, q, (((1,), (1,)), ((), ())), preferred_element_type=F32)

    @pl.when(p_id == 0)
    def _():
        for hh in range(2):
            s_ref[0, hh] = scores(q0_ref, k0_ref, hh)

    @pl.when(first)
    def _():
        m_ref[...] = jnp.full(m_ref.shape, NEG_INF, F32)
        acc_ref[...] = jnp.zeros(acc_ref.shape, F32)

    def step(masked, slot):
        if masked and not band:
            valid = lax.broadcasted_iota(jnp.int32, (T, T), 0) <= lax.broadcasted_iota(jnp.int32, (T, T), 1)
        for hh in range(2):
            s_ref[1 - slot, hh] = scores(qn_ref, kn_ref, hh)
            st = s_ref[slot, hh]
            if band:
                st = st + bias_ref[qi - kj]
            elif masked:
                st = jnp.where(valid, st, NEG_INF)
            m_prev = m_ref[hh]
            m_new = jnp.maximum(m_prev, jnp.max(st, axis=0, keepdims=True))
            alpha = jnp.exp2(m_prev - m_new)
            pt = jnp.exp2(st - m_new).astype(BF16)
            acc_ref[hh] = acc_ref[hh] * alpha + jnp.dot(vt_ref[0, hh], pt, preferred_element_type=F32)
            m_ref[hh] = m_new

    parity = p_id % 2
    for slot in range(2):
        here = parity == slot
        if band:
            pl.when(here)(functools.partial(step, True, slot))
        else:
            pl.when(jnp.logical_and(here, last))(functools.partial(step, True, slot))
            pl.when(jnp.logical_and(here, jnp.logical_not(last)))(functools.partial(step, False, slot))

    @pl.when(last)
    def _():
        outs = []
        for hh in range(2):
            a = acc_ref[hh]
            outs.append(a[0:V_DIM, :] / a[V_DIM:V_DIM + 1, :])
        o_ref[0] = jnp.concatenate(outs, axis=0).T.astype(o_ref.dtype)


def _band_bias(T, W):
    d = (jnp.arange(W + 1)[:, None, None] * T + jnp.arange(T)[None, None, :] - jnp.arange(T)[None, :, None])
    cw = jnp.zeros(d.shape, F32)
    for window, dil in DIL_PATTERNS:
        cw = cw + ((d >= 0) & (d <= window) & (d % dil == 0)).astype(F32)
    return jnp.log2(cw)


def _flash(q, k, vt, band, T=FLASH_TILE):
    B, HP, S, _ = q.shape
    H = 2 * HP
    nq = S // T
    W = DIL_MAX_WINDOW // T if band else 0
    if band:
        pairs = [(i, i - j) for i in range(nq) for j in range(min(W, i) + 1)]
    else:
        pairs = [(i, j) for i in range(nq) for j in range(i + 1)]
    n = len(pairs)
    qi = jnp.asarray(np.array([p[0] for p in pairs], np.int32))
    kj = jnp.asarray(np.array([p[1] for p in pairs], np.int32))
    nxt = lambda p: jnp.minimum(p + 1, n - 1)
    in_specs = [pl.BlockSpec((1, 1, T, 2 * LANES), lambda b, h, p, qi, kj: (b, h, qi[nxt(p)], 0)),
                pl.BlockSpec((1, 1, T, 2 * LANES), lambda b, h, p, qi, kj: (b, h, kj[nxt(p)], 0)),
                pl.BlockSpec((1, 1, T, 2 * LANES), lambda b, h, p, qi, kj: (b, h, qi[0], 0)),
                pl.BlockSpec((1, 1, T, 2 * LANES), lambda b, h, p, qi, kj: (b, h, kj[0], 0)),
                pl.BlockSpec((1, 2, V_ROWS, T), lambda b, h, p, qi, kj: (b, h, 0, kj[p]))]
    args = [q, k, q, k, vt]
    if band:
        in_specs.append(_const_spec((W + 1, T, T), lambda b, h, p, qi, kj: (0, 0, 0)))
        args.append(_band_bias(T, W))
    return pl.pallas_call(
        functools.partial(_flash_kernel, band=band, T=T, W=W),
        grid_spec=pltpu.PrefetchScalarGridSpec(
            num_scalar_prefetch=2,
            grid=(B, H // 2, n),
            in_specs=in_specs,
            out_specs=pl.BlockSpec((1, T, LANES), lambda b, h, p, qi, kj: (b, qi[p], h)),
            scratch_shapes=[pltpu.VMEM((2, 1, T), F32), pltpu.VMEM((2, V_ROWS, T), F32),
                            pltpu.VMEM((2, 2, T, T), F32)]),
        out_shape=jax.ShapeDtypeStruct((B, S, H * V_DIM), BF16),
        compiler_params=_cparams(("arbitrary", "arbitrary", "arbitrary"), 52),
        name="flash_band" if band else "flash_causal",
    )(qi, kj, *args)


def _mlstm_kernel(qk_ref, v_ref, o_ref, if_ref, cw_ref, cb_ref, gb_ref, hg_ref, y_ref,
                  xs_ref, c_ref, n_ref, m_ref, *, Lc):
    H, DK, DV = MLSTM_HEADS, MLSTM_QK, MLSTM_V
    HALO = 8

    @pl.when(pl.program_id(1) == 0)
    def _():
        xs_ref[0:HALO, :] = jnp.zeros((HALO, 2 * H * DK), F32)
        c_ref[...] = jnp.zeros(c_ref.shape, F32)
        n_ref[...] = jnp.zeros(n_ref.shape, F32)
        m_ref[...] = jnp.zeros(m_ref.shape, F32)

    x = qk_ref[...]
    xs_ref[HALO:HALO + Lc, :] = x
    y = cb_ref[0]
    for j in range(CONV_K):
        y = y + cw_ref[0, j:j + 1, :] * xs_ref[pl.ds(HALO - (CONV_K - 1) + j, Lc), :]
    xs_ref[0:HALO, :] = x[Lc - HALO:, :]
    qk = y * _sigmoid(y)

    g = if_ref[...] + gb_ref[0]
    logf = jnp.minimum(g, 0.0) - jnp.log(1.0 + jnp.exp(-jnp.abs(g)))
    row = lax.broadcasted_iota(jnp.int32, (Lc, Lc), 0)
    col = lax.broadcasted_iota(jnp.int32, (Lc, Lc), 1)
    causal = col <= row
    tri = jnp.where(causal, 1.0, 0.0).astype(F32)
    bcum = jnp.dot(tri, logf, preferred_element_type=F32, precision=HIGHEST)
    g_t = g.T
    bcum_t = bcum.T

    for h in range(H):
        q = qk[:, h * DK:(h + 1) * DK]
        k = qk[:, (H + h) * DK:(H + h + 1) * DK] * (DK ** -0.5)
        v = v_ref[:, h * DV:(h + 1) * DV]
        bt = bcum[:, H + h:H + h + 1]
        ig_c = g[:, h:h + 1]
        bs = bcum_t[H + h:H + h + 1, :]
        ig_r = g_t[h:h + 1, :]
        m_prev = m_ref[h:h + 1, 0:1]
        dlog = jnp.where(causal, bt - bs + ig_r, NEG_INF)
        inter = bt + m_prev
        m_t = jnp.maximum(inter, jnp.max(dlog, axis=1, keepdims=True))
        qb = q.astype(BF16)
        s = lax.dot_general(qb, k.astype(BF16), (((1,), (1,)), ((), ())), preferred_element_type=F32)
        sc = s * jnp.exp(dlog - m_t)
        decay = jnp.exp(inter - m_t)
        c_h = c_ref[h]
        n_h = n_ref[h:h + 1, :]
        num = (decay * jnp.dot(qb, c_h.astype(BF16), preferred_element_type=F32)
               + jnp.dot(sc.astype(BF16), v, preferred_element_type=F32))
        den = decay * jnp.sum(q * n_h, axis=1, keepdims=True) + jnp.sum(sc, axis=1, keepdims=True)
        hh = num / jnp.maximum(jnp.abs(den), jnp.exp(-m_t))
        hn = hh * lax.rsqrt(jnp.mean(hh * hh, axis=-1, keepdims=True) + EPS) * hg_ref[0, :, h * DV:(h + 1) * DV]
        o_gate = _sigmoid(o_ref[:, h * DV:(h + 1) * DV].astype(F32))
        y_ref[:, h * DV:(h + 1) * DV] = (hn * o_gate).astype(y_ref.dtype)

        b_end = bt[Lc - 1:Lc, :]
        wlog = b_end - bt + ig_c
        m_new = jnp.maximum(b_end + m_prev, jnp.max(wlog, axis=0, keepdims=True))
        cd = jnp.exp(b_end + m_prev - m_new)
        kw = k * jnp.exp(wlog - m_new)
        c_ref[h] = cd * c_h + lax.dot_general(kw.astype(BF16), v, (((0,), (0,)), ((), ())),
                                              preferred_element_type=F32)
        n_ref[h:h + 1, :] = cd * n_h + jnp.sum(kw, axis=0, keepdims=True)
        m_ref[h:h + 1, :] = jnp.broadcast_to(m_new, (1, LANES))


def _mlstm(mqk, mv, mo, mif, B, S, layer, conv_w, conv_b, gate_b, head_gain, Lc=256):
    H, DK, DV = MLSTM_HEADS, MLSTM_QK, MLSTM_V
    nc = S // Lc
    row = lambda b, c: (b * nc + c, 0)
    return pl.pallas_call(
        functools.partial(_mlstm_kernel, Lc=Lc),
        grid=(B, nc),
        in_specs=[pl.BlockSpec((Lc, 2 * H * DK), row),
                  pl.BlockSpec((Lc, H * DV), row),
                  pl.BlockSpec((Lc, H * DV), row),
                  pl.BlockSpec((Lc, LANES), row),
                  _const_spec((1, CONV_K, 2 * H * DK), lambda b, c: (layer, 0, 0)),
                  _const_spec((1, 1, 2 * H * DK), lambda b, c: (layer, 0, 0)),
                  _const_spec((1, 1, LANES), lambda b, c: (layer, 0, 0)),
                  _const_spec((1, 1, H * DV), lambda b, c: (layer, 0, 0))],
        out_specs=pl.BlockSpec((Lc, H * DV), row),
        out_shape=jax.ShapeDtypeStruct((B * S, H * DV), BF16),
        scratch_shapes=[pltpu.VMEM((Lc + 8, 2 * H * DK), F32), pltpu.VMEM((H, DK, DV), F32),
                        pltpu.VMEM((8, DK), F32), pltpu.VMEM((8, LANES), F32)],
        compiler_params=_cparams(("arbitrary", "arbitrary"), 32),
        name="mlstm",
    )(mqk, mv, mo, mif, conv_w, conv_b, gate_b, head_gain)


def _merge_kernel(ya_ref, yb_ref, yc_ref, g_ref, x_ref, ga_ref, wa_ref, wb_ref, wc_ref, wo_ref, out_ref):
    def branch(y_ref, w_ref, c):
        gate = _sigmoid(g_ref[:, c * D_MODEL:(c + 1) * D_MODEL].astype(F32))
        return gate * jnp.dot(y_ref[...], w_ref[0], preferred_element_type=F32)

    merged = branch(ya_ref, wa_ref, 0) + branch(yb_ref, wb_ref, 1) + branch(yc_ref, wc_ref, 2)
    y = jnp.dot(merged.astype(BF16), wo_ref[0], preferred_element_type=F32)
    out_ref[...] = x_ref[...] + ga_ref[0] * y


def _merge(ya, yb, yc, gates, x2, S, layer, ga, wa, wb, wc, wo, tm=512):
    N, D = x2.shape
    per_b = S // tm
    row = lambda i: (i, 0)
    wsel = lambda i: (layer, 0, 0)
    return pl.pallas_call(
        _merge_kernel,
        grid=(N // tm,),
        in_specs=[pl.BlockSpec((tm, 512), row), pl.BlockSpec((tm, 512), row), pl.BlockSpec((tm, 512), row),
                  pl.BlockSpec((tm, N_BRANCH * D), row), pl.BlockSpec((tm, D), row),
                  pl.BlockSpec((1, 1, D), lambda i: (i // per_b, 0, 0)),
                  _const_spec((1, 512, D), wsel), _const_spec((1, 512, D), wsel), _const_spec((1, 512, D), wsel),
                  _const_spec((1, D, D), wsel)],
        out_specs=pl.BlockSpec((tm, D), row),
        out_shape=jax.ShapeDtypeStruct((N, D), F32),
        compiler_params=_cparams(("parallel",), 48),
        name="merge",
    )(ya, yb, yc, gates, x2, ga, wa, wb, wc, wo)


def _ffn_kernel(x_ref, gain_ref, sc_ref, sh_ref, gf_ref, wr_ref, br_ref, wg_ref, wu_ref, wd_ref, out_ref):
    x = x_ref[...]
    tm = x.shape[0]
    ms = jnp.mean(x * x, axis=-1, keepdims=True)
    hf = x * lax.rsqrt(ms + EPS) * gain_ref[0] * (1.0 + sc_ref[0]) + sh_ref[0]

    logits = jnp.dot(hf, wr_ref[0], preferred_element_type=F32, precision=HIGHEST) + br_ref[0]
    lane_i = lax.broadcasted_iota(jnp.int32, (tm, LANES), 1)
    lane = lane_i.astype(F32)
    big = float(LANES)

    def first_argmax(z):
        zmax = jnp.max(z, axis=1, keepdims=True)
        return zmax, jnp.min(jnp.where(z == zmax, lane, big), axis=1, keepdims=True)

    lg = jnp.where(lane_i < N_GROUPS, logits, NEG_INF)
    gmax, g_idx = first_argmax(lg)
    p_sel = 1.0 / jnp.sum(jnp.exp(lg - gmax), axis=1, keepdims=True)
    lane_grp = ((lane_i - N_GROUPS) // EXPERTS_PER_GROUP).astype(F32)
    in_grp = (lane_i >= N_GROUPS) & (lane_i < N_GROUPS + N_EXPERTS) & (lane_grp == g_idx)
    le = jnp.where(in_grp, logits, NEG_INF)
    v1, i1 = first_argmax(le)
    le2 = jnp.where(lane == i1, NEG_INF, le)
    v2, i2 = first_argmax(le2)
    e2 = jnp.exp(v2 - v1)
    den = 1.0 + e2
    comb = jnp.where(lane == i1, (1.0 / den) * p_sel, 0.0) + jnp.where(lane == i2, (e2 / den) * p_sel, 0.0)

    hb = hf.astype(BF16)
    acc = jnp.zeros((tm, D_MODEL), F32)
    for e in range(N_EXPERTS):
        gate = jnp.dot(hb, wg_ref[0, e], preferred_element_type=F32)
        up = jnp.dot(hb, wu_ref[0, e], preferred_element_type=F32)
        hid = gate * _sigmoid(gate) * up * comb[:, N_GROUPS + e:N_GROUPS + e + 1]
        acc = acc + jnp.dot(hid.astype(BF16), wd_ref[0, e], preferred_element_type=F32)
    out_ref[...] = x + gf_ref[0] * acc


def _ffn(x2, S, layer, gain, sc, sh, gf, wr, br, wg, wu, wd, tm=512):
    N, D = x2.shape
    per_b = S // tm
    row = lambda i: (i, 0)
    bsel = lambda i: (i // per_b, 0, 0)
    wsel = lambda i: (layer, 0, 0, 0)
    return pl.pallas_call(
        _ffn_kernel,
        grid=(N // tm,),
        in_specs=[pl.BlockSpec((tm, D), row),
                  _const_spec((1, 1, D), lambda i: (layer, 0, 0)),
                  pl.BlockSpec((1, 1, D), bsel), pl.BlockSpec((1, 1, D), bsel), pl.BlockSpec((1, 1, D), bsel),
                  _const_spec((1, D, LANES), lambda i: (layer, 0, 0)),
                  _const_spec((1, 1, LANES), lambda i: (layer, 0, 0)),
                  _const_spec((1, N_EXPERTS, D, EXPERT_FF), wsel),
                  _const_spec((1, N_EXPERTS, D, EXPERT_FF), wsel),
                  _const_spec((1, N_EXPERTS, EXPERT_FF, D), wsel)],
        out_specs=pl.BlockSpec((tm, D), row),
        out_shape=jax.ShapeDtypeStruct((N, D), F32),
        compiler_params=_cparams(("parallel",), 56),
        name="ffn",
    )(x2, gain, sc, sh, gf, wr, br, wg, wu, wd)


def _pad_last(a, n):
    return jnp.pad(a, [(0, 0)] * (a.ndim - 1) + [(0, n - a.shape[-1])])


def _head_tiles(a, heads, dim):
    a = a.reshape(a.shape[:-1] + (heads, dim))
    return _pad_last(a, LANES).reshape(a.shape[:-2] + (heads * LANES,))


def _prep_w_in(w_in):
    pts = np.cumsum(IN_SIZES)[:-1].tolist()
    cq, ckv, kr, m_qk, m_v, m_o, m_i, m_f, d_q, d_k, d_v, gates = jnp.split(w_in, pts, axis=-1)
    seg_a = _pad_last(jnp.concatenate([cq, ckv, kr], axis=-1), 512)
    seg_if = _pad_last(jnp.concatenate([m_i, m_f], axis=-1), LANES)
    w = jnp.concatenate([seg_a, m_qk, m_v, m_o, seg_if,
                         _head_tiles(d_q, DIL_HEADS, DIL_HEAD_DIM), _head_tiles(d_k, DIL_HEADS, DIL_HEAD_DIM),
                         d_v, gates], axis=-1)
    assert w.shape[-1] == COL_END
    return w.astype(BF16)


def kernel(x, c, w_ada, b_ada, attn_norm, w_in, mla_q_norm, mla_kv_norm, mla_w_uq, mla_w_ukv, mla_q_gain, mla_k_gain, mlstm_conv_w, mlstm_conv_b, mlstm_b_i, mlstm_b_f, mlstm_head_gain, dil_q_gain, dil_k_gain, w_branch_a, w_branch_b, w_branch_c, w_out, ffn_norm, w_router_group, b_router_group, w_router_expert, b_router_expert, w_exp_gate, w_exp_up, w_exp_down):
    B, S, D = x.shape
    L = w_ada.shape[0]
    N = B * S
    assert D == D_MODEL and S % DIL_MAX_WINDOW == 0

    mod = _adaln(c, w_ada, b_ada)

    w_pad = _prep_w_in(w_in)
    wuq = _head_tiles(mla_w_uq, MLA_HEADS, MLA_QK).astype(BF16)
    ukv = mla_w_ukv.reshape(L, MLA_KV_LORA, MLA_HEADS, MLA_NOPE + MLA_V)
    wkv = jnp.concatenate([_pad_last(ukv[..., :MLA_NOPE], LANES).reshape(L, MLA_KV_LORA, MLA_HEADS * LANES),
                           ukv[..., MLA_NOPE:].reshape(L, MLA_KV_LORA, MLA_HEADS * MLA_V)], axis=-1).astype(BF16)
    rows = lambda a: a[:, None, :]
    qg = rows(_pad_last(mla_q_gain, LANES))
    kg = rows(_pad_last(mla_k_gain, LANES))
    dqg = rows(_pad_last(dil_q_gain, LANES))
    dkg = rows(_pad_last(dil_k_gain, LANES))
    gate_b = rows(_pad_last(jnp.concatenate([mlstm_b_i, mlstm_b_f], axis=-1), LANES))
    tables = _rope_tables(S)
    wa, wb, wc, wo = (w.astype(BF16) for w in (w_branch_a, w_branch_b, w_branch_c, w_out))
    wr = _pad_last(jnp.concatenate([w_router_group, w_router_expert], axis=-1), LANES)
    br = rows(_pad_last(jnp.concatenate([b_router_group, b_router_expert], axis=-1), LANES))
    wg, wu, wd = (w.astype(BF16) for w in (w_exp_gate, w_exp_up, w_exp_down))
    attn_norm, ffn_norm, mla_q_norm, mla_kv_norm, mlstm_conv_b, mlstm_head_gain = (
        rows(a) for a in (attn_norm, ffn_norm, mla_q_norm, mla_kv_norm, mlstm_conv_b, mlstm_head_gain))

    x2 = x.reshape(N, D)
    for l in range(L):
        sh_a, sc_a, g_a, sh_f, sc_f, g_f = (mod[l, :, i] for i in range(6))
        oa, mqk, mv, mo, mif, qd, kd, vd, gates = _inproj(x2, S, l, attn_norm, sc_a, sh_a, w_pad, dqg, dkg)
        q, k, vt = _mla_prep(oa, S, l, mla_q_norm, mla_kv_norm, wuq, wkv, qg, kg, tables)
        ya = _flash(q, k, vt, band=False)
        yb = _mlstm(mqk, mv, mo, mif, B, S, l, mlstm_conv_w, mlstm_conv_b, gate_b, mlstm_head_gain)
        yc = _flash(qd, kd, vd, band=True)
        x2 = _merge(ya.reshape(N, -1), yb, yc.reshape(N, -1), gates, x2, S, l, g_a, wa, wb, wc, wo)
        x2 = _ffn(x2, S, l, ffn_norm, sc_f, sh_f, g_f, wr, br, wg, wu, wd)
    return x2.reshape(B, S, D)
```

```python
import functools

import jax
import jax.numpy as jnp
import numpy as np
from jax import lax
from jax.experimental import pallas as pl
from jax.experimental.pallas import tpu as pltpu

F32 = jnp.float32
BF16 = jnp.bfloat16
HIGHEST = lax.Precision.HIGHEST
NEG_INF = float("-inf")

EPS = 1e-6
LANES = 128
VMEM_BYTES_V7X = 64 * 1024 * 1024

D_MODEL = 1024
MLA_HEADS = 8
MLA_Q_LORA = 256
MLA_KV_LORA = 128
MLA_NOPE = 64
MLA_ROPE = 32
MLA_V = 64
MLA_QK = MLA_NOPE + MLA_ROPE
ROPE_THETA = 10000.0
MLSTM_HEADS = 4
MLSTM_QK = 64
MLSTM_V = 128
CONV_K = 4
DIL_HEADS = 8
DIL_HEAD_DIM = 64
DIL_PATTERNS = ((128, 1), (512, 4), (2048, 16))
DIL_WIDTH = DIL_HEADS * DIL_HEAD_DIM
N_GROUPS = 4
EXPERTS_PER_GROUP = 4
N_EXPERTS = N_GROUPS * EXPERTS_PER_GROUP
EXPERT_FF = 256
N_BRANCH = 3
IN_SIZES = (MLA_Q_LORA, MLA_KV_LORA, MLA_ROPE,
            2 * MLSTM_HEADS * MLSTM_QK, MLSTM_HEADS * MLSTM_V, MLSTM_HEADS * MLSTM_V, MLSTM_HEADS, MLSTM_HEADS,
            DIL_WIDTH, DIL_WIDTH, DIL_WIDTH, N_BRANCH * D_MODEL)

COL_A = 0
COL_MQK = 512
COL_MV = 1024
COL_MO = 1536
COL_MIF = 2048
COL_DQ = 2176
COL_DK = 3200
COL_DV = 4224
COL_G = 4736
COL_END = 7808

FLASH_TILE = 1024
DIL_MAX_WINDOW = max(w for w, _ in DIL_PATTERNS)
V_DIM = 64
V_ROWS = 80
LOG2E = 1.4426950408889634
assert MLA_V == V_DIM and DIL_HEAD_DIM == V_DIM


def _store_v_tiles_t(v, vt_ref):
    tm = v.shape[0]
    v_t = v.astype(BF16).T
    fill = (lax.broadcasted_iota(jnp.int32, (V_ROWS - V_DIM, tm), 0) == 0).astype(BF16)
    for h in range(v.shape[1] // V_DIM):
        vt_ref[0, h, 0:V_DIM, :] = v_t[h * V_DIM:(h + 1) * V_DIM, :]
        vt_ref[0, h, V_DIM:V_ROWS, :] = fill


def _cparams(sem, vmem_mb):
    return pltpu.CompilerParams(dimension_semantics=sem, vmem_limit_bytes=vmem_mb * 1024 * 1024)


def _sigmoid(x):
    return 1.0 / (1.0 + jnp.exp(-x))


def _const_spec(shape, index_map):
    return pl.BlockSpec(shape, index_map, pipeline_mode=pl.Buffered(1))


def _adaln_kernel(c_ref, w_ref, b_ref, o_ref):
    c = c_ref[...]
    ca = c * _sigmoid(c)
    o_ref[0] = jnp.dot(ca, w_ref[0], preferred_element_type=F32, precision=HIGHEST) + b_ref[0]


def _adaln(c, w_ada, b_ada):
    L, D, D6 = w_ada.shape
    B = c.shape[0]
    rows = 8
    cp = jnp.zeros((rows, D), F32).at[:B].set(c)
    out = pl.pallas_call(
        _adaln_kernel,
        grid=(L, D6 // D),
        in_specs=[pl.BlockSpec((rows, D), lambda l, j: (0, 0)),
                  pl.BlockSpec((1, D, D), lambda l, j: (l, 0, j)),
                  pl.BlockSpec((1, 1, D), lambda l, j: (l, 0, j))],
        out_specs=pl.BlockSpec((1, rows, D), lambda l, j: (l, 0, j)),
        out_shape=jax.ShapeDtypeStruct((L, rows, D6), F32),
        compiler_params=_cparams(("parallel", "parallel"), 32),
        name="adaln",
    )(cp, w_ada, b_ada.reshape(L, 1, D6))
    return out[:, :B].reshape(L, B, D6 // D, 1, D)


def _inproj_kernel(x_ref, gain_ref, sc_ref, sh_ref, w_ref, dqg_ref, dkg_ref,
                   oa_ref, mqk_ref, mv_ref, mo_ref, mif_ref, qd_ref, kd_ref, vd_ref, g_ref):
    x = x_ref[...]
    ms = jnp.mean(x * x, axis=-1, keepdims=True)
    y = x * lax.rsqrt(ms + EPS) * gain_ref[0]
    u = (y * (1.0 + sc_ref[0]) + sh_ref[0]).astype(BF16)

    def proj(a, b):
        return jnp.dot(u, w_ref[0, :, a:b], preferred_element_type=F32)

    oa_ref[...] = proj(COL_A, COL_MQK)
    mqk_ref[...] = proj(COL_MQK, COL_MV)
    mv_ref[...] = proj(COL_MV, COL_MO).astype(BF16)
    mo_ref[...] = proj(COL_MO, COL_MIF).astype(BF16)
    mif_ref[...] = proj(COL_MIF, COL_DQ)

    def head_norm(z, g):
        for h in range(DIL_HEADS):
            zh = z[:, h * LANES:(h + 1) * LANES]
            r = lax.rsqrt(jnp.sum(zh * zh, axis=-1, keepdims=True) / DIL_HEAD_DIM + EPS)
            yield h, zh * r * g

    dq = proj(COL_DQ, COL_DK)
    for h, zn in head_norm(dq, dqg_ref[0]):
        qd_ref[0, h // 2, :, (h % 2) * LANES:(h % 2 + 1) * LANES] = (zn * (DIL_HEAD_DIM ** -0.5 * LOG2E)).astype(BF16)
    dk = proj(COL_DK, COL_DV)
    for h, zn in head_norm(dk, dkg_ref[0]):
        kd_ref[0, h // 2, :, (h % 2) * LANES:(h % 2 + 1) * LANES] = zn.astype(BF16)
    _store_v_tiles_t(proj(COL_DV, COL_G), vd_ref)
    for c in range(N_BRANCH):
        g_ref[:, c * D_MODEL:(c + 1) * D_MODEL] = proj(COL_G + c * D_MODEL, COL_G + (c + 1) * D_MODEL).astype(BF16)


def _inproj(x2, S, layer, gain, sc, sh, w_pad, dqg, dkg, tm=256):
    N, D = x2.shape
    per_b = S // tm
    row = lambda i: (i, 0)
    bsel = lambda i: (i // per_b, 0, 0)
    B = N // S
    outs = [(512, F32), (512, F32), (512, BF16), (512, BF16), (LANES, F32),
            "qk", "qk", "vt", (3072, BF16)]
    head_major = lambda i: (i // per_b, 0, i % per_b, 0)
    special_spec = {"qk": pl.BlockSpec((1, DIL_HEADS // 2, tm, 2 * LANES), head_major),
                    "vt": pl.BlockSpec((1, DIL_HEADS, V_ROWS, tm), lambda i: (i // per_b, 0, 0, i % per_b))}
    special_shape = {"qk": jax.ShapeDtypeStruct((B, DIL_HEADS // 2, S, 2 * LANES), BF16),
                     "vt": jax.ShapeDtypeStruct((B, DIL_HEADS, V_ROWS, S), BF16)}
    return pl.pallas_call(
        _inproj_kernel,
        grid=(N // tm,),
        in_specs=[pl.BlockSpec((tm, D), row),
                  _const_spec((1, 1, D), lambda i: (layer, 0, 0)),
                  pl.BlockSpec((1, 1, D), bsel),
                  pl.BlockSpec((1, 1, D), bsel),
                  _const_spec((1, D, COL_END), lambda i: (layer, 0, 0)),
                  _const_spec((1, 1, LANES), lambda i: (layer, 0, 0)),
                  _const_spec((1, 1, LANES), lambda i: (layer, 0, 0))],
        out_specs=[special_spec[o] if isinstance(o, str) else pl.BlockSpec((tm, o[0]), row) for o in outs],
        out_shape=[special_shape[o] if isinstance(o, str) else jax.ShapeDtypeStruct((N, o[0]), o[1]) for o in outs],
        compiler_params=_cparams(("parallel",), 52),
        name="inproj",
    )(x2, gain, sc, sh, w_pad, dqg, dkg)


def _mla_prep_kernel(a_ref, qn_ref, kvn_ref, wuq_ref, wkv_ref, qg_ref, qgs_ref, kg_ref, cos_ref, sin_ref,
                     q_ref, k_ref, vt_ref):
    HW = MLA_HEADS * LANES
    a = a_ref[...]
    cq = a[:, :MLA_Q_LORA]
    ckv = a[:, MLA_Q_LORA:MLA_Q_LORA + MLA_KV_LORA]
    kr_tile = a[:, MLA_Q_LORA + MLA_KV_LORA:]

    def rms(z, g):
        return z * lax.rsqrt(jnp.mean(z * z, axis=-1, keepdims=True) + EPS) * g

    cqn = rms(cq, qn_ref[0]).astype(BF16)
    ckvn = rms(ckv, kvn_ref[0]).astype(BF16)
    qr = jnp.dot(cqn, wuq_ref[0], preferred_element_type=F32)
    kvr = jnp.dot(ckvn, wkv_ref[0], preferred_element_type=F32)
    _store_v_tiles_t(kvr[:, HW:], vt_ref)

    cosf, sins = cos_ref[...], sin_ref[...]
    q_scale = MLA_QK ** -0.5 * LOG2E
    gc = cosf * (qg_ref[0] * q_scale)
    gs = sins * (qgs_ref[0] * q_scale)
    kg = kg_ref[0]
    half = MLA_ROPE // 2
    krg = pltpu.roll(kr_tile, MLA_NOPE, 1) * kg
    kr_rot = krg * cosf + (pltpu.roll(krg, LANES - half, 1) + pltpu.roll(krg, half, 1)) * sins
    kr_ss = jnp.sum(kr_tile * kr_tile, axis=-1, keepdims=True)
    for h in range(MLA_HEADS):
        sl = slice(h * LANES, (h + 1) * LANES)
        pair_sl = slice((h % 2) * LANES, (h % 2 + 1) * LANES)
        z = qr[:, sl]
        rq = lax.rsqrt(jnp.sum(z * z, axis=-1, keepdims=True) / MLA_QK + EPS)
        q_ref[0, h // 2, :, pair_sl] = ((z * gc + qr[:, HW + h * LANES:HW + (h + 1) * LANES] * gs) * rq).astype(BF16)
        zk = kvr[:, sl]
        rk = lax.rsqrt((jnp.sum(zk * zk, axis=-1, keepdims=True) + kr_ss) / MLA_QK + EPS)
        k_ref[0, h // 2, :, pair_sl] = ((zk * kg + kr_rot) * rk).astype(BF16)


def _rope_tables(S):
    half = MLA_ROPE // 2
    inv = ROPE_THETA ** (-jnp.arange(half, dtype=F32) / half)
    ang = jnp.arange(S).astype(F32)[:, None] * inv[None, :]
    cos, sin = jnp.cos(ang), jnp.sin(ang)
    one = jnp.ones((S, MLA_NOPE), F32)
    z = lambda n: jnp.zeros((S, n), F32)
    tail = LANES - MLA_QK
    cosf = jnp.concatenate([one, cos, cos, z(tail)], axis=1)
    sins = jnp.concatenate([z(MLA_NOPE), -sin, sin, z(tail)], axis=1)
    return cosf, sins


def _mla_prep(oa, S, layer, qn, kvn, wuq, wkv, qg, qgs, kg, tables, tm=512):
    N = oa.shape[0]
    B = N // S
    per_b = S // tm
    row = lambda i: (i, 0)
    pos = lambda i: (i % per_b, 0)
    HW = MLA_HEADS * LANES
    return pl.pallas_call(
        _mla_prep_kernel,
        grid=(N // tm,),
        in_specs=[pl.BlockSpec((tm, 512), row),
                  _const_spec((1, 1, MLA_Q_LORA), lambda i: (layer, 0, 0)),
                  _const_spec((1, 1, MLA_KV_LORA), lambda i: (layer, 0, 0)),
                  _const_spec((1, MLA_Q_LORA, 2 * HW), lambda i: (layer, 0, 0)),
                  _const_spec((1, MLA_KV_LORA, HW + MLA_HEADS * MLA_V), lambda i: (layer, 0, 0)),
                  _const_spec((1, 1, LANES), lambda i: (layer, 0, 0)),
                  _const_spec((1, 1, LANES), lambda i: (layer, 0, 0)),
                  _const_spec((1, 1, LANES), lambda i: (layer, 0, 0)),
                  pl.BlockSpec((tm, LANES), pos),
                  pl.BlockSpec((tm, LANES), pos)],
        out_specs=[pl.BlockSpec((1, MLA_HEADS // 2, tm, 2 * LANES), lambda i: (i // per_b, 0, i % per_b, 0)),
                   pl.BlockSpec((1, MLA_HEADS // 2, tm, 2 * LANES), lambda i: (i // per_b, 0, i % per_b, 0)),
                   pl.BlockSpec((1, MLA_HEADS, V_ROWS, tm), lambda i: (i // per_b, 0, 0, i % per_b))],
        out_shape=[jax.ShapeDtypeStruct((B, MLA_HEADS // 2, S, 2 * LANES), BF16),
                   jax.ShapeDtypeStruct((B, MLA_HEADS // 2, S, 2 * LANES), BF16),
                   jax.ShapeDtypeStruct((B, MLA_HEADS, V_ROWS, S), BF16)],
        compiler_params=_cparams(("parallel",), 32),
        name="mla_prep",
    )(oa, qn, kvn, wuq, wkv, qg, qgs, kg, *tables)


def _flash_kernel(qi_ref, kj_ref, qn_ref, kn_ref, q0_ref, k0_ref, vt_ref, *rest, band, T, W):
    if band:
        bias_ref, o_ref, m_ref, acc_ref, s_ref = rest
    else:
        o_ref, m_ref, acc_ref, s_ref = rest
    p_id = pl.program_id(2)
    qi = qi_ref[p_id]
    kj = kj_ref[p_id]
    if band:
        first = kj == qi
        last = kj == jnp.maximum(qi - W, 0)
    else:
        first = kj == 0
        last = kj == qi

    def scores(q_ref, k_ref, hh):
        q = q_ref[0, 0, :, hh * LANES:(hh + 1) * LANES]
        k = k_ref[0, 0, :, hh * LANES:(hh + 1) * LANES]
        return lax.dot_general(k, q, (((1,), (1,)), ((), ())), preferred_element_type=F32)

    @pl.when(p_id == 0)
    def _():
        for hh in range(2):
            s_ref[0, hh] = scores(q0_ref, k0_ref, hh)

    @pl.when(first)
    def _():
        m_ref[...] = jnp.full(m_ref.shape, NEG_INF, F32)
        acc_ref[...] = jnp.zeros(acc_ref.shape, F32)

    def step(masked, slot):
        if masked and not band:
            valid = lax.broadcasted_iota(jnp.int32, (T, T), 0) <= lax.broadcasted_iota(jnp.int32, (T, T), 1)
        for hh in range(2):
            s_ref[1 - slot, hh] = scores(qn_ref, kn_ref, hh)
            st = s_ref[slot, hh]
            if band:
                st = st + bias_ref[qi - kj]
            elif masked:
                st = jnp.where(valid, st, NEG_INF)
            m_prev = m_ref[hh]
            m_new = jnp.maximum(m_prev, jnp.max(st, axis=0, keepdims=True))
            alpha = jnp.exp2(m_prev - m_new)
            pt = jnp.exp2(st - m_new).astype(BF16)
            acc_ref[hh] = acc_ref[hh] * alpha + jnp.dot(vt_ref[0, hh], pt, preferred_element_type=F32)
            m_ref[hh] = m_new

    parity = p_id % 2
    for slot in range(2):
        here = parity == slot
        if band:
            pl.when(here)(functools.partial(step, True, slot))
        else:
            pl.when(jnp.logical_and(here, last))(functools.partial(step, True, slot))
            pl.when(jnp.logical_and(here, jnp.logical_not(last)))(functools.partial(step, False, slot))

    @pl.when(last)
    def _():
        outs = []
        for hh in range(2):
            a = acc_ref[hh]
            outs.append(a[0:V_DIM, :] / a[V_DIM:V_DIM + 1, :])
        o_ref[0] = jnp.concatenate(outs, axis=0).T.astype(o_ref.dtype)


def _band_bias(T, W):
    d = (jnp.arange(W + 1)[:, None, None] * T + jnp.arange(T)[None, None, :] - jnp.arange(T)[None, :, None])
    cw = jnp.zeros(d.shape, F32)
    for window, dil in DIL_PATTERNS:
        cw = cw + ((d >= 0) & (d <= window) & (d % dil == 0)).astype(F32)
    return jnp.log2(cw)


def _flash(q, k, vt, band, T=FLASH_TILE):
    B, HP, S, _ = q.shape
    H = 2 * HP
    nq = S // T
    W = DIL_MAX_WINDOW // T if band else 0
    if band:
        pairs = [(i, i - j) for i in range(nq) for j in range(min(W, i) + 1)]
    else:
        pairs = [(i, j) for i in range(nq) for j in range(i + 1)]
    n = len(pairs)
    qi = jnp.asarray(np.array([p[0] for p in pairs], np.int32))
    kj = jnp.asarray(np.array([p[1] for p in pairs], np.int32))
    nxt = lambda p: jnp.minimum(p + 1, n - 1)
    in_specs = [pl.BlockSpec((1, 1, T, 2 * LANES), lambda b, h, p, qi, kj: (b, h, qi[nxt(p)], 0)),
                pl.BlockSpec((1, 1, T, 2 * LANES), lambda b, h, p, qi, kj: (b, h, kj[nxt(p)], 0)),
                pl.BlockSpec((1, 1, T, 2 * LANES), lambda b, h, p, qi, kj: (b, h, qi[0], 0)),
                pl.BlockSpec((1, 1, T, 2 * LANES), lambda b, h, p, qi, kj: (b, h, kj[0], 0)),
                pl.BlockSpec((1, 2, V_ROWS, T), lambda b, h, p, qi, kj: (b, h, 0, kj[p]))]
    args = [q, k, q, k, vt]
    if band:
        in_specs.append(_const_spec((W + 1, T, T), lambda b, h, p, qi, kj: (0, 0, 0)))
        args.append(_band_bias(T, W))
    return pl.pallas_call(
        functools.partial(_flash_kernel, band=band, T=T, W=W),
        grid_spec=pltpu.PrefetchScalarGridSpec(
            num_scalar_prefetch=2,
            grid=(B, H // 2, n),
            in_specs=in_specs,
            out_specs=pl.BlockSpec((1, T, LANES), lambda b, h, p, qi, kj: (b, qi[p], h)),
            scratch_shapes=[pltpu.VMEM((2, 1, T), F32), pltpu.VMEM((2, V_ROWS, T), F32),
                            pltpu.VMEM((2, 2, T, T), F32)]),
        out_shape=jax.ShapeDtypeStruct((B, S, H * V_DIM), BF16),
        compiler_params=_cparams(("arbitrary", "arbitrary", "arbitrary"), 52),
        name="flash_band" if band else "flash_causal",
    )(qi, kj, *args)


def _mlstm_kernel(qk_ref, v_ref, o_ref, if_ref, cw_ref, cb_ref, gb_ref, hg_ref, y_ref,
                  xs_ref, c_ref, n_ref, m_ref, *, Lc):
    H, DK, DV = MLSTM_HEADS, MLSTM_QK, MLSTM_V
    HALO = 8

    @pl.when(pl.program_id(1) == 0)
    def _():
        xs_ref[0:HALO, :] = jnp.zeros((HALO, 2 * H * DK), F32)
        c_ref[...] = jnp.zeros(c_ref.shape, F32)
        n_ref[...] = jnp.zeros(n_ref.shape, F32)
        m_ref[...] = jnp.zeros(m_ref.shape, F32)

    x = qk_ref[...]
    xs_ref[HALO:HALO + Lc, :] = x
    y = cb_ref[0]
    for j in range(CONV_K):
        y = y + cw_ref[0, j:j + 1, :] * xs_ref[pl.ds(HALO - (CONV_K - 1) + j, Lc), :]
    xs_ref[0:HALO, :] = x[Lc - HALO:, :]
    qk = y * _sigmoid(y)

    g = if_ref[...] + gb_ref[0]
    logf = jnp.minimum(g, 0.0) - jnp.log(1.0 + jnp.exp(-jnp.abs(g)))
    row = lax.broadcasted_iota(jnp.int32, (Lc, Lc), 0)
    col = lax.broadcasted_iota(jnp.int32, (Lc, Lc), 1)
    causal = col <= row
    tri = jnp.where(causal, 1.0, 0.0).astype(F32)
    bcum = jnp.dot(tri, logf, preferred_element_type=F32, precision=HIGHEST)
    g_t = g.T
    bcum_t = bcum.T

    for h in range(H):
        q = qk[:, h * DK:(h + 1) * DK]
        k = qk[:, (H + h) * DK:(H + h + 1) * DK] * (DK ** -0.5)
        v = v_ref[:, h * DV:(h + 1) * DV]
        bt = bcum[:, H + h:H + h + 1]
        ig_c = g[:, h:h + 1]
        bs = bcum_t[H + h:H + h + 1, :]
        ig_r = g_t[h:h + 1, :]
        m_prev = m_ref[h:h + 1, 0:1]
        dlog = jnp.where(causal, bt - bs + ig_r, NEG_INF)
        inter = bt + m_prev
        m_t = jnp.maximum(inter, jnp.max(dlog, axis=1, keepdims=True))
        qb = q.astype(BF16)
        s = lax.dot_general(qb, k.astype(BF16), (((1,), (1,)), ((), ())), preferred_element_type=F32)
        sc = s * jnp.exp(dlog - m_t)
        decay = jnp.exp(inter - m_t)
        c_h = c_ref[h]
        n_h = n_ref[h:h + 1, :]
        num = (decay * jnp.dot(qb, c_h.astype(BF16), preferred_element_type=F32)
               + jnp.dot(sc.astype(BF16), v, preferred_element_type=F32))
        den = decay * jnp.sum(q * n_h, axis=1, keepdims=True) + jnp.sum(sc, axis=1, keepdims=True)
        hh = num / jnp.maximum(jnp.abs(den), jnp.exp(-m_t))
        hn = hh * lax.rsqrt(jnp.mean(hh * hh, axis=-1, keepdims=True) + EPS) * hg_ref[0, :, h * DV:(h + 1) * DV]
        o_gate = _sigmoid(o_ref[:, h * DV:(h + 1) * DV].astype(F32))
        y_ref[:, h * DV:(h + 1) * DV] = (hn * o_gate).astype(y_ref.dtype)

        b_end = bt[Lc - 1:Lc, :]
        wlog = b_end - bt + ig_c
        m_new = jnp.maximum(b_end + m_prev, jnp.max(wlog, axis=0, keepdims=True))
        cd = jnp.exp(b_end + m_prev - m_new)
        kw = k * jnp.exp(wlog - m_new)
        c_ref[h] = cd * c_h + lax.dot_general(kw.astype(BF16), v, (((0,), (0,)), ((), ())),
                                              preferred_element_type=F32)
        n_ref[h:h + 1, :] = cd * n_h + jnp.sum(kw, axis=0, keepdims=True)
        m_ref[h:h + 1, :] = jnp.broadcast_to(m_new, (1, LANES))


def _mlstm(mqk, mv, mo, mif, B, S, layer, conv_w, conv_b, gate_b, head_gain, Lc=256):
    H, DK, DV = MLSTM_HEADS, MLSTM_QK, MLSTM_V
    nc = S // Lc
    row = lambda b, c: (b * nc + c, 0)
    return pl.pallas_call(
        functools.partial(_mlstm_kernel, Lc=Lc),
        grid=(B, nc),
        in_specs=[pl.BlockSpec((Lc, 2 * H * DK), row),
                  pl.BlockSpec((Lc, H * DV), row),
                  pl.BlockSpec((Lc, H * DV), row),
                  pl.BlockSpec((Lc, LANES), row),
                  _const_spec((1, CONV_K, 2 * H * DK), lambda b, c: (layer, 0, 0)),
                  _const_spec((1, 1, 2 * H * DK), lambda b, c: (layer, 0, 0)),
                  _const_spec((1, 1, LANES), lambda b, c: (layer, 0, 0)),
                  _const_spec((1, 1, H * DV), lambda b, c: (layer, 0, 0))],
        out_specs=pl.BlockSpec((Lc, H * DV), row),
        out_shape=jax.ShapeDtypeStruct((B * S, H * DV), BF16),
        scratch_shapes=[pltpu.VMEM((Lc + 8, 2 * H * DK), F32), pltpu.VMEM((H, DK, DV), F32),
                        pltpu.VMEM((8, DK), F32), pltpu.VMEM((8, LANES), F32)],
        compiler_params=_cparams(("arbitrary", "arbitrary"), 32),
        name="mlstm",
    )(mqk, mv, mo, mif, conv_w, conv_b, gate_b, head_gain)


def _merge_kernel(ya_ref, yb_ref, yc_ref, g_ref, x_ref, ga_ref, wa_ref, wb_ref, wc_ref, wo_ref, out_ref):
    def branch(y_ref, w_ref, c):
        gate = _sigmoid(g_ref[:, c * D_MODEL:(c + 1) * D_MODEL].astype(F32))
        return gate * jnp.dot(y_ref[...], w_ref[0], preferred_element_type=F32)

    merged = branch(ya_ref, wa_ref, 0) + branch(yb_ref, wb_ref, 1) + branch(yc_ref, wc_ref, 2)
    y = jnp.dot(merged.astype(BF16), wo_ref[0], preferred_element_type=F32)
    out_ref[...] = x_ref[...] + ga_ref[0] * y


def _merge(ya, yb, yc, gates, x2, S, layer, ga, wa, wb, wc, wo, tm=512):
    N, D = x2.shape
    per_b = S // tm
    row = lambda i: (i, 0)
    wsel = lambda i: (layer, 0, 0)
    return pl.pallas_call(
        _merge_kernel,
        grid=(N // tm,),
        in_specs=[pl.BlockSpec((tm, 512), row), pl.BlockSpec((tm, 512), row), pl.BlockSpec((tm, 512), row),
                  pl.BlockSpec((tm, N_BRANCH * D), row), pl.BlockSpec((tm, D), row),
                  pl.BlockSpec((1, 1, D), lambda i: (i // per_b, 0, 0)),
                  _const_spec((1, 512, D), wsel), _const_spec((1, 512, D), wsel), _const_spec((1, 512, D), wsel),
                  _const_spec((1, D, D), wsel)],
        out_specs=pl.BlockSpec((tm, D), row),
        out_shape=jax.ShapeDtypeStruct((N, D), F32),
        compiler_params=_cparams(("parallel",), 48),
        name="merge",
    )(ya, yb, yc, gates, x2, ga, wa, wb, wc, wo)


def _ffn_kernel(x_ref, gain_ref, sc_ref, sh_ref, gf_ref, wr_ref, br_ref, wg_ref, wu_ref, wd_ref, out_ref):
    x = x_ref[...]
    tm = x.shape[0]
    ms = jnp.mean(x * x, axis=-1, keepdims=True)
    hf = x * lax.rsqrt(ms + EPS) * gain_ref[0] * (1.0 + sc_ref[0]) + sh_ref[0]

    logits = jnp.dot(hf, wr_ref[0], preferred_element_type=F32, precision=HIGHEST) + br_ref[0]
    lane_i = lax.broadcasted_iota(jnp.int32, (tm, LANES), 1)
    lane = lane_i.astype(F32)
    big = float(LANES)

    def first_argmax(z):
        zmax = jnp.max(z, axis=1, keepdims=True)
        return zmax, jnp.min(jnp.where(z == zmax, lane, big), axis=1, keepdims=True)

    lg = jnp.where(lane_i < N_GROUPS, logits, NEG_INF)
    gmax, g_idx = first_argmax(lg)
    p_sel = 1.0 / jnp.sum(jnp.exp(lg - gmax), axis=1, keepdims=True)
    lane_grp = ((lane_i - N_GROUPS) // EXPERTS_PER_GROUP).astype(F32)
    in_grp = (lane_i >= N_GROUPS) & (lane_i < N_GROUPS + N_EXPERTS) & (lane_grp == g_idx)
    le = jnp.where(in_grp, logits, NEG_INF)
    v1, i1 = first_argmax(le)
    le2 = jnp.where(lane == i1, NEG_INF, le)
    v2, i2 = first_argmax(le2)
    e2 = jnp.exp(v2 - v1)
    den = 1.0 + e2
    comb = jnp.where(lane == i1, (1.0 / den) * p_sel, 0.0) + jnp.where(lane == i2, (e2 / den) * p_sel, 0.0)

    hb = hf.astype(BF16)
    acc = jnp.zeros((tm, D_MODEL), F32)
    for e in range(N_EXPERTS):
        gate = jnp.dot(hb, wg_ref[0, e], preferred_element_type=F32)
        up = jnp.dot(hb, wu_ref[0, e], preferred_element_type=F32)
        hid = gate * _sigmoid(gate) * up * comb[:, N_GROUPS + e:N_GROUPS + e + 1]
        acc = acc + jnp.dot(hid.astype(BF16), wd_ref[0, e], preferred_element_type=F32)
    out_ref[...] = x + gf_ref[0] * acc


def _ffn(x2, S, layer, gain, sc, sh, gf, wr, br, wg, wu, wd, tm=512):
    N, D = x2.shape
    per_b = S // tm
    row = lambda i: (i, 0)
    bsel = lambda i: (i // per_b, 0, 0)
    wsel = lambda i: (layer, 0, 0, 0)
    return pl.pallas_call(
        _ffn_kernel,
        grid=(N // tm,),
        in_specs=[pl.BlockSpec((tm, D), row),
                  _const_spec((1, 1, D), lambda i: (layer, 0, 0)),
                  pl.BlockSpec((1, 1, D), bsel), pl.BlockSpec((1, 1, D), bsel), pl.BlockSpec((1, 1, D), bsel),
                  _const_spec((1, D, LANES), lambda i: (layer, 0, 0)),
                  _const_spec((1, 1, LANES), lambda i: (layer, 0, 0)),
                  _const_spec((1, N_EXPERTS, D, EXPERT_FF), wsel),
                  _const_spec((1, N_EXPERTS, D, EXPERT_FF), wsel),
                  _const_spec((1, N_EXPERTS, EXPERT_FF, D), wsel)],
        out_specs=pl.BlockSpec((tm, D), row),
        out_shape=jax.ShapeDtypeStruct((N, D), F32),
        compiler_params=_cparams(("parallel",), 56),
        name="ffn",
    )(x2, gain, sc, sh, gf, wr, br, wg, wu, wd)


def _pad_last(a, n):
    return jnp.pad(a, [(0, 0)] * (a.ndim - 1) + [(0, n - a.shape[-1])])


def _head_tiles(a, heads, dim):
    a = a.reshape(a.shape[:-1] + (heads, dim))
    return _pad_last(a, LANES).reshape(a.shape[:-2] + (heads * LANES,))


def _prep_w_in(w_in):
    pts = np.cumsum(IN_SIZES)[:-1].tolist()
    cq, ckv, kr, m_qk, m_v, m_o, m_i, m_f, d_q, d_k, d_v, gates = jnp.split(w_in, pts, axis=-1)
    seg_a = _pad_last(jnp.concatenate([cq, ckv, kr], axis=-1), 512)
    seg_if = _pad_last(jnp.concatenate([m_i, m_f], axis=-1), LANES)
    w = jnp.concatenate([seg_a, m_qk, m_v, m_o, seg_if,
                         _head_tiles(d_q, DIL_HEADS, DIL_HEAD_DIM), _head_tiles(d_k, DIL_HEADS, DIL_HEAD_DIM),
                         d_v, gates], axis=-1)
    assert w.shape[-1] == COL_END
    return w.astype(BF16)


def kernel(x, c, w_ada, b_ada, attn_norm, w_in, mla_q_norm, mla_kv_norm, mla_w_uq, mla_w_ukv, mla_q_gain, mla_k_gain, mlstm_conv_w, mlstm_conv_b, mlstm_b_i, mlstm_b_f, mlstm_head_gain, dil_q_gain, dil_k_gain, w_branch_a, w_branch_b, w_branch_c, w_out, ffn_norm, w_router_group, b_router_group, w_router_expert, b_router_expert, w_exp_gate, w_exp_up, w_exp_down):
    B, S, D = x.shape
    L = w_ada.shape[0]
    N = B * S
    assert D == D_MODEL and S % DIL_MAX_WINDOW == 0

    mod = _adaln(c, w_ada, b_ada)

    w_pad = _prep_w_in(w_in)
    half = MLA_ROPE // 2
    swap_rope = lambda a: jnp.concatenate(
        [jnp.zeros_like(a[..., :MLA_NOPE]), a[..., MLA_NOPE + half:], a[..., MLA_NOPE:MLA_NOPE + half]], axis=-1)
    uq = mla_w_uq.reshape(L, MLA_Q_LORA, MLA_HEADS, MLA_QK)
    wuq = jnp.concatenate([_pad_last(u, LANES).reshape(L, MLA_Q_LORA, MLA_HEADS * LANES)
                           for u in (uq, swap_rope(uq))], axis=-1).astype(BF16)
    ukv = mla_w_ukv.reshape(L, MLA_KV_LORA, MLA_HEADS, MLA_NOPE + MLA_V)
    wkv = jnp.concatenate([_pad_last(ukv[..., :MLA_NOPE], LANES).reshape(L, MLA_KV_LORA, MLA_HEADS * LANES),
                           ukv[..., MLA_NOPE:].reshape(L, MLA_KV_LORA, MLA_HEADS * MLA_V)], axis=-1).astype(BF16)
    rows = lambda a: a[:, None, :]
    qg = rows(_pad_last(mla_q_gain, LANES))
    qgs = rows(_pad_last(swap_rope(mla_q_gain), LANES))
    kg = rows(_pad_last(mla_k_gain, LANES))
    dqg = rows(_pad_last(dil_q_gain, LANES))
    dkg = rows(_pad_last(dil_k_gain, LANES))
    gate_b = rows(_pad_last(jnp.concatenate([mlstm_b_i, mlstm_b_f], axis=-1), LANES))
    tables = _rope_tables(S)
    wa, wb, wc, wo = (w.astype(BF16) for w in (w_branch_a, w_branch_b, w_branch_c, w_out))
    wr = _pad_last(jnp.concatenate([w_router_group, w_router_expert], axis=-1), LANES)
    br = rows(_pad_last(jnp.concatenate([b_router_group, b_router_expert], axis=-1), LANES))
    wg, wu, wd = (w.astype(BF16) for w in (w_exp_gate, w_exp_up, w_exp_down))
    attn_norm, ffn_norm, mla_q_norm, mla_kv_norm, mlstm_conv_b, mlstm_head_gain = (
        rows(a) for a in (attn_norm, ffn_norm, mla_q_norm, mla_kv_norm, mlstm_conv_b, mlstm_head_gain))

    x2 = x.reshape(N, D)
    for l in range(L):
        sh_a, sc_a, g_a, sh_f, sc_f, g_f = (mod[l, :, i] for i in range(6))
        oa, mqk, mv, mo, mif, qd, kd, vd, gates = _inproj(x2, S, l, attn_norm, sc_a, sh_a, w_pad, dqg, dkg)
        q, k, vt = _mla_prep(oa, S, l, mla_q_norm, mla_kv_norm, wuq, wkv, qg, qgs, kg, tables)
        ya = _flash(q, k, vt, band=False)
        yb = _mlstm(mqk, mv, mo, mif, B, S, l, mlstm_conv_w, mlstm_conv_b, gate_b, mlstm_head_gain)
        yc = _flash(qd, kd, vd, band=True)
        x2 = _merge(ya.reshape(N, -1), yb, yc.reshape(N, -1), gates, x2, S, l, g_a, wa, wb, wc, wo)
        x2 = _ffn(x2, S, l, ffn_norm, sc_f, sh_f, g_f, wr, br, wg, wu, wd)
    return x2.reshape(B, S, D)
```

```python
import functools

import jax
import jax.numpy as jnp
import numpy as np
from jax import lax
from jax.experimental import pallas as pl
from jax.experimental.pallas import tpu as pltpu

F32 = jnp.float32
BF16 = jnp.bfloat16
HIGHEST = lax.Precision.HIGHEST
NEG_INF = float("-inf")

EPS = 1e-6
LANES = 128
VMEM_BYTES_V7X = 64 * 1024 * 1024

D_MODEL = 1024
MLA_HEADS = 8
MLA_Q_LORA = 256
MLA_KV_LORA = 128
MLA_NOPE = 64
MLA_ROPE = 32
MLA_V = 64
MLA_QK = MLA_NOPE + MLA_ROPE
ROPE_THETA = 10000.0
MLSTM_HEADS = 4
MLSTM_QK = 64
MLSTM_V = 128
CONV_K = 4
DIL_HEADS = 8
DIL_HEAD_DIM = 64
DIL_PATTERNS = ((128, 1), (512, 4), (2048, 16))
DIL_WIDTH = DIL_HEADS * DIL_HEAD_DIM
N_GROUPS = 4
EXPERTS_PER_GROUP = 4
N_EXPERTS = N_GROUPS * EXPERTS_PER_GROUP
EXPERT_FF = 256
N_BRANCH = 3
IN_SIZES = (MLA_Q_LORA, MLA_KV_LORA, MLA_ROPE,
            2 * MLSTM_HEADS * MLSTM_QK, MLSTM_HEADS * MLSTM_V, MLSTM_HEADS * MLSTM_V, MLSTM_HEADS, MLSTM_HEADS,
            DIL_WIDTH, DIL_WIDTH, DIL_WIDTH, N_BRANCH * D_MODEL)

COL_A = 0
COL_MQK = 512
COL_MV = 1024
COL_MO = 1536
COL_MIF = 2048
COL_DQ = 2176
COL_DK = 3200
COL_DV = 4224
COL_G = 4736
COL_END = 7808

FLASH_TILE = 1024
DIL_MAX_WINDOW = max(w for w, _ in DIL_PATTERNS)
V_DIM = 64
V_ROWS = 80
LOG2E = 1.4426950408889634
assert MLA_V == V_DIM and DIL_HEAD_DIM == V_DIM


def _store_v_tiles_t(v, vt_ref):
    tm = v.shape[0]
    v_t = v.astype(BF16).T
    fill = (lax.broadcasted_iota(jnp.int32, (V_ROWS - V_DIM, tm), 0) == 0).astype(BF16)
    for h in range(v.shape[1] // V_DIM):
        vt_ref[0, h, 0:V_DIM, :] = v_t[h * V_DIM:(h + 1) * V_DIM, :]
        vt_ref[0, h, V_DIM:V_ROWS, :] = fill


def _cparams(sem, vmem_mb):
    return pltpu.CompilerParams(dimension_semantics=sem, vmem_limit_bytes=vmem_mb * 1024 * 1024)


def _sigmoid(x):
    return 1.0 / (1.0 + jnp.exp(-x))


def _const_spec(shape, index_map):
    return pl.BlockSpec(shape, index_map, pipeline_mode=pl.Buffered(1))


def _adaln_kernel(c_ref, w_ref, b_ref, o_ref):
    c = c_ref[...]
    ca = c * _sigmoid(c)
    o_ref[0] = jnp.dot(ca, w_ref[0], preferred_element_type=F32, precision=HIGHEST) + b_ref[0]


def _adaln(c, w_ada, b_ada):
    L, D, D6 = w_ada.shape
    B = c.shape[0]
    rows = 8
    cp = jnp.zeros((rows, D), F32).at[:B].set(c)
    out = pl.pallas_call(
        _adaln_kernel,
        grid=(L, D6 // D),
        in_specs=[pl.BlockSpec((rows, D), lambda l, j: (0, 0)),
                  pl.BlockSpec((1, D, D), lambda l, j: (l, 0, j)),
                  pl.BlockSpec((1, 1, D), lambda l, j: (l, 0, j))],
        out_specs=pl.BlockSpec((1, rows, D), lambda l, j: (l, 0, j)),
        out_shape=jax.ShapeDtypeStruct((L, rows, D6), F32),
        compiler_params=_cparams(("parallel", "parallel"), 32),
        name="adaln",
    )(cp, w_ada, b_ada.reshape(L, 1, D6))
    return out[:, :B].reshape(L, B, D6 // D, 1, D)


def _inproj_kernel(x_ref, gain_ref, sc_ref, sh_ref, w_ref, dqg_ref, dkg_ref,
                   oa_ref, mqk_ref, mv_ref, mo_ref, mif_ref, qd_ref, kd_ref, vd_ref, g_ref):
    x = x_ref[...]
    ms = jnp.mean(x * x, axis=-1, keepdims=True)
    y = x * lax.rsqrt(ms + EPS) * gain_ref[0]
    u = (y * (1.0 + sc_ref[0]) + sh_ref[0]).astype(BF16)

    def proj(a, b):
        return jnp.dot(u, w_ref[0, :, a:b], preferred_element_type=F32)

    oa_ref[...] = proj(COL_A, COL_MQK)
    mqk_ref[...] = proj(COL_MQK, COL_MV)
    mv_ref[...] = proj(COL_MV, COL_MO).astype(BF16)
    mo_ref[...] = proj(COL_MO, COL_MIF).astype(BF16)
    mif_ref[...] = proj(COL_MIF, COL_DQ)

    def head_norm(z, g):
        for h in range(DIL_HEADS):
            zh = z[:, h * LANES:(h + 1) * LANES]
            r = lax.rsqrt(jnp.sum(zh * zh, axis=-1, keepdims=True) / DIL_HEAD_DIM + EPS)
            yield h, zh * r * g

    dq = proj(COL_DQ, COL_DK)
    for h, zn in head_norm(dq, dqg_ref[0]):
        qd_ref[0, h // 2, :, (h % 2) * LANES:(h % 2 + 1) * LANES] = (zn * (DIL_HEAD_DIM ** -0.5 * LOG2E)).astype(BF16)
    dk = proj(COL_DK, COL_DV)
    for h, zn in head_norm(dk, dkg_ref[0]):
        kd_ref[0, h // 2, :, (h % 2) * LANES:(h % 2 + 1) * LANES] = zn.astype(BF16)
    _store_v_tiles_t(proj(COL_DV, COL_G), vd_ref)
    for c in range(N_BRANCH):
        g_ref[:, c * D_MODEL:(c + 1) * D_MODEL] = proj(COL_G + c * D_MODEL, COL_G + (c + 1) * D_MODEL).astype(BF16)


def _inproj(x2, S, layer, gain, sc, sh, w_pad, dqg, dkg, tm=256):
    N, D = x2.shape
    per_b = S // tm
    row = lambda i: (i, 0)
    bsel = lambda i: (i // per_b, 0, 0)
    B = N // S
    outs = [(512, F32), (512, F32), (512, BF16), (512, BF16), (LANES, F32),
            "qk", "qk", "vt", (3072, BF16)]
    head_major = lambda i: (i // per_b, 0, i % per_b, 0)
    special_spec = {"qk": pl.BlockSpec((1, DIL_HEADS // 2, tm, 2 * LANES), head_major),
                    "vt": pl.BlockSpec((1, DIL_HEADS, V_ROWS, tm), lambda i: (i // per_b, 0, 0, i % per_b))}
    special_shape = {"qk": jax.ShapeDtypeStruct((B, DIL_HEADS // 2, S, 2 * LANES), BF16),
                     "vt": jax.ShapeDtypeStruct((B, DIL_HEADS, V_ROWS, S), BF16)}
    return pl.pallas_call(
        _inproj_kernel,
        grid=(N // tm,),
        in_specs=[pl.BlockSpec((tm, D), row),
                  _const_spec((1, 1, D), lambda i: (layer, 0, 0)),
                  pl.BlockSpec((1, 1, D), bsel),
                  pl.BlockSpec((1, 1, D), bsel),
                  _const_spec((1, D, COL_END), lambda i: (layer, 0, 0)),
                  _const_spec((1, 1, LANES), lambda i: (layer, 0, 0)),
                  _const_spec((1, 1, LANES), lambda i: (layer, 0, 0))],
        out_specs=[special_spec[o] if isinstance(o, str) else pl.BlockSpec((tm, o[0]), row) for o in outs],
        out_shape=[special_shape[o] if isinstance(o, str) else jax.ShapeDtypeStruct((N, o[0]), o[1]) for o in outs],
        compiler_params=_cparams(("parallel",), 52),
        name="inproj",
    )(x2, gain, sc, sh, w_pad, dqg, dkg)


def _mla_prep_kernel(a_ref, qn_ref, kvn_ref, wuq_ref, wkv_ref, qg_ref, qgs_ref, kg_ref, cos_ref, sin_ref,
                     q_ref, k_ref, vt_ref):
    HW = MLA_HEADS * LANES
    a = a_ref[...]
    cq = a[:, :MLA_Q_LORA]
    ckv = a[:, MLA_Q_LORA:MLA_Q_LORA + MLA_KV_LORA]
    kr_tile = a[:, MLA_Q_LORA + MLA_KV_LORA:]

    def rms(z, g):
        return z * lax.rsqrt(jnp.mean(z * z, axis=-1, keepdims=True) + EPS) * g

    cqn = rms(cq, qn_ref[0]).astype(BF16)
    ckvn = rms(ckv, kvn_ref[0]).astype(BF16)
    qr = jnp.dot(cqn, wuq_ref[0], preferred_element_type=F32)
    kvr = jnp.dot(ckvn, wkv_ref[0], preferred_element_type=F32)
    _store_v_tiles_t(kvr[:, HW:], vt_ref)

    cosf, sins = cos_ref[...], sin_ref[...]
    q_scale = MLA_QK ** -0.5 * LOG2E
    gc = cosf * (qg_ref[0] * q_scale)
    gs = sins * (qgs_ref[0] * q_scale)
    kg = kg_ref[0]
    half = MLA_ROPE // 2
    krg = pltpu.roll(kr_tile, MLA_NOPE, 1) * kg
    kr_rot = krg * cosf + (pltpu.roll(krg, LANES - half, 1) + pltpu.roll(krg, half, 1)) * sins
    kr_ss = jnp.sum(kr_tile * kr_tile, axis=-1, keepdims=True)
    for h in range(MLA_HEADS):
        sl = slice(h * LANES, (h + 1) * LANES)
        pair_sl = slice((h % 2) * LANES, (h % 2 + 1) * LANES)
        z = qr[:, sl]
        rq = lax.rsqrt(jnp.sum(z * z, axis=-1, keepdims=True) / MLA_QK + EPS)
        q_ref[0, h // 2, :, pair_sl] = ((z * gc + qr[:, HW + h * LANES:HW + (h + 1) * LANES] * gs) * rq).astype(BF16)
        zk = kvr[:, sl]
        rk = lax.rsqrt((jnp.sum(zk * zk, axis=-1, keepdims=True) + kr_ss) / MLA_QK + EPS)
        k_ref[0, h // 2, :, pair_sl] = ((zk * kg + kr_rot) * rk).astype(BF16)


def _rope_tables(S):
    half = MLA_ROPE // 2
    inv = ROPE_THETA ** (-jnp.arange(half, dtype=F32) / half)
    ang = jnp.arange(S).astype(F32)[:, None] * inv[None, :]
    cos, sin = jnp.cos(ang), jnp.sin(ang)
    one = jnp.ones((S, MLA_NOPE), F32)
    z = lambda n: jnp.zeros((S, n), F32)
    tail = LANES - MLA_QK
    cosf = jnp.concatenate([one, cos, cos, z(tail)], axis=1)
    sins = jnp.concatenate([z(MLA_NOPE), -sin, sin, z(tail)], axis=1)
    return cosf, sins


def _mla_prep(oa, S, layer, qn, kvn, wuq, wkv, qg, qgs, kg, tables, tm=512):
    N = oa.shape[0]
    B = N // S
    per_b = S // tm
    row = lambda i: (i, 0)
    pos = lambda i: (i % per_b, 0)
    HW = MLA_HEADS * LANES
    return pl.pallas_call(
        _mla_prep_kernel,
        grid=(N // tm,),
        in_specs=[pl.BlockSpec((tm, 512), row),
                  _const_spec((1, 1, MLA_Q_LORA), lambda i: (layer, 0, 0)),
                  _const_spec((1, 1, MLA_KV_LORA), lambda i: (layer, 0, 0)),
                  _const_spec((1, MLA_Q_LORA, 2 * HW), lambda i: (layer, 0, 0)),
                  _const_spec((1, MLA_KV_LORA, HW + MLA_HEADS * MLA_V), lambda i: (layer, 0, 0)),
                  _const_spec((1, 1, LANES), lambda i: (layer, 0, 0)),
                  _const_spec((1, 1, LANES), lambda i: (layer, 0, 0)),
                  _const_spec((1, 1, LANES), lambda i: (layer, 0, 0)),
                  pl.BlockSpec((tm, LANES), pos),
                  pl.BlockSpec((tm, LANES), pos)],
        out_specs=[pl.BlockSpec((1, MLA_HEADS // 2, tm, 2 * LANES), lambda i: (i // per_b, 0, i % per_b, 0)),
                   pl.BlockSpec((1, MLA_HEADS // 2, tm, 2 * LANES), lambda i: (i // per_b, 0, i % per_b, 0)),
                   pl.BlockSpec((1, MLA_HEADS, V_ROWS, tm), lambda i: (i // per_b, 0, 0, i % per_b))],
        out_shape=[jax.ShapeDtypeStruct((B, MLA_HEADS // 2, S, 2 * LANES), BF16),
                   jax.ShapeDtypeStruct((B, MLA_HEADS // 2, S, 2 * LANES), BF16),
                   jax.ShapeDtypeStruct((B, MLA_HEADS, V_ROWS, S), BF16)],
        compiler_params=_cparams(("parallel",), 32),
        name="mla_prep",
    )(oa, qn, kvn, wuq, wkv, qg, qgs, kg, *tables)


def _flash_kernel(qi_ref, kj_ref, qn_ref, kn_ref, q0_ref, k0_ref, vt_ref, *rest, band, T, W, n_pairs):
    if band:
        bias_ref, o_ref, m_ref, acc_ref, s_ref = rest
    else:
        o_ref, m_ref, acc_ref, s_ref = rest
    p_id = pl.program_id(2)
    qi = qi_ref[p_id]
    kj = kj_ref[p_id]
    if band:
        first = kj == qi
        last = kj == jnp.maximum(qi - W, 0)
    else:
        first = kj == 0
        last = kj == qi

    def scores(q_ref, k_ref, hh, pair):
        q = q_ref[0, 0, :, hh * LANES:(hh + 1) * LANES]
        k = k_ref[0, 0, :, hh * LANES:(hh + 1) * LANES]
        s = lax.dot_general(k, q, (((1,), (1,)), ((), ())), preferred_element_type=F32)
        if band:
            s = s + bias_ref[qi_ref[pair] - kj_ref[pair]]
        return s

    @pl.when(p_id == 0)
    def _():
        for hh in range(2):
            s_ref[0, hh] = scores(q0_ref, k0_ref, hh, 0)

    @pl.when(first)
    def _():
        m_ref[...] = jnp.full(m_ref.shape, NEG_INF, F32)
        acc_ref[...] = jnp.zeros(acc_ref.shape, F32)

    nxt = jnp.minimum(p_id + 1, n_pairs - 1)

    def step(masked, slot):
        if masked:
            valid = lax.broadcasted_iota(jnp.int32, (T, T), 0) <= lax.broadcasted_iota(jnp.int32, (T, T), 1)
        for hh in range(2):
            s_ref[1 - slot, hh] = scores(qn_ref, kn_ref, hh, nxt)
            st = s_ref[slot, hh]
            if masked:
                st = jnp.where(valid, st, NEG_INF)
            m_prev = m_ref[hh]
            m_new = jnp.maximum(m_prev, jnp.max(st, axis=0, keepdims=True))
            alpha = jnp.exp2(m_prev - m_new)
            pt = jnp.exp2(st - m_new).astype(BF16)
            acc_ref[hh] = acc_ref[hh] * alpha + jnp.dot(vt_ref[0, hh], pt, preferred_element_type=F32)
            m_ref[hh] = m_new

    parity = p_id % 2
    for slot in range(2):
        here = parity == slot
        if band:
            pl.when(here)(functools.partial(step, False, slot))
        else:
            pl.when(jnp.logical_and(here, last))(functools.partial(step, True, slot))
            pl.when(jnp.logical_and(here, jnp.logical_not(last)))(functools.partial(step, False, slot))

    @pl.when(last)
    def _():
        outs = []
        for hh in range(2):
            a = acc_ref[hh]
            outs.append(a[0:V_DIM, :] / a[V_DIM:V_DIM + 1, :])
        o_ref[0] = jnp.concatenate(outs, axis=0).T.astype(o_ref.dtype)


def _band_bias(T, W):
    d = (jnp.arange(W + 1)[:, None, None] * T + jnp.arange(T)[None, None, :] - jnp.arange(T)[None, :, None])
    cw = jnp.zeros(d.shape, F32)
    for window, dil in DIL_PATTERNS:
        cw = cw + ((d >= 0) & (d <= window) & (d % dil == 0)).astype(F32)
    return jnp.log2(cw)


def _flash(q, k, vt, band, T=FLASH_TILE):
    B, HP, S, _ = q.shape
    H = 2 * HP
    nq = S // T
    W = DIL_MAX_WINDOW // T if band else 0
    if band:
        pairs = [(i, i - j) for i in range(nq) for j in range(min(W, i) + 1)]
    else:
        pairs = [(i, j) for i in range(nq) for j in range(i + 1)]
    n = len(pairs)
    qi = jnp.asarray(np.array([p[0] for p in pairs], np.int32))
    kj = jnp.asarray(np.array([p[1] for p in pairs], np.int32))
    nxt = lambda p: jnp.minimum(p + 1, n - 1)
    in_specs = [pl.BlockSpec((1, 1, T, 2 * LANES), lambda b, h, p, qi, kj: (b, h, qi[nxt(p)], 0)),
                pl.BlockSpec((1, 1, T, 2 * LANES), lambda b, h, p, qi, kj: (b, h, kj[nxt(p)], 0)),
                pl.BlockSpec((1, 1, T, 2 * LANES), lambda b, h, p, qi, kj: (b, h, qi[0], 0)),
                pl.BlockSpec((1, 1, T, 2 * LANES), lambda b, h, p, qi, kj: (b, h, kj[0], 0)),
                pl.BlockSpec((1, 2, V_ROWS, T), lambda b, h, p, qi, kj: (b, h, 0, kj[p]))]
    args = [q, k, q, k, vt]
    if band:
        in_specs.append(_const_spec((W + 1, T, T), lambda b, h, p, qi, kj: (0, 0, 0)))
        args.append(_band_bias(T, W))
    return pl.pallas_call(
        functools.partial(_flash_kernel, band=band, T=T, W=W, n_pairs=n),
        grid_spec=pltpu.PrefetchScalarGridSpec(
            num_scalar_prefetch=2,
            grid=(B, H // 2, n),
            in_specs=in_specs,
            out_specs=pl.BlockSpec((1, T, LANES), lambda b, h, p, qi, kj: (b, qi[p], h)),
            scratch_shapes=[pltpu.VMEM((2, 1, T), F32), pltpu.VMEM((2, V_ROWS, T), F32),
                            pltpu.VMEM((2, 2, T, T), F32)]),
        out_shape=jax.ShapeDtypeStruct((B, S, H * V_DIM), BF16),
        compiler_params=_cparams(("arbitrary", "arbitrary", "arbitrary"), 52),
        name="flash_band" if band else "flash_causal",
    )(qi, kj, *args)


def _mlstm_kernel(qk_ref, v_ref, o_ref, if_ref, cw_ref, cb_ref, gb_ref, hg_ref, y_ref,
                  xs_ref, c_ref, n_ref, m_ref, *, Lc):
    H, DK, DV = MLSTM_HEADS, MLSTM_QK, MLSTM_V
    HALO = 8

    @pl.when(pl.program_id(1) == 0)
    def _():
        xs_ref[0:HALO, :] = jnp.zeros((HALO, 2 * H * DK), F32)
        c_ref[...] = jnp.zeros(c_ref.shape, F32)
        n_ref[...] = jnp.zeros(n_ref.shape, F32)
        m_ref[...] = jnp.zeros(m_ref.shape, F32)

    x = qk_ref[...]
    xs_ref[HALO:HALO + Lc, :] = x
    y = cb_ref[0]
    for j in range(CONV_K):
        y = y + cw_ref[0, j:j + 1, :] * xs_ref[pl.ds(HALO - (CONV_K - 1) + j, Lc), :]
    xs_ref[0:HALO, :] = x[Lc - HALO:, :]
    qk = y * _sigmoid(y)

    g = if_ref[...] + gb_ref[0]
    logf = jnp.minimum(g, 0.0) - jnp.log(1.0 + jnp.exp(-jnp.abs(g)))
    row = lax.broadcasted_iota(jnp.int32, (Lc, Lc), 0)
    col = lax.broadcasted_iota(jnp.int32, (Lc, Lc), 1)
    causal = col <= row
    tri = jnp.where(causal, 1.0, 0.0).astype(F32)
    bcum = jnp.dot(tri, logf, preferred_element_type=F32, precision=HIGHEST)
    g_t = g.T
    bcum_t = bcum.T

    for h in range(H):
        q = qk[:, h * DK:(h + 1) * DK]
        k = qk[:, (H + h) * DK:(H + h + 1) * DK] * (DK ** -0.5)
        v = v_ref[:, h * DV:(h + 1) * DV]
        bt = bcum[:, H + h:H + h + 1]
        ig_c = g[:, h:h + 1]
        bs = bcum_t[H + h:H + h + 1, :]
        ig_r = g_t[h:h + 1, :]
        m_prev = m_ref[h:h + 1, 0:1]
        dlog = jnp.where(causal, bt - bs + ig_r, NEG_INF)
        inter = bt + m_prev
        m_t = jnp.maximum(inter, jnp.max(dlog, axis=1, keepdims=True))
        qb = q.astype(BF16)
        s = lax.dot_general(qb, k.astype(BF16), (((1,), (1,)), ((), ())), preferred_element_type=F32)
        sc = s * jnp.exp(dlog - m_t)
        decay = jnp.exp(inter - m_t)
        c_h = c_ref[h]
        n_h = n_ref[h:h + 1, :]
        num = (decay * jnp.dot(qb, c_h.astype(BF16), preferred_element_type=F32)
               + jnp.dot(sc.astype(BF16), v, preferred_element_type=F32))
        den = decay * jnp.sum(q * n_h, axis=1, keepdims=True) + jnp.sum(sc, axis=1, keepdims=True)
        hh = num / jnp.maximum(jnp.abs(den), jnp.exp(-m_t))
        hn = hh * lax.rsqrt(jnp.mean(hh * hh, axis=-1, keepdims=True) + EPS) * hg_ref[0, :, h * DV:(h + 1) * DV]
        o_gate = _sigmoid(o_ref[:, h * DV:(h + 1) * DV].astype(F32))
        y_ref[:, h * DV:(h + 1) * DV] = (hn * o_gate).astype(y_ref.dtype)

        b_end = bt[Lc - 1:Lc, :]
        wlog = b_end - bt + ig_c
        m_new = jnp.maximum(b_end + m_prev, jnp.max(wlog, axis=0, keepdims=True))
        cd = jnp.exp(b_end + m_prev - m_new)
        kw = k * jnp.exp(wlog - m_new)
        c_ref[h] = cd * c_h + lax.dot_general(kw.astype(BF16), v, (((0,), (0,)), ((), ())),
                                              preferred_element_type=F32)
        n_ref[h:h + 1, :] = cd * n_h + jnp.sum(kw, axis=0, keepdims=True)
        m_ref[h:h + 1, :] = jnp.broadcast_to(m_new, (1, LANES))


def _mlstm(mqk, mv, mo, mif, B, S, layer, conv_w, conv_b, gate_b, head_gain, Lc=256):
    H, DK, DV = MLSTM_HEADS, MLSTM_QK, MLSTM_V
    nc = S // Lc
    row = lambda b, c: (b * nc + c, 0)
    return pl.pallas_call(
        functools.partial(_mlstm_kernel, Lc=Lc),
        grid=(B, nc),
        in_specs=[pl.BlockSpec((Lc, 2 * H * DK), row),
                  pl.BlockSpec((Lc, H * DV), row),
                  pl.BlockSpec((Lc, H * DV), row),
                  pl.BlockSpec((Lc, LANES), row),
                  _const_spec((1, CONV_K, 2 * H * DK), lambda b, c: (layer, 0, 0)),
                  _const_spec((1, 1, 2 * H * DK), lambda b, c: (layer, 0, 0)),
                  _const_spec((1, 1, LANES), lambda b, c: (layer, 0, 0)),
                  _const_spec((1, 1, H * DV), lambda b, c: (layer, 0, 0))],
        out_specs=pl.BlockSpec((Lc, H * DV), row),
        out_shape=jax.ShapeDtypeStruct((B * S, H * DV), BF16),
        scratch_shapes=[pltpu.VMEM((Lc + 8, 2 * H * DK), F32), pltpu.VMEM((H, DK, DV), F32),
                        pltpu.VMEM((8, DK), F32), pltpu.VMEM((8, LANES), F32)],
        compiler_params=_cparams(("arbitrary", "arbitrary"), 32),
        name="mlstm",
    )(mqk, mv, mo, mif, conv_w, conv_b, gate_b, head_gain)


def _merge_kernel(ya_ref, yb_ref, yc_ref, g_ref, x_ref, ga_ref, wa_ref, wb_ref, wc_ref, wo_ref, out_ref):
    def branch(y_ref, w_ref, c):
        gate = _sigmoid(g_ref[:, c * D_MODEL:(c + 1) * D_MODEL].astype(F32))
        return gate * jnp.dot(y_ref[...], w_ref[0], preferred_element_type=F32)

    merged = branch(ya_ref, wa_ref, 0) + branch(yb_ref, wb_ref, 1) + branch(yc_ref, wc_ref, 2)
    y = jnp.dot(merged.astype(BF16), wo_ref[0], preferred_element_type=F32)
    out_ref[...] = x_ref[...] + ga_ref[0] * y


def _merge(ya, yb, yc, gates, x2, S, layer, ga, wa, wb, wc, wo, tm=512):
    N, D = x2.shape
    per_b = S // tm
    row = lambda i: (i, 0)
    wsel = lambda i: (layer, 0, 0)
    return pl.pallas_call(
        _merge_kernel,
        grid=(N // tm,),
        in_specs=[pl.BlockSpec((tm, 512), row), pl.BlockSpec((tm, 512), row), pl.BlockSpec((tm, 512), row),
                  pl.BlockSpec((tm, N_BRANCH * D), row), pl.BlockSpec((tm, D), row),
                  pl.BlockSpec((1, 1, D), lambda i: (i // per_b, 0, 0)),
                  _const_spec((1, 512, D), wsel), _const_spec((1, 512, D), wsel), _const_spec((1, 512, D), wsel),
                  _const_spec((1, D, D), wsel)],
        out_specs=pl.BlockSpec((tm, D), row),
        out_shape=jax.ShapeDtypeStruct((N, D), F32),
        compiler_params=_cparams(("parallel",), 48),
        name="merge",
    )(ya, yb, yc, gates, x2, ga, wa, wb, wc, wo)


def _ffn_kernel(x_ref, gain_ref, sc_ref, sh_ref, gf_ref, wr_ref, br_ref, wg_ref, wu_ref, wd_ref, out_ref):
    x = x_ref[...]
    tm = x.shape[0]
    ms = jnp.mean(x * x, axis=-1, keepdims=True)
    hf = x * lax.rsqrt(ms + EPS) * gain_ref[0] * (1.0 + sc_ref[0]) + sh_ref[0]

    logits = jnp.dot(hf, wr_ref[0], preferred_element_type=F32, precision=HIGHEST) + br_ref[0]
    lane_i = lax.broadcasted_iota(jnp.int32, (tm, LANES), 1)
    lane = lane_i.astype(F32)
    big = float(LANES)

    def first_argmax(z):
        zmax = jnp.max(z, axis=1, keepdims=True)
        return zmax, jnp.min(jnp.where(z == zmax, lane, big), axis=1, keepdims=True)

    lg = jnp.where(lane_i < N_GROUPS, logits, NEG_INF)
    gmax, g_idx = first_argmax(lg)
    p_sel = 1.0 / jnp.sum(jnp.exp(lg - gmax), axis=1, keepdims=True)
    lane_grp = ((lane_i - N_GROUPS) // EXPERTS_PER_GROUP).astype(F32)
    in_grp = (lane_i >= N_GROUPS) & (lane_i < N_GROUPS + N_EXPERTS) & (lane_grp == g_idx)
    le = jnp.where(in_grp, logits, NEG_INF)
    v1, i1 = first_argmax(le)
    le2 = jnp.where(lane == i1, NEG_INF, le)
    v2, i2 = first_argmax(le2)
    e2 = jnp.exp(v2 - v1)
    den = 1.0 + e2
    comb = jnp.where(lane == i1, (1.0 / den) * p_sel, 0.0) + jnp.where(lane == i2, (e2 / den) * p_sel, 0.0)

    hb = hf.astype(BF16)
    acc = jnp.zeros((tm, D_MODEL), F32)
    for e in range(N_EXPERTS):
        gate = jnp.dot(hb, wg_ref[0, e], preferred_element_type=F32)
        up = jnp.dot(hb, wu_ref[0, e], preferred_element_type=F32)
        hid = gate * _sigmoid(gate) * up * comb[:, N_GROUPS + e:N_GROUPS + e + 1]
        acc = acc + jnp.dot(hid.astype(BF16), wd_ref[0, e], preferred_element_type=F32)
    out_ref[...] = x + gf_ref[0] * acc


def _ffn(x2, S, layer, gain, sc, sh, gf, wr, br, wg, wu, wd, tm=512):
    N, D = x2.shape
    per_b = S // tm
    row = lambda i: (i, 0)
    bsel = lambda i: (i // per_b, 0, 0)
    wsel = lambda i: (layer, 0, 0, 0)
    return pl.pallas_call(
        _ffn_kernel,
        grid=(N // tm,),
        in_specs=[pl.BlockSpec((tm, D), row),
                  _const_spec((1, 1, D), lambda i: (layer, 0, 0)),
                  pl.BlockSpec((1, 1, D), bsel), pl.BlockSpec((1, 1, D), bsel), pl.BlockSpec((1, 1, D), bsel),
                  _const_spec((1, D, LANES), lambda i: (layer, 0, 0)),
                  _const_spec((1, 1, LANES), lambda i: (layer, 0, 0)),
                  _const_spec((1, N_EXPERTS, D, EXPERT_FF), wsel),
                  _const_spec((1, N_EXPERTS, D, EXPERT_FF), wsel),
                  _const_spec((1, N_EXPERTS, EXPERT_FF, D), wsel)],
        out_specs=pl.BlockSpec((tm, D), row),
        out_shape=jax.ShapeDtypeStruct((N, D), F32),
        compiler_params=_cparams(("parallel",), 56),
        name="ffn",
    )(x2, gain, sc, sh, gf, wr, br, wg, wu, wd)


def _pad_last(a, n):
    return jnp.pad(a, [(0, 0)] * (a.ndim - 1) + [(0, n - a.shape[-1])])


def _head_tiles(a, heads, dim):
    a = a.reshape(a.shape[:-1] + (heads, dim))
    return _pad_last(a, LANES).reshape(a.shape[:-2] + (heads * LANES,))


def _prep_w_in(w_in):
    pts = np.cumsum(IN_SIZES)[:-1].tolist()
    cq, ckv, kr, m_qk, m_v, m_o, m_i, m_f, d_q, d_k, d_v, gates = jnp.split(w_in, pts, axis=-1)
    seg_a = _pad_last(jnp.concatenate([cq, ckv, kr], axis=-1), 512)
    seg_if = _pad_last(jnp.concatenate([m_i, m_f], axis=-1), LANES)
    w = jnp.concatenate([seg_a, m_qk, m_v, m_o, seg_if,
                         _head_tiles(d_q, DIL_HEADS, DIL_HEAD_DIM), _head_tiles(d_k, DIL_HEADS, DIL_HEAD_DIM),
                         d_v, gates], axis=-1)
    assert w.shape[-1] == COL_END
    return w.astype(BF16)


def kernel(x, c, w_ada, b_ada, attn_norm, w_in, mla_q_norm, mla_kv_norm, mla_w_uq, mla_w_ukv, mla_q_gain, mla_k_gain, mlstm_conv_w, mlstm_conv_b, mlstm_b_i, mlstm_b_f, mlstm_head_gain, dil_q_gain, dil_k_gain, w_branch_a, w_branch_b, w_branch_c, w_out, ffn_norm, w_router_group, b_router_group, w_router_expert, b_router_expert, w_exp_gate, w_exp_up, w_exp_down):
    B, S, D = x.shape
    L = w_ada.shape[0]
    N = B * S
    assert D == D_MODEL and S % DIL_MAX_WINDOW == 0

    mod = _adaln(c, w_ada, b_ada)

    w_pad = _prep_w_in(w_in)
    half = MLA_ROPE // 2
    swap_rope = lambda a: jnp.concatenate(
        [jnp.zeros_like(a[..., :MLA_NOPE]), a[..., MLA_NOPE + half:], a[..., MLA_NOPE:MLA_NOPE + half]], axis=-1)
    uq = mla_w_uq.reshape(L, MLA_Q_LORA, MLA_HEADS, MLA_QK)
    wuq = jnp.concatenate([_pad_last(u, LANES).reshape(L, MLA_Q_LORA, MLA_HEADS * LANES)
                           for u in (uq, swap_rope(uq))], axis=-1).astype(BF16)
    ukv = mla_w_ukv.reshape(L, MLA_KV_LORA, MLA_HEADS, MLA_NOPE + MLA_V)
    wkv = jnp.concatenate([_pad_last(ukv[..., :MLA_NOPE], LANES).reshape(L, MLA_KV_LORA, MLA_HEADS * LANES),
                           ukv[..., MLA_NOPE:].reshape(L, MLA_KV_LORA, MLA_HEADS * MLA_V)], axis=-1).astype(BF16)
    rows = lambda a: a[:, None, :]
    qg = rows(_pad_last(mla_q_gain, LANES))
    qgs = rows(_pad_last(swap_rope(mla_q_gain), LANES))
    kg = rows(_pad_last(mla_k_gain, LANES))
    dqg = rows(_pad_last(dil_q_gain, LANES))
    dkg = rows(_pad_last(dil_k_gain, LANES))
    gate_b = rows(_pad_last(jnp.concatenate([mlstm_b_i, mlstm_b_f], axis=-1), LANES))
    tables = _rope_tables(S)
    wa, wb, wc, wo = (w.astype(BF16) for w in (w_branch_a, w_branch_b, w_branch_c, w_out))
    wr = _pad_last(jnp.concatenate([w_router_group, w_router_expert], axis=-1), LANES)
    br = rows(_pad_last(jnp.concatenate([b_router_group, b_router_expert], axis=-1), LANES))
    wg, wu, wd = (w.astype(BF16) for w in (w_exp_gate, w_exp_up, w_exp_down))
    attn_norm, ffn_norm, mla_q_norm, mla_kv_norm, mlstm_conv_b, mlstm_head_gain = (
        rows(a) for a in (attn_norm, ffn_norm, mla_q_norm, mla_kv_norm, mlstm_conv_b, mlstm_head_gain))

    x2 = x.reshape(N, D)
    for l in range(L):
        sh_a, sc_a, g_a, sh_f, sc_f, g_f = (mod[l, :, i] for i in range(6))
        oa, mqk, mv, mo, mif, qd, kd, vd, gates = _inproj(x2, S, l, attn_norm, sc_a, sh_a, w_pad, dqg, dkg)
        q, k, vt = _mla_prep(oa, S, l, mla_q_norm, mla_kv_norm, wuq, wkv, qg, qgs, kg, tables)
        ya = _flash(q, k, vt, band=False)
        yb = _mlstm(mqk, mv, mo, mif, B, S, l, mlstm_conv_w, mlstm_conv_b, gate_b, mlstm_head_gain)
        yc = _flash(qd, kd, vd, band=True)
        x2 = _merge(ya.reshape(N, -1), yb, yc.reshape(N, -1), gates, x2, S, l, g_a, wa, wb, wc, wo)
        x2 = _ffn(x2, S, l, ffn_norm, sc_f, sh_f, g_f, wr, br, wg, wu, wd)
    return x2.reshape(B, S, D)
```

```python
import functools

import jax
import jax.numpy as jnp
import numpy as np
from jax import lax
from jax.experimental import pallas as pl
from jax.experimental.pallas import tpu as pltpu

F32 = jnp.float32
BF16 = jnp.bfloat16
HIGHEST = lax.Precision.HIGHEST
NEG_INF = float("-inf")

EPS = 1e-6
LANES = 128
VMEM_BYTES_V7X = 64 * 1024 * 1024

D_MODEL = 1024
MLA_HEADS = 8
MLA_Q_LORA = 256
MLA_KV_LORA = 128
MLA_NOPE = 64
MLA_ROPE = 32
MLA_V = 64
MLA_QK = MLA_NOPE + MLA_ROPE
ROPE_THETA = 10000.0
MLSTM_HEADS = 4
MLSTM_QK = 64
MLSTM_V = 128
CONV_K = 4
DIL_HEADS = 8
DIL_HEAD_DIM = 64
DIL_PATTERNS = ((128, 1), (512, 4), (2048, 16))
DIL_WIDTH = DIL_HEADS * DIL_HEAD_DIM
N_GROUPS = 4
EXPERTS_PER_GROUP = 4
N_EXPERTS = N_GROUPS * EXPERTS_PER_GROUP
EXPERT_FF = 256
N_BRANCH = 3
IN_SIZES = (MLA_Q_LORA, MLA_KV_LORA, MLA_ROPE,
            2 * MLSTM_HEADS * MLSTM_QK, MLSTM_HEADS * MLSTM_V, MLSTM_HEADS * MLSTM_V, MLSTM_HEADS, MLSTM_HEADS,
            DIL_WIDTH, DIL_WIDTH, DIL_WIDTH, N_BRANCH * D_MODEL)

COL_A = 0
COL_MQK = 512
COL_MV = 1024
COL_MO = 1536
COL_MIF = 2048
COL_DQ = 2176
COL_DK = 3200
COL_DV = 4224
COL_G = 4736
COL_END = 7808

FLASH_TILE = 1024
DIL_MAX_WINDOW = max(w for w, _ in DIL_PATTERNS)
V_DIM = 64
V_FILL = 16
V_ROWS = V_DIM + V_FILL
LOG2E = 1.4426950408889634
assert MLA_V == V_DIM and DIL_HEAD_DIM == V_DIM


def _store_v_tiles_t(v, vt_ref, dim=V_DIM):
    tm = v.shape[0]
    v_t = v.astype(BF16).T
    fill = (lax.broadcasted_iota(jnp.int32, (V_FILL, tm), 0) == 0).astype(BF16)
    for h in range(v.shape[1] // dim):
        vt_ref[0, h, 0:dim, :] = v_t[h * dim:(h + 1) * dim, :]
        vt_ref[0, h, dim:dim + V_FILL, :] = fill


def _cparams(sem, vmem_mb):
    return pltpu.CompilerParams(dimension_semantics=sem, vmem_limit_bytes=vmem_mb * 1024 * 1024)


def _sigmoid(x):
    return 1.0 / (1.0 + jnp.exp(-x))


def _const_spec(shape, index_map):
    return pl.BlockSpec(shape, index_map, pipeline_mode=pl.Buffered(1))


def _adaln_kernel(c_ref, w_ref, b_ref, o_ref):
    c = c_ref[...]
    ca = c * _sigmoid(c)
    o_ref[0] = jnp.dot(ca, w_ref[0], preferred_element_type=F32, precision=HIGHEST) + b_ref[0]


def _adaln(c, w_ada, b_ada):
    L, D, D6 = w_ada.shape
    B = c.shape[0]
    rows = 8
    cp = jnp.zeros((rows, D), F32).at[:B].set(c)
    out = pl.pallas_call(
        _adaln_kernel,
        grid=(L, D6 // D),
        in_specs=[pl.BlockSpec((rows, D), lambda l, j: (0, 0)),
                  pl.BlockSpec((1, D, D), lambda l, j: (l, 0, j)),
                  pl.BlockSpec((1, 1, D), lambda l, j: (l, 0, j))],
        out_specs=pl.BlockSpec((1, rows, D), lambda l, j: (l, 0, j)),
        out_shape=jax.ShapeDtypeStruct((L, rows, D6), F32),
        compiler_params=_cparams(("parallel", "parallel"), 32),
        name="adaln",
    )(cp, w_ada, b_ada.reshape(L, 1, D6))
    return out[:, :B].reshape(L, B, D6 // D, 1, D)


def _inproj_kernel(x_ref, gain_ref, sc_ref, sh_ref, w_ref, dqg_ref, dkg_ref,
                   oa_ref, mqk_ref, mv_ref, mo_ref, mif_ref, qd_ref, kd_ref, vd_ref, g_ref):
    x = x_ref[...]
    ms = jnp.mean(x * x, axis=-1, keepdims=True)
    y = x * lax.rsqrt(ms + EPS) * gain_ref[0]
    u = (y * (1.0 + sc_ref[0]) + sh_ref[0]).astype(BF16)

    def proj(a, b):
        return jnp.dot(u, w_ref[0, :, a:b], preferred_element_type=F32)

    oa_ref[...] = proj(COL_A, COL_MQK)
    mqk_ref[...] = proj(COL_MQK, COL_MV)
    _store_v_tiles_t(proj(COL_MV, COL_MO), mv_ref, MLSTM_V)
    mo_ref[...] = proj(COL_MO, COL_MIF).astype(BF16)
    mif_ref[...] = proj(COL_MIF, COL_DQ)

    def head_norm(z, g):
        for h in range(DIL_HEADS):
            zh = z[:, h * LANES:(h + 1) * LANES]
            r = lax.rsqrt(jnp.sum(zh * zh, axis=-1, keepdims=True) / DIL_HEAD_DIM + EPS)
            yield h, zh * r * g

    dq = proj(COL_DQ, COL_DK)
    for h, zn in head_norm(dq, dqg_ref[0]):
        qd_ref[0, h // 2, :, (h % 2) * LANES:(h % 2 + 1) * LANES] = (zn * (DIL_HEAD_DIM ** -0.5 * LOG2E)).astype(BF16)
    dk = proj(COL_DK, COL_DV)
    for h, zn in head_norm(dk, dkg_ref[0]):
        kd_ref[0, h // 2, :, (h % 2) * LANES:(h % 2 + 1) * LANES] = zn.astype(BF16)
    _store_v_tiles_t(proj(COL_DV, COL_G), vd_ref)
    for c in range(N_BRANCH):
        g_ref[:, c * D_MODEL:(c + 1) * D_MODEL] = proj(COL_G + c * D_MODEL, COL_G + (c + 1) * D_MODEL).astype(BF16)


def _inproj(x2, S, layer, gain, sc, sh, w_pad, dqg, dkg, tm=256):
    N, D = x2.shape
    per_b = S // tm
    row = lambda i: (i, 0)
    bsel = lambda i: (i // per_b, 0, 0)
    B = N // S
    outs = [(512, F32), (512, F32), "mvt", (512, BF16), (LANES, F32),
            "qk", "qk", "vt", (3072, BF16)]
    head_major = lambda i: (i // per_b, 0, i % per_b, 0)
    time_minor = lambda i: (i // per_b, 0, 0, i % per_b)
    special_spec = {"qk": pl.BlockSpec((1, DIL_HEADS // 2, tm, 2 * LANES), head_major),
                    "vt": pl.BlockSpec((1, DIL_HEADS, V_ROWS, tm), time_minor),
                    "mvt": pl.BlockSpec((1, MLSTM_HEADS, MLSTM_V + V_FILL, tm), time_minor)}
    special_shape = {"qk": jax.ShapeDtypeStruct((B, DIL_HEADS // 2, S, 2 * LANES), BF16),
                     "vt": jax.ShapeDtypeStruct((B, DIL_HEADS, V_ROWS, S), BF16),
                     "mvt": jax.ShapeDtypeStruct((B, MLSTM_HEADS, MLSTM_V + V_FILL, S), BF16)}
    return pl.pallas_call(
        _inproj_kernel,
        grid=(N // tm,),
        in_specs=[pl.BlockSpec((tm, D), row),
                  _const_spec((1, 1, D), lambda i: (layer, 0, 0)),
                  pl.BlockSpec((1, 1, D), bsel),
                  pl.BlockSpec((1, 1, D), bsel),
                  _const_spec((1, D, COL_END), lambda i: (layer, 0, 0)),
                  _const_spec((1, 1, LANES), lambda i: (layer, 0, 0)),
                  _const_spec((1, 1, LANES), lambda i: (layer, 0, 0))],
        out_specs=[special_spec[o] if isinstance(o, str) else pl.BlockSpec((tm, o[0]), row) for o in outs],
        out_shape=[special_shape[o] if isinstance(o, str) else jax.ShapeDtypeStruct((N, o[0]), o[1]) for o in outs],
        compiler_params=_cparams(("parallel",), 52),
        name="inproj",
    )(x2, gain, sc, sh, w_pad, dqg, dkg)


def _mla_prep_kernel(a_ref, qn_ref, kvn_ref, wuq_ref, wkv_ref, qg_ref, qgs_ref, kg_ref, cos_ref, sin_ref,
                     q_ref, k_ref, vt_ref):
    HW = MLA_HEADS * LANES
    a = a_ref[...]
    cq = a[:, :MLA_Q_LORA]
    ckv = a[:, MLA_Q_LORA:MLA_Q_LORA + MLA_KV_LORA]
    kr_tile = a[:, MLA_Q_LORA + MLA_KV_LORA:]

    def rms(z, g):
        return z * lax.rsqrt(jnp.mean(z * z, axis=-1, keepdims=True) + EPS) * g

    cqn = rms(cq, qn_ref[0]).astype(BF16)
    ckvn = rms(ckv, kvn_ref[0]).astype(BF16)
    qr = jnp.dot(cqn, wuq_ref[0], preferred_element_type=F32)
    kvr = jnp.dot(ckvn, wkv_ref[0], preferred_element_type=F32)
    _store_v_tiles_t(kvr[:, HW:], vt_ref)

    cosf, sins = cos_ref[...], sin_ref[...]
    q_scale = MLA_QK ** -0.5 * LOG2E
    gc = cosf * (qg_ref[0] * q_scale)
    gs = sins * (qgs_ref[0] * q_scale)
    kg = kg_ref[0]
    half = MLA_ROPE // 2
    krg = pltpu.roll(kr_tile, MLA_NOPE, 1) * kg
    kr_rot = krg * cosf + (pltpu.roll(krg, LANES - half, 1) + pltpu.roll(krg, half, 1)) * sins
    kr_ss = jnp.sum(kr_tile * kr_tile, axis=-1, keepdims=True)
    for h in range(MLA_HEADS):
        sl = slice(h * LANES, (h + 1) * LANES)
        pair_sl = slice((h % 2) * LANES, (h % 2 + 1) * LANES)
        z = qr[:, sl]
        rq = lax.rsqrt(jnp.sum(z * z, axis=-1, keepdims=True) / MLA_QK + EPS)
        q_ref[0, h // 2, :, pair_sl] = ((z * gc + qr[:, HW + h * LANES:HW + (h + 1) * LANES] * gs) * rq).astype(BF16)
        zk = kvr[:, sl]
        rk = lax.rsqrt((jnp.sum(zk * zk, axis=-1, keepdims=True) + kr_ss) / MLA_QK + EPS)
        k_ref[0, h // 2, :, pair_sl] = ((zk * kg + kr_rot) * rk).astype(BF16)


def _rope_tables(S):
    half = MLA_ROPE // 2
    inv = ROPE_THETA ** (-jnp.arange(half, dtype=F32) / half)
    ang = jnp.arange(S).astype(F32)[:, None] * inv[None, :]
    cos, sin = jnp.cos(ang), jnp.sin(ang)
    one = jnp.ones((S, MLA_NOPE), F32)
    z = lambda n: jnp.zeros((S, n), F32)
    tail = LANES - MLA_QK
    cosf = jnp.concatenate([one, cos, cos, z(tail)], axis=1)
    sins = jnp.concatenate([z(MLA_NOPE), -sin, sin, z(tail)], axis=1)
    return cosf, sins


def _mla_prep(oa, S, layer, qn, kvn, wuq, wkv, qg, qgs, kg, tables, tm=512):
    N = oa.shape[0]
    B = N // S
    per_b = S // tm
    row = lambda i: (i, 0)
    pos = lambda i: (i % per_b, 0)
    HW = MLA_HEADS * LANES
    return pl.pallas_call(
        _mla_prep_kernel,
        grid=(N // tm,),
        in_specs=[pl.BlockSpec((tm, 512), row),
                  _const_spec((1, 1, MLA_Q_LORA), lambda i: (layer, 0, 0)),
                  _const_spec((1, 1, MLA_KV_LORA), lambda i: (layer, 0, 0)),
                  _const_spec((1, MLA_Q_LORA, 2 * HW), lambda i: (layer, 0, 0)),
                  _const_spec((1, MLA_KV_LORA, HW + MLA_HEADS * MLA_V), lambda i: (layer, 0, 0)),
                  _const_spec((1, 1, LANES), lambda i: (layer, 0, 0)),
                  _const_spec((1, 1, LANES), lambda i: (layer, 0, 0)),
                  _const_spec((1, 1, LANES), lambda i: (layer, 0, 0)),
                  pl.BlockSpec((tm, LANES), pos),
                  pl.BlockSpec((tm, LANES), pos)],
        out_specs=[pl.BlockSpec((1, MLA_HEADS // 2, tm, 2 * LANES), lambda i: (i // per_b, 0, i % per_b, 0)),
                   pl.BlockSpec((1, MLA_HEADS // 2, tm, 2 * LANES), lambda i: (i // per_b, 0, i % per_b, 0)),
                   pl.BlockSpec((1, MLA_HEADS, V_ROWS, tm), lambda i: (i // per_b, 0, 0, i % per_b))],
        out_shape=[jax.ShapeDtypeStruct((B, MLA_HEADS // 2, S, 2 * LANES), BF16),
                   jax.ShapeDtypeStruct((B, MLA_HEADS // 2, S, 2 * LANES), BF16),
                   jax.ShapeDtypeStruct((B, MLA_HEADS, V_ROWS, S), BF16)],
        compiler_params=_cparams(("parallel",), 32),
        name="mla_prep",
    )(oa, qn, kvn, wuq, wkv, qg, qgs, kg, *tables)


def _flash_kernel(qi_ref, kj_ref, qn_ref, kn_ref, q0_ref, k0_ref, vt_ref, *rest, band, T, W, n_pairs):
    if band:
        bias_ref, o_ref, m_ref, acc_ref, s_ref = rest
    else:
        o_ref, m_ref, acc_ref, s_ref = rest
    p_id = pl.program_id(2)
    qi = qi_ref[p_id]
    kj = kj_ref[p_id]
    if band:
        first = kj == qi
        last = kj == jnp.maximum(qi - W, 0)
    else:
        first = kj == 0
        last = kj == qi

    def scores(q_ref, k_ref, hh, pair):
        q = q_ref[0, 0, :, hh * LANES:(hh + 1) * LANES]
        k = k_ref[0, 0, :, hh * LANES:(hh + 1) * LANES]
        s = lax.dot_general(k, q, (((1,), (1,)), ((), ())), preferred_element_type=F32)
        if band:
            s = s + bias_ref[qi_ref[pair] - kj_ref[pair]]
        return s

    @pl.when(p_id == 0)
    def _():
        for hh in range(2):
            s_ref[0, hh] = scores(q0_ref, k0_ref, hh, 0)

    @pl.when(first)
    def _():
        m_ref[...] = jnp.full(m_ref.shape, NEG_INF, F32)
        acc_ref[...] = jnp.zeros(acc_ref.shape, F32)

    nxt = jnp.minimum(p_id + 1, n_pairs - 1)

    def step(masked, slot):
        if masked:
            valid = lax.broadcasted_iota(jnp.int32, (T, T), 0) <= lax.broadcasted_iota(jnp.int32, (T, T), 1)
        for hh in range(2):
            s_ref[1 - slot, hh] = scores(qn_ref, kn_ref, hh, nxt)
            st = s_ref[slot, hh]
            if masked:
                st = jnp.where(valid, st, NEG_INF)
            m_prev = m_ref[hh]
            m_new = jnp.maximum(m_prev, jnp.max(st, axis=0, keepdims=True))
            alpha = jnp.exp2(m_prev - m_new)
            pt = jnp.exp2(st - m_new).astype(BF16)
            acc_ref[hh] = acc_ref[hh] * alpha + jnp.dot(vt_ref[0, hh], pt, preferred_element_type=F32)
            m_ref[hh] = m_new

    parity = p_id % 2
    for slot in range(2):
        here = parity == slot
        if band:
            pl.when(here)(functools.partial(step, False, slot))
        else:
            pl.when(jnp.logical_and(here, last))(functools.partial(step, True, slot))
            pl.when(jnp.logical_and(here, jnp.logical_not(last)))(functools.partial(step, False, slot))

    @pl.when(last)
    def _():
        outs = []
        for hh in range(2):
            a = acc_ref[hh]
            outs.append(a[0:V_DIM, :] / a[V_DIM:V_DIM + 1, :])
        o_ref[0] = jnp.concatenate(outs, axis=0).T.astype(o_ref.dtype)


def _band_bias(T, W):
    d = (jnp.arange(W + 1)[:, None, None] * T + jnp.arange(T)[None, None, :] - jnp.arange(T)[None, :, None])
    cw = jnp.zeros(d.shape, F32)
    for window, dil in DIL_PATTERNS:
        cw = cw + ((d >= 0) & (d <= window) & (d % dil == 0)).astype(F32)
    return jnp.log2(cw)


def _flash(q, k, vt, band, T=FLASH_TILE):
    B, HP, S, _ = q.shape
    H = 2 * HP
    nq = S // T
    W = DIL_MAX_WINDOW // T if band else 0
    if band:
        pairs = [(i, i - j) for i in range(nq) for j in range(min(W, i) + 1)]
    else:
        pairs = [(i, j) for i in range(nq) for j in range(i + 1)]
    n = len(pairs)
    qi = jnp.asarray(np.array([p[0] for p in pairs], np.int32))
    kj = jnp.asarray(np.array([p[1] for p in pairs], np.int32))
    nxt = lambda p: jnp.minimum(p + 1, n - 1)
    in_specs = [pl.BlockSpec((1, 1, T, 2 * LANES), lambda b, h, p, qi, kj: (b, h, qi[nxt(p)], 0)),
                pl.BlockSpec((1, 1, T, 2 * LANES), lambda b, h, p, qi, kj: (b, h, kj[nxt(p)], 0)),
                pl.BlockSpec((1, 1, T, 2 * LANES), lambda b, h, p, qi, kj: (b, h, qi[0], 0)),
                pl.BlockSpec((1, 1, T, 2 * LANES), lambda b, h, p, qi, kj: (b, h, kj[0], 0)),
                pl.BlockSpec((1, 2, V_ROWS, T), lambda b, h, p, qi, kj: (b, h, 0, kj[p]))]
    args = [q, k, q, k, vt]
    if band:
        in_specs.append(_const_spec((W + 1, T, T), lambda b, h, p, qi, kj: (0, 0, 0)))
        args.append(_band_bias(T, W))
    return pl.pallas_call(
        functools.partial(_flash_kernel, band=band, T=T, W=W, n_pairs=n),
        grid_spec=pltpu.PrefetchScalarGridSpec(
            num_scalar_prefetch=2,
            grid=(B, H // 2, n),
            in_specs=in_specs,
            out_specs=pl.BlockSpec((1, T, LANES), lambda b, h, p, qi, kj: (b, qi[p], h)),
            scratch_shapes=[pltpu.VMEM((2, 1, T), F32), pltpu.VMEM((2, V_ROWS, T), F32),
                            pltpu.VMEM((2, 2, T, T), F32)]),
        out_shape=jax.ShapeDtypeStruct((B, S, H * V_DIM), BF16),
        compiler_params=_cparams(("arbitrary", "arbitrary", "arbitrary"), 52),
        name="flash_band" if band else "flash_causal",
    )(qi, kj, *args)


def _mlstm_kernel(qk_ref, vt_ref, o_ref, if_ref, cw_ref, cb_ref, gb_ref, hg_ref, y_ref,
                  xs_ref, cn_ref, m_ref, *, Lc):
    H, DK, DV = MLSTM_HEADS, MLSTM_QK, MLSTM_V
    HALO = 8

    @pl.when(pl.program_id(1) == 0)
    def _():
        xs_ref[0:HALO, :] = jnp.zeros((HALO, 2 * H * DK), F32)
        cn_ref[...] = jnp.zeros(cn_ref.shape, F32)
        m_ref[...] = jnp.zeros(m_ref.shape, F32)

    x = qk_ref[...]
    xs_ref[HALO:HALO + Lc, :] = x
    y = cb_ref[0]
    for j in range(CONV_K):
        y = y + cw_ref[0, j:j + 1, :] * xs_ref[pl.ds(HALO - (CONV_K - 1) + j, Lc), :]
    xs_ref[0:HALO, :] = x[Lc - HALO:, :]
    qk = y * _sigmoid(y)
    q_t = qk[:, :H * DK].T.astype(BF16)
    k_all = (qk[:, H * DK:] * (DK ** -0.5)).astype(BF16)

    g = if_ref[...] + gb_ref[0]
    logf = jnp.minimum(g, 0.0) - jnp.log(1.0 + jnp.exp(-jnp.abs(g)))
    src = lax.broadcasted_iota(jnp.int32, (Lc, Lc), 0)
    dst = lax.broadcasted_iota(jnp.int32, (Lc, Lc), 1)
    tri = jnp.where(dst <= src, 1.0, 0.0).astype(F32)
    bcum = jnp.dot(tri, logf, preferred_element_type=F32, precision=HIGHEST)
    causal = src <= dst
    g_t = g.T
    bcum_t = bcum.T

    for h in range(H):
        b_row = bcum_t[H + h:H + h + 1, :]
        w_col = g[:, h:h + 1] - bcum[:, H + h:H + h + 1]
        w_row = g_t[h:h + 1, :] - b_row
        m_prev = m_ref[h:h + 1, 0:1]
        dlog = jnp.where(causal, w_col + b_row, NEG_INF)
        inter = b_row + m_prev
        m_t = jnp.maximum(inter, jnp.max(dlog, axis=0, keepdims=True))
        qh_t = q_t[h * DK:(h + 1) * DK, :]
        kh = k_all[:, h * DK:(h + 1) * DK]
        s_t = jnp.dot(kh, qh_t, preferred_element_type=F32)
        sc_t = (s_t * jnp.exp(dlog - m_t)).astype(BF16)
        decay = jnp.exp(inter - m_t)
        cn = cn_ref[h]
        vt = vt_ref[0, h]
        num = (decay * jnp.dot(cn.astype(BF16), qh_t, preferred_element_type=F32)
               + jnp.dot(vt, sc_t, preferred_element_type=F32))
        den = num[DV:DV + 1, :]
        hh = num[0:DV, :] / jnp.maximum(jnp.abs(den), jnp.exp(-m_t))
        hn = hh * lax.rsqrt(jnp.mean(hh * hh, axis=0, keepdims=True) + EPS)
        o_gate = _sigmoid(o_ref[:, h * DV:(h + 1) * DV].astype(F32))
        y_ref[:, h * DV:(h + 1) * DV] = (hn.T * hg_ref[0, :, h * DV:(h + 1) * DV] * o_gate).astype(y_ref.dtype)

        b_end = b_row[:, Lc - 1:Lc]
        wlog = b_end + w_row
        m_new = jnp.maximum(b_end + m_prev, jnp.max(wlog, axis=1, keepdims=True))
        cd = jnp.exp(b_end + m_prev - m_new)
        vw = (vt.astype(F32) * jnp.exp(wlog - m_new)).astype(BF16)
        cn_ref[h] = cd * cn + jnp.dot(vw, kh, preferred_element_type=F32)
        m_ref[h:h + 1, :] = jnp.broadcast_to(m_new, (1, LANES))


def _mlstm(mqk, mvt, mo, mif, B, S, layer, conv_w, conv_b, gate_b, head_gain, Lc=256):
    H, DK, DV = MLSTM_HEADS, MLSTM_QK, MLSTM_V
    nc = S // Lc
    row = lambda b, c: (b * nc + c, 0)
    VR = mvt.shape[2]
    return pl.pallas_call(
        functools.partial(_mlstm_kernel, Lc=Lc),
        grid=(B, nc),
        in_specs=[pl.BlockSpec((Lc, 2 * H * DK), row),
                  pl.BlockSpec((1, H, VR, Lc), lambda b, c: (b, 0, 0, c)),
                  pl.BlockSpec((Lc, H * DV), row),
                  pl.BlockSpec((Lc, LANES), row),
                  _const_spec((1, CONV_K, 2 * H * DK), lambda b, c: (layer, 0, 0)),
                  _const_spec((1, 1, 2 * H * DK), lambda b, c: (layer, 0, 0)),
                  _const_spec((1, 1, LANES), lambda b, c: (layer, 0, 0)),
                  _const_spec((1, 1, H * DV), lambda b, c: (layer, 0, 0))],
        out_specs=pl.BlockSpec((Lc, H * DV), row),
        out_shape=jax.ShapeDtypeStruct((B * S, H * DV), BF16),
        scratch_shapes=[pltpu.VMEM((Lc + 8, 2 * H * DK), F32), pltpu.VMEM((H, VR, DK), F32),
                        pltpu.VMEM((8, LANES), F32)],
        compiler_params=_cparams(("arbitrary", "arbitrary"), 32),
        name="mlstm",
    )(mqk, mvt, mo, mif, conv_w, conv_b, gate_b, head_gain)


def _merge_kernel(ya_ref, yb_ref, yc_ref, g_ref, x_ref, ga_ref, wa_ref, wb_ref, wc_ref, wo_ref, out_ref):
    def branch(y_ref, w_ref, c):
        gate = _sigmoid(g_ref[:, c * D_MODEL:(c + 1) * D_MODEL].astype(F32))
        return gate * jnp.dot(y_ref[...], w_ref[0], preferred_element_type=F32)

    merged = branch(ya_ref, wa_ref, 0) + branch(yb_ref, wb_ref, 1) + branch(yc_ref, wc_ref, 2)
    y = jnp.dot(merged.astype(BF16), wo_ref[0], preferred_element_type=F32)
    out_ref[...] = x_ref[...] + ga_ref[0] * y


def _merge(ya, yb, yc, gates, x2, S, layer, ga, wa, wb, wc, wo, tm=512):
    N, D = x2.shape
    per_b = S // tm
    row = lambda i: (i, 0)
    wsel = lambda i: (layer, 0, 0)
    return pl.pallas_call(
        _merge_kernel,
        grid=(N // tm,),
        in_specs=[pl.BlockSpec((tm, 512), row), pl.BlockSpec((tm, 512), row), pl.BlockSpec((tm, 512), row),
                  pl.BlockSpec((tm, N_BRANCH * D), row), pl.BlockSpec((tm, D), row),
                  pl.BlockSpec((1, 1, D), lambda i: (i // per_b, 0, 0)),
                  _const_spec((1, 512, D), wsel), _const_spec((1, 512, D), wsel), _const_spec((1, 512, D), wsel),
                  _const_spec((1, D, D), wsel)],
        out_specs=pl.BlockSpec((tm, D), row),
        out_shape=jax.ShapeDtypeStruct((N, D), F32),
        compiler_params=_cparams(("parallel",), 48),
        name="merge",
    )(ya, yb, yc, gates, x2, ga, wa, wb, wc, wo)


def _ffn_kernel(x_ref, gain_ref, sc_ref, sh_ref, gf_ref, wr_ref, br_ref, wg_ref, wu_ref, wd_ref, out_ref):
    x = x_ref[...]
    tm = x.shape[0]
    ms = jnp.mean(x * x, axis=-1, keepdims=True)
    hf = x * lax.rsqrt(ms + EPS) * gain_ref[0] * (1.0 + sc_ref[0]) + sh_ref[0]

    logits = jnp.dot(hf, wr_ref[0], preferred_element_type=F32, precision=HIGHEST) + br_ref[0]
    lane_i = lax.broadcasted_iota(jnp.int32, (tm, LANES), 1)
    lane = lane_i.astype(F32)
    big = float(LANES)

    def first_argmax(z):
        zmax = jnp.max(z, axis=1, keepdims=True)
        return zmax, jnp.min(jnp.where(z == zmax, lane, big), axis=1, keepdims=True)

    lg = jnp.where(lane_i < N_GROUPS, logits, NEG_INF)
    gmax, g_idx = first_argmax(lg)
    p_sel = 1.0 / jnp.sum(jnp.exp(lg - gmax), axis=1, keepdims=True)
    lane_grp = ((lane_i - N_GROUPS) // EXPERTS_PER_GROUP).astype(F32)
    in_grp = (lane_i >= N_GROUPS) & (lane_i < N_GROUPS + N_EXPERTS) & (lane_grp == g_idx)
    le = jnp.where(in_grp, logits, NEG_INF)
    v1, i1 = first_argmax(le)
    le2 = jnp.where(lane == i1, NEG_INF, le)
    v2, i2 = first_argmax(le2)
    e2 = jnp.exp(v2 - v1)
    den = 1.0 + e2
    comb = jnp.where(lane == i1, (1.0 / den) * p_sel, 0.0) + jnp.where(lane == i2, (e2 / den) * p_sel, 0.0)

    hb = hf.astype(BF16)
    acc = jnp.zeros((tm, D_MODEL), F32)
    for e in range(N_EXPERTS):
        gate = jnp.dot(hb, wg_ref[0, e], preferred_element_type=F32)
        up = jnp.dot(hb, wu_ref[0, e], preferred_element_type=F32)
        hid = gate * _sigmoid(gate) * up * comb[:, N_GROUPS + e:N_GROUPS + e + 1]
        acc = acc + jnp.dot(hid.astype(BF16), wd_ref[0, e], preferred_element_type=F32)
    out_ref[...] = x + gf_ref[0] * acc


def _ffn(x2, S, layer, gain, sc, sh, gf, wr, br, wg, wu, wd, tm=512):
    N, D = x2.shape
    per_b = S // tm
    row = lambda i: (i, 0)
    bsel = lambda i: (i // per_b, 0, 0)
    wsel = lambda i: (layer, 0, 0, 0)
    return pl.pallas_call(
        _ffn_kernel,
        grid=(N // tm,),
        in_specs=[pl.BlockSpec((tm, D), row),
                  _const_spec((1, 1, D), lambda i: (layer, 0, 0)),
                  pl.BlockSpec((1, 1, D), bsel), pl.BlockSpec((1, 1, D), bsel), pl.BlockSpec((1, 1, D), bsel),
                  _const_spec((1, D, LANES), lambda i: (layer, 0, 0)),
                  _const_spec((1, 1, LANES), lambda i: (layer, 0, 0)),
                  _const_spec((1, N_EXPERTS, D, EXPERT_FF), wsel),
                  _const_spec((1, N_EXPERTS, D, EXPERT_FF), wsel),
                  _const_spec((1, N_EXPERTS, EXPERT_FF, D), wsel)],
        out_specs=pl.BlockSpec((tm, D), row),
        out_shape=jax.ShapeDtypeStruct((N, D), F32),
        compiler_params=_cparams(("parallel",), 56),
        name="ffn",
    )(x2, gain, sc, sh, gf, wr, br, wg, wu, wd)


def _pad_last(a, n):
    return jnp.pad(a, [(0, 0)] * (a.ndim - 1) + [(0, n - a.shape[-1])])


def _head_tiles(a, heads, dim):
    a = a.reshape(a.shape[:-1] + (heads, dim))
    return _pad_last(a, LANES).reshape(a.shape[:-2] + (heads * LANES,))


def _prep_w_in(w_in):
    pts = np.cumsum(IN_SIZES)[:-1].tolist()
    cq, ckv, kr, m_qk, m_v, m_o, m_i, m_f, d_q, d_k, d_v, gates = jnp.split(w_in, pts, axis=-1)
    seg_a = _pad_last(jnp.concatenate([cq, ckv, kr], axis=-1), 512)
    seg_if = _pad_last(jnp.concatenate([m_i, m_f], axis=-1), LANES)
    w = jnp.concatenate([seg_a, m_qk, m_v, m_o, seg_if,
                         _head_tiles(d_q, DIL_HEADS, DIL_HEAD_DIM), _head_tiles(d_k, DIL_HEADS, DIL_HEAD_DIM),
                         d_v, gates], axis=-1)
    assert w.shape[-1] == COL_END
    return w.astype(BF16)


def kernel(x, c, w_ada, b_ada, attn_norm, w_in, mla_q_norm, mla_kv_norm, mla_w_uq, mla_w_ukv, mla_q_gain, mla_k_gain, mlstm_conv_w, mlstm_conv_b, mlstm_b_i, mlstm_b_f, mlstm_head_gain, dil_q_gain, dil_k_gain, w_branch_a, w_branch_b, w_branch_c, w_out, ffn_norm, w_router_group, b_router_group, w_router_expert, b_router_expert, w_exp_gate, w_exp_up, w_exp_down):
    B, S, D = x.shape
    L = w_ada.shape[0]
    N = B * S
    assert D == D_MODEL and S % DIL_MAX_WINDOW == 0

    mod = _adaln(c, w_ada, b_ada)

    w_pad = _prep_w_in(w_in)
    half = MLA_ROPE // 2
    swap_rope = lambda a: jnp.concatenate(
        [jnp.zeros_like(a[..., :MLA_NOPE]), a[..., MLA_NOPE + half:], a[..., MLA_NOPE:MLA_NOPE + half]], axis=-1)
    uq = mla_w_uq.reshape(L, MLA_Q_LORA, MLA_HEADS, MLA_QK)
    wuq = jnp.concatenate([_pad_last(u, LANES).reshape(L, MLA_Q_LORA, MLA_HEADS * LANES)
                           for u in (uq, swap_rope(uq))], axis=-1).astype(BF16)
    ukv = mla_w_ukv.reshape(L, MLA_KV_LORA, MLA_HEADS, MLA_NOPE + MLA_V)
    wkv = jnp.concatenate([_pad_last(ukv[..., :MLA_NOPE], LANES).reshape(L, MLA_KV_LORA, MLA_HEADS * LANES),
                           ukv[..., MLA_NOPE:].reshape(L, MLA_KV_LORA, MLA_HEADS * MLA_V)], axis=-1).astype(BF16)
    rows = lambda a: a[:, None, :]
    qg = rows(_pad_last(mla_q_gain, LANES))
    qgs = rows(_pad_last(swap_rope(mla_q_gain), LANES))
    kg = rows(_pad_last(mla_k_gain, LANES))
    dqg = rows(_pad_last(dil_q_gain, LANES))
    dkg = rows(_pad_last(dil_k_gain, LANES))
    gate_b = rows(_pad_last(jnp.concatenate([mlstm_b_i, mlstm_b_f], axis=-1), LANES))
    tables = _rope_tables(S)
    wa, wb, wc, wo = (w.astype(BF16) for w in (w_branch_a, w_branch_b, w_branch_c, w_out))
    wr = _pad_last(jnp.concatenate([w_router_group, w_router_expert], axis=-1), LANES)
    br = rows(_pad_last(jnp.concatenate([b_router_group, b_router_expert], axis=-1), LANES))
    wg, wu, wd = (w.astype(BF16) for w in (w_exp_gate, w_exp_up, w_exp_down))
    attn_norm, ffn_norm, mla_q_norm, mla_kv_norm, mlstm_conv_b, mlstm_head_gain = (
        rows(a) for a in (attn_norm, ffn_norm, mla_q_norm, mla_kv_norm, mlstm_conv_b, mlstm_head_gain))

    x2 = x.reshape(N, D)
    for l in range(L):
        sh_a, sc_a, g_a, sh_f, sc_f, g_f = (mod[l, :, i] for i in range(6))
        oa, mqk, mv, mo, mif, qd, kd, vd, gates = _inproj(x2, S, l, attn_norm, sc_a, sh_a, w_pad, dqg, dkg)
        q, k, vt = _mla_prep(oa, S, l, mla_q_norm, mla_kv_norm, wuq, wkv, qg, qgs, kg, tables)
        ya = _flash(q, k, vt, band=False)
        yb = _mlstm(mqk, mv, mo, mif, B, S, l, mlstm_conv_w, mlstm_conv_b, gate_b, mlstm_head_gain)
        yc = _flash(qd, kd, vd, band=True)
        x2 = _merge(ya.reshape(N, -1), yb, yc.reshape(N, -1), gates, x2, S, l, g_a, wa, wb, wc, wo)
        x2 = _ffn(x2, S, l, ffn_norm, sc_f, sh_f, g_f, wr, br, wg, wu, wd)
    return x2.reshape(B, S, D)
```

```python
import functools

import jax
import jax.numpy as jnp
import numpy as np
from jax import lax
from jax.experimental import pallas as pl
from jax.experimental.pallas import tpu as pltpu

F32 = jnp.float32
BF16 = jnp.bfloat16
HIGHEST = lax.Precision.HIGHEST
NEG_INF = float("-inf")

EPS = 1e-6
LANES = 128
MXU_COLS = 256
VMEM_BYTES_V7X = 64 * 1024 * 1024

D_MODEL = 1024
MLA_HEADS = 8
MLA_Q_LORA = 256
MLA_KV_LORA = 128
MLA_NOPE = 64
MLA_ROPE = 32
MLA_V = 64
MLA_QK = MLA_NOPE + MLA_ROPE
ROPE_THETA = 10000.0
MLSTM_HEADS = 4
MLSTM_QK = 64
MLSTM_V = 128
CONV_K = 4
DIL_HEADS = 8
DIL_HEAD_DIM = 64
DIL_PATTERNS = ((128, 1), (512, 4), (2048, 16))
DIL_WIDTH = DIL_HEADS * DIL_HEAD_DIM
N_GROUPS = 4
EXPERTS_PER_GROUP = 4
N_EXPERTS = N_GROUPS * EXPERTS_PER_GROUP
EXPERT_FF = 256
N_BRANCH = 3
IN_SIZES = (MLA_Q_LORA, MLA_KV_LORA, MLA_ROPE,
            2 * MLSTM_HEADS * MLSTM_QK, MLSTM_HEADS * MLSTM_V, MLSTM_HEADS * MLSTM_V, MLSTM_HEADS, MLSTM_HEADS,
            DIL_WIDTH, DIL_WIDTH, DIL_WIDTH, N_BRANCH * D_MODEL)

COL_A = 0
COL_MQK = 512
COL_MV = 1024
COL_MO = 1536
COL_MIF = 2048
COL_DQ = 2176
COL_DK = 3200
COL_DV = 4224
COL_G = 4736
COL_END = 7808

FLASH_TILE = 1024
DIL_MAX_WINDOW = max(w for w, _ in DIL_PATTERNS)
V_DIM = 64
V_FILL = 16
V_ROWS = V_DIM + V_FILL
LOG2E = 1.4426950408889634
assert MLA_V == V_DIM and DIL_HEAD_DIM == V_DIM


def _store_v_tiles_t(v, vt_ref, dim=V_DIM):
    tm = v.shape[0]
    v_t = v.astype(BF16).T
    fill = (lax.broadcasted_iota(jnp.int32, (V_FILL, tm), 0) == 0).astype(BF16)
    for h in range(v.shape[1] // dim):
        vt_ref[0, h, 0:dim, :] = v_t[h * dim:(h + 1) * dim, :]
        vt_ref[0, h, dim:dim + V_FILL, :] = fill


def _cparams(sem, vmem_mb):
    return pltpu.CompilerParams(dimension_semantics=sem, vmem_limit_bytes=vmem_mb * 1024 * 1024)


def _sigmoid(x):
    return 1.0 / (1.0 + jnp.exp(-x))


def _const_spec(shape, index_map):
    return pl.BlockSpec(shape, index_map, pipeline_mode=pl.Buffered(1))


def _adaln_kernel(c_ref, w_ref, b_ref, o_ref):
    c = c_ref[...]
    ca = c * _sigmoid(c)
    o_ref[0] = jnp.dot(ca, w_ref[0], preferred_element_type=F32, precision=HIGHEST) + b_ref[0]


def _adaln(c, w_ada, b_ada):
    L, D, D6 = w_ada.shape
    B = c.shape[0]
    rows = 8
    cp = jnp.zeros((rows, D), F32).at[:B].set(c)
    out = pl.pallas_call(
        _adaln_kernel,
        grid=(L, D6 // D),
        in_specs=[pl.BlockSpec((rows, D), lambda l, j: (0, 0)),
                  pl.BlockSpec((1, D, D), lambda l, j: (l, 0, j)),
                  pl.BlockSpec((1, 1, D), lambda l, j: (l, 0, j))],
        out_specs=pl.BlockSpec((1, rows, D), lambda l, j: (l, 0, j)),
        out_shape=jax.ShapeDtypeStruct((L, rows, D6), F32),
        compiler_params=_cparams(("parallel", "parallel"), 32),
        name="adaln",
    )(cp, w_ada, b_ada.reshape(L, 1, D6))
    return out[:, :B].reshape(L, B, D6 // D, 1, D)


def _inproj_kernel(x_ref, gain_ref, sc_ref, sh_ref, w_ref, dqg_ref, dkg_ref,
                   oa_ref, mqk_ref, mv_ref, mo_ref, mif_ref, qd_ref, kd_ref, vd_ref, g_ref):
    x = x_ref[...]
    ms = jnp.mean(x * x, axis=-1, keepdims=True)
    y = x * lax.rsqrt(ms + EPS) * gain_ref[0]
    u = (y * (1.0 + sc_ref[0]) + sh_ref[0]).astype(BF16)

    def proj(a, b):
        return jnp.dot(u, w_ref[0, :, a:b], preferred_element_type=F32)

    oa_ref[...] = proj(COL_A, COL_MQK)
    mqk_ref[...] = proj(COL_MQK, COL_MV)
    _store_v_tiles_t(proj(COL_MV, COL_MO), mv_ref, MLSTM_V)
    mo_ref[...] = proj(COL_MO, COL_MIF).astype(BF16)
    mif_ref[...] = proj(COL_MIF, COL_DQ)

    def head_norm(z, g):
        for h in range(DIL_HEADS):
            zh = z[:, h * LANES:(h + 1) * LANES]
            r = lax.rsqrt(jnp.sum(zh * zh, axis=-1, keepdims=True) / DIL_HEAD_DIM + EPS)
            yield h, zh * r * g

    dq = proj(COL_DQ, COL_DK)
    for h, zn in head_norm(dq, dqg_ref[0]):
        qd_ref[0, h // 2, :, (h % 2) * LANES:(h % 2 + 1) * LANES] = (zn * (DIL_HEAD_DIM ** -0.5 * LOG2E)).astype(BF16)
    dk = proj(COL_DK, COL_DV)
    for h, zn in head_norm(dk, dkg_ref[0]):
        kd_ref[0, h // 2, :, (h % 2) * LANES:(h % 2 + 1) * LANES] = zn.astype(BF16)
    _store_v_tiles_t(proj(COL_DV, COL_G), vd_ref)
    for c in range(N_BRANCH):
        g_ref[:, c * D_MODEL:(c + 1) * D_MODEL] = proj(COL_G + c * D_MODEL, COL_G + (c + 1) * D_MODEL).astype(BF16)


def _inproj(x2, S, layer, gain, sc, sh, w_pad, dqg, dkg, tm=256):
    N, D = x2.shape
    per_b = S // tm
    row = lambda i: (i, 0)
    bsel = lambda i: (i // per_b, 0, 0)
    B = N // S
    outs = [(512, F32), (512, F32), "mvt", (512, BF16), (LANES, F32),
            "qk", "qk", "vt", (3072, BF16)]
    head_major = lambda i: (i // per_b, 0, i % per_b, 0)
    time_minor = lambda i: (i // per_b, 0, 0, i % per_b)
    special_spec = {"qk": pl.BlockSpec((1, DIL_HEADS // 2, tm, 2 * LANES), head_major),
                    "vt": pl.BlockSpec((1, DIL_HEADS, V_ROWS, tm), time_minor),
                    "mvt": pl.BlockSpec((1, MLSTM_HEADS, MLSTM_V + V_FILL, tm), time_minor)}
    special_shape = {"qk": jax.ShapeDtypeStruct((B, DIL_HEADS // 2, S, 2 * LANES), BF16),
                     "vt": jax.ShapeDtypeStruct((B, DIL_HEADS, V_ROWS, S), BF16),
                     "mvt": jax.ShapeDtypeStruct((B, MLSTM_HEADS, MLSTM_V + V_FILL, S), BF16)}
    return pl.pallas_call(
        _inproj_kernel,
        grid=(N // tm,),
        in_specs=[pl.BlockSpec((tm, D), row),
                  _const_spec((1, 1, D), lambda i: (layer, 0, 0)),
                  pl.BlockSpec((1, 1, D), bsel),
                  pl.BlockSpec((1, 1, D), bsel),
                  _const_spec((1, D, COL_END), lambda i: (layer, 0, 0)),
                  _const_spec((1, 1, LANES), lambda i: (layer, 0, 0)),
                  _const_spec((1, 1, LANES), lambda i: (layer, 0, 0))],
        out_specs=[special_spec[o] if isinstance(o, str) else pl.BlockSpec((tm, o[0]), row) for o in outs],
        out_shape=[special_shape[o] if isinstance(o, str) else jax.ShapeDtypeStruct((N, o[0]), o[1]) for o in outs],
        compiler_params=_cparams(("parallel",), 52),
        name="inproj",
    )(x2, gain, sc, sh, w_pad, dqg, dkg)


def _mla_prep_kernel(a_ref, qn_ref, kvn_ref, wuq_ref, wkv_ref, qg_ref, qgs_ref, kg_ref, cos_ref, sin_ref,
                     q_ref, k_ref, vt_ref):
    HW = MLA_HEADS * LANES
    a = a_ref[...]
    cq = a[:, :MLA_Q_LORA]
    ckv = a[:, MLA_Q_LORA:MLA_Q_LORA + MLA_KV_LORA]
    kr_tile = a[:, MLA_Q_LORA + MLA_KV_LORA:]

    def rms(z, g):
        return z * lax.rsqrt(jnp.mean(z * z, axis=-1, keepdims=True) + EPS) * g

    cqn = rms(cq, qn_ref[0]).astype(BF16)
    ckvn = rms(ckv, kvn_ref[0]).astype(BF16)
    qr = jnp.dot(cqn, wuq_ref[0], preferred_element_type=F32)
    kvr = jnp.dot(ckvn, wkv_ref[0], preferred_element_type=F32)
    _store_v_tiles_t(kvr[:, HW:], vt_ref)

    cosf, sins = cos_ref[...], sin_ref[...]
    q_scale = MLA_QK ** -0.5 * LOG2E
    gc = cosf * (qg_ref[0] * q_scale)
    gs = sins * (qgs_ref[0] * q_scale)
    kg = kg_ref[0]
    half = MLA_ROPE // 2
    krg = pltpu.roll(kr_tile, MLA_NOPE, 1) * kg
    kr_rot = krg * cosf + (pltpu.roll(krg, LANES - half, 1) + pltpu.roll(krg, half, 1)) * sins
    kr_ss = jnp.sum(kr_tile * kr_tile, axis=-1, keepdims=True)
    for h in range(MLA_HEADS):
        sl = slice(h * LANES, (h + 1) * LANES)
        pair_sl = slice((h % 2) * LANES, (h % 2 + 1) * LANES)
        z = qr[:, sl]
        rq = lax.rsqrt(jnp.sum(z * z, axis=-1, keepdims=True) / MLA_QK + EPS)
        q_ref[0, h // 2, :, pair_sl] = ((z * gc + qr[:, HW + h * LANES:HW + (h + 1) * LANES] * gs) * rq).astype(BF16)
        zk = kvr[:, sl]
        rk = lax.rsqrt((jnp.sum(zk * zk, axis=-1, keepdims=True) + kr_ss) / MLA_QK + EPS)
        k_ref[0, h // 2, :, pair_sl] = ((zk * kg + kr_rot) * rk).astype(BF16)


def _rope_tables(S):
    half = MLA_ROPE // 2
    inv = ROPE_THETA ** (-jnp.arange(half, dtype=F32) / half)
    ang = jnp.arange(S).astype(F32)[:, None] * inv[None, :]
    cos, sin = jnp.cos(ang), jnp.sin(ang)
    one = jnp.ones((S, MLA_NOPE), F32)
    z = lambda n: jnp.zeros((S, n), F32)
    tail = LANES - MLA_QK
    cosf = jnp.concatenate([one, cos, cos, z(tail)], axis=1)
    sins = jnp.concatenate([z(MLA_NOPE), -sin, sin, z(tail)], axis=1)
    return cosf, sins


def _mla_prep(oa, S, layer, qn, kvn, wuq, wkv, qg, qgs, kg, tables, tm=512):
    N = oa.shape[0]
    B = N // S
    per_b = S // tm
    row = lambda i: (i, 0)
    pos = lambda i: (i % per_b, 0)
    HW = MLA_HEADS * LANES
    return pl.pallas_call(
        _mla_prep_kernel,
        grid=(N // tm,),
        in_specs=[pl.BlockSpec((tm, 512), row),
                  _const_spec((1, 1, MLA_Q_LORA), lambda i: (layer, 0, 0)),
                  _const_spec((1, 1, MLA_KV_LORA), lambda i: (layer, 0, 0)),
                  _const_spec((1, MLA_Q_LORA, 2 * HW), lambda i: (layer, 0, 0)),
                  _const_spec((1, MLA_KV_LORA, HW + MLA_HEADS * MLA_V), lambda i: (layer, 0, 0)),
                  _const_spec((1, 1, LANES), lambda i: (layer, 0, 0)),
                  _const_spec((1, 1, LANES), lambda i: (layer, 0, 0)),
                  _const_spec((1, 1, LANES), lambda i: (layer, 0, 0)),
                  pl.BlockSpec((tm, LANES), pos),
                  pl.BlockSpec((tm, LANES), pos)],
        out_specs=[pl.BlockSpec((1, MLA_HEADS // 2, tm, 2 * LANES), lambda i: (i // per_b, 0, i % per_b, 0)),
                   pl.BlockSpec((1, MLA_HEADS // 2, tm, 2 * LANES), lambda i: (i // per_b, 0, i % per_b, 0)),
                   pl.BlockSpec((1, MLA_HEADS, V_ROWS, tm), lambda i: (i // per_b, 0, 0, i % per_b))],
        out_shape=[jax.ShapeDtypeStruct((B, MLA_HEADS // 2, S, 2 * LANES), BF16),
                   jax.ShapeDtypeStruct((B, MLA_HEADS // 2, S, 2 * LANES), BF16),
                   jax.ShapeDtypeStruct((B, MLA_HEADS, V_ROWS, S), BF16)],
        compiler_params=_cparams(("parallel",), 32),
        name="mla_prep",
    )(oa, qn, kvn, wuq, wkv, qg, qgs, kg, *tables)


def _flash_kernel(qi_ref, kj_ref, qn_ref, kn_ref, q0_ref, k0_ref, vt_ref, *rest, band, T, W, n_pairs):
    if band:
        bias_ref, o_ref, m_ref, acc_ref, s_ref = rest
    else:
        o_ref, m_ref, acc_ref, s_ref = rest
    p_id = pl.program_id(2)
    qi = qi_ref[p_id]
    kj = kj_ref[p_id]
    if band:
        first = kj == qi
        last = kj == jnp.maximum(qi - W, 0)
    else:
        first = kj == 0
        last = kj == qi

    units = [(hh, slice(j * MXU_COLS, (j + 1) * MXU_COLS)) for hh in range(2) for j in range(T // MXU_COLS)]

    def scores(q_ref, k_ref, hh, qs, pair):
        q = q_ref[0, 0, qs, hh * LANES:(hh + 1) * LANES]
        k = k_ref[0, 0, :, hh * LANES:(hh + 1) * LANES]
        s = lax.dot_general(k, q, (((1,), (1,)), ((), ())), preferred_element_type=F32)
        if band:
            s = s + bias_ref[qi_ref[pair] - kj_ref[pair], :, qs]
        return s

    @pl.when(p_id == 0)
    def _():
        for hh, qs in units:
            s_ref[0, hh, :, qs] = scores(q0_ref, k0_ref, hh, qs, 0)

    @pl.when(first)
    def _():
        m_ref[...] = jnp.full(m_ref.shape, NEG_INF, F32)
        acc_ref[...] = jnp.zeros(acc_ref.shape, F32)

    nxt = jnp.minimum(p_id + 1, n_pairs - 1)

    def step(masked, slot):
        for hh, qs in units:
            s_ref[1 - slot, hh, :, qs] = scores(qn_ref, kn_ref, hh, qs, nxt)
            st = s_ref[slot, hh, :, qs]
            if masked:
                key = lax.broadcasted_iota(jnp.int32, (T, MXU_COLS), 0)
                qry = lax.broadcasted_iota(jnp.int32, (T, MXU_COLS), 1) + qs.start
                st = jnp.where(key <= qry, st, NEG_INF)
            m_prev = m_ref[hh, :, qs]
            m_new = jnp.maximum(m_prev, jnp.max(st, axis=0, keepdims=True))
            alpha = jnp.exp2(m_prev - m_new)
            pt = jnp.exp2(st - m_new).astype(BF16)
            acc_ref[hh, :, qs] = (acc_ref[hh, :, qs] * alpha
                                  + jnp.dot(vt_ref[0, hh], pt, preferred_element_type=F32))
            m_ref[hh, :, qs] = m_new

    parity = p_id % 2
    for slot in range(2):
        here = parity == slot
        if band:
            pl.when(here)(functools.partial(step, False, slot))
        else:
            pl.when(jnp.logical_and(here, last))(functools.partial(step, True, slot))
            pl.when(jnp.logical_and(here, jnp.logical_not(last)))(functools.partial(step, False, slot))

    @pl.when(last)
    def _():
        outs = []
        for hh in range(2):
            a = acc_ref[hh]
            outs.append(a[0:V_DIM, :] / a[V_DIM:V_DIM + 1, :])
        o_ref[0] = jnp.concatenate(outs, axis=0).T.astype(o_ref.dtype)


def _band_bias(T, W):
    d = (jnp.arange(W + 1)[:, None, None] * T + jnp.arange(T)[None, None, :] - jnp.arange(T)[None, :, None])
    cw = jnp.zeros(d.shape, F32)
    for window, dil in DIL_PATTERNS:
        cw = cw + ((d >= 0) & (d <= window) & (d % dil == 0)).astype(F32)
    return jnp.log2(cw)


def _flash(q, k, vt, band, T=FLASH_TILE):
    B, HP, S, _ = q.shape
    H = 2 * HP
    nq = S // T
    W = DIL_MAX_WINDOW // T if band else 0
    if band:
        pairs = [(i, i - j) for i in range(nq) for j in range(min(W, i) + 1)]
    else:
        pairs = [(i, j) for i in range(nq) for j in range(i + 1)]
    n = len(pairs)
    qi = jnp.asarray(np.array([p[0] for p in pairs], np.int32))
    kj = jnp.asarray(np.array([p[1] for p in pairs], np.int32))
    nxt = lambda p: jnp.minimum(p + 1, n - 1)
    in_specs = [pl.BlockSpec((1, 1, T, 2 * LANES), lambda b, h, p, qi, kj: (b, h, qi[nxt(p)], 0)),
                pl.BlockSpec((1, 1, T, 2 * LANES), lambda b, h, p, qi, kj: (b, h, kj[nxt(p)], 0)),
                pl.BlockSpec((1, 1, T, 2 * LANES), lambda b, h, p, qi, kj: (b, h, qi[0], 0)),
                pl.BlockSpec((1, 1, T, 2 * LANES), lambda b, h, p, qi, kj: (b, h, kj[0], 0)),
                pl.BlockSpec((1, 2, V_ROWS, T), lambda b, h, p, qi, kj: (b, h, 0, kj[p]))]
    args = [q, k, q, k, vt]
    if band:
        in_specs.append(_const_spec((W + 1, T, T), lambda b, h, p, qi, kj: (0, 0, 0)))
        args.append(_band_bias(T, W))
    return pl.pallas_call(
        functools.partial(_flash_kernel, band=band, T=T, W=W, n_pairs=n),
        grid_spec=pltpu.PrefetchScalarGridSpec(
            num_scalar_prefetch=2,
            grid=(B, H // 2, n),
            in_specs=in_specs,
            out_specs=pl.BlockSpec((1, T, LANES), lambda b, h, p, qi, kj: (b, qi[p], h)),
            scratch_shapes=[pltpu.VMEM((2, 1, T), F32), pltpu.VMEM((2, V_ROWS, T), F32),
                            pltpu.VMEM((2, 2, T, T), F32)]),
        out_shape=jax.ShapeDtypeStruct((B, S, H * V_DIM), BF16),
        compiler_params=_cparams(("arbitrary", "arbitrary", "arbitrary"), 52),
        name="flash_band" if band else "flash_causal",
    )(qi, kj, *args)


def _mlstm_kernel(qk_ref, vt_ref, o_ref, if_ref, cw_ref, cb_ref, gb_ref, hg_ref, y_ref,
                  xs_ref, cn_ref, m_ref, *, Lc):
    H, DK, DV = MLSTM_HEADS, MLSTM_QK, MLSTM_V
    HALO = 8

    @pl.when(pl.program_id(1) == 0)
    def _():
        xs_ref[0:HALO, :] = jnp.zeros((HALO, 2 * H * DK), F32)
        cn_ref[...] = jnp.zeros(cn_ref.shape, F32)
        m_ref[...] = jnp.zeros(m_ref.shape, F32)

    x = qk_ref[...]
    xs_ref[HALO:HALO + Lc, :] = x
    y = cb_ref[0]
    for j in range(CONV_K):
        y = y + cw_ref[0, j:j + 1, :] * xs_ref[pl.ds(HALO - (CONV_K - 1) + j, Lc), :]
    xs_ref[0:HALO, :] = x[Lc - HALO:, :]
    qk = y * _sigmoid(y)
    q_t = qk[:, :H * DK].T.astype(BF16)
    k_all = (qk[:, H * DK:] * (DK ** -0.5)).astype(BF16)

    g = if_ref[...] + gb_ref[0]
    logf = jnp.minimum(g, 0.0) - jnp.log(1.0 + jnp.exp(-jnp.abs(g)))
    src = lax.broadcasted_iota(jnp.int32, (Lc, Lc), 0)
    dst = lax.broadcasted_iota(jnp.int32, (Lc, Lc), 1)
    tri = jnp.where(dst <= src, 1.0, 0.0).astype(F32)
    bcum = jnp.dot(tri, logf, preferred_element_type=F32, precision=HIGHEST)
    causal = src <= dst
    g_t = g.T
    bcum_t = bcum.T

    for h in range(H):
        b_row = bcum_t[H + h:H + h + 1, :]
        w_col = g[:, h:h + 1] - bcum[:, H + h:H + h + 1]
        w_row = g_t[h:h + 1, :] - b_row
        m_prev = m_ref[h:h + 1, 0:1]
        dlog = jnp.where(causal, w_col + b_row, NEG_INF)
        inter = b_row + m_prev
        m_t = jnp.maximum(inter, jnp.max(dlog, axis=0, keepdims=True))
        qh_t = q_t[h * DK:(h + 1) * DK, :]
        kh = k_all[:, h * DK:(h + 1) * DK]
        s_t = jnp.dot(kh, qh_t, preferred_element_type=F32)
        sc_t = (s_t * jnp.exp(dlog - m_t)).astype(BF16)
        decay = jnp.exp(inter - m_t)
        cn = cn_ref[h]
        vt = vt_ref[0, h]
        num = (decay * jnp.dot(cn.astype(BF16), qh_t, preferred_element_type=F32)
               + jnp.dot(vt, sc_t, preferred_element_type=F32))
        den = num[DV:DV + 1, :]
        hh = num[0:DV, :] / jnp.maximum(jnp.abs(den), jnp.exp(-m_t))
        hn = hh * lax.rsqrt(jnp.mean(hh * hh, axis=0, keepdims=True) + EPS)
        o_gate = _sigmoid(o_ref[:, h * DV:(h + 1) * DV].astype(F32))
        y_ref[:, h * DV:(h + 1) * DV] = (hn.T * hg_ref[0, :, h * DV:(h + 1) * DV] * o_gate).astype(y_ref.dtype)

        b_end = b_row[:, Lc - 1:Lc]
        wlog = b_end + w_row
        m_new = jnp.maximum(b_end + m_prev, jnp.max(wlog, axis=1, keepdims=True))
        cd = jnp.exp(b_end + m_prev - m_new)
        vw = (vt.astype(F32) * jnp.exp(wlog - m_new)).astype(BF16)
        cn_ref[h] = cd * cn + jnp.dot(vw, kh, preferred_element_type=F32)
        m_ref[h:h + 1, :] = jnp.broadcast_to(m_new, (1, LANES))


def _mlstm(mqk, mvt, mo, mif, B, S, layer, conv_w, conv_b, gate_b, head_gain, Lc=256):
    H, DK, DV = MLSTM_HEADS, MLSTM_QK, MLSTM_V
    nc = S // Lc
    row = lambda b, c: (b * nc + c, 0)
    VR = mvt.shape[2]
    return pl.pallas_call(
        functools.partial(_mlstm_kernel, Lc=Lc),
        grid=(B, nc),
        in_specs=[pl.BlockSpec((Lc, 2 * H * DK), row),
                  pl.BlockSpec((1, H, VR, Lc), lambda b, c: (b, 0, 0, c)),
                  pl.BlockSpec((Lc, H * DV), row),
                  pl.BlockSpec((Lc, LANES), row),
                  _const_spec((1, CONV_K, 2 * H * DK), lambda b, c: (layer, 0, 0)),
                  _const_spec((1, 1, 2 * H * DK), lambda b, c: (layer, 0, 0)),
                  _const_spec((1, 1, LANES), lambda b, c: (layer, 0, 0)),
                  _const_spec((1, 1, H * DV), lambda b, c: (layer, 0, 0))],
        out_specs=pl.BlockSpec((Lc, H * DV), row),
        out_shape=jax.ShapeDtypeStruct((B * S, H * DV), BF16),
        scratch_shapes=[pltpu.VMEM((Lc + 8, 2 * H * DK), F32), pltpu.VMEM((H, VR, DK), F32),
                        pltpu.VMEM((8, LANES), F32)],
        compiler_params=_cparams(("arbitrary", "arbitrary"), 32),
        name="mlstm",
    )(mqk, mvt, mo, mif, conv_w, conv_b, gate_b, head_gain)


def _merge_kernel(ya_ref, yb_ref, yc_ref, g_ref, x_ref, ga_ref, wa_ref, wb_ref, wc_ref, wo_ref, out_ref):
    def branch(y_ref, w_ref, c):
        gate = _sigmoid(g_ref[:, c * D_MODEL:(c + 1) * D_MODEL].astype(F32))
        return gate * jnp.dot(y_ref[...], w_ref[0], preferred_element_type=F32)

    merged = branch(ya_ref, wa_ref, 0) + branch(yb_ref, wb_ref, 1) + branch(yc_ref, wc_ref, 2)
    y = jnp.dot(merged.astype(BF16), wo_ref[0], preferred_element_type=F32)
    out_ref[...] = x_ref[...] + ga_ref[0] * y


def _merge(ya, yb, yc, gates, x2, S, layer, ga, wa, wb, wc, wo, tm=512):
    N, D = x2.shape
    per_b = S // tm
    row = lambda i: (i, 0)
    wsel = lambda i: (layer, 0, 0)
    return pl.pallas_call(
        _merge_kernel,
        grid=(N // tm,),
        in_specs=[pl.BlockSpec((tm, 512), row), pl.BlockSpec((tm, 512), row), pl.BlockSpec((tm, 512), row),
                  pl.BlockSpec((tm, N_BRANCH * D), row), pl.BlockSpec((tm, D), row),
                  pl.BlockSpec((1, 1, D), lambda i: (i // per_b, 0, 0)),
                  _const_spec((1, 512, D), wsel), _const_spec((1, 512, D), wsel), _const_spec((1, 512, D), wsel),
                  _const_spec((1, D, D), wsel)],
        out_specs=pl.BlockSpec((tm, D), row),
        out_shape=jax.ShapeDtypeStruct((N, D), F32),
        compiler_params=_cparams(("parallel",), 48),
        name="merge",
    )(ya, yb, yc, gates, x2, ga, wa, wb, wc, wo)


def _ffn_kernel(x_ref, gain_ref, sc_ref, sh_ref, gf_ref, wr_ref, br_ref, wg_ref, wu_ref, wd_ref, out_ref):
    x = x_ref[...]
    tm = x.shape[0]
    ms = jnp.mean(x * x, axis=-1, keepdims=True)
    hf = x * lax.rsqrt(ms + EPS) * gain_ref[0] * (1.0 + sc_ref[0]) + sh_ref[0]

    logits = jnp.dot(hf, wr_ref[0], preferred_element_type=F32, precision=HIGHEST) + br_ref[0]
    lane_i = lax.broadcasted_iota(jnp.int32, (tm, LANES), 1)
    lane = lane_i.astype(F32)
    big = float(LANES)

    def first_argmax(z):
        zmax = jnp.max(z, axis=1, keepdims=True)
        return zmax, jnp.min(jnp.where(z == zmax, lane, big), axis=1, keepdims=True)

    lg = jnp.where(lane_i < N_GROUPS, logits, NEG_INF)
    gmax, g_idx = first_argmax(lg)
    p_sel = 1.0 / jnp.sum(jnp.exp(lg - gmax), axis=1, keepdims=True)
    lane_grp = ((lane_i - N_GROUPS) // EXPERTS_PER_GROUP).astype(F32)
    in_grp = (lane_i >= N_GROUPS) & (lane_i < N_GROUPS + N_EXPERTS) & (lane_grp == g_idx)
    le = jnp.where(in_grp, logits, NEG_INF)
    v1, i1 = first_argmax(le)
    le2 = jnp.where(lane == i1, NEG_INF, le)
    v2, i2 = first_argmax(le2)
    e2 = jnp.exp(v2 - v1)
    den = 1.0 + e2
    comb = jnp.where(lane == i1, (1.0 / den) * p_sel, 0.0) + jnp.where(lane == i2, (e2 / den) * p_sel, 0.0)

    hb = hf.astype(BF16)
    acc = jnp.zeros((tm, D_MODEL), F32)
    for e in range(N_EXPERTS):
        gate = jnp.dot(hb, wg_ref[0, e], preferred_element_type=F32)
        up = jnp.dot(hb, wu_ref[0, e], preferred_element_type=F32)
        hid = gate * _sigmoid(gate) * up * comb[:, N_GROUPS + e:N_GROUPS + e + 1]
        acc = acc + jnp.dot(hid.astype(BF16), wd_ref[0, e], preferred_element_type=F32)
    out_ref[...] = x + gf_ref[0] * acc


def _ffn(x2, S, layer, gain, sc, sh, gf, wr, br, wg, wu, wd, tm=512):
    N, D = x2.shape
    per_b = S // tm
    row = lambda i: (i, 0)
    bsel = lambda i: (i // per_b, 0, 0)
    wsel = lambda i: (layer, 0, 0, 0)
    return pl.pallas_call(
        _ffn_kernel,
        grid=(N // tm,),
        in_specs=[pl.BlockSpec((tm, D), row),
                  _const_spec((1, 1, D), lambda i: (layer, 0, 0)),
                  pl.BlockSpec((1, 1, D), bsel), pl.BlockSpec((1, 1, D), bsel), pl.BlockSpec((1, 1, D), bsel),
                  _const_spec((1, D, LANES), lambda i: (layer, 0, 0)),
                  _const_spec((1, 1, LANES), lambda i: (layer, 0, 0)),
                  _const_spec((1, N_EXPERTS, D, EXPERT_FF), wsel),
                  _const_spec((1, N_EXPERTS, D, EXPERT_FF), wsel),
                  _const_spec((1, N_EXPERTS, EXPERT_FF, D), wsel)],
        out_specs=pl.BlockSpec((tm, D), row),
        out_shape=jax.ShapeDtypeStruct((N, D), F32),
        compiler_params=_cparams(("parallel",), 56),
        name="ffn",
    )(x2, gain, sc, sh, gf, wr, br, wg, wu, wd)


def _pad_last(a, n):
    return jnp.pad(a, [(0, 0)] * (a.ndim - 1) + [(0, n - a.shape[-1])])


def _head_tiles(a, heads, dim):
    a = a.reshape(a.shape[:-1] + (heads, dim))
    return _pad_last(a, LANES).reshape(a.shape[:-2] + (heads * LANES,))


def _prep_w_in(w_in):
    pts = np.cumsum(IN_SIZES)[:-1].tolist()
    cq, ckv, kr, m_qk, m_v, m_o, m_i, m_f, d_q, d_k, d_v, gates = jnp.split(w_in, pts, axis=-1)
    seg_a = _pad_last(jnp.concatenate([cq, ckv, kr], axis=-1), 512)
    seg_if = _pad_last(jnp.concatenate([m_i, m_f], axis=-1), LANES)
    w = jnp.concatenate([seg_a, m_qk, m_v, m_o, seg_if,
                         _head_tiles(d_q, DIL_HEADS, DIL_HEAD_DIM), _head_tiles(d_k, DIL_HEADS, DIL_HEAD_DIM),
                         d_v, gates], axis=-1)
    assert w.shape[-1] == COL_END
    return w.astype(BF16)


def kernel(x, c, w_ada, b_ada, attn_norm, w_in, mla_q_norm, mla_kv_norm, mla_w_uq, mla_w_ukv, mla_q_gain, mla_k_gain, mlstm_conv_w, mlstm_conv_b, mlstm_b_i, mlstm_b_f, mlstm_head_gain, dil_q_gain, dil_k_gain, w_branch_a, w_branch_b, w_branch_c, w_out, ffn_norm, w_router_group, b_router_group, w_router_expert, b_router_expert, w_exp_gate, w_exp_up, w_exp_down):
    B, S, D = x.shape
    L = w_ada.shape[0]
    N = B * S
    assert D == D_MODEL and S % DIL_MAX_WINDOW == 0

    mod = _adaln(c, w_ada, b_ada)

    w_pad = _prep_w_in(w_in)
    half = MLA_ROPE // 2
    swap_rope = lambda a: jnp.concatenate(
        [jnp.zeros_like(a[..., :MLA_NOPE]), a[..., MLA_NOPE + half:], a[..., MLA_NOPE:MLA_NOPE + half]], axis=-1)
    uq = mla_w_uq.reshape(L, MLA_Q_LORA, MLA_HEADS, MLA_QK)
    wuq = jnp.concatenate([_pad_last(u, LANES).reshape(L, MLA_Q_LORA, MLA_HEADS * LANES)
                           for u in (uq, swap_rope(uq))], axis=-1).astype(BF16)
    ukv = mla_w_ukv.reshape(L, MLA_KV_LORA, MLA_HEADS, MLA_NOPE + MLA_V)
    wkv = jnp.concatenate([_pad_last(ukv[..., :MLA_NOPE], LANES).reshape(L, MLA_KV_LORA, MLA_HEADS * LANES),
                           ukv[..., MLA_NOPE:].reshape(L, MLA_KV_LORA, MLA_HEADS * MLA_V)], axis=-1).astype(BF16)
    rows = lambda a: a[:, None, :]
    qg = rows(_pad_last(mla_q_gain, LANES))
    qgs = rows(_pad_last(swap_rope(mla_q_gain), LANES))
    kg = rows(_pad_last(mla_k_gain, LANES))
    dqg = rows(_pad_last(dil_q_gain, LANES))
    dkg = rows(_pad_last(dil_k_gain, LANES))
    gate_b = rows(_pad_last(jnp.concatenate([mlstm_b_i, mlstm_b_f], axis=-1), LANES))
    tables = _rope_tables(S)
    wa, wb, wc, wo = (w.astype(BF16) for w in (w_branch_a, w_branch_b, w_branch_c, w_out))
    wr = _pad_last(jnp.concatenate([w_router_group, w_router_expert], axis=-1), LANES)
    br = rows(_pad_last(jnp.concatenate([b_router_group, b_router_expert], axis=-1), LANES))
    wg, wu, wd = (w.astype(BF16) for w in (w_exp_gate, w_exp_up, w_exp_down))
    attn_norm, ffn_norm, mla_q_norm, mla_kv_norm, mlstm_conv_b, mlstm_head_gain = (
        rows(a) for a in (attn_norm, ffn_norm, mla_q_norm, mla_kv_norm, mlstm_conv_b, mlstm_head_gain))

    x2 = x.reshape(N, D)
    for l in range(L):
        sh_a, sc_a, g_a, sh_f, sc_f, g_f = (mod[l, :, i] for i in range(6))
        oa, mqk, mv, mo, mif, qd, kd, vd, gates = _inproj(x2, S, l, attn_norm, sc_a, sh_a, w_pad, dqg, dkg)
        q, k, vt = _mla_prep(oa, S, l, mla_q_norm, mla_kv_norm, wuq, wkv, qg, qgs, kg, tables)
        ya = _flash(q, k, vt, band=False)
        yb = _mlstm(mqk, mv, mo, mif, B, S, l, mlstm_conv_w, mlstm_conv_b, gate_b, mlstm_head_gain)
        yc = _flash(qd, kd, vd, band=True)
        x2 = _merge(ya.reshape(N, -1), yb, yc.reshape(N, -1), gates, x2, S, l, g_a, wa, wb, wc, wo)
        x2 = _ffn(x2, S, l, ffn_norm, sc_f, sh_f, g_f, wr, br, wg, wu, wd)
    return x2.reshape(B, S, D)
```

```python
import functools

import jax
import jax.numpy as jnp
import numpy as np
from jax import lax
from jax.experimental import pallas as pl
from jax.experimental.pallas import tpu as pltpu

F32 = jnp.float32
BF16 = jnp.bfloat16
HIGHEST = lax.Precision.HIGHEST
NEG_INF = float("-inf")

EPS = 1e-6
LANES = 128
MXU_COLS = 256
VMEM_BYTES_V7X = 64 * 1024 * 1024

D_MODEL = 1024
MLA_HEADS = 8
MLA_Q_LORA = 256
MLA_KV_LORA = 128
MLA_NOPE = 64
MLA_ROPE = 32
MLA_V = 64
MLA_QK = MLA_NOPE + MLA_ROPE
ROPE_THETA = 10000.0
MLSTM_HEADS = 4
MLSTM_QK = 64
MLSTM_V = 128
CONV_K = 4
DIL_HEADS = 8
DIL_HEAD_DIM = 64
DIL_PATTERNS = ((128, 1), (512, 4), (2048, 16))
DIL_WIDTH = DIL_HEADS * DIL_HEAD_DIM
N_GROUPS = 4
EXPERTS_PER_GROUP = 4
N_EXPERTS = N_GROUPS * EXPERTS_PER_GROUP
EXPERT_FF = 256
N_BRANCH = 3
IN_SIZES = (MLA_Q_LORA, MLA_KV_LORA, MLA_ROPE,
            2 * MLSTM_HEADS * MLSTM_QK, MLSTM_HEADS * MLSTM_V, MLSTM_HEADS * MLSTM_V, MLSTM_HEADS, MLSTM_HEADS,
            DIL_WIDTH, DIL_WIDTH, DIL_WIDTH, N_BRANCH * D_MODEL)

COL_A = 0
COL_MQK = 512
COL_MV = 1024
COL_MO = 1536
COL_MIF = 2048
COL_DQ = 2176
COL_DK = 2688
COL_DV = 3200
COL_G = 3712
COL_END = 6784

FLASH_TILE = 1024
DIL_MAX_WINDOW = max(w for w, _ in DIL_PATTERNS)
V_DIM = 64
V_FILL = 16
V_ROWS = V_DIM + V_FILL
LOG2E = 1.4426950408889634
assert MLA_V == V_DIM and DIL_HEAD_DIM == V_DIM


def _store_v_tiles_t(v, vt_ref, dim=V_DIM):
    tm = v.shape[0]
    v_t = v.astype(BF16).T
    fill = (lax.broadcasted_iota(jnp.int32, (V_FILL, tm), 0) == 0).astype(BF16)
    for h in range(v.shape[1] // dim):
        vt_ref[0, h, 0:dim, :] = v_t[h * dim:(h + 1) * dim, :]
        vt_ref[0, h, dim:dim + V_FILL, :] = fill


def _cparams(sem, vmem_mb):
    return pltpu.CompilerParams(dimension_semantics=sem, vmem_limit_bytes=vmem_mb * 1024 * 1024)


def _sigmoid(x):
    return 1.0 / (1.0 + jnp.exp(-x))


def _const_spec(shape, index_map):
    return pl.BlockSpec(shape, index_map, pipeline_mode=pl.Buffered(1))


def _adaln_kernel(c_ref, w_ref, b_ref, o_ref):
    c = c_ref[...]
    ca = c * _sigmoid(c)
    o_ref[0] = jnp.dot(ca, w_ref[0], preferred_element_type=F32, precision=HIGHEST) + b_ref[0]


def _adaln(c, w_ada, b_ada):
    L, D, D6 = w_ada.shape
    B = c.shape[0]
    rows = 8
    cp = jnp.zeros((rows, D), F32).at[:B].set(c)
    out = pl.pallas_call(
        _adaln_kernel,
        grid=(L, D6 // D),
        in_specs=[pl.BlockSpec((rows, D), lambda l, j: (0, 0)),
                  pl.BlockSpec((1, D, D), lambda l, j: (l, 0, j)),
                  pl.BlockSpec((1, 1, D), lambda l, j: (l, 0, j))],
        out_specs=pl.BlockSpec((1, rows, D), lambda l, j: (l, 0, j)),
        out_shape=jax.ShapeDtypeStruct((L, rows, D6), F32),
        compiler_params=_cparams(("parallel", "parallel"), 32),
        name="adaln",
    )(cp, w_ada, b_ada.reshape(L, 1, D6))
    return out[:, :B].reshape(L, B, D6 // D, 1, D)


def _inproj_kernel(x_ref, gain_ref, sc_ref, sh_ref, w_ref, dqg_ref, dkg_ref,
                   oa_ref, mqk_ref, mv_ref, mo_ref, mif_ref, qd_ref, kd_ref, vd_ref, g_ref):
    x = x_ref[...]
    ms = jnp.mean(x * x, axis=-1, keepdims=True)
    y = x * lax.rsqrt(ms + EPS) * gain_ref[0]
    u = (y * (1.0 + sc_ref[0]) + sh_ref[0]).astype(BF16)

    def proj(a, b):
        return jnp.dot(u, w_ref[0, :, a:b], preferred_element_type=F32)

    oa_ref[...] = proj(COL_A, COL_MQK)
    mqk_ref[...] = proj(COL_MQK, COL_MV)
    _store_v_tiles_t(proj(COL_MV, COL_MO), mv_ref, MLSTM_V)
    mo_ref[...] = proj(COL_MO, COL_MIF).astype(BF16)
    mif_ref[...] = proj(COL_MIF, COL_DQ)

    half = LANES // 2
    low = lax.broadcasted_iota(jnp.int32, (1, LANES), 1) < half

    def head_norm(z, g):
        for h in range(DIL_HEADS):
            pair = z[:, (h // 2) * LANES:(h // 2 + 1) * LANES]
            zh = jnp.where(low, pair if h % 2 == 0 else pltpu.roll(pair, half, 1), 0.0)
            r = lax.rsqrt(jnp.sum(zh * zh, axis=-1, keepdims=True) / DIL_HEAD_DIM + EPS)
            yield h, zh * r * g

    dq = proj(COL_DQ, COL_DK)
    for h, zn in head_norm(dq, dqg_ref[0]):
        qd_ref[0, h // 2, :, (h % 2) * LANES:(h % 2 + 1) * LANES] = (zn * (DIL_HEAD_DIM ** -0.5 * LOG2E)).astype(BF16)
    dk = proj(COL_DK, COL_DV)
    for h, zn in head_norm(dk, dkg_ref[0]):
        kd_ref[0, h // 2, :, (h % 2) * LANES:(h % 2 + 1) * LANES] = zn.astype(BF16)
    _store_v_tiles_t(proj(COL_DV, COL_G), vd_ref)
    for c in range(N_BRANCH):
        g_ref[:, c * D_MODEL:(c + 1) * D_MODEL] = proj(COL_G + c * D_MODEL, COL_G + (c + 1) * D_MODEL).astype(BF16)


def _inproj(x2, S, layer, gain, sc, sh, w_pad, dqg, dkg, tm=256):
    N, D = x2.shape
    per_b = S // tm
    row = lambda i: (i, 0)
    bsel = lambda i: (i // per_b, 0, 0)
    B = N // S
    outs = [(512, F32), (512, F32), "mvt", (512, BF16), (LANES, F32),
            "qk", "qk", "vt", (3072, BF16)]
    head_major = lambda i: (i // per_b, 0, i % per_b, 0)
    time_minor = lambda i: (i // per_b, 0, 0, i % per_b)
    special_spec = {"qk": pl.BlockSpec((1, DIL_HEADS // 2, tm, 2 * LANES), head_major),
                    "vt": pl.BlockSpec((1, DIL_HEADS, V_ROWS, tm), time_minor),
                    "mvt": pl.BlockSpec((1, MLSTM_HEADS, MLSTM_V + V_FILL, tm), time_minor)}
    special_shape = {"qk": jax.ShapeDtypeStruct((B, DIL_HEADS // 2, S, 2 * LANES), BF16),
                     "vt": jax.ShapeDtypeStruct((B, DIL_HEADS, V_ROWS, S), BF16),
                     "mvt": jax.ShapeDtypeStruct((B, MLSTM_HEADS, MLSTM_V + V_FILL, S), BF16)}
    return pl.pallas_call(
        _inproj_kernel,
        grid=(N // tm,),
        in_specs=[pl.BlockSpec((tm, D), row),
                  _const_spec((1, 1, D), lambda i: (layer, 0, 0)),
                  pl.BlockSpec((1, 1, D), bsel),
                  pl.BlockSpec((1, 1, D), bsel),
                  _const_spec((1, D, COL_END), lambda i: (layer, 0, 0)),
                  _const_spec((1, 1, LANES), lambda i: (layer, 0, 0)),
                  _const_spec((1, 1, LANES), lambda i: (layer, 0, 0))],
        out_specs=[special_spec[o] if isinstance(o, str) else pl.BlockSpec((tm, o[0]), row) for o in outs],
        out_shape=[special_shape[o] if isinstance(o, str) else jax.ShapeDtypeStruct((N, o[0]), o[1]) for o in outs],
        compiler_params=_cparams(("parallel",), 52),
        name="inproj",
    )(x2, gain, sc, sh, w_pad, dqg, dkg)


def _mla_prep_kernel(a_ref, qn_ref, kvn_ref, wuq_ref, wkv_ref, qg_ref, qgs_ref, kg_ref, cos_ref, sin_ref,
                     q_ref, k_ref, vt_ref):
    HW = MLA_HEADS * LANES
    a = a_ref[...]
    cq = a[:, :MLA_Q_LORA]
    ckv = a[:, MLA_Q_LORA:MLA_Q_LORA + MLA_KV_LORA]
    kr_tile = a[:, MLA_Q_LORA + MLA_KV_LORA:]

    def rms(z, g):
        return z * lax.rsqrt(jnp.mean(z * z, axis=-1, keepdims=True) + EPS) * g

    cqn = rms(cq, qn_ref[0]).astype(BF16)
    ckvn = rms(ckv, kvn_ref[0]).astype(BF16)
    qr = jnp.dot(cqn, wuq_ref[0], preferred_element_type=F32)
    kvr = jnp.dot(ckvn, wkv_ref[0], preferred_element_type=F32)
    _store_v_tiles_t(kvr[:, HW:], vt_ref)

    cosf, sins = cos_ref[...], sin_ref[...]
    q_scale = MLA_QK ** -0.5 * LOG2E
    gc = cosf * (qg_ref[0] * q_scale)
    gs = sins * (qgs_ref[0] * q_scale)
    kg = kg_ref[0]
    half = MLA_ROPE // 2
    krg = pltpu.roll(kr_tile, MLA_NOPE, 1) * kg
    kr_rot = krg * cosf + (pltpu.roll(krg, LANES - half, 1) + pltpu.roll(krg, half, 1)) * sins
    kr_ss = jnp.sum(kr_tile * kr_tile, axis=-1, keepdims=True)
    for h in range(MLA_HEADS):
        sl = slice(h * LANES, (h + 1) * LANES)
        pair_sl = slice((h % 2) * LANES, (h % 2 + 1) * LANES)
        z = qr[:, sl]
        rq = lax.rsqrt(jnp.sum(z * z, axis=-1, keepdims=True) / MLA_QK + EPS)
        q_ref[0, h // 2, :, pair_sl] = ((z * gc + qr[:, HW + h * LANES:HW + (h + 1) * LANES] * gs) * rq).astype(BF16)
        zk = kvr[:, sl]
        rk = lax.rsqrt((jnp.sum(zk * zk, axis=-1, keepdims=True) + kr_ss) / MLA_QK + EPS)
        k_ref[0, h // 2, :, pair_sl] = ((zk * kg + kr_rot) * rk).astype(BF16)


def _rope_tables(S):
    half = MLA_ROPE // 2
    inv = ROPE_THETA ** (-jnp.arange(half, dtype=F32) / half)
    ang = jnp.arange(S).astype(F32)[:, None] * inv[None, :]
    cos, sin = jnp.cos(ang), jnp.sin(ang)
    one = jnp.ones((S, MLA_NOPE), F32)
    z = lambda n: jnp.zeros((S, n), F32)
    tail = LANES - MLA_QK
    cosf = jnp.concatenate([one, cos, cos, z(tail)], axis=1)
    sins = jnp.concatenate([z(MLA_NOPE), -sin, sin, z(tail)], axis=1)
    return cosf, sins


def _mla_prep(oa, S, layer, qn, kvn, wuq, wkv, qg, qgs, kg, tables, tm=512):
    N = oa.shape[0]
    B = N // S
    per_b = S // tm
    row = lambda i: (i, 0)
    pos = lambda i: (i % per_b, 0)
    HW = MLA_HEADS * LANES
    return pl.pallas_call(
        _mla_prep_kernel,
        grid=(N // tm,),
        in_specs=[pl.BlockSpec((tm, 512), row),
                  _const_spec((1, 1, MLA_Q_LORA), lambda i: (layer, 0, 0)),
                  _const_spec((1, 1, MLA_KV_LORA), lambda i: (layer, 0, 0)),
                  _const_spec((1, MLA_Q_LORA, 2 * HW), lambda i: (layer, 0, 0)),
                  _const_spec((1, MLA_KV_LORA, HW + MLA_HEADS * MLA_V), lambda i: (layer, 0, 0)),
                  _const_spec((1, 1, LANES), lambda i: (layer, 0, 0)),
                  _const_spec((1, 1, LANES), lambda i: (layer, 0, 0)),
                  _const_spec((1, 1, LANES), lambda i: (layer, 0, 0)),
                  pl.BlockSpec((tm, LANES), pos),
                  pl.BlockSpec((tm, LANES), pos)],
        out_specs=[pl.BlockSpec((1, MLA_HEADS // 2, tm, 2 * LANES), lambda i: (i // per_b, 0, i % per_b, 0)),
                   pl.BlockSpec((1, MLA_HEADS // 2, tm, 2 * LANES), lambda i: (i // per_b, 0, i % per_b, 0)),
                   pl.BlockSpec((1, MLA_HEADS, V_ROWS, tm), lambda i: (i // per_b, 0, 0, i % per_b))],
        out_shape=[jax.ShapeDtypeStruct((B, MLA_HEADS // 2, S, 2 * LANES), BF16),
                   jax.ShapeDtypeStruct((B, MLA_HEADS // 2, S, 2 * LANES), BF16),
                   jax.ShapeDtypeStruct((B, MLA_HEADS, V_ROWS, S), BF16)],
        compiler_params=_cparams(("parallel",), 32),
        name="mla_prep",
    )(oa, qn, kvn, wuq, wkv, qg, qgs, kg, *tables)


def _flash_kernel(qi_ref, kj_ref, qn_ref, kn_ref, q0_ref, k0_ref, vt_ref, *rest, band, T, W, n_pairs):
    if band:
        bias_ref, o_ref, m_ref, acc_ref, s_ref = rest
    else:
        o_ref, m_ref, acc_ref, s_ref = rest
    p_id = pl.program_id(2)
    qi = qi_ref[p_id]
    kj = kj_ref[p_id]
    if band:
        first = kj == qi
        last = kj == jnp.maximum(qi - W, 0)
    else:
        first = kj == 0
        last = kj == qi

    units = [(hh, slice(j * MXU_COLS, (j + 1) * MXU_COLS)) for hh in range(2) for j in range(T // MXU_COLS)]

    def scores(q_ref, k_ref, hh, qs, pair):
        q = q_ref[0, 0, qs, hh * LANES:(hh + 1) * LANES]
        k = k_ref[0, 0, :, hh * LANES:(hh + 1) * LANES]
        s = lax.dot_general(k, q, (((1,), (1,)), ((), ())), preferred_element_type=F32)
        if band:
            s = s + bias_ref[qi_ref[pair] - kj_ref[pair], :, qs]
        return s

    @pl.when(p_id == 0)
    def _():
        for hh, qs in units:
            s_ref[0, hh, :, qs] = scores(q0_ref, k0_ref, hh, qs, 0)

    @pl.when(first)
    def _():
        m_ref[...] = jnp.full(m_ref.shape, NEG_INF, F32)
        acc_ref[...] = jnp.zeros(acc_ref.shape, F32)

    nxt = jnp.minimum(p_id + 1, n_pairs - 1)

    def step(masked, slot):
        for hh, qs in units:
            s_ref[1 - slot, hh, :, qs] = scores(qn_ref, kn_ref, hh, qs, nxt)
            st = s_ref[slot, hh, :, qs]
            if masked:
                key = lax.broadcasted_iota(jnp.int32, (T, MXU_COLS), 0)
                qry = lax.broadcasted_iota(jnp.int32, (T, MXU_COLS), 1) + qs.start
                st = jnp.where(key <= qry, st, NEG_INF)
            m_prev = m_ref[hh, :, qs]
            m_new = jnp.maximum(m_prev, jnp.max(st, axis=0, keepdims=True))
            alpha = jnp.exp2(m_prev - m_new)
            pt = jnp.exp2(st - m_new).astype(BF16)
            acc_ref[hh, :, qs] = (acc_ref[hh, :, qs] * alpha
                                  + jnp.dot(vt_ref[0, hh], pt, preferred_element_type=F32))
            m_ref[hh, :, qs] = m_new

    parity = p_id % 2
    for slot in range(2):
        here = parity == slot
        if band:
            pl.when(here)(functools.partial(step, False, slot))
        else:
            pl.when(jnp.logical_and(here, last))(functools.partial(step, True, slot))
            pl.when(jnp.logical_and(here, jnp.logical_not(last)))(functools.partial(step, False, slot))

    @pl.when(last)
    def _():
        outs = []
        for hh in range(2):
            a = acc_ref[hh]
            outs.append(a[0:V_DIM, :] / a[V_DIM:V_DIM + 1, :])
        o_ref[0] = jnp.concatenate(outs, axis=0).T.astype(o_ref.dtype)


def _band_bias(T, W):
    d = (jnp.arange(W + 1)[:, None, None] * T + jnp.arange(T)[None, None, :] - jnp.arange(T)[None, :, None])
    cw = jnp.zeros(d.shape, F32)
    for window, dil in DIL_PATTERNS:
        cw = cw + ((d >= 0) & (d <= window) & (d % dil == 0)).astype(F32)
    return jnp.log2(cw)


def _flash(q, k, vt, band, T=FLASH_TILE):
    B, HP, S, _ = q.shape
    H = 2 * HP
    nq = S // T
    W = DIL_MAX_WINDOW // T if band else 0
    if band:
        pairs = [(i, i - j) for i in range(nq) for j in range(min(W, i) + 1)]
    else:
        pairs = [(i, j) for i in range(nq) for j in range(i + 1)]
    n = len(pairs)
    qi = jnp.asarray(np.array([p[0] for p in pairs], np.int32))
    kj = jnp.asarray(np.array([p[1] for p in pairs], np.int32))
    nxt = lambda p: jnp.minimum(p + 1, n - 1)
    in_specs = [pl.BlockSpec((1, 1, T, 2 * LANES), lambda b, h, p, qi, kj: (b, h, qi[nxt(p)], 0)),
                pl.BlockSpec((1, 1, T, 2 * LANES), lambda b, h, p, qi, kj: (b, h, kj[nxt(p)], 0)),
                pl.BlockSpec((1, 1, T, 2 * LANES), lambda b, h, p, qi, kj: (b, h, qi[0], 0)),
                pl.BlockSpec((1, 1, T, 2 * LANES), lambda b, h, p, qi, kj: (b, h, kj[0], 0)),
                pl.BlockSpec((1, 2, V_ROWS, T), lambda b, h, p, qi, kj: (b, h, 0, kj[p]))]
    args = [q, k, q, k, vt]
    if band:
        in_specs.append(_const_spec((W + 1, T, T), lambda b, h, p, qi, kj: (0, 0, 0)))
        args.append(_band_bias(T, W))
    return pl.pallas_call(
        functools.partial(_flash_kernel, band=band, T=T, W=W, n_pairs=n),
        grid_spec=pltpu.PrefetchScalarGridSpec(
            num_scalar_prefetch=2,
            grid=(B, H // 2, n),
            in_specs=in_specs,
            out_specs=pl.BlockSpec((1, T, LANES), lambda b, h, p, qi, kj: (b, qi[p], h)),
            scratch_shapes=[pltpu.VMEM((2, 1, T), F32), pltpu.VMEM((2, V_ROWS, T), F32),
                            pltpu.VMEM((2, 2, T, T), F32)]),
        out_shape=jax.ShapeDtypeStruct((B, S, H * V_DIM), BF16),
        compiler_params=_cparams(("arbitrary", "arbitrary", "arbitrary"), 52),
        name="flash_band" if band else "flash_causal",
    )(qi, kj, *args)


def _mlstm_kernel(qk_ref, vt_ref, o_ref, if_ref, cw_ref, cb_ref, gb_ref, hg_ref, y_ref,
                  xs_ref, cn_ref, m_ref, *, Lc):
    H, DK, DV = MLSTM_HEADS, MLSTM_QK, MLSTM_V
    HALO = 8

    @pl.when(pl.program_id(1) == 0)
    def _():
        xs_ref[0:HALO, :] = jnp.zeros((HALO, 2 * H * DK), F32)
        cn_ref[...] = jnp.zeros(cn_ref.shape, F32)
        m_ref[...] = jnp.zeros(m_ref.shape, F32)

    x = qk_ref[...]
    xs_ref[HALO:HALO + Lc, :] = x
    y = cb_ref[0]
    for j in range(CONV_K):
        y = y + cw_ref[0, j:j + 1, :] * xs_ref[pl.ds(HALO - (CONV_K - 1) + j, Lc), :]
    xs_ref[0:HALO, :] = x[Lc - HALO:, :]
    qk = y * _sigmoid(y)
    q_t = qk[:, :H * DK].T.astype(BF16)
    k_all = (qk[:, H * DK:] * (DK ** -0.5)).astype(BF16)

    g = if_ref[...] + gb_ref[0]
    logf = jnp.minimum(g, 0.0) - jnp.log(1.0 + jnp.exp(-jnp.abs(g)))
    src = lax.broadcasted_iota(jnp.int32, (Lc, Lc), 0)
    dst = lax.broadcasted_iota(jnp.int32, (Lc, Lc), 1)
    tri = jnp.where(dst <= src, 1.0, 0.0).astype(F32)
    bcum = jnp.dot(tri, logf, preferred_element_type=F32, precision=HIGHEST)
    causal = src <= dst
    g_t = g.T
    bcum_t = bcum.T

    for h in range(H):
        b_row = bcum_t[H + h:H + h + 1, :]
        w_col = g[:, h:h + 1] - bcum[:, H + h:H + h + 1]
        w_row = g_t[h:h + 1, :] - b_row
        m_prev = m_ref[h:h + 1, 0:1]
        dlog = jnp.where(causal, w_col + b_row, NEG_INF)
        inter = b_row + m_prev
        m_t = jnp.maximum(inter, jnp.max(dlog, axis=0, keepdims=True))
        qh_t = q_t[h * DK:(h + 1) * DK, :]
        kh = k_all[:, h * DK:(h + 1) * DK]
        s_t = jnp.dot(kh, qh_t, preferred_element_type=F32)
        sc_t = (s_t * jnp.exp(dlog - m_t)).astype(BF16)
        decay = jnp.exp(inter - m_t)
        cn = cn_ref[h]
        vt = vt_ref[0, h]
        num = (decay * jnp.dot(cn.astype(BF16), qh_t, preferred_element_type=F32)
               + jnp.dot(vt, sc_t, preferred_element_type=F32))
        den = num[DV:DV + 1, :]
        hh = num[0:DV, :] / jnp.maximum(jnp.abs(den), jnp.exp(-m_t))
        hn = hh * lax.rsqrt(jnp.mean(hh * hh, axis=0, keepdims=True) + EPS)
        o_gate = _sigmoid(o_ref[:, h * DV:(h + 1) * DV].astype(F32))
        y_ref[:, h * DV:(h + 1) * DV] = (hn.T * hg_ref[0, :, h * DV:(h + 1) * DV] * o_gate).astype(y_ref.dtype)

        b_end = b_row[:, Lc - 1:Lc]
        wlog = b_end + w_row
        m_new = jnp.maximum(b_end + m_prev, jnp.max(wlog, axis=1, keepdims=True))
        cd = jnp.exp(b_end + m_prev - m_new)
        vw = (vt.astype(F32) * jnp.exp(wlog - m_new)).astype(BF16)
        cn_ref[h] = cd * cn + jnp.dot(vw, kh, preferred_element_type=F32)
        m_ref[h:h + 1, :] = jnp.broadcast_to(m_new, (1, LANES))


def _mlstm(mqk, mvt, mo, mif, B, S, layer, conv_w, conv_b, gate_b, head_gain, Lc=256):
    H, DK, DV = MLSTM_HEADS, MLSTM_QK, MLSTM_V
    nc = S // Lc
    row = lambda b, c: (b * nc + c, 0)
    VR = mvt.shape[2]
    return pl.pallas_call(
        functools.partial(_mlstm_kernel, Lc=Lc),
        grid=(B, nc),
        in_specs=[pl.BlockSpec((Lc, 2 * H * DK), row),
                  pl.BlockSpec((1, H, VR, Lc), lambda b, c: (b, 0, 0, c)),
                  pl.BlockSpec((Lc, H * DV), row),
                  pl.BlockSpec((Lc, LANES), row),
                  _const_spec((1, CONV_K, 2 * H * DK), lambda b, c: (layer, 0, 0)),
                  _const_spec((1, 1, 2 * H * DK), lambda b, c: (layer, 0, 0)),
                  _const_spec((1, 1, LANES), lambda b, c: (layer, 0, 0)),
                  _const_spec((1, 1, H * DV), lambda b, c: (layer, 0, 0))],
        out_specs=pl.BlockSpec((Lc, H * DV), row),
        out_shape=jax.ShapeDtypeStruct((B * S, H * DV), BF16),
        scratch_shapes=[pltpu.VMEM((Lc + 8, 2 * H * DK), F32), pltpu.VMEM((H, VR, DK), F32),
                        pltpu.VMEM((8, LANES), F32)],
        compiler_params=_cparams(("arbitrary", "arbitrary"), 32),
        name="mlstm",
    )(mqk, mvt, mo, mif, conv_w, conv_b, gate_b, head_gain)


def _merge_kernel(ya_ref, yb_ref, yc_ref, g_ref, x_ref, ga_ref, wa_ref, wb_ref, wc_ref, wo_ref, out_ref):
    def branch(y_ref, w_ref, c):
        gate = _sigmoid(g_ref[:, c * D_MODEL:(c + 1) * D_MODEL].astype(F32))
        return gate * jnp.dot(y_ref[...], w_ref[0], preferred_element_type=F32)

    merged = branch(ya_ref, wa_ref, 0) + branch(yb_ref, wb_ref, 1) + branch(yc_ref, wc_ref, 2)
    y = jnp.dot(merged.astype(BF16), wo_ref[0], preferred_element_type=F32)
    out_ref[...] = x_ref[...] + ga_ref[0] * y


def _merge(ya, yb, yc, gates, x2, S, layer, ga, wa, wb, wc, wo, tm=512):
    N, D = x2.shape
    per_b = S // tm
    row = lambda i: (i, 0)
    wsel = lambda i: (layer, 0, 0)
    return pl.pallas_call(
        _merge_kernel,
        grid=(N // tm,),
        in_specs=[pl.BlockSpec((tm, 512), row), pl.BlockSpec((tm, 512), row), pl.BlockSpec((tm, 512), row),
                  pl.BlockSpec((tm, N_BRANCH * D), row), pl.BlockSpec((tm, D), row),
                  pl.BlockSpec((1, 1, D), lambda i: (i // per_b, 0, 0)),
                  _const_spec((1, 512, D), wsel), _const_spec((1, 512, D), wsel), _const_spec((1, 512, D), wsel),
                  _const_spec((1, D, D), wsel)],
        out_specs=pl.BlockSpec((tm, D), row),
        out_shape=jax.ShapeDtypeStruct((N, D), F32),
        compiler_params=_cparams(("parallel",), 48),
        name="merge",
    )(ya, yb, yc, gates, x2, ga, wa, wb, wc, wo)


def _ffn_kernel(x_ref, gain_ref, sc_ref, sh_ref, gf_ref, wr_ref, br_ref, wg_ref, wu_ref, wd_ref, out_ref):
    x = x_ref[...]
    tm = x.shape[0]
    ms = jnp.mean(x * x, axis=-1, keepdims=True)
    hf = x * lax.rsqrt(ms + EPS) * gain_ref[0] * (1.0 + sc_ref[0]) + sh_ref[0]

    hf_hi = hf.astype(BF16)
    hf_lo = (hf - hf_hi.astype(F32)).astype(BF16)
    parts = (jnp.dot(hf_hi, wr_ref[0], preferred_element_type=F32)
             + jnp.dot(hf_lo, wr_ref[0], preferred_element_type=F32))
    logits = parts[:, :LANES] + parts[:, LANES:] + br_ref[0]
    lane_i = lax.broadcasted_iota(jnp.int32, (tm, LANES), 1)
    lane = lane_i.astype(F32)
    big = float(LANES)

    def first_argmax(z):
        zmax = jnp.max(z, axis=1, keepdims=True)
        return zmax, jnp.min(jnp.where(z == zmax, lane, big), axis=1, keepdims=True)

    lg = jnp.where(lane_i < N_GROUPS, logits, NEG_INF)
    gmax, g_idx = first_argmax(lg)
    p_sel = 1.0 / jnp.sum(jnp.exp(lg - gmax), axis=1, keepdims=True)
    lane_grp = ((lane_i - N_GROUPS) // EXPERTS_PER_GROUP).astype(F32)
    in_grp = (lane_i >= N_GROUPS) & (lane_i < N_GROUPS + N_EXPERTS) & (lane_grp == g_idx)
    le = jnp.where(in_grp, logits, NEG_INF)
    v1, i1 = first_argmax(le)
    le2 = jnp.where(lane == i1, NEG_INF, le)
    v2, i2 = first_argmax(le2)
    e2 = jnp.exp(v2 - v1)
    den = 1.0 + e2
    comb = jnp.where(lane == i1, (1.0 / den) * p_sel, 0.0) + jnp.where(lane == i2, (e2 / den) * p_sel, 0.0)

    hb = hf_hi
    acc = jnp.zeros((tm, D_MODEL), F32)
    for e in range(N_EXPERTS):
        gate = jnp.dot(hb, wg_ref[0, e], preferred_element_type=F32)
        up = jnp.dot(hb, wu_ref[0, e], preferred_element_type=F32)
        hid = gate * _sigmoid(gate) * up * comb[:, N_GROUPS + e:N_GROUPS + e + 1]
        acc = acc + jnp.dot(hid.astype(BF16), wd_ref[0, e], preferred_element_type=F32)
    out_ref[...] = x + gf_ref[0] * acc


def _ffn(x2, S, layer, gain, sc, sh, gf, wr, br, wg, wu, wd, tm=512):
    N, D = x2.shape
    per_b = S // tm
    row = lambda i: (i, 0)
    bsel = lambda i: (i // per_b, 0, 0)
    wsel = lambda i: (layer, 0, 0, 0)
    return pl.pallas_call(
        _ffn_kernel,
        grid=(N // tm,),
        in_specs=[pl.BlockSpec((tm, D), row),
                  _const_spec((1, 1, D), lambda i: (layer, 0, 0)),
                  pl.BlockSpec((1, 1, D), bsel), pl.BlockSpec((1, 1, D), bsel), pl.BlockSpec((1, 1, D), bsel),
                  _const_spec((1, D, 2 * LANES), lambda i: (layer, 0, 0)),
                  _const_spec((1, 1, LANES), lambda i: (layer, 0, 0)),
                  _const_spec((1, N_EXPERTS, D, EXPERT_FF), wsel),
                  _const_spec((1, N_EXPERTS, D, EXPERT_FF), wsel),
                  _const_spec((1, N_EXPERTS, EXPERT_FF, D), wsel)],
        out_specs=pl.BlockSpec((tm, D), row),
        out_shape=jax.ShapeDtypeStruct((N, D), F32),
        compiler_params=_cparams(("parallel",), 56),
        name="ffn",
    )(x2, gain, sc, sh, gf, wr, br, wg, wu, wd)


def _pad_last(a, n):
    return jnp.pad(a, [(0, 0)] * (a.ndim - 1) + [(0, n - a.shape[-1])])


def _prep_w_in(w_in):
    pts = np.cumsum(IN_SIZES)[:-1].tolist()
    cq, ckv, kr, m_qk, m_v, m_o, m_i, m_f, d_q, d_k, d_v, gates = jnp.split(w_in, pts, axis=-1)
    seg_a = _pad_last(jnp.concatenate([cq, ckv, kr], axis=-1), 512)
    seg_if = _pad_last(jnp.concatenate([m_i, m_f], axis=-1), LANES)
    w = jnp.concatenate([seg_a, m_qk, m_v, m_o, seg_if,
                         d_q, d_k,
                         d_v, gates], axis=-1)
    assert w.shape[-1] == COL_END
    return w.astype(BF16)


def kernel(x, c, w_ada, b_ada, attn_norm, w_in, mla_q_norm, mla_kv_norm, mla_w_uq, mla_w_ukv, mla_q_gain, mla_k_gain, mlstm_conv_w, mlstm_conv_b, mlstm_b_i, mlstm_b_f, mlstm_head_gain, dil_q_gain, dil_k_gain, w_branch_a, w_branch_b, w_branch_c, w_out, ffn_norm, w_router_group, b_router_group, w_router_expert, b_router_expert, w_exp_gate, w_exp_up, w_exp_down):
    B, S, D = x.shape
    L = w_ada.shape[0]
    N = B * S
    assert D == D_MODEL and S % DIL_MAX_WINDOW == 0

    mod = _adaln(c, w_ada, b_ada)

    w_pad = _prep_w_in(w_in)
    half = MLA_ROPE // 2
    swap_rope = lambda a: jnp.concatenate(
        [jnp.zeros_like(a[..., :MLA_NOPE]), a[..., MLA_NOPE + half:], a[..., MLA_NOPE:MLA_NOPE + half]], axis=-1)
    uq = mla_w_uq.reshape(L, MLA_Q_LORA, MLA_HEADS, MLA_QK)
    wuq = jnp.concatenate([_pad_last(u, LANES).reshape(L, MLA_Q_LORA, MLA_HEADS * LANES)
                           for u in (uq, swap_rope(uq))], axis=-1).astype(BF16)
    ukv = mla_w_ukv.reshape(L, MLA_KV_LORA, MLA_HEADS, MLA_NOPE + MLA_V)
    wkv = jnp.concatenate([_pad_last(ukv[..., :MLA_NOPE], LANES).reshape(L, MLA_KV_LORA, MLA_HEADS * LANES),
                           ukv[..., MLA_NOPE:].reshape(L, MLA_KV_LORA, MLA_HEADS * MLA_V)], axis=-1).astype(BF16)
    rows = lambda a: a[:, None, :]
    qg = rows(_pad_last(mla_q_gain, LANES))
    qgs = rows(_pad_last(swap_rope(mla_q_gain), LANES))
    kg = rows(_pad_last(mla_k_gain, LANES))
    dqg = rows(_pad_last(dil_q_gain, LANES))
    dkg = rows(_pad_last(dil_k_gain, LANES))
    gate_b = rows(_pad_last(jnp.concatenate([mlstm_b_i, mlstm_b_f], axis=-1), LANES))
    tables = _rope_tables(S)
    wa, wb, wc, wo = (w.astype(BF16) for w in (w_branch_a, w_branch_b, w_branch_c, w_out))
    wr = _pad_last(jnp.concatenate([w_router_group, w_router_expert], axis=-1), LANES)
    wr_hi = wr.astype(BF16)
    wr = jnp.concatenate([wr_hi, (wr - wr_hi.astype(F32)).astype(BF16)], axis=-1)
    br = rows(_pad_last(jnp.concatenate([b_router_group, b_router_expert], axis=-1), LANES))
    wg, wu, wd = (w.astype(BF16) for w in (w_exp_gate, w_exp_up, w_exp_down))
    attn_norm, ffn_norm, mla_q_norm, mla_kv_norm, mlstm_conv_b, mlstm_head_gain = (
        rows(a) for a in (attn_norm, ffn_norm, mla_q_norm, mla_kv_norm, mlstm_conv_b, mlstm_head_gain))

    x2 = x.reshape(N, D)
    for l in range(L):
        sh_a, sc_a, g_a, sh_f, sc_f, g_f = (mod[l, :, i] for i in range(6))
        oa, mqk, mv, mo, mif, qd, kd, vd, gates = _inproj(x2, S, l, attn_norm, sc_a, sh_a, w_pad, dqg, dkg)
        q, k, vt = _mla_prep(oa, S, l, mla_q_norm, mla_kv_norm, wuq, wkv, qg, qgs, kg, tables)
        ya = _flash(q, k, vt, band=False)
        yb = _mlstm(mqk, mv, mo, mif, B, S, l, mlstm_conv_w, mlstm_conv_b, gate_b, mlstm_head_gain)
        yc = _flash(qd, kd, vd, band=True)
        x2 = _merge(ya.reshape(N, -1), yb, yc.reshape(N, -1), gates, x2, S, l, g_a, wa, wb, wc, wo)
        x2 = _ffn(x2, S, l, ffn_norm, sc_f, sh_f, g_f, wr, br, wg, wu, wd)
    return x2.reshape(B, S, D)
```

```python
import functools

import jax
import jax.numpy as jnp
import numpy as np
from jax import lax
from jax.experimental import pallas as pl
from jax.experimental.pallas import tpu as pltpu

F32 = jnp.float32
BF16 = jnp.bfloat16
HIGHEST = lax.Precision.HIGHEST
NEG_INF = float("-inf")

EPS = 1e-6
LANES = 128
MXU_COLS = 256
VMEM_BYTES_V7X = 64 * 1024 * 1024

D_MODEL = 1024
MLA_HEADS = 8
MLA_Q_LORA = 256
MLA_KV_LORA = 128
MLA_NOPE = 64
MLA_ROPE = 32
MLA_V = 64
MLA_QK = MLA_NOPE + MLA_ROPE
ROPE_THETA = 10000.0
MLSTM_HEADS = 4
MLSTM_QK = 64
MLSTM_V = 128
CONV_K = 4
DIL_HEADS = 8
DIL_HEAD_DIM = 64
DIL_PATTERNS = ((128, 1), (512, 4), (2048, 16))
DIL_WIDTH = DIL_HEADS * DIL_HEAD_DIM
N_GROUPS = 4
EXPERTS_PER_GROUP = 4
N_EXPERTS = N_GROUPS * EXPERTS_PER_GROUP
EXPERT_FF = 256
N_BRANCH = 3
IN_SIZES = (MLA_Q_LORA, MLA_KV_LORA, MLA_ROPE,
            2 * MLSTM_HEADS * MLSTM_QK, MLSTM_HEADS * MLSTM_V, MLSTM_HEADS * MLSTM_V, MLSTM_HEADS, MLSTM_HEADS,
            DIL_WIDTH, DIL_WIDTH, DIL_WIDTH, N_BRANCH * D_MODEL)

COL_A = 0
COL_MQK = 512
COL_MV = 1024
COL_MO = 1536
COL_MIF = 2048
COL_DQ = 2176
COL_DK = 2688
COL_DV = 3200
COL_G = 3712
COL_END = 6784

FLASH_TILE = 1024
DIL_MAX_WINDOW = max(w for w, _ in DIL_PATTERNS)
V_DIM = 64
V_FILL = 16
V_ROWS = V_DIM + V_FILL
LOG2E = 1.4426950408889634
assert MLA_V == V_DIM and DIL_HEAD_DIM == V_DIM


def _store_v_tiles_t(v, vt_ref, dim=V_DIM):
    tm = v.shape[0]
    v_t = v.astype(BF16).T
    fill = (lax.broadcasted_iota(jnp.int32, (V_FILL, tm), 0) == 0).astype(BF16)
    for h in range(v.shape[1] // dim):
        vt_ref[0, h, 0:dim, :] = v_t[h * dim:(h + 1) * dim, :]
        vt_ref[0, h, dim:dim + V_FILL, :] = fill


def _cparams(sem, vmem_mb):
    return pltpu.CompilerParams(dimension_semantics=sem, vmem_limit_bytes=vmem_mb * 1024 * 1024)


def _sigmoid(x):
    return 1.0 / (1.0 + jnp.exp(-x))


def _const_spec(shape, index_map):
    return pl.BlockSpec(shape, index_map, pipeline_mode=pl.Buffered(1))


def _adaln_kernel(c_ref, w_ref, b_ref, o_ref):
    c = c_ref[...]
    ca = c * _sigmoid(c)
    o_ref[0] = jnp.dot(ca, w_ref[0], preferred_element_type=F32, precision=HIGHEST) + b_ref[0]


def _adaln(c, w_ada, b_ada):
    L, D, D6 = w_ada.shape
    B = c.shape[0]
    rows = 8
    cp = jnp.zeros((rows, D), F32).at[:B].set(c)
    out = pl.pallas_call(
        _adaln_kernel,
        grid=(L, D6 // D),
        in_specs=[pl.BlockSpec((rows, D), lambda l, j: (0, 0)),
                  pl.BlockSpec((1, D, D), lambda l, j: (l, 0, j)),
                  pl.BlockSpec((1, 1, D), lambda l, j: (l, 0, j))],
        out_specs=pl.BlockSpec((1, rows, D), lambda l, j: (l, 0, j)),
        out_shape=jax.ShapeDtypeStruct((L, rows, D6), F32),
        compiler_params=_cparams(("parallel", "parallel"), 32),
        name="adaln",
    )(cp, w_ada, b_ada.reshape(L, 1, D6))
    return out[:, :B].reshape(L, B, D6 // D, 1, D)


def _inproj_kernel(x_ref, gain_ref, sc_ref, sh_ref, w_ref, dqg_ref, dkg_ref,
                   oa_ref, mqk_ref, mv_ref, mo_ref, mif_ref, qd_ref, kd_ref, vd_ref, g_ref):
    x = x_ref[...]
    ms = jnp.mean(x * x, axis=-1, keepdims=True)
    y = x * lax.rsqrt(ms + EPS) * gain_ref[0]
    u = (y * (1.0 + sc_ref[0]) + sh_ref[0]).astype(BF16)

    def proj(a, b):
        return jnp.dot(u, w_ref[0, :, a:b], preferred_element_type=F32)

    oa_ref[...] = proj(COL_A, COL_MQK)
    mqk_ref[...] = proj(COL_MQK, COL_MV)
    _store_v_tiles_t(proj(COL_MV, COL_MO), mv_ref, MLSTM_V)
    mo_ref[...] = proj(COL_MO, COL_MIF).astype(BF16)
    mif_ref[...] = proj(COL_MIF, COL_DQ)

    half = LANES // 2
    low = lax.broadcasted_iota(jnp.int32, (1, LANES), 1) < half

    def head_norm(z, g):
        for h in range(DIL_HEADS):
            pair = z[:, (h // 2) * LANES:(h // 2 + 1) * LANES]
            zh = jnp.where(low, pair if h % 2 == 0 else pltpu.roll(pair, half, 1), 0.0)
            r = lax.rsqrt(jnp.sum(zh * zh, axis=-1, keepdims=True) / DIL_HEAD_DIM + EPS)
            yield h, zh * r * g

    dq = proj(COL_DQ, COL_DK)
    for h, zn in head_norm(dq, dqg_ref[0]):
        qd_ref[0, h // 2, :, (h % 2) * LANES:(h % 2 + 1) * LANES] = (zn * (DIL_HEAD_DIM ** -0.5 * LOG2E)).astype(BF16)
    dk = proj(COL_DK, COL_DV)
    for h, zn in head_norm(dk, dkg_ref[0]):
        kd_ref[0, h // 2, :, (h % 2) * LANES:(h % 2 + 1) * LANES] = zn.astype(BF16)
    _store_v_tiles_t(proj(COL_DV, COL_G), vd_ref)
    for c in range(N_BRANCH):
        g_ref[:, c * D_MODEL:(c + 1) * D_MODEL] = proj(COL_G + c * D_MODEL, COL_G + (c + 1) * D_MODEL).astype(BF16)


def _inproj(x2, S, layer, gain, sc, sh, w_pad, dqg, dkg, tm=256):
    N, D = x2.shape
    per_b = S // tm
    row = lambda i: (i, 0)
    bsel = lambda i: (i // per_b, 0, 0)
    B = N // S
    outs = [(512, F32), (512, F32), "mvt", (512, BF16), (LANES, F32),
            "qk", "qk", "vt", (3072, BF16)]
    head_major = lambda i: (i // per_b, 0, i % per_b, 0)
    time_minor = lambda i: (i // per_b, 0, 0, i % per_b)
    special_spec = {"qk": pl.BlockSpec((1, DIL_HEADS // 2, tm, 2 * LANES), head_major),
                    "vt": pl.BlockSpec((1, DIL_HEADS, V_ROWS, tm), time_minor),
                    "mvt": pl.BlockSpec((1, MLSTM_HEADS, MLSTM_V + V_FILL, tm), time_minor)}
    special_shape = {"qk": jax.ShapeDtypeStruct((B, DIL_HEADS // 2, S, 2 * LANES), BF16),
                     "vt": jax.ShapeDtypeStruct((B, DIL_HEADS, V_ROWS, S), BF16),
                     "mvt": jax.ShapeDtypeStruct((B, MLSTM_HEADS, MLSTM_V + V_FILL, S), BF16)}
    return pl.pallas_call(
        _inproj_kernel,
        grid=(N // tm,),
        in_specs=[pl.BlockSpec((tm, D), row),
                  _const_spec((1, 1, D), lambda i: (layer, 0, 0)),
                  pl.BlockSpec((1, 1, D), bsel),
                  pl.BlockSpec((1, 1, D), bsel),
                  _const_spec((1, D, COL_END), lambda i: (layer, 0, 0)),
                  _const_spec((1, 1, LANES), lambda i: (layer, 0, 0)),
                  _const_spec((1, 1, LANES), lambda i: (layer, 0, 0))],
        out_specs=[special_spec[o] if isinstance(o, str) else pl.BlockSpec((tm, o[0]), row) for o in outs],
        out_shape=[special_shape[o] if isinstance(o, str) else jax.ShapeDtypeStruct((N, o[0]), o[1]) for o in outs],
        compiler_params=_cparams(("parallel",), 52),
        name="inproj",
    )(x2, gain, sc, sh, w_pad, dqg, dkg)


def _mla_prep_kernel(a_ref, qn_ref, kvn_ref, wuq_ref, wkv_ref, qg_ref, qgs_ref, kg_ref, cos_ref, sin_ref,
                     q_ref, k_ref, vt_ref):
    HW = MLA_HEADS * LANES
    a = a_ref[...]
    cq = a[:, :MLA_Q_LORA]
    ckv = a[:, MLA_Q_LORA:MLA_Q_LORA + MLA_KV_LORA]
    kr_tile = a[:, MLA_Q_LORA + MLA_KV_LORA:]

    def rms(z, g):
        return z * lax.rsqrt(jnp.mean(z * z, axis=-1, keepdims=True) + EPS) * g

    cqn = rms(cq, qn_ref[0]).astype(BF16)
    ckvn = rms(ckv, kvn_ref[0]).astype(BF16)
    qr = jnp.dot(cqn, wuq_ref[0], preferred_element_type=F32)
    kvr = jnp.dot(ckvn, wkv_ref[0], preferred_element_type=F32)
    _store_v_tiles_t(kvr[:, HW:], vt_ref)

    cosf, sins = cos_ref[...], sin_ref[...]
    q_scale = MLA_QK ** -0.5 * LOG2E
    gc = cosf * (qg_ref[0] * q_scale)
    gs = sins * (qgs_ref[0] * q_scale)
    kg = kg_ref[0]
    half = MLA_ROPE // 2
    krg = pltpu.roll(kr_tile, MLA_NOPE, 1) * kg
    kr_rot = krg * cosf + (pltpu.roll(krg, LANES - half, 1) + pltpu.roll(krg, half, 1)) * sins
    kr_ss = jnp.sum(kr_tile * kr_tile, axis=-1, keepdims=True)
    for h in range(MLA_HEADS):
        sl = slice(h * LANES, (h + 1) * LANES)
        pair_sl = slice((h % 2) * LANES, (h % 2 + 1) * LANES)
        z = qr[:, sl]
        rq = lax.rsqrt(jnp.sum(z * z, axis=-1, keepdims=True) / MLA_QK + EPS)
        q_ref[0, h // 2, :, pair_sl] = ((z * gc + qr[:, HW + h * LANES:HW + (h + 1) * LANES] * gs) * rq).astype(BF16)
        zk = kvr[:, sl]
        rk = lax.rsqrt((jnp.sum(zk * zk, axis=-1, keepdims=True) + kr_ss) / MLA_QK + EPS)
        k_ref[0, h // 2, :, pair_sl] = ((zk * kg + kr_rot) * rk).astype(BF16)


def _rope_tables(S):
    half = MLA_ROPE // 2
    inv = ROPE_THETA ** (-jnp.arange(half, dtype=F32) / half)
    ang = jnp.arange(S).astype(F32)[:, None] * inv[None, :]
    cos, sin = jnp.cos(ang), jnp.sin(ang)
    one = jnp.ones((S, MLA_NOPE), F32)
    z = lambda n: jnp.zeros((S, n), F32)
    tail = LANES - MLA_QK
    cosf = jnp.concatenate([one, cos, cos, z(tail)], axis=1)
    sins = jnp.concatenate([z(MLA_NOPE), -sin, sin, z(tail)], axis=1)
    return cosf, sins


def _mla_prep(oa, S, layer, qn, kvn, wuq, wkv, qg, qgs, kg, tables, tm=512):
    N = oa.shape[0]
    B = N // S
    per_b = S // tm
    row = lambda i: (i, 0)
    pos = lambda i: (i % per_b, 0)
    HW = MLA_HEADS * LANES
    return pl.pallas_call(
        _mla_prep_kernel,
        grid=(N // tm,),
        in_specs=[pl.BlockSpec((tm, 512), row),
                  _const_spec((1, 1, MLA_Q_LORA), lambda i: (layer, 0, 0)),
                  _const_spec((1, 1, MLA_KV_LORA), lambda i: (layer, 0, 0)),
                  _const_spec((1, MLA_Q_LORA, 2 * HW), lambda i: (layer, 0, 0)),
                  _const_spec((1, MLA_KV_LORA, HW + MLA_HEADS * MLA_V), lambda i: (layer, 0, 0)),
                  _const_spec((1, 1, LANES), lambda i: (layer, 0, 0)),
                  _const_spec((1, 1, LANES), lambda i: (layer, 0, 0)),
                  _const_spec((1, 1, LANES), lambda i: (layer, 0, 0)),
                  pl.BlockSpec((tm, LANES), pos),
                  pl.BlockSpec((tm, LANES), pos)],
        out_specs=[pl.BlockSpec((1, MLA_HEADS // 2, tm, 2 * LANES), lambda i: (i // per_b, 0, i % per_b, 0)),
                   pl.BlockSpec((1, MLA_HEADS // 2, tm, 2 * LANES), lambda i: (i // per_b, 0, i % per_b, 0)),
                   pl.BlockSpec((1, MLA_HEADS, V_ROWS, tm), lambda i: (i // per_b, 0, 0, i % per_b))],
        out_shape=[jax.ShapeDtypeStruct((B, MLA_HEADS // 2, S, 2 * LANES), BF16),
                   jax.ShapeDtypeStruct((B, MLA_HEADS // 2, S, 2 * LANES), BF16),
                   jax.ShapeDtypeStruct((B, MLA_HEADS, V_ROWS, S), BF16)],
        compiler_params=_cparams(("parallel",), 32),
        name="mla_prep",
    )(oa, qn, kvn, wuq, wkv, qg, qgs, kg, *tables)


def _flash_kernel(qi_ref, kj_ref, qn_ref, kn_ref, q0_ref, k0_ref, vt_ref, *rest, band, T, W, n_pairs):
    if band:
        bias_ref, o_ref, m_ref, acc_ref, s_ref = rest
    else:
        o_ref, m_ref, acc_ref, s_ref = rest
    p_id = pl.program_id(2)
    qi = qi_ref[p_id]
    kj = kj_ref[p_id]
    if band:
        first = kj == qi
        last = kj == jnp.maximum(qi - W, 0)
    else:
        first = kj == 0
        last = kj == qi

    units = [(hh, slice(j * MXU_COLS, (j + 1) * MXU_COLS)) for hh in range(2) for j in range(T // MXU_COLS)]

    def scores(q_ref, k_ref, hh, qs, pair, rows=T):
        q = q_ref[0, 0, qs, hh * LANES:(hh + 1) * LANES]
        k = k_ref[0, 0, 0:rows, hh * LANES:(hh + 1) * LANES]
        s = lax.dot_general(k, q, (((1,), (1,)), ((), ())), preferred_element_type=F32)
        if band:
            s = s + bias_ref[qi_ref[pair] - kj_ref[pair], :, qs]
        return s

    @pl.when(p_id == 0)
    def _():
        for hh, qs in units:
            s_ref[0, hh, :, qs] = scores(q0_ref, k0_ref, hh, qs, 0)

    @pl.when(first)
    def _():
        m_ref[...] = jnp.full(m_ref.shape, NEG_INF, F32)
        acc_ref[...] = jnp.zeros(acc_ref.shape, F32)

    nxt = jnp.minimum(p_id + 1, n_pairs - 1)

    def step(cur_diag, next_diag, slot):
        for hh, qs in units:
            rows_next = qs.stop if next_diag else T
            s_ref[1 - slot, hh, 0:rows_next, qs] = scores(qn_ref, kn_ref, hh, qs, nxt, rows_next)
            rows = qs.stop if cur_diag else T
            st = s_ref[slot, hh, 0:rows, qs]
            if cur_diag:
                key = lax.broadcasted_iota(jnp.int32, (rows, MXU_COLS), 0)
                qry = lax.broadcasted_iota(jnp.int32, (rows, MXU_COLS), 1) + qs.start
                st = jnp.where(key <= qry, st, NEG_INF)
            m_prev = m_ref[hh, :, qs]
            m_new = jnp.maximum(m_prev, jnp.max(st, axis=0, keepdims=True))
            alpha = jnp.exp2(m_prev - m_new)
            pt = jnp.exp2(st - m_new).astype(BF16)
            acc_ref[hh, :, qs] = (acc_ref[hh, :, qs] * alpha
                                  + jnp.dot(vt_ref[0, hh, :, 0:rows], pt, preferred_element_type=F32))
            m_ref[hh, :, qs] = m_new

    parity = p_id % 2
    for slot in range(2):
        here = parity == slot
        if band:
            pl.when(here)(functools.partial(step, False, False, slot))
        else:
            cur_diag = last
            next_diag = jnp.logical_and(jnp.logical_not(cur_diag), qi_ref[nxt] == kj_ref[nxt])
            plain = jnp.logical_not(jnp.logical_or(cur_diag, next_diag))
            pl.when(jnp.logical_and(here, cur_diag))(functools.partial(step, True, False, slot))
            pl.when(jnp.logical_and(here, next_diag))(functools.partial(step, False, True, slot))
            pl.when(jnp.logical_and(here, plain))(functools.partial(step, False, False, slot))

    @pl.when(last)
    def _():
        outs = []
        for hh in range(2):
            a = acc_ref[hh]
            outs.append(a[0:V_DIM, :] / a[V_DIM:V_DIM + 1, :])
        o_ref[0] = jnp.concatenate(outs, axis=0).T.astype(o_ref.dtype)


def _band_bias(T, W):
    d = (jnp.arange(W + 1)[:, None, None] * T + jnp.arange(T)[None, None, :] - jnp.arange(T)[None, :, None])
    cw = jnp.zeros(d.shape, F32)
    for window, dil in DIL_PATTERNS:
        cw = cw + ((d >= 0) & (d <= window) & (d % dil == 0)).astype(F32)
    return jnp.log2(cw)


def _flash(q, k, vt, band, T=FLASH_TILE):
    B, HP, S, _ = q.shape
    H = 2 * HP
    nq = S // T
    W = DIL_MAX_WINDOW // T if band else 0
    if band:
        pairs = [(i, i - j) for i in range(nq) for j in range(min(W, i) + 1)]
    else:
        pairs = [(i, j) for i in range(nq) for j in range(i + 1)]
    n = len(pairs)
    qi = jnp.asarray(np.array([p[0] for p in pairs], np.int32))
    kj = jnp.asarray(np.array([p[1] for p in pairs], np.int32))
    nxt = lambda p: jnp.minimum(p + 1, n - 1)
    in_specs = [pl.BlockSpec((1, 1, T, 2 * LANES), lambda b, h, p, qi, kj: (b, h, qi[nxt(p)], 0)),
                pl.BlockSpec((1, 1, T, 2 * LANES), lambda b, h, p, qi, kj: (b, h, kj[nxt(p)], 0)),
                pl.BlockSpec((1, 1, T, 2 * LANES), lambda b, h, p, qi, kj: (b, h, qi[0], 0)),
                pl.BlockSpec((1, 1, T, 2 * LANES), lambda b, h, p, qi, kj: (b, h, kj[0], 0)),
                pl.BlockSpec((1, 2, V_ROWS, T), lambda b, h, p, qi, kj: (b, h, 0, kj[p]))]
    args = [q, k, q, k, vt]
    if band:
        in_specs.append(_const_spec((W + 1, T, T), lambda b, h, p, qi, kj: (0, 0, 0)))
        args.append(_band_bias(T, W))
    return pl.pallas_call(
        functools.partial(_flash_kernel, band=band, T=T, W=W, n_pairs=n),
        grid_spec=pltpu.PrefetchScalarGridSpec(
            num_scalar_prefetch=2,
            grid=(B, H // 2, n),
            in_specs=in_specs,
            out_specs=pl.BlockSpec((1, T, LANES), lambda b, h, p, qi, kj: (b, qi[p], h)),
            scratch_shapes=[pltpu.VMEM((2, 1, T), F32), pltpu.VMEM((2, V_ROWS, T), F32),
                            pltpu.VMEM((2, 2, T, T), F32)]),
        out_shape=jax.ShapeDtypeStruct((B, S, H * V_DIM), BF16),
        compiler_params=_cparams(("arbitrary", "arbitrary", "arbitrary"), 52),
        name="flash_band" if band else "flash_causal",
    )(qi, kj, *args)


def _mlstm_kernel(qk_ref, vt_ref, o_ref, if_ref, cw_ref, cb_ref, gb_ref, hg_ref, y_ref,
                  xs_ref, cn_ref, m_ref, *, Lc):
    H, DK, DV = MLSTM_HEADS, MLSTM_QK, MLSTM_V
    HALO = 8

    @pl.when(pl.program_id(1) == 0)
    def _():
        xs_ref[0:HALO, :] = jnp.zeros((HALO, 2 * H * DK), F32)
        cn_ref[...] = jnp.zeros(cn_ref.shape, F32)
        m_ref[...] = jnp.zeros(m_ref.shape, F32)

    x = qk_ref[...]
    xs_ref[HALO:HALO + Lc, :] = x
    y = cb_ref[0]
    for j in range(CONV_K):
        y = y + cw_ref[0, j:j + 1, :] * xs_ref[pl.ds(HALO - (CONV_K - 1) + j, Lc), :]
    xs_ref[0:HALO, :] = x[Lc - HALO:, :]
    qk = y * _sigmoid(y)
    q_t = qk[:, :H * DK].T.astype(BF16)
    k_all = (qk[:, H * DK:] * (DK ** -0.5)).astype(BF16)

    g = if_ref[...] + gb_ref[0]
    logf = jnp.minimum(g, 0.0) - jnp.log(1.0 + jnp.exp(-jnp.abs(g)))
    src = lax.broadcasted_iota(jnp.int32, (Lc, Lc), 0)
    dst = lax.broadcasted_iota(jnp.int32, (Lc, Lc), 1)
    tri = jnp.where(dst <= src, 1.0, 0.0).astype(F32)
    bcum = jnp.dot(tri, logf, preferred_element_type=F32, precision=HIGHEST)
    causal = src <= dst
    g_t = g.T
    bcum_t = bcum.T

    for h in range(H):
        b_row = bcum_t[H + h:H + h + 1, :]
        w_col = g[:, h:h + 1] - bcum[:, H + h:H + h + 1]
        w_row = g_t[h:h + 1, :] - b_row
        m_prev = m_ref[h:h + 1, 0:1]
        dlog = jnp.where(causal, w_col + b_row, NEG_INF)
        inter = b_row + m_prev
        m_t = jnp.maximum(inter, jnp.max(dlog, axis=0, keepdims=True))
        qh_t = q_t[h * DK:(h + 1) * DK, :]
        kh = k_all[:, h * DK:(h + 1) * DK]
        s_t = jnp.dot(kh, qh_t, preferred_element_type=F32)
        sc_t = (s_t * jnp.exp(dlog - m_t)).astype(BF16)
        decay = jnp.exp(inter - m_t)
        cn = cn_ref[h]
        vt = vt_ref[0, h]
        num = (decay * jnp.dot(cn.astype(BF16), qh_t, preferred_element_type=F32)
               + jnp.dot(vt, sc_t, preferred_element_type=F32))
        den = num[DV:DV + 1, :]
        hh = num[0:DV, :] / jnp.maximum(jnp.abs(den), jnp.exp(-m_t))
        hn = hh * lax.rsqrt(jnp.mean(hh * hh, axis=0, keepdims=True) + EPS)
        o_gate = _sigmoid(o_ref[:, h * DV:(h + 1) * DV].astype(F32))
        y_ref[:, h * DV:(h + 1) * DV] = (hn.T * hg_ref[0, :, h * DV:(h + 1) * DV] * o_gate).astype(y_ref.dtype)

        b_end = b_row[:, Lc - 1:Lc]
        wlog = b_end + w_row
        m_new = jnp.maximum(b_end + m_prev, jnp.max(wlog, axis=1, keepdims=True))
        cd = jnp.exp(b_end + m_prev - m_new)
        vw = (vt.astype(F32) * jnp.exp(wlog - m_new)).astype(BF16)
        cn_ref[h] = cd * cn + jnp.dot(vw, kh, preferred_element_type=F32)
        m_ref[h:h + 1, :] = jnp.broadcast_to(m_new, (1, LANES))


def _mlstm(mqk, mvt, mo, mif, B, S, layer, conv_w, conv_b, gate_b, head_gain, Lc=256):
    H, DK, DV = MLSTM_HEADS, MLSTM_QK, MLSTM_V
    nc = S // Lc
    row = lambda b, c: (b * nc + c, 0)
    VR = mvt.shape[2]
    return pl.pallas_call(
        functools.partial(_mlstm_kernel, Lc=Lc),
        grid=(B, nc),
        in_specs=[pl.BlockSpec((Lc, 2 * H * DK), row),
                  pl.BlockSpec((1, H, VR, Lc), lambda b, c: (b, 0, 0, c)),
                  pl.BlockSpec((Lc, H * DV), row),
                  pl.BlockSpec((Lc, LANES), row),
                  _const_spec((1, CONV_K, 2 * H * DK), lambda b, c: (layer, 0, 0)),
                  _const_spec((1, 1, 2 * H * DK), lambda b, c: (layer, 0, 0)),
                  _const_spec((1, 1, LANES), lambda b, c: (layer, 0, 0)),
                  _const_spec((1, 1, H * DV), lambda b, c: (layer, 0, 0))],
        out_specs=pl.BlockSpec((Lc, H * DV), row),
        out_shape=jax.ShapeDtypeStruct((B * S, H * DV), BF16),
        scratch_shapes=[pltpu.VMEM((Lc + 8, 2 * H * DK), F32), pltpu.VMEM((H, VR, DK), F32),
                        pltpu.VMEM((8, LANES), F32)],
        compiler_params=_cparams(("arbitrary", "arbitrary"), 32),
        name="mlstm",
    )(mqk, mvt, mo, mif, conv_w, conv_b, gate_b, head_gain)


def _merge_kernel(ya_ref, yb_ref, yc_ref, g_ref, x_ref, ga_ref, wa_ref, wb_ref, wc_ref, wo_ref, out_ref):
    def branch(y_ref, w_ref, c):
        gate = _sigmoid(g_ref[:, c * D_MODEL:(c + 1) * D_MODEL].astype(F32))
        return gate * jnp.dot(y_ref[...], w_ref[0], preferred_element_type=F32)

    merged = branch(ya_ref, wa_ref, 0) + branch(yb_ref, wb_ref, 1) + branch(yc_ref, wc_ref, 2)
    y = jnp.dot(merged.astype(BF16), wo_ref[0], preferred_element_type=F32)
    out_ref[...] = x_ref[...] + ga_ref[0] * y


def _merge(ya, yb, yc, gates, x2, S, layer, ga, wa, wb, wc, wo, tm=512):
    N, D = x2.shape
    per_b = S // tm
    row = lambda i: (i, 0)
    wsel = lambda i: (layer, 0, 0)
    return pl.pallas_call(
        _merge_kernel,
        grid=(N // tm,),
        in_specs=[pl.BlockSpec((tm, 512), row), pl.BlockSpec((tm, 512), row), pl.BlockSpec((tm, 512), row),
                  pl.BlockSpec((tm, N_BRANCH * D), row), pl.BlockSpec((tm, D), row),
                  pl.BlockSpec((1, 1, D), lambda i: (i // per_b, 0, 0)),
                  _const_spec((1, 512, D), wsel), _const_spec((1, 512, D), wsel), _const_spec((1, 512, D), wsel),
                  _const_spec((1, D, D), wsel)],
        out_specs=pl.BlockSpec((tm, D), row),
        out_shape=jax.ShapeDtypeStruct((N, D), F32),
        compiler_params=_cparams(("parallel",), 48),
        name="merge",
    )(ya, yb, yc, gates, x2, ga, wa, wb, wc, wo)


def _ffn_kernel(x_ref, gain_ref, sc_ref, sh_ref, gf_ref, wr_ref, br_ref, wg_ref, wu_ref, wd_ref, out_ref):
    x = x_ref[...]
    tm = x.shape[0]
    ms = jnp.mean(x * x, axis=-1, keepdims=True)
    hf = x * lax.rsqrt(ms + EPS) * gain_ref[0] * (1.0 + sc_ref[0]) + sh_ref[0]

    hf_hi = hf.astype(BF16)
    hf_lo = (hf - hf_hi.astype(F32)).astype(BF16)
    parts = (jnp.dot(hf_hi, wr_ref[0], preferred_element_type=F32)
             + jnp.dot(hf_lo, wr_ref[0], preferred_element_type=F32))
    logits = parts[:, :LANES] + parts[:, LANES:] + br_ref[0]
    lane_i = lax.broadcasted_iota(jnp.int32, (tm, LANES), 1)
    lane = lane_i.astype(F32)
    big = float(LANES)

    def first_argmax(z):
        zmax = jnp.max(z, axis=1, keepdims=True)
        return zmax, jnp.min(jnp.where(z == zmax, lane, big), axis=1, keepdims=True)

    lg = jnp.where(lane_i < N_GROUPS, logits, NEG_INF)
    gmax, g_idx = first_argmax(lg)
    p_sel = 1.0 / jnp.sum(jnp.exp(lg - gmax), axis=1, keepdims=True)
    lane_grp = ((lane_i - N_GROUPS) // EXPERTS_PER_GROUP).astype(F32)
    in_grp = (lane_i >= N_GROUPS) & (lane_i < N_GROUPS + N_EXPERTS) & (lane_grp == g_idx)
    le = jnp.where(in_grp, logits, NEG_INF)
    v1, i1 = first_argmax(le)
    le2 = jnp.where(lane == i1, NEG_INF, le)
    v2, i2 = first_argmax(le2)
    e2 = jnp.exp(v2 - v1)
    den = 1.0 + e2
    comb = jnp.where(lane == i1, (1.0 / den) * p_sel, 0.0) + jnp.where(lane == i2, (e2 / den) * p_sel, 0.0)

    hb = hf_hi
    acc = jnp.zeros((tm, D_MODEL), F32)
    for e in range(N_EXPERTS):
        gate = jnp.dot(hb, wg_ref[0, e], preferred_element_type=F32)
        up = jnp.dot(hb, wu_ref[0, e], preferred_element_type=F32)
        hid = gate * _sigmoid(gate) * up * comb[:, N_GROUPS + e:N_GROUPS + e + 1]
        acc = acc + jnp.dot(hid.astype(BF16), wd_ref[0, e], preferred_element_type=F32)
    out_ref[...] = x + gf_ref[0] * acc


def _ffn(x2, S, layer, gain, sc, sh, gf, wr, br, wg, wu, wd, tm=512):
    N, D = x2.shape
    per_b = S // tm
    row = lambda i: (i, 0)
    bsel = lambda i: (i // per_b, 0, 0)
    wsel = lambda i: (layer, 0, 0, 0)
    return pl.pallas_call(
        _ffn_kernel,
        grid=(N // tm,),
        in_specs=[pl.BlockSpec((tm, D), row),
                  _const_spec((1, 1, D), lambda i: (layer, 0, 0)),
                  pl.BlockSpec((1, 1, D), bsel), pl.BlockSpec((1, 1, D), bsel), pl.BlockSpec((1, 1, D), bsel),
                  _const_spec((1, D, 2 * LANES), lambda i: (layer, 0, 0)),
                  _const_spec((1, 1, LANES), lambda i: (layer, 0, 0)),
                  _const_spec((1, N_EXPERTS, D, EXPERT_FF), wsel),
                  _const_spec((1, N_EXPERTS, D, EXPERT_FF), wsel),
                  _const_spec((1, N_EXPERTS, EXPERT_FF, D), wsel)],
        out_specs=pl.BlockSpec((tm, D), row),
        out_shape=jax.ShapeDtypeStruct((N, D), F32),
        compiler_params=_cparams(("parallel",), 56),
        name="ffn",
    )(x2, gain, sc, sh, gf, wr, br, wg, wu, wd)


def _pad_last(a, n):
    return jnp.pad(a, [(0, 0)] * (a.ndim - 1) + [(0, n - a.shape[-1])])


def _prep_w_in(w_in):
    pts = np.cumsum(IN_SIZES)[:-1].tolist()
    cq, ckv, kr, m_qk, m_v, m_o, m_i, m_f, d_q, d_k, d_v, gates = jnp.split(w_in, pts, axis=-1)
    seg_a = _pad_last(jnp.concatenate([cq, ckv, kr], axis=-1), 512)
    seg_if = _pad_last(jnp.concatenate([m_i, m_f], axis=-1), LANES)
    w = jnp.concatenate([seg_a, m_qk, m_v, m_o, seg_if,
                         d_q, d_k,
                         d_v, gates], axis=-1)
    assert w.shape[-1] == COL_END
    return w.astype(BF16)


def kernel(x, c, w_ada, b_ada, attn_norm, w_in, mla_q_norm, mla_kv_norm, mla_w_uq, mla_w_ukv, mla_q_gain, mla_k_gain, mlstm_conv_w, mlstm_conv_b, mlstm_b_i, mlstm_b_f, mlstm_head_gain, dil_q_gain, dil_k_gain, w_branch_a, w_branch_b, w_branch_c, w_out, ffn_norm, w_router_group, b_router_group, w_router_expert, b_router_expert, w_exp_gate, w_exp_up, w_exp_down):
    B, S, D = x.shape
    L = w_ada.shape[0]
    N = B * S
    assert D == D_MODEL and S % DIL_MAX_WINDOW == 0

    mod = _adaln(c, w_ada, b_ada)

    w_pad = _prep_w_in(w_in)
    half = MLA_ROPE // 2
    swap_rope = lambda a: jnp.concatenate(
        [jnp.zeros_like(a[..., :MLA_NOPE]), a[..., MLA_NOPE + half:], a[..., MLA_NOPE:MLA_NOPE + half]], axis=-1)
    uq = mla_w_uq.reshape(L, MLA_Q_LORA, MLA_HEADS, MLA_QK)
    wuq = jnp.concatenate([_pad_last(u, LANES).reshape(L, MLA_Q_LORA, MLA_HEADS * LANES)
                           for u in (uq, swap_rope(uq))], axis=-1).astype(BF16)
    ukv = mla_w_ukv.reshape(L, MLA_KV_LORA, MLA_HEADS, MLA_NOPE + MLA_V)
    wkv = jnp.concatenate([_pad_last(ukv[..., :MLA_NOPE], LANES).reshape(L, MLA_KV_LORA, MLA_HEADS * LANES),
                           ukv[..., MLA_NOPE:].reshape(L, MLA_KV_LORA, MLA_HEADS * MLA_V)], axis=-1).astype(BF16)
    rows = lambda a: a[:, None, :]
    qg = rows(_pad_last(mla_q_gain, LANES))
    qgs = rows(_pad_last(swap_rope(mla_q_gain), LANES))
    kg = rows(_pad_last(mla_k_gain, LANES))
    dqg = rows(_pad_last(dil_q_gain, LANES))
    dkg = rows(_pad_last(dil_k_gain, LANES))
    gate_b = rows(_pad_last(jnp.concatenate([mlstm_b_i, mlstm_b_f], axis=-1), LANES))
    tables = _rope_tables(S)
    wa, wb, wc, wo = (w.astype(BF16) for w in (w_branch_a, w_branch_b, w_branch_c, w_out))
    wr = _pad_last(jnp.concatenate([w_router_group, w_router_expert], axis=-1), LANES)
    wr_hi = lax.reduce_precision(wr, exponent_bits=8, mantissa_bits=7)
    wr = jnp.concatenate([wr_hi, wr - wr_hi], axis=-1).astype(BF16)
    br = rows(_pad_last(jnp.concatenate([b_router_group, b_router_expert], axis=-1), LANES))
    wg, wu, wd = (w.astype(BF16) for w in (w_exp_gate, w_exp_up, w_exp_down))
    attn_norm, ffn_norm, mla_q_norm, mla_kv_norm, mlstm_conv_b, mlstm_head_gain = (
        rows(a) for a in (attn_norm, ffn_norm, mla_q_norm, mla_kv_norm, mlstm_conv_b, mlstm_head_gain))

    x2 = x.reshape(N, D)
    for l in range(L):
        sh_a, sc_a, g_a, sh_f, sc_f, g_f = (mod[l, :, i] for i in range(6))
        oa, mqk, mv, mo, mif, qd, kd, vd, gates = _inproj(x2, S, l, attn_norm, sc_a, sh_a, w_pad, dqg, dkg)
        q, k, vt = _mla_prep(oa, S, l, mla_q_norm, mla_kv_norm, wuq, wkv, qg, qgs, kg, tables)
        ya = _flash(q, k, vt, band=False)
        yb = _mlstm(mqk, mv, mo, mif, B, S, l, mlstm_conv_w, mlstm_conv_b, gate_b, mlstm_head_gain)
        yc = _flash(qd, kd, vd, band=True)
        x2 = _merge(ya.reshape(N, -1), yb, yc.reshape(N, -1), gates, x2, S, l, g_a, wa, wb, wc, wo)
        x2 = _ffn(x2, S, l, ffn_norm, sc_f, sh_f, g_f, wr, br, wg, wu, wd)
    return x2.reshape(B, S, D)
```

```python
import functools

import jax
import jax.numpy as jnp
import numpy as np
from jax import lax
from jax.experimental import pallas as pl
from jax.experimental.pallas import tpu as pltpu

F32 = jnp.float32
BF16 = jnp.bfloat16
HIGHEST = lax.Precision.HIGHEST
NEG_INF = float("-inf")

EPS = 1e-6
LANES = 128
MXU_COLS = 256
VMEM_BYTES_V7X = 64 * 1024 * 1024

D_MODEL = 1024
MLA_HEADS = 8
MLA_Q_LORA = 256
MLA_KV_LORA = 128
MLA_NOPE = 64
MLA_ROPE = 32
MLA_V = 64
MLA_QK = MLA_NOPE + MLA_ROPE
ROPE_THETA = 10000.0
MLSTM_HEADS = 4
MLSTM_QK = 64
MLSTM_V = 128
CONV_K = 4
DIL_HEADS = 8
DIL_HEAD_DIM = 64
DIL_PATTERNS = ((128, 1), (512, 4), (2048, 16))
DIL_WIDTH = DIL_HEADS * DIL_HEAD_DIM
N_GROUPS = 4
EXPERTS_PER_GROUP = 4
N_EXPERTS = N_GROUPS * EXPERTS_PER_GROUP
EXPERT_FF = 256
N_BRANCH = 3
IN_SIZES = (MLA_Q_LORA, MLA_KV_LORA, MLA_ROPE,
            2 * MLSTM_HEADS * MLSTM_QK, MLSTM_HEADS * MLSTM_V, MLSTM_HEADS * MLSTM_V, MLSTM_HEADS, MLSTM_HEADS,
            DIL_WIDTH, DIL_WIDTH, DIL_WIDTH, N_BRANCH * D_MODEL)

COL_A = 0
COL_MQK = 512
COL_MV = 1024
COL_MO = 1536
COL_MIF = 2048
COL_DQ = 2176
COL_DK = 2688
COL_DV = 3200
COL_G = 3712
COL_END = 6784

FLASH_TILE = 1024
DIL_MAX_WINDOW = max(w for w, _ in DIL_PATTERNS)
V_DIM = 64
V_FILL = 16
V_ROWS = V_DIM + V_FILL
LOG2E = 1.4426950408889634
assert MLA_V == V_DIM and DIL_HEAD_DIM == V_DIM


def _store_v_tiles_t(v, vt_ref, dim=V_DIM):
    tm = v.shape[0]
    v_t = v.astype(BF16).T
    fill = (lax.broadcasted_iota(jnp.int32, (V_FILL, tm), 0) == 0).astype(BF16)
    for h in range(v.shape[1] // dim):
        vt_ref[0, h, 0:dim, :] = v_t[h * dim:(h + 1) * dim, :]
        vt_ref[0, h, dim:dim + V_FILL, :] = fill


def _cparams(sem, vmem_mb):
    return pltpu.CompilerParams(dimension_semantics=sem, vmem_limit_bytes=vmem_mb * 1024 * 1024)


def _sigmoid(x):
    return 1.0 / (1.0 + jnp.exp(-x))


def _const_spec(shape, index_map):
    return pl.BlockSpec(shape, index_map, pipeline_mode=pl.Buffered(1))


def _adaln_kernel(c_ref, w_ref, b_ref, o_ref):
    c = c_ref[...]
    ca = c * _sigmoid(c)
    o_ref[0] = jnp.dot(ca, w_ref[0], preferred_element_type=F32, precision=HIGHEST) + b_ref[0]


def _adaln(c, w_ada, b_ada):
    L, D, D6 = w_ada.shape
    B = c.shape[0]
    rows = 8
    cp = jnp.zeros((rows, D), F32).at[:B].set(c)
    out = pl.pallas_call(
        _adaln_kernel,
        grid=(L, D6 // D),
        in_specs=[pl.BlockSpec((rows, D), lambda l, j: (0, 0)),
                  pl.BlockSpec((1, D, D), lambda l, j: (l, 0, j)),
                  pl.BlockSpec((1, 1, D), lambda l, j: (l, 0, j))],
        out_specs=pl.BlockSpec((1, rows, D), lambda l, j: (l, 0, j)),
        out_shape=jax.ShapeDtypeStruct((L, rows, D6), F32),
        compiler_params=_cparams(("parallel", "parallel"), 32),
        name="adaln",
    )(cp, w_ada, b_ada.reshape(L, 1, D6))
    return out[:, :B].reshape(L, B, D6 // D, 1, D)


def _inproj_kernel(x_ref, gain_ref, sc_ref, sh_ref, w_ref, dqg_ref, dkg_ref,
                   oa_ref, mqk_ref, mv_ref, mo_ref, mif_ref, qd_ref, kd_ref, vd_ref, g_ref):
    x = x_ref[...]
    ms = jnp.mean(x * x, axis=-1, keepdims=True)
    y = x * lax.rsqrt(ms + EPS) * gain_ref[0]
    u = (y * (1.0 + sc_ref[0]) + sh_ref[0]).astype(BF16)

    def proj(a, b):
        return jnp.dot(u, w_ref[0, :, a:b], preferred_element_type=F32)

    oa_ref[...] = proj(COL_A, COL_MQK)
    mqk_ref[...] = proj(COL_MQK, COL_MV)
    _store_v_tiles_t(proj(COL_MV, COL_MO), mv_ref, MLSTM_V)
    mo_ref[...] = proj(COL_MO, COL_MIF).astype(BF16)
    mif_ref[...] = proj(COL_MIF, COL_DQ)

    half = LANES // 2
    low = lax.broadcasted_iota(jnp.int32, (1, LANES), 1) < half

    def head_norm(z, g):
        for h in range(DIL_HEADS):
            pair = z[:, (h // 2) * LANES:(h // 2 + 1) * LANES]
            zh = jnp.where(low, pair if h % 2 == 0 else pltpu.roll(pair, half, 1), 0.0)
            r = lax.rsqrt(jnp.sum(zh * zh, axis=-1, keepdims=True) / DIL_HEAD_DIM + EPS)
            yield h, zh * r * g

    dq = proj(COL_DQ, COL_DK)
    for h, zn in head_norm(dq, dqg_ref[0]):
        qd_ref[0, h // 2, :, (h % 2) * LANES:(h % 2 + 1) * LANES] = (zn * (DIL_HEAD_DIM ** -0.5 * LOG2E)).astype(BF16)
    dk = proj(COL_DK, COL_DV)
    for h, zn in head_norm(dk, dkg_ref[0]):
        kd_ref[0, h // 2, :, (h % 2) * LANES:(h % 2 + 1) * LANES] = zn.astype(BF16)
    _store_v_tiles_t(proj(COL_DV, COL_G), vd_ref)
    for c in range(N_BRANCH):
        g_ref[:, c * D_MODEL:(c + 1) * D_MODEL] = proj(COL_G + c * D_MODEL, COL_G + (c + 1) * D_MODEL).astype(BF16)


def _inproj(x2, S, layer, gain, sc, sh, w_pad, dqg, dkg, tm=256):
    N, D = x2.shape
    per_b = S // tm
    row = lambda i: (i, 0)
    bsel = lambda i: (i // per_b, 0, 0)
    B = N // S
    outs = [(512, F32), (512, F32), "mvt", (512, BF16), (LANES, F32),
            "qk", "qk", "vt", (3072, BF16)]
    head_major = lambda i: (i // per_b, 0, i % per_b, 0)
    time_minor = lambda i: (i // per_b, 0, 0, i % per_b)
    special_spec = {"qk": pl.BlockSpec((1, DIL_HEADS // 2, tm, 2 * LANES), head_major),
                    "vt": pl.BlockSpec((1, DIL_HEADS, V_ROWS, tm), time_minor),
                    "mvt": pl.BlockSpec((1, MLSTM_HEADS, MLSTM_V + V_FILL, tm), time_minor)}
    special_shape = {"qk": jax.ShapeDtypeStruct((B, DIL_HEADS // 2, S, 2 * LANES), BF16),
                     "vt": jax.ShapeDtypeStruct((B, DIL_HEADS, V_ROWS, S), BF16),
                     "mvt": jax.ShapeDtypeStruct((B, MLSTM_HEADS, MLSTM_V + V_FILL, S), BF16)}
    return pl.pallas_call(
        _inproj_kernel,
        grid=(N // tm,),
        in_specs=[pl.BlockSpec((tm, D), row),
                  _const_spec((1, 1, D), lambda i: (layer, 0, 0)),
                  pl.BlockSpec((1, 1, D), bsel),
                  pl.BlockSpec((1, 1, D), bsel),
                  _const_spec((1, D, COL_END), lambda i: (layer, 0, 0)),
                  _const_spec((1, 1, LANES), lambda i: (layer, 0, 0)),
                  _const_spec((1, 1, LANES), lambda i: (layer, 0, 0))],
        out_specs=[special_spec[o] if isinstance(o, str) else pl.BlockSpec((tm, o[0]), row) for o in outs],
        out_shape=[special_shape[o] if isinstance(o, str) else jax.ShapeDtypeStruct((N, o[0]), o[1]) for o in outs],
        compiler_params=_cparams(("parallel",), 52),
        name="inproj",
    )(x2, gain, sc, sh, w_pad, dqg, dkg)


def _mla_prep_kernel(a_ref, qn_ref, kvn_ref, wuq_ref, wkv_ref, qg_ref, qgs_ref, kg_ref, cos_ref, sin_ref,
                     q_ref, k_ref, vt_ref):
    HW = MLA_HEADS * LANES
    a = a_ref[...]
    cq = a[:, :MLA_Q_LORA]
    ckv = a[:, MLA_Q_LORA:MLA_Q_LORA + MLA_KV_LORA]
    kr_tile = a[:, MLA_Q_LORA + MLA_KV_LORA:]

    def rms(z, g):
        return z * lax.rsqrt(jnp.mean(z * z, axis=-1, keepdims=True) + EPS) * g

    cqn = rms(cq, qn_ref[0]).astype(BF16)
    ckvn = rms(ckv, kvn_ref[0]).astype(BF16)
    qr = jnp.dot(cqn, wuq_ref[0], preferred_element_type=F32)
    kvr = jnp.dot(ckvn, wkv_ref[0], preferred_element_type=F32)
    _store_v_tiles_t(kvr[:, HW:], vt_ref)

    cosf, sins = cos_ref[...], sin_ref[...]
    q_scale = MLA_QK ** -0.5 * LOG2E
    gc = cosf * (qg_ref[0] * q_scale)
    gs = sins * (qgs_ref[0] * q_scale)
    kg = kg_ref[0]
    half = MLA_ROPE // 2
    krg = pltpu.roll(kr_tile, MLA_NOPE, 1) * kg
    kr_rot = krg * cosf + (pltpu.roll(krg, LANES - half, 1) + pltpu.roll(krg, half, 1)) * sins
    kr_ss = jnp.sum(kr_tile * kr_tile, axis=-1, keepdims=True)
    for h in range(MLA_HEADS):
        sl = slice(h * LANES, (h + 1) * LANES)
        pair_sl = slice((h % 2) * LANES, (h % 2 + 1) * LANES)
        z = qr[:, sl]
        rq = lax.rsqrt(jnp.sum(z * z, axis=-1, keepdims=True) / MLA_QK + EPS)
        q_ref[0, h // 2, :, pair_sl] = ((z * gc + qr[:, HW + h * LANES:HW + (h + 1) * LANES] * gs) * rq).astype(BF16)
        zk = kvr[:, sl]
        rk = lax.rsqrt((jnp.sum(zk * zk, axis=-1, keepdims=True) + kr_ss) / MLA_QK + EPS)
        k_ref[0, h // 2, :, pair_sl] = ((zk * kg + kr_rot) * rk).astype(BF16)


def _rope_tables(S):
    half = MLA_ROPE // 2
    inv = ROPE_THETA ** (-jnp.arange(half, dtype=F32) / half)
    ang = jnp.arange(S).astype(F32)[:, None] * inv[None, :]
    cos, sin = jnp.cos(ang), jnp.sin(ang)
    one = jnp.ones((S, MLA_NOPE), F32)
    z = lambda n: jnp.zeros((S, n), F32)
    tail = LANES - MLA_QK
    cosf = jnp.concatenate([one, cos, cos, z(tail)], axis=1)
    sins = jnp.concatenate([z(MLA_NOPE), -sin, sin, z(tail)], axis=1)
    return cosf, sins


def _mla_prep(oa, S, layer, qn, kvn, wuq, wkv, qg, qgs, kg, tables, tm=512):
    N = oa.shape[0]
    B = N // S
    per_b = S // tm
    row = lambda i: (i, 0)
    pos = lambda i: (i % per_b, 0)
    HW = MLA_HEADS * LANES
    return pl.pallas_call(
        _mla_prep_kernel,
        grid=(N // tm,),
        in_specs=[pl.BlockSpec((tm, 512), row),
                  _const_spec((1, 1, MLA_Q_LORA), lambda i: (layer, 0, 0)),
                  _const_spec((1, 1, MLA_KV_LORA), lambda i: (layer, 0, 0)),
                  _const_spec((1, MLA_Q_LORA, 2 * HW), lambda i: (layer, 0, 0)),
                  _const_spec((1, MLA_KV_LORA, HW + MLA_HEADS * MLA_V), lambda i: (layer, 0, 0)),
                  _const_spec((1, 1, LANES), lambda i: (layer, 0, 0)),
                  _const_spec((1, 1, LANES), lambda i: (layer, 0, 0)),
                  _const_spec((1, 1, LANES), lambda i: (layer, 0, 0)),
                  pl.BlockSpec((tm, LANES), pos),
                  pl.BlockSpec((tm, LANES), pos)],
        out_specs=[pl.BlockSpec((1, MLA_HEADS // 2, tm, 2 * LANES), lambda i: (i // per_b, 0, i % per_b, 0)),
                   pl.BlockSpec((1, MLA_HEADS // 2, tm, 2 * LANES), lambda i: (i // per_b, 0, i % per_b, 0)),
                   pl.BlockSpec((1, MLA_HEADS, V_ROWS, tm), lambda i: (i // per_b, 0, 0, i % per_b))],
        out_shape=[jax.ShapeDtypeStruct((B, MLA_HEADS // 2, S, 2 * LANES), BF16),
                   jax.ShapeDtypeStruct((B, MLA_HEADS // 2, S, 2 * LANES), BF16),
                   jax.ShapeDtypeStruct((B, MLA_HEADS, V_ROWS, S), BF16)],
        compiler_params=_cparams(("parallel",), 32),
        name="mla_prep",
    )(oa, qn, kvn, wuq, wkv, qg, qgs, kg, *tables)


def _flash_kernel(qi_ref, kj_ref, qn_ref, kn_ref, q0_ref, k0_ref, vt_ref, o_ref, m_ref, acc_ref, s_ref, *, T, n_pairs):
    p_id = pl.program_id(2)
    qi = qi_ref[p_id]
    kj = kj_ref[p_id]
    first = kj == 0
    last = kj == qi

    units = [(hh, slice(j * MXU_COLS, (j + 1) * MXU_COLS)) for hh in range(2) for j in range(T // MXU_COLS)]

    def scores(q_ref, k_ref, hh, qs, rows=T):
        q = q_ref[0, 0, qs, hh * LANES:(hh + 1) * LANES]
        k = k_ref[0, 0, 0:rows, hh * LANES:(hh + 1) * LANES]
        return lax.dot_general(k, q, (((1,), (1,)), ((), ())), preferred_element_type=F32)

    @pl.when(p_id == 0)
    def _():
        for hh, qs in units:
            s_ref[0, hh, :, qs] = scores(q0_ref, k0_ref, hh, qs)

    @pl.when(first)
    def _():
        m_ref[...] = jnp.full(m_ref.shape, NEG_INF, F32)
        acc_ref[...] = jnp.zeros(acc_ref.shape, F32)

    nxt = jnp.minimum(p_id + 1, n_pairs - 1)

    def step(cur_diag, next_diag, slot):
        for hh, qs in units:
            rows_next = qs.stop if next_diag else T
            s_ref[1 - slot, hh, 0:rows_next, qs] = scores(qn_ref, kn_ref, hh, qs, rows_next)
            rows = qs.stop if cur_diag else T
            st = s_ref[slot, hh, 0:rows, qs]
            if cur_diag:
                key = lax.broadcasted_iota(jnp.int32, (rows, MXU_COLS), 0)
                qry = lax.broadcasted_iota(jnp.int32, (rows, MXU_COLS), 1) + qs.start
                st = jnp.where(key <= qry, st, NEG_INF)
            m_prev = m_ref[hh, :, qs]
            m_new = jnp.maximum(m_prev, jnp.max(st, axis=0, keepdims=True))
            alpha = jnp.exp2(m_prev - m_new)
            pt = jnp.exp2(st - m_new).astype(BF16)
            acc_ref[hh, :, qs] = (acc_ref[hh, :, qs] * alpha
                                  + jnp.dot(vt_ref[0, hh, :, 0:rows], pt, preferred_element_type=F32))
            m_ref[hh, :, qs] = m_new

    parity = p_id % 2
    cur_diag = last
    next_diag = jnp.logical_and(jnp.logical_not(cur_diag), qi_ref[nxt] == kj_ref[nxt])
    plain = jnp.logical_not(jnp.logical_or(cur_diag, next_diag))
    for slot in range(2):
        here = parity == slot
        pl.when(jnp.logical_and(here, cur_diag))(functools.partial(step, True, False, slot))
        pl.when(jnp.logical_and(here, next_diag))(functools.partial(step, False, True, slot))
        pl.when(jnp.logical_and(here, plain))(functools.partial(step, False, False, slot))

    @pl.when(last)
    def _():
        outs = []
        for hh in range(2):
            a = acc_ref[hh]
            outs.append(a[0:V_DIM, :] / a[V_DIM:V_DIM + 1, :])
        o_ref[0] = jnp.concatenate(outs, axis=0).T.astype(o_ref.dtype)


def _band_bias(T, W):
    d = (jnp.arange(W + 1)[:, None, None] * T + jnp.arange(T)[None, None, :] - jnp.arange(T)[None, :, None])
    cw = jnp.zeros(d.shape, F32)
    for window, dil in DIL_PATTERNS:
        cw = cw + ((d >= 0) & (d <= window) & (d % dil == 0)).astype(F32)
    return jnp.log2(cw)


def _causal_attention(q, k, vt, T=FLASH_TILE):
    B, HP, S, _ = q.shape
    H = 2 * HP
    nq = S // T
    pairs = [(i, j) for i in range(nq) for j in range(i + 1)]
    n = len(pairs)
    qi = jnp.asarray(np.array([p[0] for p in pairs], np.int32))
    kj = jnp.asarray(np.array([p[1] for p in pairs], np.int32))
    nxt = lambda p: jnp.minimum(p + 1, n - 1)
    in_specs = [pl.BlockSpec((1, 1, T, 2 * LANES), lambda b, h, p, qi, kj: (b, h, qi[nxt(p)], 0)),
                pl.BlockSpec((1, 1, T, 2 * LANES), lambda b, h, p, qi, kj: (b, h, kj[nxt(p)], 0)),
                pl.BlockSpec((1, 1, T, 2 * LANES), lambda b, h, p, qi, kj: (b, h, qi[0], 0)),
                pl.BlockSpec((1, 1, T, 2 * LANES), lambda b, h, p, qi, kj: (b, h, kj[0], 0)),
                pl.BlockSpec((1, 2, V_ROWS, T), lambda b, h, p, qi, kj: (b, h, 0, kj[p]))]
    return pl.pallas_call(
        functools.partial(_flash_kernel, T=T, n_pairs=n),
        grid_spec=pltpu.PrefetchScalarGridSpec(
            num_scalar_prefetch=2,
            grid=(B, H // 2, n),
            in_specs=in_specs,
            out_specs=pl.BlockSpec((1, T, LANES), lambda b, h, p, qi, kj: (b, qi[p], h)),
            scratch_shapes=[pltpu.VMEM((2, 1, T), F32), pltpu.VMEM((2, V_ROWS, T), F32),
                            pltpu.VMEM((2, 2, T, T), F32)]),
        out_shape=jax.ShapeDtypeStruct((B, S, H * V_DIM), BF16),
        compiler_params=_cparams(("arbitrary", "arbitrary", "arbitrary"), 52),
        name="flash_causal",
    )(qi, kj, q, k, q, k, vt)


def _band_kernel(qc_ref, qn_ref, kn_ref, *rest, T, W):
    k_refs, vt_refs = rest[:W + 1], rest[W + 1:2 * W + 2]
    bias_ref, o_ref, m_ref, acc_ref, s_ref = rest[2 * W + 2:]
    i = pl.program_id(2)
    units = [(hh, j) for hh in range(2) for j in range(T // MXU_COLS)]

    def rows_of(d, j):
        if d == 0:
            return 0, (j + 1) * MXU_COLS
        if d == W:
            return j * MXU_COLS, T
        return 0, T

    def produce(q_ref, k_ref, d, hh, j, bias_idx):
        lo, hi = rows_of(d, j)
        qs = slice(j * MXU_COLS, (j + 1) * MXU_COLS)
        q = q_ref[0, 0, qs, hh * LANES:(hh + 1) * LANES]
        k = k_ref[0, 0, lo:hi, hh * LANES:(hh + 1) * LANES]
        s = lax.dot_general(k, q, (((1,), (1,)), ((), ())), preferred_element_type=F32)
        s_ref[d, hh, lo:hi, qs] = s + bias_ref[bias_idx, lo:hi, qs]

    def consume(d, hh, j):
        lo, hi = rows_of(d, j)
        qs = slice(j * MXU_COLS, (j + 1) * MXU_COLS)
        st = s_ref[d, hh, lo:hi, qs]
        m_prev = m_ref[hh, :, qs]
        m_new = jnp.maximum(m_prev, jnp.max(st, axis=0, keepdims=True))
        alpha = jnp.exp2(m_prev - m_new)
        pt = jnp.exp2(st - m_new).astype(BF16)
        acc_ref[hh, :, qs] = (acc_ref[hh, :, qs] * alpha
                              + jnp.dot(vt_refs[d][0, hh, :, lo:hi], pt, preferred_element_type=F32))
        m_ref[hh, :, qs] = m_new

    @pl.when(i == 0)
    def _():
        for hh, j in units:
            produce(qc_ref, k_refs[0], 0, hh, j, 0)

    m_ref[...] = jnp.full(m_ref.shape, NEG_INF, F32)
    acc_ref[...] = jnp.zeros(acc_ref.shape, F32)
    for d in range(W + 1):
        for hh, j in units:
            if d < W:
                produce(qc_ref, k_refs[d + 1], d + 1, hh, j, jnp.where(i >= d + 1, d + 1, W + 1))
            else:
                produce(qn_ref, kn_ref, 0, hh, j, 0)
            consume(d, hh, j)

    outs = []
    for hh in range(2):
        a = acc_ref[hh]
        outs.append(a[0:V_DIM, :] / a[V_DIM:V_DIM + 1, :])
    o_ref[0] = jnp.concatenate(outs, axis=0).T.astype(o_ref.dtype)


def _band_attention(q, k, vt, T=FLASH_TILE):
    B, HP, S, _ = q.shape
    n_q = S // T
    W = DIL_MAX_WINDOW // T
    bias = jnp.concatenate([_band_bias(T, W), jnp.full((1, T, T), NEG_INF, F32)], axis=0)
    qk_block = lambda f: pl.BlockSpec((1, 1, T, 2 * LANES), lambda b, h, i: (b, h, f(i), 0))
    vt_block = lambda f: pl.BlockSpec((1, 2, V_ROWS, T), lambda b, h, i: (b, h, 0, f(i)))
    back = lambda d: (lambda i: jnp.maximum(i - d, 0))
    nxt = lambda i: jnp.minimum(i + 1, n_q - 1)
    in_specs = ([qk_block(lambda i: i), qk_block(nxt), qk_block(nxt)]
                + [qk_block(back(d)) for d in range(W + 1)]
                + [vt_block(back(d)) for d in range(W + 1)]
                + [_const_spec((W + 2, T, T), lambda b, h, i: (0, 0, 0))])
    return pl.pallas_call(
        functools.partial(_band_kernel, T=T, W=W),
        grid=(B, HP, n_q),
        in_specs=in_specs,
        out_specs=pl.BlockSpec((1, T, LANES), lambda b, h, i: (b, i, h)),
        out_shape=jax.ShapeDtypeStruct((B, S, 2 * HP * V_DIM), BF16),
        scratch_shapes=[pltpu.VMEM((2, 1, T), F32), pltpu.VMEM((2, V_ROWS, T), F32),
                        pltpu.VMEM((W + 1, 2, T, T), F32)],
        compiler_params=_cparams(("arbitrary", "arbitrary", "arbitrary"), 58),
        name="flash_band",
    )(q, q, k, *([k] * (W + 1)), *([vt] * (W + 1)), bias)


def _mlstm_kernel(qk_ref, vt_ref, o_ref, if_ref, cw_ref, cb_ref, gb_ref, hg_ref, y_ref,
                  xs_ref, cn_ref, m_ref, *, Lc):
    H, DK, DV = MLSTM_HEADS, MLSTM_QK, MLSTM_V
    HALO = 8

    @pl.when(pl.program_id(1) == 0)
    def _():
        xs_ref[0:HALO, :] = jnp.zeros((HALO, 2 * H * DK), F32)
        cn_ref[...] = jnp.zeros(cn_ref.shape, F32)
        m_ref[...] = jnp.zeros(m_ref.shape, F32)

    x = qk_ref[...]
    xs_ref[HALO:HALO + Lc, :] = x
    y = cb_ref[0]
    for j in range(CONV_K):
        y = y + cw_ref[0, j:j + 1, :] * xs_ref[pl.ds(HALO - (CONV_K - 1) + j, Lc), :]
    xs_ref[0:HALO, :] = x[Lc - HALO:, :]
    qk = y * _sigmoid(y)
    q_t = qk[:, :H * DK].T.astype(BF16)
    k_all = (qk[:, H * DK:] * (DK ** -0.5)).astype(BF16)

    g = if_ref[...] + gb_ref[0]
    logf = jnp.minimum(g, 0.0) - jnp.log(1.0 + jnp.exp(-jnp.abs(g)))
    src = lax.broadcasted_iota(jnp.int32, (Lc, Lc), 0)
    dst = lax.broadcasted_iota(jnp.int32, (Lc, Lc), 1)
    tri = jnp.where(dst <= src, 1.0, 0.0).astype(F32)
    bcum = jnp.dot(tri, logf, preferred_element_type=F32, precision=HIGHEST)
    causal = src <= dst
    g_t = g.T
    bcum_t = bcum.T

    for h in range(H):
        b_row = bcum_t[H + h:H + h + 1, :]
        w_col = g[:, h:h + 1] - bcum[:, H + h:H + h + 1]
        w_row = g_t[h:h + 1, :] - b_row
        m_prev = m_ref[h:h + 1, 0:1]
        dlog = jnp.where(causal, w_col + b_row, NEG_INF)
        inter = b_row + m_prev
        m_t = jnp.maximum(inter, jnp.max(dlog, axis=0, keepdims=True))
        qh_t = q_t[h * DK:(h + 1) * DK, :]
        kh = k_all[:, h * DK:(h + 1) * DK]
        s_t = jnp.dot(kh, qh_t, preferred_element_type=F32)
        sc_t = (s_t * jnp.exp(dlog - m_t)).astype(BF16)
        decay = jnp.exp(inter - m_t)
        cn = cn_ref[h]
        vt = vt_ref[0, h]
        num = (decay * jnp.dot(cn.astype(BF16), qh_t, preferred_element_type=F32)
               + jnp.dot(vt, sc_t, preferred_element_type=F32))
        den = num[DV:DV + 1, :]
        hh = num[0:DV, :] / jnp.maximum(jnp.abs(den), jnp.exp(-m_t))
        hn = hh * lax.rsqrt(jnp.mean(hh * hh, axis=0, keepdims=True) + EPS)
        o_gate = _sigmoid(o_ref[:, h * DV:(h + 1) * DV].astype(F32))
        y_ref[:, h * DV:(h + 1) * DV] = (hn.T * hg_ref[0, :, h * DV:(h + 1) * DV] * o_gate).astype(y_ref.dtype)

        b_end = b_row[:, Lc - 1:Lc]
        wlog = b_end + w_row
        m_new = jnp.maximum(b_end + m_prev, jnp.max(wlog, axis=1, keepdims=True))
        cd = jnp.exp(b_end + m_prev - m_new)
        vw = (vt.astype(F32) * jnp.exp(wlog - m_new)).astype(BF16)
        cn_ref[h] = cd * cn + jnp.dot(vw, kh, preferred_element_type=F32)
        m_ref[h:h + 1, :] = jnp.broadcast_to(m_new, (1, LANES))


def _mlstm(mqk, mvt, mo, mif, B, S, layer, conv_w, conv_b, gate_b, head_gain, Lc=256):
    H, DK, DV = MLSTM_HEADS, MLSTM_QK, MLSTM_V
    nc = S // Lc
    row = lambda b, c: (b * nc + c, 0)
    VR = mvt.shape[2]
    return pl.pallas_call(
        functools.partial(_mlstm_kernel, Lc=Lc),
        grid=(B, nc),
        in_specs=[pl.BlockSpec((Lc, 2 * H * DK), row),
                  pl.BlockSpec((1, H, VR, Lc), lambda b, c: (b, 0, 0, c)),
                  pl.BlockSpec((Lc, H * DV), row),
                  pl.BlockSpec((Lc, LANES), row),
                  _const_spec((1, CONV_K, 2 * H * DK), lambda b, c: (layer, 0, 0)),
                  _const_spec((1, 1, 2 * H * DK), lambda b, c: (layer, 0, 0)),
                  _const_spec((1, 1, LANES), lambda b, c: (layer, 0, 0)),
                  _const_spec((1, 1, H * DV), lambda b, c: (layer, 0, 0))],
        out_specs=pl.BlockSpec((Lc, H * DV), row),
        out_shape=jax.ShapeDtypeStruct((B * S, H * DV), BF16),
        scratch_shapes=[pltpu.VMEM((Lc + 8, 2 * H * DK), F32), pltpu.VMEM((H, VR, DK), F32),
                        pltpu.VMEM((8, LANES), F32)],
        compiler_params=_cparams(("arbitrary", "arbitrary"), 32),
        name="mlstm",
    )(mqk, mvt, mo, mif, conv_w, conv_b, gate_b, head_gain)


def _merge_kernel(ya_ref, yb_ref, yc_ref, g_ref, x_ref, ga_ref, wa_ref, wb_ref, wc_ref, wo_ref, out_ref):
    def branch(y_ref, w_ref, c):
        gate = _sigmoid(g_ref[:, c * D_MODEL:(c + 1) * D_MODEL].astype(F32))
        return gate * jnp.dot(y_ref[...], w_ref[0], preferred_element_type=F32)

    merged = branch(ya_ref, wa_ref, 0) + branch(yb_ref, wb_ref, 1) + branch(yc_ref, wc_ref, 2)
    y = jnp.dot(merged.astype(BF16), wo_ref[0], preferred_element_type=F32)
    out_ref[...] = x_ref[...] + ga_ref[0] * y


def _merge(ya, yb, yc, gates, x2, S, layer, ga, wa, wb, wc, wo, tm=512):
    N, D = x2.shape
    per_b = S // tm
    row = lambda i: (i, 0)
    wsel = lambda i: (layer, 0, 0)
    return pl.pallas_call(
        _merge_kernel,
        grid=(N // tm,),
        in_specs=[pl.BlockSpec((tm, 512), row), pl.BlockSpec((tm, 512), row), pl.BlockSpec((tm, 512), row),
                  pl.BlockSpec((tm, N_BRANCH * D), row), pl.BlockSpec((tm, D), row),
                  pl.BlockSpec((1, 1, D), lambda i: (i // per_b, 0, 0)),
                  _const_spec((1, 512, D), wsel), _const_spec((1, 512, D), wsel), _const_spec((1, 512, D), wsel),
                  _const_spec((1, D, D), wsel)],
        out_specs=pl.BlockSpec((tm, D), row),
        out_shape=jax.ShapeDtypeStruct((N, D), F32),
        compiler_params=_cparams(("parallel",), 48),
        name="merge",
    )(ya, yb, yc, gates, x2, ga, wa, wb, wc, wo)


def _ffn_kernel(x_ref, gain_ref, sc_ref, sh_ref, gf_ref, wr_ref, br_ref, wg_ref, wu_ref, wd_ref, out_ref):
    x = x_ref[...]
    tm = x.shape[0]
    ms = jnp.mean(x * x, axis=-1, keepdims=True)
    hf = x * lax.rsqrt(ms + EPS) * gain_ref[0] * (1.0 + sc_ref[0]) + sh_ref[0]

    hf_hi = hf.astype(BF16)
    hf_lo = (hf - hf_hi.astype(F32)).astype(BF16)
    wr = wr_ref[0]
    wr_hi = wr.astype(BF16)
    wr_split = jnp.concatenate([wr_hi, (wr - wr_hi.astype(F32)).astype(BF16)], axis=1)
    parts = (jnp.dot(hf_hi, wr_split, preferred_element_type=F32)
             + jnp.dot(hf_lo, wr_split, preferred_element_type=F32))
    logits = parts[:, :LANES] + parts[:, LANES:] + br_ref[0]
    lane_i = lax.broadcasted_iota(jnp.int32, (tm, LANES), 1)
    lane = lane_i.astype(F32)
    big = float(LANES)

    def first_argmax(z):
        zmax = jnp.max(z, axis=1, keepdims=True)
        return zmax, jnp.min(jnp.where(z == zmax, lane, big), axis=1, keepdims=True)

    lg = jnp.where(lane_i < N_GROUPS, logits, NEG_INF)
    gmax, g_idx = first_argmax(lg)
    p_sel = 1.0 / jnp.sum(jnp.exp(lg - gmax), axis=1, keepdims=True)
    lane_grp = ((lane_i - N_GROUPS) // EXPERTS_PER_GROUP).astype(F32)
    in_grp = (lane_i >= N_GROUPS) & (lane_i < N_GROUPS + N_EXPERTS) & (lane_grp == g_idx)
    le = jnp.where(in_grp, logits, NEG_INF)
    v1, i1 = first_argmax(le)
    le2 = jnp.where(lane == i1, NEG_INF, le)
    v2, i2 = first_argmax(le2)
    e2 = jnp.exp(v2 - v1)
    den = 1.0 + e2
    comb = jnp.where(lane == i1, (1.0 / den) * p_sel, 0.0) + jnp.where(lane == i2, (e2 / den) * p_sel, 0.0)

    hb = hf_hi
    acc = jnp.zeros((tm, D_MODEL), F32)
    for e in range(N_EXPERTS):
        gate = jnp.dot(hb, wg_ref[0, e], preferred_element_type=F32)
        up = jnp.dot(hb, wu_ref[0, e], preferred_element_type=F32)
        hid = gate * _sigmoid(gate) * up * comb[:, N_GROUPS + e:N_GROUPS + e + 1]
        acc = acc + jnp.dot(hid.astype(BF16), wd_ref[0, e], preferred_element_type=F32)
    out_ref[...] = x + gf_ref[0] * acc


def _ffn(x2, S, layer, gain, sc, sh, gf, wr, br, wg, wu, wd, tm=512):
    N, D = x2.shape
    per_b = S // tm
    row = lambda i: (i, 0)
    bsel = lambda i: (i // per_b, 0, 0)
    wsel = lambda i: (layer, 0, 0, 0)
    return pl.pallas_call(
        _ffn_kernel,
        grid=(N // tm,),
        in_specs=[pl.BlockSpec((tm, D), row),
                  _const_spec((1, 1, D), lambda i: (layer, 0, 0)),
                  pl.BlockSpec((1, 1, D), bsel), pl.BlockSpec((1, 1, D), bsel), pl.BlockSpec((1, 1, D), bsel),
                  _const_spec((1, D, LANES), lambda i: (layer, 0, 0)),
                  _const_spec((1, 1, LANES), lambda i: (layer, 0, 0)),
                  _const_spec((1, N_EXPERTS, D, EXPERT_FF), wsel),
                  _const_spec((1, N_EXPERTS, D, EXPERT_FF), wsel),
                  _const_spec((1, N_EXPERTS, EXPERT_FF, D), wsel)],
        out_specs=pl.BlockSpec((tm, D), row),
        out_shape=jax.ShapeDtypeStruct((N, D), F32),
        compiler_params=_cparams(("parallel",), 56),
        name="ffn",
    )(x2, gain, sc, sh, gf, wr, br, wg, wu, wd)


def _pad_last(a, n):
    return jnp.pad(a, [(0, 0)] * (a.ndim - 1) + [(0, n - a.shape[-1])])


def _prep_w_in(w_in):
    pts = np.cumsum(IN_SIZES)[:-1].tolist()
    cq, ckv, kr, m_qk, m_v, m_o, m_i, m_f, d_q, d_k, d_v, gates = jnp.split(w_in, pts, axis=-1)
    seg_a = _pad_last(jnp.concatenate([cq, ckv, kr], axis=-1), 512)
    seg_if = _pad_last(jnp.concatenate([m_i, m_f], axis=-1), LANES)
    w = jnp.concatenate([seg_a, m_qk, m_v, m_o, seg_if,
                         d_q, d_k,
                         d_v, gates], axis=-1)
    assert w.shape[-1] == COL_END
    return w.astype(BF16)


def kernel(x, c, w_ada, b_ada, attn_norm, w_in, mla_q_norm, mla_kv_norm, mla_w_uq, mla_w_ukv, mla_q_gain, mla_k_gain, mlstm_conv_w, mlstm_conv_b, mlstm_b_i, mlstm_b_f, mlstm_head_gain, dil_q_gain, dil_k_gain, w_branch_a, w_branch_b, w_branch_c, w_out, ffn_norm, w_router_group, b_router_group, w_router_expert, b_router_expert, w_exp_gate, w_exp_up, w_exp_down):
    B, S, D = x.shape
    L = w_ada.shape[0]
    N = B * S
    assert D == D_MODEL and S % DIL_MAX_WINDOW == 0

    mod = _adaln(c, w_ada, b_ada)

    w_pad = _prep_w_in(w_in)
    half = MLA_ROPE // 2
    swap_rope = lambda a: jnp.concatenate(
        [jnp.zeros_like(a[..., :MLA_NOPE]), a[..., MLA_NOPE + half:], a[..., MLA_NOPE:MLA_NOPE + half]], axis=-1)
    uq = mla_w_uq.reshape(L, MLA_Q_LORA, MLA_HEADS, MLA_QK)
    wuq = jnp.concatenate([_pad_last(u, LANES).reshape(L, MLA_Q_LORA, MLA_HEADS * LANES)
                           for u in (uq, swap_rope(uq))], axis=-1).astype(BF16)
    ukv = mla_w_ukv.reshape(L, MLA_KV_LORA, MLA_HEADS, MLA_NOPE + MLA_V)
    wkv = jnp.concatenate([_pad_last(ukv[..., :MLA_NOPE], LANES).reshape(L, MLA_KV_LORA, MLA_HEADS * LANES),
                           ukv[..., MLA_NOPE:].reshape(L, MLA_KV_LORA, MLA_HEADS * MLA_V)], axis=-1).astype(BF16)
    rows = lambda a: a[:, None, :]
    qg = rows(_pad_last(mla_q_gain, LANES))
    qgs = rows(_pad_last(swap_rope(mla_q_gain), LANES))
    kg = rows(_pad_last(mla_k_gain, LANES))
    dqg = rows(_pad_last(dil_q_gain, LANES))
    dkg = rows(_pad_last(dil_k_gain, LANES))
    gate_b = rows(_pad_last(jnp.concatenate([mlstm_b_i, mlstm_b_f], axis=-1), LANES))
    tables = _rope_tables(S)
    wa, wb, wc, wo = (w.astype(BF16) for w in (w_branch_a, w_branch_b, w_branch_c, w_out))
    wr = _pad_last(jnp.concatenate([w_router_group, w_router_expert], axis=-1), LANES)
    br = rows(_pad_last(jnp.concatenate([b_router_group, b_router_expert], axis=-1), LANES))
    wg, wu, wd = (w.astype(BF16) for w in (w_exp_gate, w_exp_up, w_exp_down))
    attn_norm, ffn_norm, mla_q_norm, mla_kv_norm, mlstm_conv_b, mlstm_head_gain = (
        rows(a) for a in (attn_norm, ffn_norm, mla_q_norm, mla_kv_norm, mlstm_conv_b, mlstm_head_gain))

    x2 = x.reshape(N, D)
    for l in range(L):
        sh_a, sc_a, g_a, sh_f, sc_f, g_f = (mod[l, :, i] for i in range(6))
        oa, mqk, mv, mo, mif, qd, kd, vd, gates = _inproj(x2, S, l, attn_norm, sc_a, sh_a, w_pad, dqg, dkg)
        q, k, vt = _mla_prep(oa, S, l, mla_q_norm, mla_kv_norm, wuq, wkv, qg, qgs, kg, tables)
        ya = _causal_attention(q, k, vt)
        yb = _mlstm(mqk, mv, mo, mif, B, S, l, mlstm_conv_w, mlstm_conv_b, gate_b, mlstm_head_gain)
        yc = _band_attention(qd, kd, vd)
        x2 = _merge(ya.reshape(N, -1), yb, yc.reshape(N, -1), gates, x2, S, l, g_a, wa, wb, wc, wo)
        x2 = _ffn(x2, S, l, ffn_norm, sc_f, sh_f, g_f, wr, br, wg, wu, wd)
    return x2.reshape(B, S, D)
```

```python
import functools

import jax
import jax.numpy as jnp
import numpy as np
from jax import lax
from jax.experimental import pallas as pl
from jax.experimental.pallas import tpu as pltpu

F32 = jnp.float32
BF16 = jnp.bfloat16
HIGHEST = lax.Precision.HIGHEST
NEG_INF = float("-inf")

EPS = 1e-6
LANES = 128
MXU_COLS = 256
VMEM_BYTES_V7X = 64 * 1024 * 1024

D_MODEL = 1024
MLA_HEADS = 8
MLA_Q_LORA = 256
MLA_KV_LORA = 128
MLA_NOPE = 64
MLA_ROPE = 32
MLA_V = 64
MLA_QK = MLA_NOPE + MLA_ROPE
ROPE_THETA = 10000.0
MLSTM_HEADS = 4
MLSTM_QK = 64
MLSTM_V = 128
CONV_K = 4
DIL_HEADS = 8
DIL_HEAD_DIM = 64
DIL_PATTERNS = ((128, 1), (512, 4), (2048, 16))
DIL_WIDTH = DIL_HEADS * DIL_HEAD_DIM
N_GROUPS = 4
EXPERTS_PER_GROUP = 4
N_EXPERTS = N_GROUPS * EXPERTS_PER_GROUP
EXPERT_FF = 256
N_BRANCH = 3
IN_SIZES = (MLA_Q_LORA, MLA_KV_LORA, MLA_ROPE,
            2 * MLSTM_HEADS * MLSTM_QK, MLSTM_HEADS * MLSTM_V, MLSTM_HEADS * MLSTM_V, MLSTM_HEADS, MLSTM_HEADS,
            DIL_WIDTH, DIL_WIDTH, DIL_WIDTH, N_BRANCH * D_MODEL)

COL_A = 0
COL_MQK = 512
COL_MV = 1024
COL_MO = 1536
COL_MIF = 2048
COL_DQ = 2176
COL_DK = 2688
COL_DV = 3200
COL_G = 3712
COL_END = 6784

FLASH_TILE = 1024
MLA_HEADS_PER_STEP = 4
DIL_MAX_WINDOW = max(w for w, _ in DIL_PATTERNS)
V_DIM = 64
V_FILL = 16
V_ROWS = V_DIM + V_FILL
LOG2E = 1.4426950408889634
assert MLA_V == V_DIM and DIL_HEAD_DIM == V_DIM


def _store_v_tiles_t(v, vt_ref, dim=V_DIM):
    tm = v.shape[0]
    v_t = v.astype(BF16).T
    fill = (lax.broadcasted_iota(jnp.int32, (V_FILL, tm), 0) == 0).astype(BF16)
    for h in range(v.shape[1] // dim):
        vt_ref[0, h, 0:dim, :] = v_t[h * dim:(h + 1) * dim, :]
        vt_ref[0, h, dim:dim + V_FILL, :] = fill


def _cparams(sem, vmem_mb):
    return pltpu.CompilerParams(dimension_semantics=sem, vmem_limit_bytes=vmem_mb * 1024 * 1024)


def _sigmoid(x):
    return 1.0 / (1.0 + jnp.exp(-x))


def _const_spec(shape, index_map):
    return pl.BlockSpec(shape, index_map, pipeline_mode=pl.Buffered(1))


def _adaln_kernel(c_ref, w_ref, b_ref, o_ref):
    c = c_ref[...]
    ca = c * _sigmoid(c)
    o_ref[0] = jnp.dot(ca, w_ref[0], preferred_element_type=F32, precision=HIGHEST) + b_ref[0]


def _adaln(c, w_ada, b_ada):
    L, D, D6 = w_ada.shape
    B = c.shape[0]
    rows = 8
    cp = jnp.zeros((rows, D), F32).at[:B].set(c)
    out = pl.pallas_call(
        _adaln_kernel,
        grid=(L, D6 // D),
        in_specs=[pl.BlockSpec((rows, D), lambda l, j: (0, 0)),
                  pl.BlockSpec((1, D, D), lambda l, j: (l, 0, j)),
                  pl.BlockSpec((1, 1, D), lambda l, j: (l, 0, j))],
        out_specs=pl.BlockSpec((1, rows, D), lambda l, j: (l, 0, j)),
        out_shape=jax.ShapeDtypeStruct((L, rows, D6), F32),
        compiler_params=_cparams(("parallel", "parallel"), 32),
        name="adaln",
    )(cp, w_ada, b_ada.reshape(L, 1, D6))
    return out[:, :B].reshape(L, B, D6 // D, 1, D)


def _inproj_kernel(x_ref, gain_ref, sc_ref, sh_ref, w_ref, dqg_ref, dkg_ref,
                   oa_ref, mqk_ref, mv_ref, mo_ref, mif_ref, qd_ref, kd_ref, vd_ref, g_ref):
    x = x_ref[...]
    ms = jnp.mean(x * x, axis=-1, keepdims=True)
    y = x * lax.rsqrt(ms + EPS) * gain_ref[0]
    u = (y * (1.0 + sc_ref[0]) + sh_ref[0]).astype(BF16)

    def proj(a, b):
        return jnp.dot(u, w_ref[0, :, a:b], preferred_element_type=F32)

    oa_ref[...] = proj(COL_A, COL_MQK)
    mqk_ref[...] = proj(COL_MQK, COL_MV)
    _store_v_tiles_t(proj(COL_MV, COL_MO), mv_ref, MLSTM_V)
    mo_ref[...] = proj(COL_MO, COL_MIF).astype(BF16)
    mif_ref[...] = proj(COL_MIF, COL_DQ)

    half = LANES // 2
    low = lax.broadcasted_iota(jnp.int32, (1, LANES), 1) < half

    def head_norm(z, g):
        for h in range(DIL_HEADS):
            pair = z[:, (h // 2) * LANES:(h // 2 + 1) * LANES]
            zh = jnp.where(low, pair if h % 2 == 0 else pltpu.roll(pair, half, 1), 0.0)
            r = lax.rsqrt(jnp.sum(zh * zh, axis=-1, keepdims=True) / DIL_HEAD_DIM + EPS)
            yield h, zh * r * g

    dq = proj(COL_DQ, COL_DK)
    for h, zn in head_norm(dq, dqg_ref[0]):
        qd_ref[0, h // 2, :, (h % 2) * LANES:(h % 2 + 1) * LANES] = (zn * (DIL_HEAD_DIM ** -0.5 * LOG2E)).astype(BF16)
    dk = proj(COL_DK, COL_DV)
    for h, zn in head_norm(dk, dkg_ref[0]):
        kd_ref[0, h // 2, :, (h % 2) * LANES:(h % 2 + 1) * LANES] = zn.astype(BF16)
    _store_v_tiles_t(proj(COL_DV, COL_G), vd_ref)
    for c in range(N_BRANCH):
        g_ref[:, c * D_MODEL:(c + 1) * D_MODEL] = proj(COL_G + c * D_MODEL, COL_G + (c + 1) * D_MODEL).astype(BF16)


def _inproj(x2, S, layer, gain, sc, sh, w_pad, dqg, dkg, tm=256):
    N, D = x2.shape
    per_b = S // tm
    row = lambda i: (i, 0)
    bsel = lambda i: (i // per_b, 0, 0)
    B = N // S
    outs = [(512, F32), (512, F32), "mvt", (512, BF16), (LANES, F32),
            "qk", "qk", "vt", (3072, BF16)]
    head_major = lambda i: (i // per_b, 0, i % per_b, 0)
    time_minor = lambda i: (i // per_b, 0, 0, i % per_b)
    special_spec = {"qk": pl.BlockSpec((1, DIL_HEADS // 2, tm, 2 * LANES), head_major),
                    "vt": pl.BlockSpec((1, DIL_HEADS, V_ROWS, tm), time_minor),
                    "mvt": pl.BlockSpec((1, MLSTM_HEADS, MLSTM_V + V_FILL, tm), time_minor)}
    special_shape = {"qk": jax.ShapeDtypeStruct((B, DIL_HEADS // 2, S, 2 * LANES), BF16),
                     "vt": jax.ShapeDtypeStruct((B, DIL_HEADS, V_ROWS, S), BF16),
                     "mvt": jax.ShapeDtypeStruct((B, MLSTM_HEADS, MLSTM_V + V_FILL, S), BF16)}
    return pl.pallas_call(
        _inproj_kernel,
        grid=(N // tm,),
        in_specs=[pl.BlockSpec((tm, D), row),
                  _const_spec((1, 1, D), lambda i: (layer, 0, 0)),
                  pl.BlockSpec((1, 1, D), bsel),
                  pl.BlockSpec((1, 1, D), bsel),
                  _const_spec((1, D, COL_END), lambda i: (layer, 0, 0)),
                  _const_spec((1, 1, LANES), lambda i: (layer, 0, 0)),
                  _const_spec((1, 1, LANES), lambda i: (layer, 0, 0))],
        out_specs=[special_spec[o] if isinstance(o, str) else pl.BlockSpec((tm, o[0]), row) for o in outs],
        out_shape=[special_shape[o] if isinstance(o, str) else jax.ShapeDtypeStruct((N, o[0]), o[1]) for o in outs],
        compiler_params=_cparams(("parallel",), 52),
        name="inproj",
    )(x2, gain, sc, sh, w_pad, dqg, dkg)


def _mla_prep_kernel(a_ref, qn_ref, kvn_ref, wuq_ref, wkv_ref, qg_ref, qgs_ref, kg_ref, cos_ref, sin_ref,
                     q_ref, k_ref, vt_ref):
    HW = MLA_HEADS * LANES
    a = a_ref[...]
    cq = a[:, :MLA_Q_LORA]
    ckv = a[:, MLA_Q_LORA:MLA_Q_LORA + MLA_KV_LORA]
    kr_tile = a[:, MLA_Q_LORA + MLA_KV_LORA:]

    def rms(z, g):
        return z * lax.rsqrt(jnp.mean(z * z, axis=-1, keepdims=True) + EPS) * g

    cqn = rms(cq, qn_ref[0]).astype(BF16)
    ckvn = rms(ckv, kvn_ref[0]).astype(BF16)
    qr = jnp.dot(cqn, wuq_ref[0], preferred_element_type=F32)
    kvr = jnp.dot(ckvn, wkv_ref[0], preferred_element_type=F32)
    _store_v_tiles_t(kvr[:, HW:], vt_ref)

    cosf, sins = cos_ref[...], sin_ref[...]
    q_scale = MLA_QK ** -0.5 * LOG2E
    gc = cosf * (qg_ref[0] * q_scale)
    gs = sins * (qgs_ref[0] * q_scale)
    kg = kg_ref[0]
    half = MLA_ROPE // 2
    krg = pltpu.roll(kr_tile, MLA_NOPE, 1) * kg
    kr_rot = krg * cosf + (pltpu.roll(krg, LANES - half, 1) + pltpu.roll(krg, half, 1)) * sins
    kr_ss = jnp.sum(kr_tile * kr_tile, axis=-1, keepdims=True)
    for h in range(MLA_HEADS):
        sl = slice(h * LANES, (h + 1) * LANES)
        grp, pair_sl = h // MLA_HEADS_PER_STEP, slice((h % MLA_HEADS_PER_STEP) * LANES,
                                                     (h % MLA_HEADS_PER_STEP + 1) * LANES)
        z = qr[:, sl]
        rq = lax.rsqrt(jnp.sum(z * z, axis=-1, keepdims=True) / MLA_QK + EPS)
        q_ref[0, grp, :, pair_sl] = ((z * gc + qr[:, HW + h * LANES:HW + (h + 1) * LANES] * gs) * rq).astype(BF16)
        zk = kvr[:, sl]
        rk = lax.rsqrt((jnp.sum(zk * zk, axis=-1, keepdims=True) + kr_ss) / MLA_QK + EPS)
        k_ref[0, grp, :, pair_sl] = ((zk * kg + kr_rot) * rk).astype(BF16)


def _rope_tables(S):
    half = MLA_ROPE // 2
    inv = ROPE_THETA ** (-jnp.arange(half, dtype=F32) / half)
    ang = jnp.arange(S).astype(F32)[:, None] * inv[None, :]
    cos, sin = jnp.cos(ang), jnp.sin(ang)
    one = jnp.ones((S, MLA_NOPE), F32)
    z = lambda n: jnp.zeros((S, n), F32)
    tail = LANES - MLA_QK
    cosf = jnp.concatenate([one, cos, cos, z(tail)], axis=1)
    sins = jnp.concatenate([z(MLA_NOPE), -sin, sin, z(tail)], axis=1)
    return cosf, sins


def _mla_prep(oa, S, layer, qn, kvn, wuq, wkv, qg, qgs, kg, tables, tm=512):
    N = oa.shape[0]
    B = N // S
    per_b = S // tm
    row = lambda i: (i, 0)
    pos = lambda i: (i % per_b, 0)
    HW = MLA_HEADS * LANES
    G, GW = MLA_HEADS // MLA_HEADS_PER_STEP, MLA_HEADS_PER_STEP * LANES
    return pl.pallas_call(
        _mla_prep_kernel,
        grid=(N // tm,),
        in_specs=[pl.BlockSpec((tm, 512), row),
                  _const_spec((1, 1, MLA_Q_LORA), lambda i: (layer, 0, 0)),
                  _const_spec((1, 1, MLA_KV_LORA), lambda i: (layer, 0, 0)),
                  _const_spec((1, MLA_Q_LORA, 2 * HW), lambda i: (layer, 0, 0)),
                  _const_spec((1, MLA_KV_LORA, HW + MLA_HEADS * MLA_V), lambda i: (layer, 0, 0)),
                  _const_spec((1, 1, LANES), lambda i: (layer, 0, 0)),
                  _const_spec((1, 1, LANES), lambda i: (layer, 0, 0)),
                  _const_spec((1, 1, LANES), lambda i: (layer, 0, 0)),
                  pl.BlockSpec((tm, LANES), pos),
                  pl.BlockSpec((tm, LANES), pos)],
        out_specs=[pl.BlockSpec((1, G, tm, GW), lambda i: (i // per_b, 0, i % per_b, 0)),
                   pl.BlockSpec((1, G, tm, GW), lambda i: (i // per_b, 0, i % per_b, 0)),
                   pl.BlockSpec((1, MLA_HEADS, V_ROWS, tm), lambda i: (i // per_b, 0, 0, i % per_b))],
        out_shape=[jax.ShapeDtypeStruct((B, G, S, GW), BF16),
                   jax.ShapeDtypeStruct((B, G, S, GW), BF16),
                   jax.ShapeDtypeStruct((B, MLA_HEADS, V_ROWS, S), BF16)],
        compiler_params=_cparams(("parallel",), 32),
        name="mla_prep",
    )(oa, qn, kvn, wuq, wkv, qg, qgs, kg, *tables)


def _flash_kernel(qi_ref, kj_ref, qn_ref, kn_ref, q0_ref, k0_ref, vt_ref, o_ref, m_ref, acc_ref, s_ref, *, T, n_pairs):
    p_id = pl.program_id(2)
    qi = qi_ref[p_id]
    kj = kj_ref[p_id]
    first = kj == 0
    last = kj == qi

    n_heads = m_ref.shape[0]
    units = [(hh, slice(j * MXU_COLS, (j + 1) * MXU_COLS)) for hh in range(n_heads) for j in range(T // MXU_COLS)]

    def scores(q_ref, k_ref, hh, qs, rows=T):
        q = q_ref[0, 0, qs, hh * LANES:(hh + 1) * LANES]
        k = k_ref[0, 0, 0:rows, hh * LANES:(hh + 1) * LANES]
        return lax.dot_general(k, q, (((1,), (1,)), ((), ())), preferred_element_type=F32)

    @pl.when(p_id == 0)
    def _():
        for hh, qs in units:
            s_ref[0, hh, :, qs] = scores(q0_ref, k0_ref, hh, qs)

    @pl.when(first)
    def _():
        m_ref[...] = jnp.full(m_ref.shape, NEG_INF, F32)
        acc_ref[...] = jnp.zeros(acc_ref.shape, F32)

    nxt = jnp.minimum(p_id + 1, n_pairs - 1)

    def step(cur_diag, next_diag, slot):
        for hh, qs in units:
            rows_next = qs.stop if next_diag else T
            s_ref[1 - slot, hh, 0:rows_next, qs] = scores(qn_ref, kn_ref, hh, qs, rows_next)
            rows = qs.stop if cur_diag else T
            st = s_ref[slot, hh, 0:rows, qs]
            if cur_diag:
                key = lax.broadcasted_iota(jnp.int32, (rows, MXU_COLS), 0)
                qry = lax.broadcasted_iota(jnp.int32, (rows, MXU_COLS), 1) + qs.start
                st = jnp.where(key <= qry, st, NEG_INF)
            m_prev = m_ref[hh, :, qs]
            m_new = jnp.maximum(m_prev, jnp.max(st, axis=0, keepdims=True))
            alpha = jnp.exp2(m_prev - m_new)
            pt = jnp.exp2(st - m_new).astype(BF16)
            acc_ref[hh, :, qs] = (acc_ref[hh, :, qs] * alpha
                                  + jnp.dot(vt_ref[0, hh, :, 0:rows], pt, preferred_element_type=F32))
            m_ref[hh, :, qs] = m_new

    parity = p_id % 2
    cur_diag = last
    next_diag = jnp.logical_and(jnp.logical_not(cur_diag), qi_ref[nxt] == kj_ref[nxt])
    plain = jnp.logical_not(jnp.logical_or(cur_diag, next_diag))
    for slot in range(2):
        here = parity == slot
        pl.when(jnp.logical_and(here, cur_diag))(functools.partial(step, True, False, slot))
        pl.when(jnp.logical_and(here, next_diag))(functools.partial(step, False, True, slot))
        pl.when(jnp.logical_and(here, plain))(functools.partial(step, False, False, slot))

    @pl.when(last)
    def _():
        outs = []
        for hh in range(n_heads):
            a = acc_ref[hh]
            outs.append(a[0:V_DIM, :] / a[V_DIM:V_DIM + 1, :])
        o_ref[0] = jnp.concatenate(outs, axis=0).T.astype(o_ref.dtype)


def _band_bias(T, W):
    d = (jnp.arange(W + 1)[:, None, None] * T + jnp.arange(T)[None, None, :] - jnp.arange(T)[None, :, None])
    cw = jnp.zeros(d.shape, F32)
    for window, dil in DIL_PATTERNS:
        cw = cw + ((d >= 0) & (d <= window) & (d % dil == 0)).astype(F32)
    return jnp.log2(cw)


def _causal_attention(q, k, vt, T=FLASH_TILE):
    B, G, S, GW = q.shape
    g = GW // LANES
    nq = S // T
    pairs = [(i, j) for i in range(nq) for j in range(i + 1)]
    n = len(pairs)
    qi = jnp.asarray(np.array([p[0] for p in pairs], np.int32))
    kj = jnp.asarray(np.array([p[1] for p in pairs], np.int32))
    nxt = lambda p: jnp.minimum(p + 1, n - 1)
    in_specs = [pl.BlockSpec((1, 1, T, GW), lambda b, h, p, qi, kj: (b, h, qi[nxt(p)], 0)),
                pl.BlockSpec((1, 1, T, GW), lambda b, h, p, qi, kj: (b, h, kj[nxt(p)], 0)),
                pl.BlockSpec((1, 1, T, GW), lambda b, h, p, qi, kj: (b, h, qi[0], 0)),
                pl.BlockSpec((1, 1, T, GW), lambda b, h, p, qi, kj: (b, h, kj[0], 0)),
                pl.BlockSpec((1, g, V_ROWS, T), lambda b, h, p, qi, kj: (b, h, 0, kj[p]))]
    return pl.pallas_call(
        functools.partial(_flash_kernel, T=T, n_pairs=n),
        grid_spec=pltpu.PrefetchScalarGridSpec(
            num_scalar_prefetch=2,
            grid=(B, G, n),
            in_specs=in_specs,
            out_specs=pl.BlockSpec((1, T, g * V_DIM), lambda b, h, p, qi, kj: (b, qi[p], h)),
            scratch_shapes=[pltpu.VMEM((g, 1, T), F32), pltpu.VMEM((g, V_ROWS, T), F32),
                            pltpu.VMEM((2, g, T, T), F32)]),
        out_shape=jax.ShapeDtypeStruct((B, S, G * g * V_DIM), BF16),
        compiler_params=_cparams(("arbitrary", "arbitrary", "arbitrary"), 56),
        name="flash_causal",
    )(qi, kj, q, k, q, k, vt)


def _band_kernel(qc_ref, qn_ref, kn_ref, *rest, T, W):
    k_refs, vt_refs = rest[:W + 1], rest[W + 1:2 * W + 2]
    bias_ref, o_ref, m_ref, acc_ref, s_ref = rest[2 * W + 2:]
    i = pl.program_id(2)
    units = [(hh, j) for hh in range(2) for j in range(T // MXU_COLS)]

    def rows_of(d, j):
        if d == 0:
            return 0, (j + 1) * MXU_COLS
        if d == W:
            return j * MXU_COLS, T
        return 0, T

    def produce(q_ref, k_ref, d, hh, j, bias_idx):
        lo, hi = rows_of(d, j)
        qs = slice(j * MXU_COLS, (j + 1) * MXU_COLS)
        q = q_ref[0, 0, qs, hh * LANES:(hh + 1) * LANES]
        k = k_ref[0, 0, lo:hi, hh * LANES:(hh + 1) * LANES]
        s = lax.dot_general(k, q, (((1,), (1,)), ((), ())), preferred_element_type=F32)
        s_ref[d, hh, lo:hi, qs] = s + bias_ref[bias_idx, lo:hi, qs]

    def consume(d, hh, j):
        lo, hi = rows_of(d, j)
        qs = slice(j * MXU_COLS, (j + 1) * MXU_COLS)
        st = s_ref[d, hh, lo:hi, qs]
        m_prev = m_ref[hh, :, qs]
        m_new = jnp.maximum(m_prev, jnp.max(st, axis=0, keepdims=True))
        alpha = jnp.exp2(m_prev - m_new)
        pt = jnp.exp2(st - m_new).astype(BF16)
        acc_ref[hh, :, qs] = (acc_ref[hh, :, qs] * alpha
                              + jnp.dot(vt_refs[d][0, hh, :, lo:hi], pt, preferred_element_type=F32))
        m_ref[hh, :, qs] = m_new

    @pl.when(i == 0)
    def _():
        for hh, j in units:
            produce(qc_ref, k_refs[0], 0, hh, j, 0)

    m_ref[...] = jnp.full(m_ref.shape, NEG_INF, F32)
    acc_ref[...] = jnp.zeros(acc_ref.shape, F32)
    for d in range(W + 1):
        for hh, j in units:
            if d < W:
                produce(qc_ref, k_refs[d + 1], d + 1, hh, j, jnp.where(i >= d + 1, d + 1, W + 1))
            else:
                produce(qn_ref, kn_ref, 0, hh, j, 0)
            consume(d, hh, j)

    outs = []
    for hh in range(2):
        a = acc_ref[hh]
        outs.append(a[0:V_DIM, :] / a[V_DIM:V_DIM + 1, :])
    o_ref[0] = jnp.concatenate(outs, axis=0).T.astype(o_ref.dtype)


def _band_attention(q, k, vt, T=FLASH_TILE):
    B, HP, S, _ = q.shape
    n_q = S // T
    W = DIL_MAX_WINDOW // T
    bias = jnp.concatenate([_band_bias(T, W), jnp.full((1, T, T), NEG_INF, F32)], axis=0)
    qk_block = lambda f: pl.BlockSpec((1, 1, T, 2 * LANES), lambda b, h, i: (b, h, f(i), 0))
    vt_block = lambda f: pl.BlockSpec((1, 2, V_ROWS, T), lambda b, h, i: (b, h, 0, f(i)))
    back = lambda d: (lambda i: jnp.maximum(i - d, 0))
    nxt = lambda i: jnp.minimum(i + 1, n_q - 1)
    in_specs = ([qk_block(lambda i: i), qk_block(nxt), qk_block(nxt)]
                + [qk_block(back(d)) for d in range(W + 1)]
                + [vt_block(back(d)) for d in range(W + 1)]
                + [_const_spec((W + 2, T, T), lambda b, h, i: (0, 0, 0))])
    return pl.pallas_call(
        functools.partial(_band_kernel, T=T, W=W),
        grid=(B, HP, n_q),
        in_specs=in_specs,
        out_specs=pl.BlockSpec((1, T, LANES), lambda b, h, i: (b, i, h)),
        out_shape=jax.ShapeDtypeStruct((B, S, 2 * HP * V_DIM), BF16),
        scratch_shapes=[pltpu.VMEM((2, 1, T), F32), pltpu.VMEM((2, V_ROWS, T), F32),
                        pltpu.VMEM((W + 1, 2, T, T), F32)],
        compiler_params=_cparams(("arbitrary", "arbitrary", "arbitrary"), 58),
        name="flash_band",
    )(q, q, k, *([k] * (W + 1)), *([vt] * (W + 1)), bias)


def _mlstm_kernel(qk_ref, vt_ref, o_ref, if_ref, cw_ref, cb_ref, gb_ref, hg_ref, y_ref,
                  xs_ref, cn_ref, m_ref, *, Lc):
    H, DK, DV = MLSTM_HEADS, MLSTM_QK, MLSTM_V
    HALO = 8

    @pl.when(pl.program_id(1) == 0)
    def _():
        xs_ref[0:HALO, :] = jnp.zeros((HALO, 2 * H * DK), F32)
        cn_ref[...] = jnp.zeros(cn_ref.shape, F32)
        m_ref[...] = jnp.zeros(m_ref.shape, F32)

    x = qk_ref[...]
    xs_ref[HALO:HALO + Lc, :] = x
    y = cb_ref[0]
    for j in range(CONV_K):
        y = y + cw_ref[0, j:j + 1, :] * xs_ref[pl.ds(HALO - (CONV_K - 1) + j, Lc), :]
    xs_ref[0:HALO, :] = x[Lc - HALO:, :]
    qk = y * _sigmoid(y)
    q_t = qk[:, :H * DK].T.astype(BF16)
    k_all = (qk[:, H * DK:] * (DK ** -0.5)).astype(BF16)

    g = if_ref[...] + gb_ref[0]
    logf = jnp.minimum(g, 0.0) - jnp.log(1.0 + jnp.exp(-jnp.abs(g)))
    src = lax.broadcasted_iota(jnp.int32, (Lc, Lc), 0)
    dst = lax.broadcasted_iota(jnp.int32, (Lc, Lc), 1)
    tri = jnp.where(dst <= src, 1.0, 0.0).astype(F32)
    bcum = jnp.dot(tri, logf, preferred_element_type=F32, precision=HIGHEST)
    causal = src <= dst
    g_t = g.T
    bcum_t = bcum.T

    for h in range(H):
        b_row = bcum_t[H + h:H + h + 1, :]
        w_col = g[:, h:h + 1] - bcum[:, H + h:H + h + 1]
        w_row = g_t[h:h + 1, :] - b_row
        m_prev = m_ref[h:h + 1, 0:1]
        dlog = jnp.where(causal, w_col + b_row, NEG_INF)
        inter = b_row + m_prev
        m_t = jnp.maximum(inter, jnp.max(dlog, axis=0, keepdims=True))
        qh_t = q_t[h * DK:(h + 1) * DK, :]
        kh = k_all[:, h * DK:(h + 1) * DK]
        s_t = jnp.dot(kh, qh_t, preferred_element_type=F32)
        sc_t = (s_t * jnp.exp(dlog - m_t)).astype(BF16)
        decay = jnp.exp(inter - m_t)
        cn = cn_ref[h]
        vt = vt_ref[0, h]
        num = (decay * jnp.dot(cn.astype(BF16), qh_t, preferred_element_type=F32)
               + jnp.dot(vt, sc_t, preferred_element_type=F32))
        den = num[DV:DV + 1, :]
        hh = num[0:DV, :] / jnp.maximum(jnp.abs(den), jnp.exp(-m_t))
        hn = hh * lax.rsqrt(jnp.mean(hh * hh, axis=0, keepdims=True) + EPS)
        o_gate = _sigmoid(o_ref[:, h * DV:(h + 1) * DV].astype(F32))
        y_ref[:, h * DV:(h + 1) * DV] = (hn.T * hg_ref[0, :, h * DV:(h + 1) * DV] * o_gate).astype(y_ref.dtype)

        b_end = b_row[:, Lc - 1:Lc]
        wlog = b_end + w_row
        m_new = jnp.maximum(b_end + m_prev, jnp.max(wlog, axis=1, keepdims=True))
        cd = jnp.exp(b_end + m_prev - m_new)
        vw = (vt.astype(F32) * jnp.exp(wlog - m_new)).astype(BF16)
        cn_ref[h] = cd * cn + jnp.dot(vw, kh, preferred_element_type=F32)
        m_ref[h:h + 1, :] = jnp.broadcast_to(m_new, (1, LANES))


def _mlstm(mqk, mvt, mo, mif, B, S, layer, conv_w, conv_b, gate_b, head_gain, Lc=256):
    H, DK, DV = MLSTM_HEADS, MLSTM_QK, MLSTM_V
    nc = S // Lc
    row = lambda b, c: (b * nc + c, 0)
    VR = mvt.shape[2]
    return pl.pallas_call(
        functools.partial(_mlstm_kernel, Lc=Lc),
        grid=(B, nc),
        in_specs=[pl.BlockSpec((Lc, 2 * H * DK), row),
                  pl.BlockSpec((1, H, VR, Lc), lambda b, c: (b, 0, 0, c)),
                  pl.BlockSpec((Lc, H * DV), row),
                  pl.BlockSpec((Lc, LANES), row),
                  _const_spec((1, CONV_K, 2 * H * DK), lambda b, c: (layer, 0, 0)),
                  _const_spec((1, 1, 2 * H * DK), lambda b, c: (layer, 0, 0)),
                  _const_spec((1, 1, LANES), lambda b, c: (layer, 0, 0)),
                  _const_spec((1, 1, H * DV), lambda b, c: (layer, 0, 0))],
        out_specs=pl.BlockSpec((Lc, H * DV), row),
        out_shape=jax.ShapeDtypeStruct((B * S, H * DV), BF16),
        scratch_shapes=[pltpu.VMEM((Lc + 8, 2 * H * DK), F32), pltpu.VMEM((H, VR, DK), F32),
                        pltpu.VMEM((8, LANES), F32)],
        compiler_params=_cparams(("arbitrary", "arbitrary"), 32),
        name="mlstm",
    )(mqk, mvt, mo, mif, conv_w, conv_b, gate_b, head_gain)


def _merge_kernel(ya_ref, yb_ref, yc_ref, g_ref, x_ref, ga_ref, wa_ref, wb_ref, wc_ref, wo_ref, out_ref):
    def branch(y_ref, w_ref, c):
        gate = _sigmoid(g_ref[:, c * D_MODEL:(c + 1) * D_MODEL].astype(F32))
        return gate * jnp.dot(y_ref[...], w_ref[0], preferred_element_type=F32)

    merged = branch(ya_ref, wa_ref, 0) + branch(yb_ref, wb_ref, 1) + branch(yc_ref, wc_ref, 2)
    y = jnp.dot(merged.astype(BF16), wo_ref[0], preferred_element_type=F32)
    out_ref[...] = x_ref[...] + ga_ref[0] * y


def _merge(ya, yb, yc, gates, x2, S, layer, ga, wa, wb, wc, wo, tm=512):
    N, D = x2.shape
    per_b = S // tm
    row = lambda i: (i, 0)
    wsel = lambda i: (layer, 0, 0)
    return pl.pallas_call(
        _merge_kernel,
        grid=(N // tm,),
        in_specs=[pl.BlockSpec((tm, 512), row), pl.BlockSpec((tm, 512), row), pl.BlockSpec((tm, 512), row),
                  pl.BlockSpec((tm, N_BRANCH * D), row), pl.BlockSpec((tm, D), row),
                  pl.BlockSpec((1, 1, D), lambda i: (i // per_b, 0, 0)),
                  _const_spec((1, 512, D), wsel), _const_spec((1, 512, D), wsel), _const_spec((1, 512, D), wsel),
                  _const_spec((1, D, D), wsel)],
        out_specs=pl.BlockSpec((tm, D), row),
        out_shape=jax.ShapeDtypeStruct((N, D), F32),
        compiler_params=_cparams(("parallel",), 48),
        name="merge",
    )(ya, yb, yc, gates, x2, ga, wa, wb, wc, wo)


def _ffn_kernel(x_ref, gain_ref, sc_ref, sh_ref, gf_ref, wr_ref, br_ref, wg_ref, wu_ref, wd_ref, out_ref):
    x = x_ref[...]
    tm = x.shape[0]
    ms = jnp.mean(x * x, axis=-1, keepdims=True)
    hf = x * lax.rsqrt(ms + EPS) * gain_ref[0] * (1.0 + sc_ref[0]) + sh_ref[0]

    hf_hi = hf.astype(BF16)
    hf_lo = (hf - hf_hi.astype(F32)).astype(BF16)
    wr = wr_ref[0]
    wr_hi = wr.astype(BF16)
    wr_split = jnp.concatenate([wr_hi, (wr - wr_hi.astype(F32)).astype(BF16)], axis=1)
    parts = (jnp.dot(hf_hi, wr_split, preferred_element_type=F32)
             + jnp.dot(hf_lo, wr_split, preferred_element_type=F32))
    logits = parts[:, :LANES] + parts[:, LANES:] + br_ref[0]
    lane_i = lax.broadcasted_iota(jnp.int32, (tm, LANES), 1)
    lane = lane_i.astype(F32)
    big = float(LANES)

    def first_argmax(z):
        zmax = jnp.max(z, axis=1, keepdims=True)
        return zmax, jnp.min(jnp.where(z == zmax, lane, big), axis=1, keepdims=True)

    lg = jnp.where(lane_i < N_GROUPS, logits, NEG_INF)
    gmax, g_idx = first_argmax(lg)
    p_sel = 1.0 / jnp.sum(jnp.exp(lg - gmax), axis=1, keepdims=True)
    lane_grp = ((lane_i - N_GROUPS) // EXPERTS_PER_GROUP).astype(F32)
    in_grp = (lane_i >= N_GROUPS) & (lane_i < N_GROUPS + N_EXPERTS) & (lane_grp == g_idx)
    le = jnp.where(in_grp, logits, NEG_INF)
    v1, i1 = first_argmax(le)
    le2 = jnp.where(lane == i1, NEG_INF, le)
    v2, i2 = first_argmax(le2)
    e2 = jnp.exp(v2 - v1)
    den = 1.0 + e2
    comb = jnp.where(lane == i1, (1.0 / den) * p_sel, 0.0) + jnp.where(lane == i2, (e2 / den) * p_sel, 0.0)

    hb = hf_hi
    acc = jnp.zeros((tm, D_MODEL), F32)
    for e in range(N_EXPERTS):
        gate = jnp.dot(hb, wg_ref[0, e], preferred_element_type=F32)
        up = jnp.dot(hb, wu_ref[0, e], preferred_element_type=F32)
        hid = gate * _sigmoid(gate) * up * comb[:, N_GROUPS + e:N_GROUPS + e + 1]
        acc = acc + jnp.dot(hid.astype(BF16), wd_ref[0, e], preferred_element_type=F32)
    out_ref[...] = x + gf_ref[0] * acc


def _ffn(x2, S, layer, gain, sc, sh, gf, wr, br, wg, wu, wd, tm=512):
    N, D = x2.shape
    per_b = S // tm
    row = lambda i: (i, 0)
    bsel = lambda i: (i // per_b, 0, 0)
    wsel = lambda i: (layer, 0, 0, 0)
    return pl.pallas_call(
        _ffn_kernel,
        grid=(N // tm,),
        in_specs=[pl.BlockSpec((tm, D), row),
                  _const_spec((1, 1, D), lambda i: (layer, 0, 0)),
                  pl.BlockSpec((1, 1, D), bsel), pl.BlockSpec((1, 1, D), bsel), pl.BlockSpec((1, 1, D), bsel),
                  _const_spec((1, D, LANES), lambda i: (layer, 0, 0)),
                  _const_spec((1, 1, LANES), lambda i: (layer, 0, 0)),
                  _const_spec((1, N_EXPERTS, D, EXPERT_FF), wsel),
                  _const_spec((1, N_EXPERTS, D, EXPERT_FF), wsel),
                  _const_spec((1, N_EXPERTS, EXPERT_FF, D), wsel)],
        out_specs=pl.BlockSpec((tm, D), row),
        out_shape=jax.ShapeDtypeStruct((N, D), F32),
        compiler_params=_cparams(("parallel",), 56),
        name="ffn",
    )(x2, gain, sc, sh, gf, wr, br, wg, wu, wd)


def _pad_last(a, n):
    return jnp.pad(a, [(0, 0)] * (a.ndim - 1) + [(0, n - a.shape[-1])])


def _prep_w_in(w_in):
    pts = np.cumsum(IN_SIZES)[:-1].tolist()
    cq, ckv, kr, m_qk, m_v, m_o, m_i, m_f, d_q, d_k, d_v, gates = jnp.split(w_in, pts, axis=-1)
    seg_a = _pad_last(jnp.concatenate([cq, ckv, kr], axis=-1), 512)
    seg_if = _pad_last(jnp.concatenate([m_i, m_f], axis=-1), LANES)
    w = jnp.concatenate([seg_a, m_qk, m_v, m_o, seg_if,
                         d_q, d_k,
                         d_v, gates], axis=-1)
    assert w.shape[-1] == COL_END
    return w.astype(BF16)


def kernel(x, c, w_ada, b_ada, attn_norm, w_in, mla_q_norm, mla_kv_norm, mla_w_uq, mla_w_ukv, mla_q_gain, mla_k_gain, mlstm_conv_w, mlstm_conv_b, mlstm_b_i, mlstm_b_f, mlstm_head_gain, dil_q_gain, dil_k_gain, w_branch_a, w_branch_b, w_branch_c, w_out, ffn_norm, w_router_group, b_router_group, w_router_expert, b_router_expert, w_exp_gate, w_exp_up, w_exp_down):
    B, S, D = x.shape
    L = w_ada.shape[0]
    N = B * S
    assert D == D_MODEL and S % DIL_MAX_WINDOW == 0

    mod = _adaln(c, w_ada, b_ada)

    w_pad = _prep_w_in(w_in)
    half = MLA_ROPE // 2
    swap_rope = lambda a: jnp.concatenate(
        [jnp.zeros_like(a[..., :MLA_NOPE]), a[..., MLA_NOPE + half:], a[..., MLA_NOPE:MLA_NOPE + half]], axis=-1)
    uq = mla_w_uq.reshape(L, MLA_Q_LORA, MLA_HEADS, MLA_QK)
    wuq = jnp.concatenate([_pad_last(u, LANES).reshape(L, MLA_Q_LORA, MLA_HEADS * LANES)
                           for u in (uq, swap_rope(uq))], axis=-1).astype(BF16)
    ukv = mla_w_ukv.reshape(L, MLA_KV_LORA, MLA_HEADS, MLA_NOPE + MLA_V)
    wkv = jnp.concatenate([_pad_last(ukv[..., :MLA_NOPE], LANES).reshape(L, MLA_KV_LORA, MLA_HEADS * LANES),
                           ukv[..., MLA_NOPE:].reshape(L, MLA_KV_LORA, MLA_HEADS * MLA_V)], axis=-1).astype(BF16)
    rows = lambda a: a[:, None, :]
    qg = rows(_pad_last(mla_q_gain, LANES))
    qgs = rows(_pad_last(swap_rope(mla_q_gain), LANES))
    kg = rows(_pad_last(mla_k_gain, LANES))
    dqg = rows(_pad_last(dil_q_gain, LANES))
    dkg = rows(_pad_last(dil_k_gain, LANES))
    gate_b = rows(_pad_last(jnp.concatenate([mlstm_b_i, mlstm_b_f], axis=-1), LANES))
    tables = _rope_tables(S)
    wa, wb, wc, wo = (w.astype(BF16) for w in (w_branch_a, w_branch_b, w_branch_c, w_out))
    wr = _pad_last(jnp.concatenate([w_router_group, w_router_expert], axis=-1), LANES)
    br = rows(_pad_last(jnp.concatenate([b_router_group, b_router_expert], axis=-1), LANES))
    wg, wu, wd = (w.astype(BF16) for w in (w_exp_gate, w_exp_up, w_exp_down))
    attn_norm, ffn_norm, mla_q_norm, mla_kv_norm, mlstm_conv_b, mlstm_head_gain = (
        rows(a) for a in (attn_norm, ffn_norm, mla_q_norm, mla_kv_norm, mlstm_conv_b, mlstm_head_gain))

    x2 = x.reshape(N, D)
    for l in range(L):
        sh_a, sc_a, g_a, sh_f, sc_f, g_f = (mod[l, :, i] for i in range(6))
        oa, mqk, mv, mo, mif, qd, kd, vd, gates = _inproj(x2, S, l, attn_norm, sc_a, sh_a, w_pad, dqg, dkg)
        q, k, vt = _mla_prep(oa, S, l, mla_q_norm, mla_kv_norm, wuq, wkv, qg, qgs, kg, tables)
        ya = _causal_attention(q, k, vt)
        yb = _mlstm(mqk, mv, mo, mif, B, S, l, mlstm_conv_w, mlstm_conv_b, gate_b, mlstm_head_gain)
        yc = _band_attention(qd, kd, vd)
        x2 = _merge(ya.reshape(N, -1), yb, yc.reshape(N, -1), gates, x2, S, l, g_a, wa, wb, wc, wo)
        x2 = _ffn(x2, S, l, ffn_norm, sc_f, sh_f, g_f, wr, br, wg, wu, wd)
    return x2.reshape(B, S, D)
```

```python
import functools

import jax
import jax.numpy as jnp
import numpy as np
from jax import lax
from jax.experimental import pallas as pl
from jax.experimental.pallas import tpu as pltpu

F32 = jnp.float32
BF16 = jnp.bfloat16
HIGHEST = lax.Precision.HIGHEST
NEG_INF = float("-inf")

EPS = 1e-6
LANES = 128
MXU_COLS = 256
VMEM_BYTES_V7X = 64 * 1024 * 1024

D_MODEL = 1024
MLA_HEADS = 8
MLA_Q_LORA = 256
MLA_KV_LORA = 128
MLA_NOPE = 64
MLA_ROPE = 32
MLA_V = 64
MLA_QK = MLA_NOPE + MLA_ROPE
ROPE_THETA = 10000.0
MLSTM_HEADS = 4
MLSTM_QK = 64
MLSTM_V = 128
CONV_K = 4
DIL_HEADS = 8
DIL_HEAD_DIM = 64
DIL_PATTERNS = ((128, 1), (512, 4), (2048, 16))
DIL_WIDTH = DIL_HEADS * DIL_HEAD_DIM
N_GROUPS = 4
EXPERTS_PER_GROUP = 4
N_EXPERTS = N_GROUPS * EXPERTS_PER_GROUP
EXPERT_FF = 256
N_BRANCH = 3
IN_SIZES = (MLA_Q_LORA, MLA_KV_LORA, MLA_ROPE,
            2 * MLSTM_HEADS * MLSTM_QK, MLSTM_HEADS * MLSTM_V, MLSTM_HEADS * MLSTM_V, MLSTM_HEADS, MLSTM_HEADS,
            DIL_WIDTH, DIL_WIDTH, DIL_WIDTH, N_BRANCH * D_MODEL)

COL_A = 0
COL_MQK = 512
COL_MV = 1024
COL_MO = 1536
COL_MIF = 2048
COL_DQ = 2176
COL_DK = 2688
COL_DV = 3200
COL_G = 3712
COL_END = 6784

FLASH_TILE = 1024
MLA_HEADS_PER_STEP = 4
DIL_MAX_WINDOW = max(w for w, _ in DIL_PATTERNS)
V_DIM = 64
V_FILL = 16
V_ROWS = V_DIM + V_FILL
LOG2E = 1.4426950408889634
assert MLA_V == V_DIM and DIL_HEAD_DIM == V_DIM


def _store_v_tiles_t(v, vt_ref, dim=V_DIM):
    tm = v.shape[0]
    v_t = v.astype(BF16).T
    fill = (lax.broadcasted_iota(jnp.int32, (V_FILL, tm), 0) == 0).astype(BF16)
    for h in range(v.shape[1] // dim):
        vt_ref[0, h, 0:dim, :] = v_t[h * dim:(h + 1) * dim, :]
        vt_ref[0, h, dim:dim + V_FILL, :] = fill


def _cparams(sem, vmem_mb):
    limit = vmem_mb * 1024 * 1024
    assert limit < VMEM_BYTES_V7X
    return pltpu.CompilerParams(dimension_semantics=sem, vmem_limit_bytes=limit)


def _sigmoid(x):
    return 1.0 / (1.0 + jnp.exp(-x))


def _const_spec(shape, index_map):
    return pl.BlockSpec(shape, index_map, pipeline_mode=pl.Buffered(1))


def _adaln_kernel(c_ref, w_ref, b_ref, o_ref):
    c = c_ref[...]
    ca = c * _sigmoid(c)
    o_ref[0] = jnp.dot(ca, w_ref[0], preferred_element_type=F32, precision=HIGHEST) + b_ref[0]


def _adaln(c, w_ada, b_ada):
    L, D, D6 = w_ada.shape
    B = c.shape[0]
    rows = 8
    cp = jnp.zeros((rows, D), F32).at[:B].set(c)
    out = pl.pallas_call(
        _adaln_kernel,
        grid=(L, D6 // D),
        in_specs=[pl.BlockSpec((rows, D), lambda l, j: (0, 0)),
                  pl.BlockSpec((1, D, D), lambda l, j: (l, 0, j)),
                  pl.BlockSpec((1, 1, D), lambda l, j: (l, 0, j))],
        out_specs=pl.BlockSpec((1, rows, D), lambda l, j: (l, 0, j)),
        out_shape=jax.ShapeDtypeStruct((L, rows, D6), F32),
        compiler_params=_cparams(("parallel", "parallel"), 32),
        name="adaln",
    )(cp, w_ada, b_ada.reshape(L, 1, D6))
    return out[:, :B].reshape(L, B, D6 // D, 1, D)


def _inproj_kernel(x_ref, gain_ref, sc_ref, sh_ref, w_ref, dqg_ref, dkg_ref,
                   oa_ref, mqk_ref, mv_ref, mo_ref, mif_ref, qd_ref, kd_ref, vd_ref, g_ref):
    x = x_ref[...]
    ms = jnp.mean(x * x, axis=-1, keepdims=True)
    y = x * lax.rsqrt(ms + EPS) * gain_ref[0]
    u = (y * (1.0 + sc_ref[0]) + sh_ref[0]).astype(BF16)

    def proj(a, b):
        return jnp.dot(u, w_ref[0, :, a:b], preferred_element_type=F32)

    oa_ref[...] = proj(COL_A, COL_MQK)
    mqk_ref[...] = proj(COL_MQK, COL_MV)
    _store_v_tiles_t(proj(COL_MV, COL_MO), mv_ref, MLSTM_V)
    mo_ref[...] = proj(COL_MO, COL_MIF).astype(BF16)
    mif_ref[...] = proj(COL_MIF, COL_DQ)

    half = LANES // 2
    low = lax.broadcasted_iota(jnp.int32, (1, LANES), 1) < half

    def head_norm(z, g):
        for h in range(DIL_HEADS):
            pair = z[:, (h // 2) * LANES:(h // 2 + 1) * LANES]
            zh = jnp.where(low, pair if h % 2 == 0 else pltpu.roll(pair, half, 1), 0.0)
            r = lax.rsqrt(jnp.sum(zh * zh, axis=-1, keepdims=True) / DIL_HEAD_DIM + EPS)
            yield h, zh * r * g

    dq = proj(COL_DQ, COL_DK)
    for h, zn in head_norm(dq, dqg_ref[0]):
        qd_ref[0, h // 2, :, (h % 2) * LANES:(h % 2 + 1) * LANES] = (zn * (DIL_HEAD_DIM ** -0.5 * LOG2E)).astype(BF16)
    dk = proj(COL_DK, COL_DV)
    for h, zn in head_norm(dk, dkg_ref[0]):
        kd_ref[0, h // 2, :, (h % 2) * LANES:(h % 2 + 1) * LANES] = zn.astype(BF16)
    _store_v_tiles_t(proj(COL_DV, COL_G), vd_ref)
    for c in range(N_BRANCH):
        g_ref[:, c * D_MODEL:(c + 1) * D_MODEL] = proj(COL_G + c * D_MODEL, COL_G + (c + 1) * D_MODEL).astype(BF16)


def _inproj(x2, S, layer, gain, sc, sh, w_pad, dqg, dkg, tm=512):
    N, D = x2.shape
    per_b = S // tm
    row = lambda i: (i, 0)
    bsel = lambda i: (i // per_b, 0, 0)
    B = N // S
    outs = [(512, F32), (512, F32), "mvt", (512, BF16), (LANES, F32),
            "qk", "qk", "vt", (3072, BF16)]
    head_major = lambda i: (i // per_b, 0, i % per_b, 0)
    time_minor = lambda i: (i // per_b, 0, 0, i % per_b)
    special_spec = {"qk": pl.BlockSpec((1, DIL_HEADS // 2, tm, 2 * LANES), head_major),
                    "vt": pl.BlockSpec((1, DIL_HEADS, V_ROWS, tm), time_minor),
                    "mvt": pl.BlockSpec((1, MLSTM_HEADS, MLSTM_V + V_FILL, tm), time_minor)}
    special_shape = {"qk": jax.ShapeDtypeStruct((B, DIL_HEADS // 2, S, 2 * LANES), BF16),
                     "vt": jax.ShapeDtypeStruct((B, DIL_HEADS, V_ROWS, S), BF16),
                     "mvt": jax.ShapeDtypeStruct((B, MLSTM_HEADS, MLSTM_V + V_FILL, S), BF16)}
    return pl.pallas_call(
        _inproj_kernel,
        grid=(N // tm,),
        in_specs=[pl.BlockSpec((tm, D), row),
                  _const_spec((1, 1, D), lambda i: (layer, 0, 0)),
                  pl.BlockSpec((1, 1, D), bsel),
                  pl.BlockSpec((1, 1, D), bsel),
                  _const_spec((1, D, COL_END), lambda i: (layer, 0, 0)),
                  _const_spec((1, 1, LANES), lambda i: (layer, 0, 0)),
                  _const_spec((1, 1, LANES), lambda i: (layer, 0, 0))],
        out_specs=[special_spec[o] if isinstance(o, str) else pl.BlockSpec((tm, o[0]), row) for o in outs],
        out_shape=[special_shape[o] if isinstance(o, str) else jax.ShapeDtypeStruct((N, o[0]), o[1]) for o in outs],
        compiler_params=_cparams(("parallel",), 52),
        name="inproj",
    )(x2, gain, sc, sh, w_pad, dqg, dkg)


def _mla_prep_kernel(a_ref, qn_ref, kvn_ref, wuq_ref, wkv_ref, qg_ref, qgs_ref, kg_ref, cos_ref, sin_ref,
                     q_ref, k_ref, vt_ref):
    HW = MLA_HEADS * LANES
    a = a_ref[...]
    cq = a[:, :MLA_Q_LORA]
    ckv = a[:, MLA_Q_LORA:MLA_Q_LORA + MLA_KV_LORA]
    kr_tile = a[:, MLA_Q_LORA + MLA_KV_LORA:]

    def rms(z, g):
        return z * lax.rsqrt(jnp.mean(z * z, axis=-1, keepdims=True) + EPS) * g

    cqn = rms(cq, qn_ref[0]).astype(BF16)
    ckvn = rms(ckv, kvn_ref[0]).astype(BF16)
    qr = jnp.dot(cqn, wuq_ref[0], preferred_element_type=F32)
    kvr = jnp.dot(ckvn, wkv_ref[0], preferred_element_type=F32)
    _store_v_tiles_t(kvr[:, HW:], vt_ref)

    cosf, sins = cos_ref[...], sin_ref[...]
    q_scale = MLA_QK ** -0.5 * LOG2E
    gc = cosf * (qg_ref[0] * q_scale)
    gs = sins * (qgs_ref[0] * q_scale)
    kg = kg_ref[0]
    half = MLA_ROPE // 2
    krg = pltpu.roll(kr_tile, MLA_NOPE, 1) * kg
    kr_rot = krg * cosf + (pltpu.roll(krg, LANES - half, 1) + pltpu.roll(krg, half, 1)) * sins
    kr_ss = jnp.sum(kr_tile * kr_tile, axis=-1, keepdims=True)
    for h in range(MLA_HEADS):
        sl = slice(h * LANES, (h + 1) * LANES)
        grp, pair_sl = h // MLA_HEADS_PER_STEP, slice((h % MLA_HEADS_PER_STEP) * LANES,
                                                     (h % MLA_HEADS_PER_STEP + 1) * LANES)
        z = qr[:, sl]
        rq = lax.rsqrt(jnp.sum(z * z, axis=-1, keepdims=True) / MLA_QK + EPS)
        q_ref[0, grp, :, pair_sl] = ((z * gc + qr[:, HW + h * LANES:HW + (h + 1) * LANES] * gs) * rq).astype(BF16)
        zk = kvr[:, sl]
        rk = lax.rsqrt((jnp.sum(zk * zk, axis=-1, keepdims=True) + kr_ss) / MLA_QK + EPS)
        k_ref[0, grp, :, pair_sl] = ((zk * kg + kr_rot) * rk).astype(BF16)


def _rope_tables(S):
    half = MLA_ROPE // 2
    inv = ROPE_THETA ** (-jnp.arange(half, dtype=F32) / half)
    ang = jnp.arange(S).astype(F32)[:, None] * inv[None, :]
    cos, sin = jnp.cos(ang), jnp.sin(ang)
    one = jnp.ones((S, MLA_NOPE), F32)
    z = lambda n: jnp.zeros((S, n), F32)
    tail = LANES - MLA_QK
    cosf = jnp.concatenate([one, cos, cos, z(tail)], axis=1)
    sins = jnp.concatenate([z(MLA_NOPE), -sin, sin, z(tail)], axis=1)
    return cosf, sins


def _mla_prep(oa, S, layer, qn, kvn, wuq, wkv, qg, qgs, kg, tables, tm=512):
    N = oa.shape[0]
    B = N // S
    per_b = S // tm
    row = lambda i: (i, 0)
    pos = lambda i: (i % per_b, 0)
    HW = MLA_HEADS * LANES
    G, GW = MLA_HEADS // MLA_HEADS_PER_STEP, MLA_HEADS_PER_STEP * LANES
    return pl.pallas_call(
        _mla_prep_kernel,
        grid=(N // tm,),
        in_specs=[pl.BlockSpec((tm, 512), row),
                  _const_spec((1, 1, MLA_Q_LORA), lambda i: (layer, 0, 0)),
                  _const_spec((1, 1, MLA_KV_LORA), lambda i: (layer, 0, 0)),
                  _const_spec((1, MLA_Q_LORA, 2 * HW), lambda i: (layer, 0, 0)),
                  _const_spec((1, MLA_KV_LORA, HW + MLA_HEADS * MLA_V), lambda i: (layer, 0, 0)),
                  _const_spec((1, 1, LANES), lambda i: (layer, 0, 0)),
                  _const_spec((1, 1, LANES), lambda i: (layer, 0, 0)),
                  _const_spec((1, 1, LANES), lambda i: (layer, 0, 0)),
                  pl.BlockSpec((tm, LANES), pos),
                  pl.BlockSpec((tm, LANES), pos)],
        out_specs=[pl.BlockSpec((1, G, tm, GW), lambda i: (i // per_b, 0, i % per_b, 0)),
                   pl.BlockSpec((1, G, tm, GW), lambda i: (i // per_b, 0, i % per_b, 0)),
                   pl.BlockSpec((1, MLA_HEADS, V_ROWS, tm), lambda i: (i // per_b, 0, 0, i % per_b))],
        out_shape=[jax.ShapeDtypeStruct((B, G, S, GW), BF16),
                   jax.ShapeDtypeStruct((B, G, S, GW), BF16),
                   jax.ShapeDtypeStruct((B, MLA_HEADS, V_ROWS, S), BF16)],
        compiler_params=_cparams(("parallel",), 32),
        name="mla_prep",
    )(oa, qn, kvn, wuq, wkv, qg, qgs, kg, *tables)


def _flash_kernel(qi_ref, kj_ref, qn_ref, kn_ref, q0_ref, k0_ref, vt_ref, o_ref, m_ref, acc_ref, s_ref, *, T, n_pairs):
    p_id = pl.program_id(2)
    qi = qi_ref[p_id]
    kj = kj_ref[p_id]
    first = kj == 0
    last = kj == qi

    n_heads = m_ref.shape[0]
    units = [(hh, slice(j * MXU_COLS, (j + 1) * MXU_COLS)) for hh in range(n_heads) for j in range(T // MXU_COLS)]

    def scores(q_ref, k_ref, hh, qs, rows=T):
        q = q_ref[0, 0, qs, hh * LANES:(hh + 1) * LANES]
        k = k_ref[0, 0, 0:rows, hh * LANES:(hh + 1) * LANES]
        return lax.dot_general(k, q, (((1,), (1,)), ((), ())), preferred_element_type=F32)

    @pl.when(p_id == 0)
    def _():
        for hh, qs in units:
            s_ref[0, hh, :, qs] = scores(q0_ref, k0_ref, hh, qs)

    @pl.when(first)
    def _():
        m_ref[...] = jnp.full(m_ref.shape, NEG_INF, F32)
        acc_ref[...] = jnp.zeros(acc_ref.shape, F32)

    nxt = jnp.minimum(p_id + 1, n_pairs - 1)

    def step(cur_diag, next_diag, slot):
        for hh, qs in units:
            rows_next = qs.stop if next_diag else T
            s_ref[1 - slot, hh, 0:rows_next, qs] = scores(qn_ref, kn_ref, hh, qs, rows_next)
            rows = qs.stop if cur_diag else T
            st = s_ref[slot, hh, 0:rows, qs]
            if cur_diag:
                key = lax.broadcasted_iota(jnp.int32, (rows, MXU_COLS), 0)
                qry = lax.broadcasted_iota(jnp.int32, (rows, MXU_COLS), 1) + qs.start
                st = jnp.where(key <= qry, st, NEG_INF)
            m_prev = m_ref[hh, :, qs]
            m_new = jnp.maximum(m_prev, jnp.max(st, axis=0, keepdims=True))
            alpha = jnp.exp2(m_prev - m_new)
            pt = jnp.exp2(st - m_new).astype(BF16)
            acc_ref[hh, :, qs] = (acc_ref[hh, :, qs] * alpha
                                  + jnp.dot(vt_ref[0, hh, :, 0:rows], pt, preferred_element_type=F32))
            m_ref[hh, :, qs] = m_new

    parity = p_id % 2
    cur_diag = last
    next_diag = jnp.logical_and(jnp.logical_not(cur_diag), qi_ref[nxt] == kj_ref[nxt])
    plain = jnp.logical_not(jnp.logical_or(cur_diag, next_diag))
    for slot in range(2):
        here = parity == slot
        pl.when(jnp.logical_and(here, cur_diag))(functools.partial(step, True, False, slot))
        pl.when(jnp.logical_and(here, next_diag))(functools.partial(step, False, True, slot))
        pl.when(jnp.logical_and(here, plain))(functools.partial(step, False, False, slot))

    @pl.when(last)
    def _():
        outs = []
        for hh in range(n_heads):
            a = acc_ref[hh]
            outs.append(a[0:V_DIM, :] / a[V_DIM:V_DIM + 1, :])
        o_ref[0] = jnp.concatenate(outs, axis=0).T.astype(o_ref.dtype)


def _band_bias(T, W):
    d = (jnp.arange(W + 1)[:, None, None] * T + jnp.arange(T)[None, None, :] - jnp.arange(T)[None, :, None])
    cw = jnp.zeros(d.shape, F32)
    for window, dil in DIL_PATTERNS:
        cw = cw + ((d >= 0) & (d <= window) & (d % dil == 0)).astype(F32)
    return jnp.log2(cw)


def _causal_attention(q, k, vt, T=FLASH_TILE):
    B, G, S, GW = q.shape
    g = GW // LANES
    nq = S // T
    pairs = [(i, j) for i in range(nq) for j in range(i + 1)]
    n = len(pairs)
    qi = jnp.asarray(np.array([p[0] for p in pairs], np.int32))
    kj = jnp.asarray(np.array([p[1] for p in pairs], np.int32))
    nxt = lambda p: jnp.minimum(p + 1, n - 1)
    in_specs = [pl.BlockSpec((1, 1, T, GW), lambda b, h, p, qi, kj: (b, h, qi[nxt(p)], 0)),
                pl.BlockSpec((1, 1, T, GW), lambda b, h, p, qi, kj: (b, h, kj[nxt(p)], 0)),
                pl.BlockSpec((1, 1, T, GW), lambda b, h, p, qi, kj: (b, h, qi[0], 0)),
                pl.BlockSpec((1, 1, T, GW), lambda b, h, p, qi, kj: (b, h, kj[0], 0)),
                pl.BlockSpec((1, g, V_ROWS, T), lambda b, h, p, qi, kj: (b, h, 0, kj[p]))]
    return pl.pallas_call(
        functools.partial(_flash_kernel, T=T, n_pairs=n),
        grid_spec=pltpu.PrefetchScalarGridSpec(
            num_scalar_prefetch=2,
            grid=(B, G, n),
            in_specs=in_specs,
            out_specs=pl.BlockSpec((1, T, g * V_DIM), lambda b, h, p, qi, kj: (b, qi[p], h)),
            scratch_shapes=[pltpu.VMEM((g, 1, T), F32), pltpu.VMEM((g, V_ROWS, T), F32),
                            pltpu.VMEM((2, g, T, T), F32)]),
        out_shape=jax.ShapeDtypeStruct((B, S, G * g * V_DIM), BF16),
        compiler_params=_cparams(("arbitrary", "arbitrary", "arbitrary"), 56),
        name="flash_causal",
    )(qi, kj, q, k, q, k, vt)


def _band_kernel(qc_ref, qn_ref, kn_ref, *rest, T, W):
    k_refs, vt_refs = rest[:W + 1], rest[W + 1:2 * W + 2]
    bias_ref, o_ref, m_ref, acc_ref, s_ref = rest[2 * W + 2:]
    i = pl.program_id(2)
    units = [(hh, j) for hh in range(2) for j in range(T // MXU_COLS)]

    def rows_of(d, j):
        if d == 0:
            return 0, (j + 1) * MXU_COLS
        if d == W:
            return j * MXU_COLS, T
        return 0, T

    def produce(q_ref, k_ref, d, hh, j, bias_idx):
        lo, hi = rows_of(d, j)
        qs = slice(j * MXU_COLS, (j + 1) * MXU_COLS)
        q = q_ref[0, 0, qs, hh * LANES:(hh + 1) * LANES]
        k = k_ref[0, 0, lo:hi, hh * LANES:(hh + 1) * LANES]
        s = lax.dot_general(k, q, (((1,), (1,)), ((), ())), preferred_element_type=F32)
        s_ref[d, hh, lo:hi, qs] = s + bias_ref[bias_idx, lo:hi, qs]

    def consume(d, hh, j):
        lo, hi = rows_of(d, j)
        qs = slice(j * MXU_COLS, (j + 1) * MXU_COLS)
        st = s_ref[d, hh, lo:hi, qs]
        m_prev = m_ref[hh, :, qs]
        m_new = jnp.maximum(m_prev, jnp.max(st, axis=0, keepdims=True))
        alpha = jnp.exp2(m_prev - m_new)
        pt = jnp.exp2(st - m_new).astype(BF16)
        acc_ref[hh, :, qs] = (acc_ref[hh, :, qs] * alpha
                              + jnp.dot(vt_refs[d][0, hh, :, lo:hi], pt, preferred_element_type=F32))
        m_ref[hh, :, qs] = m_new

    @pl.when(i == 0)
    def _():
        for hh, j in units:
            produce(qc_ref, k_refs[0], 0, hh, j, 0)

    m_ref[...] = jnp.full(m_ref.shape, NEG_INF, F32)
    acc_ref[...] = jnp.zeros(acc_ref.shape, F32)
    for d in range(W + 1):
        for hh, j in units:
            if d < W:
                produce(qc_ref, k_refs[d + 1], d + 1, hh, j, jnp.where(i >= d + 1, d + 1, W + 1))
            else:
                produce(qn_ref, kn_ref, 0, hh, j, 0)
            consume(d, hh, j)

    outs = []
    for hh in range(2):
        a = acc_ref[hh]
        outs.append(a[0:V_DIM, :] / a[V_DIM:V_DIM + 1, :])
    o_ref[0] = jnp.concatenate(outs, axis=0).T.astype(o_ref.dtype)


def _band_attention(q, k, vt, T=FLASH_TILE):
    B, HP, S, _ = q.shape
    n_q = S // T
    W = DIL_MAX_WINDOW // T
    bias = jnp.concatenate([_band_bias(T, W), jnp.full((1, T, T), NEG_INF, F32)], axis=0)
    qk_block = lambda f: pl.BlockSpec((1, 1, T, 2 * LANES), lambda b, h, i: (b, h, f(i), 0))
    vt_block = lambda f: pl.BlockSpec((1, 2, V_ROWS, T), lambda b, h, i: (b, h, 0, f(i)))
    back = lambda d: (lambda i: jnp.maximum(i - d, 0))
    nxt = lambda i: jnp.minimum(i + 1, n_q - 1)
    in_specs = ([qk_block(lambda i: i), qk_block(nxt), qk_block(nxt)]
                + [qk_block(back(d)) for d in range(W + 1)]
                + [vt_block(back(d)) for d in range(W + 1)]
                + [_const_spec((W + 2, T, T), lambda b, h, i: (0, 0, 0))])
    return pl.pallas_call(
        functools.partial(_band_kernel, T=T, W=W),
        grid=(B, HP, n_q),
        in_specs=in_specs,
        out_specs=pl.BlockSpec((1, T, LANES), lambda b, h, i: (b, i, h)),
        out_shape=jax.ShapeDtypeStruct((B, S, 2 * HP * V_DIM), BF16),
        scratch_shapes=[pltpu.VMEM((2, 1, T), F32), pltpu.VMEM((2, V_ROWS, T), F32),
                        pltpu.VMEM((W + 1, 2, T, T), F32)],
        compiler_params=_cparams(("arbitrary", "arbitrary", "arbitrary"), 58),
        name="flash_band",
    )(q, q, k, *([k] * (W + 1)), *([vt] * (W + 1)), bias)


def _mlstm_kernel(qk_ref, vt_ref, o_ref, if_ref, cw_ref, cb_ref, gb_ref, hg_ref, y_ref,
                  xs_ref, cn_ref, m_ref, *, Lc):
    H, DK, DV = MLSTM_HEADS, MLSTM_QK, MLSTM_V
    HALO = 8

    @pl.when(pl.program_id(1) == 0)
    def _():
        xs_ref[0:HALO, :] = jnp.zeros((HALO, 2 * H * DK), F32)
        cn_ref[...] = jnp.zeros(cn_ref.shape, F32)
        m_ref[...] = jnp.zeros(m_ref.shape, F32)

    x = qk_ref[...]
    xs_ref[HALO:HALO + Lc, :] = x
    y = cb_ref[0]
    for j in range(CONV_K):
        y = y + cw_ref[0, j:j + 1, :] * xs_ref[pl.ds(HALO - (CONV_K - 1) + j, Lc), :]
    xs_ref[0:HALO, :] = x[Lc - HALO:, :]
    qk = y * _sigmoid(y)
    q_t = qk[:, :H * DK].T.astype(BF16)
    k_all = (qk[:, H * DK:] * (DK ** -0.5)).astype(BF16)

    g = if_ref[...] + gb_ref[0]
    logf = jnp.minimum(g, 0.0) - jnp.log(1.0 + jnp.exp(-jnp.abs(g)))
    src = lax.broadcasted_iota(jnp.int32, (Lc, Lc), 0)
    dst = lax.broadcasted_iota(jnp.int32, (Lc, Lc), 1)
    tri = jnp.where(dst <= src, 1.0, 0.0).astype(F32)
    bcum = jnp.dot(tri, logf, preferred_element_type=F32, precision=HIGHEST)
    causal = src <= dst
    g_t = g.T
    bcum_t = bcum.T

    for h in range(H):
        b_row = bcum_t[H + h:H + h + 1, :]
        w_col = g[:, h:h + 1] - bcum[:, H + h:H + h + 1]
        w_row = g_t[h:h + 1, :] - b_row
        m_prev = m_ref[h:h + 1, 0:1]
        dlog = jnp.where(causal, w_col + b_row, NEG_INF)
        inter = b_row + m_prev
        m_t = jnp.maximum(inter, jnp.max(dlog, axis=0, keepdims=True))
        qh_t = q_t[h * DK:(h + 1) * DK, :]
        kh = k_all[:, h * DK:(h + 1) * DK]
        s_t = jnp.dot(kh, qh_t, preferred_element_type=F32)
        sc_t = (s_t * jnp.exp(dlog - m_t)).astype(BF16)
        decay = jnp.exp(inter - m_t)
        cn = cn_ref[h]
        vt = vt_ref[0, h]
        num = (decay * jnp.dot(cn.astype(BF16), qh_t, preferred_element_type=F32)
               + jnp.dot(vt, sc_t, preferred_element_type=F32))
        den = num[DV:DV + 1, :]
        hh = num[0:DV, :] / jnp.maximum(jnp.abs(den), jnp.exp(-m_t))
        hn = hh * lax.rsqrt(jnp.mean(hh * hh, axis=0, keepdims=True) + EPS)
        o_gate = _sigmoid(o_ref[:, h * DV:(h + 1) * DV].astype(F32))
        y_ref[:, h * DV:(h + 1) * DV] = (hn.T * hg_ref[0, :, h * DV:(h + 1) * DV] * o_gate).astype(y_ref.dtype)

        b_end = b_row[:, Lc - 1:Lc]
        wlog = b_end + w_row
        m_new = jnp.maximum(b_end + m_prev, jnp.max(wlog, axis=1, keepdims=True))
        cd = jnp.exp(b_end + m_prev - m_new)
        vw = (vt.astype(F32) * jnp.exp(wlog - m_new)).astype(BF16)
        cn_ref[h] = cd * cn + jnp.dot(vw, kh, preferred_element_type=F32)
        m_ref[h:h + 1, :] = jnp.broadcast_to(m_new, (1, LANES))


def _mlstm(mqk, mvt, mo, mif, B, S, layer, conv_w, conv_b, gate_b, head_gain, Lc=256):
    H, DK, DV = MLSTM_HEADS, MLSTM_QK, MLSTM_V
    nc = S // Lc
    row = lambda b, c: (b * nc + c, 0)
    VR = mvt.shape[2]
    return pl.pallas_call(
        functools.partial(_mlstm_kernel, Lc=Lc),
        grid=(B, nc),
        in_specs=[pl.BlockSpec((Lc, 2 * H * DK), row),
                  pl.BlockSpec((1, H, VR, Lc), lambda b, c: (b, 0, 0, c)),
                  pl.BlockSpec((Lc, H * DV), row),
                  pl.BlockSpec((Lc, LANES), row),
                  _const_spec((1, CONV_K, 2 * H * DK), lambda b, c: (layer, 0, 0)),
                  _const_spec((1, 1, 2 * H * DK), lambda b, c: (layer, 0, 0)),
                  _const_spec((1, 1, LANES), lambda b, c: (layer, 0, 0)),
                  _const_spec((1, 1, H * DV), lambda b, c: (layer, 0, 0))],
        out_specs=pl.BlockSpec((Lc, H * DV), row),
        out_shape=jax.ShapeDtypeStruct((B * S, H * DV), BF16),
        scratch_shapes=[pltpu.VMEM((Lc + 8, 2 * H * DK), F32), pltpu.VMEM((H, VR, DK), F32),
                        pltpu.VMEM((8, LANES), F32)],
        compiler_params=_cparams(("arbitrary", "arbitrary"), 32),
        name="mlstm",
    )(mqk, mvt, mo, mif, conv_w, conv_b, gate_b, head_gain)


def _merge_kernel(ya_ref, yb_ref, yc_ref, g_ref, x_ref, ga_ref, wa_ref, wb_ref, wc_ref, wo_ref, out_ref):
    def branch(y_ref, w_ref, c):
        gate = _sigmoid(g_ref[:, c * D_MODEL:(c + 1) * D_MODEL].astype(F32))
        return gate * jnp.dot(y_ref[...], w_ref[0], preferred_element_type=F32)

    merged = branch(ya_ref, wa_ref, 0) + branch(yb_ref, wb_ref, 1) + branch(yc_ref, wc_ref, 2)
    y = jnp.dot(merged.astype(BF16), wo_ref[0], preferred_element_type=F32)
    out_ref[...] = x_ref[...] + ga_ref[0] * y


def _merge(ya, yb, yc, gates, x2, S, layer, ga, wa, wb, wc, wo, tm=512):
    N, D = x2.shape
    per_b = S // tm
    row = lambda i: (i, 0)
    wsel = lambda i: (layer, 0, 0)
    return pl.pallas_call(
        _merge_kernel,
        grid=(N // tm,),
        in_specs=[pl.BlockSpec((tm, 512), row), pl.BlockSpec((tm, 512), row), pl.BlockSpec((tm, 512), row),
                  pl.BlockSpec((tm, N_BRANCH * D), row), pl.BlockSpec((tm, D), row),
                  pl.BlockSpec((1, 1, D), lambda i: (i // per_b, 0, 0)),
                  _const_spec((1, 512, D), wsel), _const_spec((1, 512, D), wsel), _const_spec((1, 512, D), wsel),
                  _const_spec((1, D, D), wsel)],
        out_specs=pl.BlockSpec((tm, D), row),
        out_shape=jax.ShapeDtypeStruct((N, D), F32),
        compiler_params=_cparams(("parallel",), 48),
        name="merge",
    )(ya, yb, yc, gates, x2, ga, wa, wb, wc, wo)


def _ffn_kernel(x_ref, gain_ref, sc_ref, sh_ref, gf_ref, wr_ref, br_ref, wg_ref, wu_ref, wd_ref, out_ref):
    x = x_ref[...]
    tm = x.shape[0]
    ms = jnp.mean(x * x, axis=-1, keepdims=True)
    hf = x * lax.rsqrt(ms + EPS) * gain_ref[0] * (1.0 + sc_ref[0]) + sh_ref[0]

    hf_hi = hf.astype(BF16)
    hf_lo = (hf - hf_hi.astype(F32)).astype(BF16)
    wr = wr_ref[0]
    wr_hi = wr.astype(BF16)
    wr_split = jnp.concatenate([wr_hi, (wr - wr_hi.astype(F32)).astype(BF16)], axis=1)
    parts = (jnp.dot(hf_hi, wr_split, preferred_element_type=F32)
             + jnp.dot(hf_lo, wr_split, preferred_element_type=F32))
    logits = parts[:, :LANES] + parts[:, LANES:] + br_ref[0]
    lane_i = lax.broadcasted_iota(jnp.int32, (tm, LANES), 1)
    lane = lane_i.astype(F32)
    big = float(LANES)

    def first_argmax(z):
        zmax = jnp.max(z, axis=1, keepdims=True)
        return zmax, jnp.min(jnp.where(z == zmax, lane, big), axis=1, keepdims=True)

    lg = jnp.where(lane_i < N_GROUPS, logits, NEG_INF)
    gmax, g_idx = first_argmax(lg)
    p_sel = 1.0 / jnp.sum(jnp.exp(lg - gmax), axis=1, keepdims=True)
    lane_grp = ((lane_i - N_GROUPS) // EXPERTS_PER_GROUP).astype(F32)
    in_grp = (lane_i >= N_GROUPS) & (lane_i < N_GROUPS + N_EXPERTS) & (lane_grp == g_idx)
    le = jnp.where(in_grp, logits, NEG_INF)
    v1, i1 = first_argmax(le)
    le2 = jnp.where(lane == i1, NEG_INF, le)
    v2, i2 = first_argmax(le2)
    e2 = jnp.exp(v2 - v1)
    den = 1.0 + e2
    comb = jnp.where(lane == i1, (1.0 / den) * p_sel, 0.0) + jnp.where(lane == i2, (e2 / den) * p_sel, 0.0)

    hb = hf_hi
    acc = jnp.zeros((tm, D_MODEL), F32)
    for e in range(N_EXPERTS):
        gate = jnp.dot(hb, wg_ref[0, e], preferred_element_type=F32)
        up = jnp.dot(hb, wu_ref[0, e], preferred_element_type=F32)
        hid = gate * _sigmoid(gate) * up * comb[:, N_GROUPS + e:N_GROUPS + e + 1]
        acc = acc + jnp.dot(hid.astype(BF16), wd_ref[0, e], preferred_element_type=F32)
    out_ref[...] = x + gf_ref[0] * acc


def _ffn(x2, S, layer, gain, sc, sh, gf, wr, br, wg, wu, wd, tm=512):
    N, D = x2.shape
    per_b = S // tm
    row = lambda i: (i, 0)
    bsel = lambda i: (i // per_b, 0, 0)
    wsel = lambda i: (layer, 0, 0, 0)
    return pl.pallas_call(
        _ffn_kernel,
        grid=(N // tm,),
        in_specs=[pl.BlockSpec((tm, D), row),
                  _const_spec((1, 1, D), lambda i: (layer, 0, 0)),
                  pl.BlockSpec((1, 1, D), bsel), pl.BlockSpec((1, 1, D), bsel), pl.BlockSpec((1, 1, D), bsel),
                  _const_spec((1, D, LANES), lambda i: (layer, 0, 0)),
                  _const_spec((1, 1, LANES), lambda i: (layer, 0, 0)),
                  _const_spec((1, N_EXPERTS, D, EXPERT_FF), wsel),
                  _const_spec((1, N_EXPERTS, D, EXPERT_FF), wsel),
                  _const_spec((1, N_EXPERTS, EXPERT_FF, D), wsel)],
        out_specs=pl.BlockSpec((tm, D), row),
        out_shape=jax.ShapeDtypeStruct((N, D), F32),
        compiler_params=_cparams(("parallel",), 56),
        name="ffn",
    )(x2, gain, sc, sh, gf, wr, br, wg, wu, wd)


def _pad_last(a, n):
    return jnp.pad(a, [(0, 0)] * (a.ndim - 1) + [(0, n - a.shape[-1])])


def _prep_w_in(w_in):
    pts = np.cumsum(IN_SIZES)[:-1].tolist()
    cq, ckv, kr, m_qk, m_v, m_o, m_i, m_f, d_q, d_k, d_v, gates = jnp.split(w_in, pts, axis=-1)
    seg_a = _pad_last(jnp.concatenate([cq, ckv, kr], axis=-1), 512)
    seg_if = _pad_last(jnp.concatenate([m_i, m_f], axis=-1), LANES)
    w = jnp.concatenate([seg_a, m_qk, m_v, m_o, seg_if,
                         d_q, d_k,
                         d_v, gates], axis=-1)
    assert w.shape[-1] == COL_END
    return w.astype(BF16)


def kernel(x, c, w_ada, b_ada, attn_norm, w_in, mla_q_norm, mla_kv_norm, mla_w_uq, mla_w_ukv, mla_q_gain, mla_k_gain, mlstm_conv_w, mlstm_conv_b, mlstm_b_i, mlstm_b_f, mlstm_head_gain, dil_q_gain, dil_k_gain, w_branch_a, w_branch_b, w_branch_c, w_out, ffn_norm, w_router_group, b_router_group, w_router_expert, b_router_expert, w_exp_gate, w_exp_up, w_exp_down):
    B, S, D = x.shape
    L = w_ada.shape[0]
    N = B * S
    assert D == D_MODEL and S % DIL_MAX_WINDOW == 0

    mod = _adaln(c, w_ada, b_ada)

    w_pad = _prep_w_in(w_in)
    half = MLA_ROPE // 2
    swap_rope = lambda a: jnp.concatenate(
        [jnp.zeros_like(a[..., :MLA_NOPE]), a[..., MLA_NOPE + half:], a[..., MLA_NOPE:MLA_NOPE + half]], axis=-1)
    uq = mla_w_uq.reshape(L, MLA_Q_LORA, MLA_HEADS, MLA_QK)
    wuq = jnp.concatenate([_pad_last(u, LANES).reshape(L, MLA_Q_LORA, MLA_HEADS * LANES)
                           for u in (uq, swap_rope(uq))], axis=-1).astype(BF16)
    ukv = mla_w_ukv.reshape(L, MLA_KV_LORA, MLA_HEADS, MLA_NOPE + MLA_V)
    wkv = jnp.concatenate([_pad_last(ukv[..., :MLA_NOPE], LANES).reshape(L, MLA_KV_LORA, MLA_HEADS * LANES),
                           ukv[..., MLA_NOPE:].reshape(L, MLA_KV_LORA, MLA_HEADS * MLA_V)], axis=-1).astype(BF16)
    rows = lambda a: a[:, None, :]
    qg = rows(_pad_last(mla_q_gain, LANES))
    qgs = rows(_pad_last(swap_rope(mla_q_gain), LANES))
    kg = rows(_pad_last(mla_k_gain, LANES))
    dqg = rows(_pad_last(dil_q_gain, LANES))
    dkg = rows(_pad_last(dil_k_gain, LANES))
    gate_b = rows(_pad_last(jnp.concatenate([mlstm_b_i, mlstm_b_f], axis=-1), LANES))
    tables = _rope_tables(S)
    wa, wb, wc, wo = (w.astype(BF16) for w in (w_branch_a, w_branch_b, w_branch_c, w_out))
    wr = _pad_last(jnp.concatenate([w_router_group, w_router_expert], axis=-1), LANES)
    br = rows(_pad_last(jnp.concatenate([b_router_group, b_router_expert], axis=-1), LANES))
    wg, wu, wd = (w.astype(BF16) for w in (w_exp_gate, w_exp_up, w_exp_down))
    attn_norm, ffn_norm, mla_q_norm, mla_kv_norm, mlstm_conv_b, mlstm_head_gain = (
        rows(a) for a in (attn_norm, ffn_norm, mla_q_norm, mla_kv_norm, mlstm_conv_b, mlstm_head_gain))

    x2 = x.reshape(N, D)
    for l in range(L):
        sh_a, sc_a, g_a, sh_f, sc_f, g_f = (mod[l, :, i] for i in range(6))
        oa, mqk, mv, mo, mif, qd, kd, vd, gates = _inproj(x2, S, l, attn_norm, sc_a, sh_a, w_pad, dqg, dkg)
        q, k, vt = _mla_prep(oa, S, l, mla_q_norm, mla_kv_norm, wuq, wkv, qg, qgs, kg, tables)
        ya = _causal_attention(q, k, vt)
        yb = _mlstm(mqk, mv, mo, mif, B, S, l, mlstm_conv_w, mlstm_conv_b, gate_b, mlstm_head_gain)
        yc = _band_attention(qd, kd, vd)
        x2 = _merge(ya.reshape(N, -1), yb, yc.reshape(N, -1), gates, x2, S, l, g_a, wa, wb, wc, wo)
        x2 = _ffn(x2, S, l, ffn_norm, sc_f, sh_f, g_f, wr, br, wg, wu, wd)
    return x2.reshape(B, S, D)
```

```python
import functools

import jax
import jax.numpy as jnp
import numpy as np
from jax import lax
from jax.experimental import pallas as pl
from jax.experimental.pallas import tpu as pltpu

F32 = jnp.float32
BF16 = jnp.bfloat16
HIGHEST = lax.Precision.HIGHEST
NEG_INF = float("-inf")

EPS = 1e-6
LANES = 128
MXU_COLS = 256
VMEM_BYTES_V7X = 64 * 1024 * 1024

D_MODEL = 1024
MLA_HEADS = 8
MLA_Q_LORA = 256
MLA_KV_LORA = 128
MLA_NOPE = 64
MLA_ROPE = 32
MLA_V = 64
MLA_QK = MLA_NOPE + MLA_ROPE
ROPE_THETA = 10000.0
MLSTM_HEADS = 4
MLSTM_QK = 64
MLSTM_V = 128
CONV_K = 4
DIL_HEADS = 8
DIL_HEAD_DIM = 64
DIL_PATTERNS = ((128, 1), (512, 4), (2048, 16))
DIL_WIDTH = DIL_HEADS * DIL_HEAD_DIM
N_GROUPS = 4
EXPERTS_PER_GROUP = 4
N_EXPERTS = N_GROUPS * EXPERTS_PER_GROUP
EXPERT_FF = 256
N_BRANCH = 3
IN_SIZES = (MLA_Q_LORA, MLA_KV_LORA, MLA_ROPE,
            2 * MLSTM_HEADS * MLSTM_QK, MLSTM_HEADS * MLSTM_V, MLSTM_HEADS * MLSTM_V, MLSTM_HEADS, MLSTM_HEADS,
            DIL_WIDTH, DIL_WIDTH, DIL_WIDTH, N_BRANCH * D_MODEL)

COL_A = 0
COL_MQK = 512
COL_MV = 1024
COL_MO = 1536
COL_MIF = 2048
COL_DQ = 2176
COL_DK = 2688
COL_DV = 3200
COL_G = 3712
COL_END = 6784

FLASH_TILE = 1024
MLA_HEADS_PER_STEP = 4
DIL_MAX_WINDOW = max(w for w, _ in DIL_PATTERNS)
V_DIM = 64
MLA_V_ROWS = V_DIM + 16
DIL_V_ROWS = V_DIM + 64
MLSTM_V_ROWS = MLSTM_V + 64
LOG2E = 1.4426950408889634
assert MLA_V == V_DIM and DIL_HEAD_DIM == V_DIM


def _store_v_tiles_t(v, vt_ref, dim=V_DIM):
    tm = v.shape[0]
    rows = vt_ref.shape[2]
    v_t = v.astype(BF16).T
    fill = (lax.broadcasted_iota(jnp.int32, (rows - dim, tm), 0) == 0).astype(BF16)
    for h in range(v.shape[1] // dim):
        vt_ref[0, h, 0:dim, :] = v_t[h * dim:(h + 1) * dim, :]
        vt_ref[0, h, dim:rows, :] = fill


def _cparams(sem, vmem_mb):
    limit = vmem_mb * 1024 * 1024
    assert limit < VMEM_BYTES_V7X
    return pltpu.CompilerParams(dimension_semantics=sem, vmem_limit_bytes=limit)


def _sigmoid(x):
    return 1.0 / (1.0 + jnp.exp(-x))


def _const_spec(shape, index_map):
    return pl.BlockSpec(shape, index_map, pipeline_mode=pl.Buffered(1))


def _adaln_kernel(c_ref, w_ref, b_ref, o_ref):
    c = c_ref[...]
    ca = c * _sigmoid(c)
    o_ref[0] = jnp.dot(ca, w_ref[0], preferred_element_type=F32, precision=HIGHEST) + b_ref[0]


def _adaln(c, w_ada, b_ada):
    L, D, D6 = w_ada.shape
    B = c.shape[0]
    rows = 8
    cp = jnp.zeros((rows, D), F32).at[:B].set(c)
    out = pl.pallas_call(
        _adaln_kernel,
        grid=(L, D6 // D),
        in_specs=[pl.BlockSpec((rows, D), lambda l, j: (0, 0)),
                  pl.BlockSpec((1, D, D), lambda l, j: (l, 0, j)),
                  pl.BlockSpec((1, 1, D), lambda l, j: (l, 0, j))],
        out_specs=pl.BlockSpec((1, rows, D), lambda l, j: (l, 0, j)),
        out_shape=jax.ShapeDtypeStruct((L, rows, D6), F32),
        compiler_params=_cparams(("parallel", "parallel"), 32),
        name="adaln",
    )(cp, w_ada, b_ada.reshape(L, 1, D6))
    return out[:, :B].reshape(L, B, D6 // D, 1, D)


def _inproj_kernel(x_ref, gain_ref, sc_ref, sh_ref, w_ref, dqg_ref, dkg_ref,
                   oa_ref, mqk_ref, mv_ref, mo_ref, mif_ref, qd_ref, kd_ref, vd_ref, g_ref):
    x = x_ref[...]
    ms = jnp.mean(x * x, axis=-1, keepdims=True)
    y = x * lax.rsqrt(ms + EPS) * gain_ref[0]
    u = (y * (1.0 + sc_ref[0]) + sh_ref[0]).astype(BF16)

    def proj(a, b):
        return jnp.dot(u, w_ref[0, :, a:b], preferred_element_type=F32)

    oa_ref[...] = proj(COL_A, COL_MQK)
    mqk_ref[...] = proj(COL_MQK, COL_MV)
    _store_v_tiles_t(proj(COL_MV, COL_MO), mv_ref, MLSTM_V)
    mo_ref[...] = proj(COL_MO, COL_MIF).astype(BF16)
    mif_ref[...] = proj(COL_MIF, COL_DQ)

    half = LANES // 2
    low = lax.broadcasted_iota(jnp.int32, (1, LANES), 1) < half

    def head_norm(z, g):
        for h in range(DIL_HEADS):
            pair = z[:, (h // 2) * LANES:(h // 2 + 1) * LANES]
            zh = jnp.where(low, pair if h % 2 == 0 else pltpu.roll(pair, half, 1), 0.0)
            r = lax.rsqrt(jnp.sum(zh * zh, axis=-1, keepdims=True) / DIL_HEAD_DIM + EPS)
            yield h, zh * r * g

    dq = proj(COL_DQ, COL_DK)
    for h, zn in head_norm(dq, dqg_ref[0]):
        qd_ref[0, h // 2, :, (h % 2) * LANES:(h % 2 + 1) * LANES] = (zn * (DIL_HEAD_DIM ** -0.5 * LOG2E)).astype(BF16)
    dk = proj(COL_DK, COL_DV)
    for h, zn in head_norm(dk, dkg_ref[0]):
        kd_ref[0, h // 2, :, (h % 2) * LANES:(h % 2 + 1) * LANES] = zn.astype(BF16)
    _store_v_tiles_t(proj(COL_DV, COL_G), vd_ref)
    for c in range(N_BRANCH):
        g_ref[:, c * D_MODEL:(c + 1) * D_MODEL] = proj(COL_G + c * D_MODEL, COL_G + (c + 1) * D_MODEL).astype(BF16)


def _inproj(x2, S, layer, gain, sc, sh, w_pad, dqg, dkg, tm=512):
    N, D = x2.shape
    per_b = S // tm
    row = lambda i: (i, 0)
    bsel = lambda i: (i // per_b, 0, 0)
    B = N // S
    outs = [(512, F32), (512, F32), "mvt", (512, BF16), (LANES, F32),
            "qk", "qk", "vt", (3072, BF16)]
    head_major = lambda i: (i // per_b, 0, i % per_b, 0)
    time_minor = lambda i: (i // per_b, 0, 0, i % per_b)
    special_spec = {"qk": pl.BlockSpec((1, DIL_HEADS // 2, tm, 2 * LANES), head_major),
                    "vt": pl.BlockSpec((1, DIL_HEADS, DIL_V_ROWS, tm), time_minor),
                    "mvt": pl.BlockSpec((1, MLSTM_HEADS, MLSTM_V_ROWS, tm), time_minor)}
    special_shape = {"qk": jax.ShapeDtypeStruct((B, DIL_HEADS // 2, S, 2 * LANES), BF16),
                     "vt": jax.ShapeDtypeStruct((B, DIL_HEADS, DIL_V_ROWS, S), BF16),
                     "mvt": jax.ShapeDtypeStruct((B, MLSTM_HEADS, MLSTM_V_ROWS, S), BF16)}
    return pl.pallas_call(
        _inproj_kernel,
        grid=(N // tm,),
        in_specs=[pl.BlockSpec((tm, D), row),
                  _const_spec((1, 1, D), lambda i: (layer, 0, 0)),
                  pl.BlockSpec((1, 1, D), bsel),
                  pl.BlockSpec((1, 1, D), bsel),
                  _const_spec((1, D, COL_END), lambda i: (layer, 0, 0)),
                  _const_spec((1, 1, LANES), lambda i: (layer, 0, 0)),
                  _const_spec((1, 1, LANES), lambda i: (layer, 0, 0))],
        out_specs=[special_spec[o] if isinstance(o, str) else pl.BlockSpec((tm, o[0]), row) for o in outs],
        out_shape=[special_shape[o] if isinstance(o, str) else jax.ShapeDtypeStruct((N, o[0]), o[1]) for o in outs],
        compiler_params=_cparams(("parallel",), 52),
        name="inproj",
    )(x2, gain, sc, sh, w_pad, dqg, dkg)


def _mla_prep_kernel(a_ref, qn_ref, kvn_ref, wuq_ref, wkv_ref, qg_ref, qgs_ref, kg_ref, cos_ref, sin_ref,
                     q_ref, k_ref, vt_ref):
    HW = MLA_HEADS * LANES
    a = a_ref[...]
    cq = a[:, :MLA_Q_LORA]
    ckv = a[:, MLA_Q_LORA:MLA_Q_LORA + MLA_KV_LORA]
    kr_tile = a[:, MLA_Q_LORA + MLA_KV_LORA:]

    def rms(z, g):
        return z * lax.rsqrt(jnp.mean(z * z, axis=-1, keepdims=True) + EPS) * g

    cqn = rms(cq, qn_ref[0]).astype(BF16)
    ckvn = rms(ckv, kvn_ref[0]).astype(BF16)
    qr = jnp.dot(cqn, wuq_ref[0], preferred_element_type=F32)
    kvr = jnp.dot(ckvn, wkv_ref[0], preferred_element_type=F32)
    _store_v_tiles_t(kvr[:, HW:], vt_ref)

    cosf, sins = cos_ref[...], sin_ref[...]
    q_scale = MLA_QK ** -0.5 * LOG2E
    gc = cosf * (qg_ref[0] * q_scale)
    gs = sins * (qgs_ref[0] * q_scale)
    kg = kg_ref[0]
    half = MLA_ROPE // 2
    krg = pltpu.roll(kr_tile, MLA_NOPE, 1) * kg
    kr_rot = krg * cosf + (pltpu.roll(krg, LANES - half, 1) + pltpu.roll(krg, half, 1)) * sins
    kr_ss = jnp.sum(kr_tile * kr_tile, axis=-1, keepdims=True)
    for h in range(MLA_HEADS):
        sl = slice(h * LANES, (h + 1) * LANES)
        grp, pair_sl = h // MLA_HEADS_PER_STEP, slice((h % MLA_HEADS_PER_STEP) * LANES,
                                                     (h % MLA_HEADS_PER_STEP + 1) * LANES)
        z = qr[:, sl]
        rq = lax.rsqrt(jnp.sum(z * z, axis=-1, keepdims=True) / MLA_QK + EPS)
        q_ref[0, grp, :, pair_sl] = ((z * gc + qr[:, HW + h * LANES:HW + (h + 1) * LANES] * gs) * rq).astype(BF16)
        zk = kvr[:, sl]
        rk = lax.rsqrt((jnp.sum(zk * zk, axis=-1, keepdims=True) + kr_ss) / MLA_QK + EPS)
        k_ref[0, grp, :, pair_sl] = ((zk * kg + kr_rot) * rk).astype(BF16)


def _rope_tables(S):
    half = MLA_ROPE // 2
    inv = ROPE_THETA ** (-jnp.arange(half, dtype=F32) / half)
    ang = jnp.arange(S).astype(F32)[:, None] * inv[None, :]
    cos, sin = jnp.cos(ang), jnp.sin(ang)
    one = jnp.ones((S, MLA_NOPE), F32)
    z = lambda n: jnp.zeros((S, n), F32)
    tail = LANES - MLA_QK
    cosf = jnp.concatenate([one, cos, cos, z(tail)], axis=1)
    sins = jnp.concatenate([z(MLA_NOPE), -sin, sin, z(tail)], axis=1)
    return cosf, sins


def _mla_prep(oa, S, layer, qn, kvn, wuq, wkv, qg, qgs, kg, tables, tm=512):
    N = oa.shape[0]
    B = N // S
    per_b = S // tm
    row = lambda i: (i, 0)
    pos = lambda i: (i % per_b, 0)
    HW = MLA_HEADS * LANES
    G, GW = MLA_HEADS // MLA_HEADS_PER_STEP, MLA_HEADS_PER_STEP * LANES
    return pl.pallas_call(
        _mla_prep_kernel,
        grid=(N // tm,),
        in_specs=[pl.BlockSpec((tm, 512), row),
                  _const_spec((1, 1, MLA_Q_LORA), lambda i: (layer, 0, 0)),
                  _const_spec((1, 1, MLA_KV_LORA), lambda i: (layer, 0, 0)),
                  _const_spec((1, MLA_Q_LORA, 2 * HW), lambda i: (layer, 0, 0)),
                  _const_spec((1, MLA_KV_LORA, HW + MLA_HEADS * MLA_V), lambda i: (layer, 0, 0)),
                  _const_spec((1, 1, LANES), lambda i: (layer, 0, 0)),
                  _const_spec((1, 1, LANES), lambda i: (layer, 0, 0)),
                  _const_spec((1, 1, LANES), lambda i: (layer, 0, 0)),
                  pl.BlockSpec((tm, LANES), pos),
                  pl.BlockSpec((tm, LANES), pos)],
        out_specs=[pl.BlockSpec((1, G, tm, GW), lambda i: (i // per_b, 0, i % per_b, 0)),
                   pl.BlockSpec((1, G, tm, GW), lambda i: (i // per_b, 0, i % per_b, 0)),
                   pl.BlockSpec((1, MLA_HEADS, MLA_V_ROWS, tm), lambda i: (i // per_b, 0, 0, i % per_b))],
        out_shape=[jax.ShapeDtypeStruct((B, G, S, GW), BF16),
                   jax.ShapeDtypeStruct((B, G, S, GW), BF16),
                   jax.ShapeDtypeStruct((B, MLA_HEADS, MLA_V_ROWS, S), BF16)],
        compiler_params=_cparams(("parallel",), 32),
        name="mla_prep",
    )(oa, qn, kvn, wuq, wkv, qg, qgs, kg, *tables)


def _flash_kernel(qi_ref, kj_ref, qn_ref, kn_ref, q0_ref, k0_ref, vt_ref, o_ref, m_ref, acc_ref, s_ref, *, T, n_pairs):
    p_id = pl.program_id(2)
    qi = qi_ref[p_id]
    kj = kj_ref[p_id]
    first = kj == 0
    last = kj == qi

    n_heads = m_ref.shape[0]
    units = [(hh, slice(j * MXU_COLS, (j + 1) * MXU_COLS)) for hh in range(n_heads) for j in range(T // MXU_COLS)]

    def scores(q_ref, k_ref, hh, qs, rows=T):
        q = q_ref[0, 0, qs, hh * LANES:(hh + 1) * LANES]
        k = k_ref[0, 0, 0:rows, hh * LANES:(hh + 1) * LANES]
        return lax.dot_general(k, q, (((1,), (1,)), ((), ())), preferred_element_type=F32)

    @pl.when(p_id == 0)
    def _():
        for hh, qs in units:
            s_ref[0, hh, :, qs] = scores(q0_ref, k0_ref, hh, qs)

    @pl.when(first)
    def _():
        m_ref[...] = jnp.full(m_ref.shape, NEG_INF, F32)
        acc_ref[...] = jnp.zeros(acc_ref.shape, F32)

    nxt = jnp.minimum(p_id + 1, n_pairs - 1)

    def step(cur_diag, next_diag, slot):
        for hh, qs in units:
            rows_next = qs.stop if next_diag else T
            s_ref[1 - slot, hh, 0:rows_next, qs] = scores(qn_ref, kn_ref, hh, qs, rows_next)
            rows = qs.stop if cur_diag else T
            st = s_ref[slot, hh, 0:rows, qs]
            if cur_diag:
                key = lax.broadcasted_iota(jnp.int32, (rows, MXU_COLS), 0)
                qry = lax.broadcasted_iota(jnp.int32, (rows, MXU_COLS), 1) + qs.start
                st = jnp.where(key <= qry, st, NEG_INF)
            m_prev = m_ref[hh, :, qs]
            m_new = jnp.maximum(m_prev, jnp.max(st, axis=0, keepdims=True))
            alpha = jnp.exp2(m_prev - m_new)
            pt = jnp.exp2(st - m_new).astype(BF16)
            acc_ref[hh, :, qs] = (acc_ref[hh, :, qs] * alpha
                                  + jnp.dot(vt_ref[0, hh, :, 0:rows], pt, preferred_element_type=F32))
            m_ref[hh, :, qs] = m_new

    parity = p_id % 2
    cur_diag = last
    next_diag = jnp.logical_and(jnp.logical_not(cur_diag), qi_ref[nxt] == kj_ref[nxt])
    plain = jnp.logical_not(jnp.logical_or(cur_diag, next_diag))
    for slot in range(2):
        here = parity == slot
        pl.when(jnp.logical_and(here, cur_diag))(functools.partial(step, True, False, slot))
        pl.when(jnp.logical_and(here, next_diag))(functools.partial(step, False, True, slot))
        pl.when(jnp.logical_and(here, plain))(functools.partial(step, False, False, slot))

    @pl.when(last)
    def _():
        outs = []
        for hh in range(n_heads):
            a = acc_ref[hh]
            outs.append(a[0:V_DIM, :] / a[V_DIM:V_DIM + 1, :])
        o_ref[0] = jnp.concatenate(outs, axis=0).T.astype(o_ref.dtype)


def _band_bias(T, W):
    d = (jnp.arange(W + 1)[:, None, None] * T + jnp.arange(T)[None, None, :] - jnp.arange(T)[None, :, None])
    cw = jnp.zeros(d.shape, F32)
    for window, dil in DIL_PATTERNS:
        cw = cw + ((d >= 0) & (d <= window) & (d % dil == 0)).astype(F32)
    return jnp.log2(cw)


def _causal_attention(q, k, vt, T=FLASH_TILE):
    B, G, S, GW = q.shape
    g = GW // LANES
    v_rows = vt.shape[2]
    nq = S // T
    pairs = [(i, j) for i in range(nq) for j in range(i + 1)]
    n = len(pairs)
    qi = jnp.asarray(np.array([p[0] for p in pairs], np.int32))
    kj = jnp.asarray(np.array([p[1] for p in pairs], np.int32))
    nxt = lambda p: jnp.minimum(p + 1, n - 1)
    in_specs = [pl.BlockSpec((1, 1, T, GW), lambda b, h, p, qi, kj: (b, h, qi[nxt(p)], 0)),
                pl.BlockSpec((1, 1, T, GW), lambda b, h, p, qi, kj: (b, h, kj[nxt(p)], 0)),
                pl.BlockSpec((1, 1, T, GW), lambda b, h, p, qi, kj: (b, h, qi[0], 0)),
                pl.BlockSpec((1, 1, T, GW), lambda b, h, p, qi, kj: (b, h, kj[0], 0)),
                pl.BlockSpec((1, g, v_rows, T), lambda b, h, p, qi, kj: (b, h, 0, kj[p]))]
    return pl.pallas_call(
        functools.partial(_flash_kernel, T=T, n_pairs=n),
        grid_spec=pltpu.PrefetchScalarGridSpec(
            num_scalar_prefetch=2,
            grid=(B, G, n),
            in_specs=in_specs,
            out_specs=pl.BlockSpec((1, T, g * V_DIM), lambda b, h, p, qi, kj: (b, qi[p], h)),
            scratch_shapes=[pltpu.VMEM((g, 1, T), F32), pltpu.VMEM((g, v_rows, T), F32),
                            pltpu.VMEM((2, g, T, T), F32)]),
        out_shape=jax.ShapeDtypeStruct((B, S, G * g * V_DIM), BF16),
        compiler_params=_cparams(("arbitrary", "arbitrary", "arbitrary"), 56),
        name="flash_causal",
    )(qi, kj, q, k, q, k, vt)


def _band_kernel(qc_ref, qn_ref, kn_ref, *rest, T, W):
    k_refs, vt_refs = rest[:W + 1], rest[W + 1:2 * W + 2]
    bias_ref, o_ref, m_ref, acc_ref, s_ref = rest[2 * W + 2:]
    i = pl.program_id(2)
    units = [(hh, j) for hh in range(2) for j in range(T // MXU_COLS)]

    def rows_of(d, j):
        if d == 0:
            return 0, (j + 1) * MXU_COLS
        if d == W:
            return j * MXU_COLS, T
        return 0, T

    def produce(q_ref, k_ref, d, hh, j, bias_idx):
        lo, hi = rows_of(d, j)
        qs = slice(j * MXU_COLS, (j + 1) * MXU_COLS)
        q = q_ref[0, 0, qs, hh * LANES:(hh + 1) * LANES]
        k = k_ref[0, 0, lo:hi, hh * LANES:(hh + 1) * LANES]
        s = lax.dot_general(k, q, (((1,), (1,)), ((), ())), preferred_element_type=F32)
        s_ref[d, hh, lo:hi, qs] = s + bias_ref[bias_idx, lo:hi, qs]

    def consume(d, hh, j):
        lo, hi = rows_of(d, j)
        qs = slice(j * MXU_COLS, (j + 1) * MXU_COLS)
        st = s_ref[d, hh, lo:hi, qs]
        m_prev = m_ref[hh, :, qs]
        m_new = jnp.maximum(m_prev, jnp.max(st, axis=0, keepdims=True))
        alpha = jnp.exp2(m_prev - m_new)
        pt = jnp.exp2(st - m_new).astype(BF16)
        acc_ref[hh, :, qs] = (acc_ref[hh, :, qs] * alpha
                              + jnp.dot(vt_refs[d][0, hh, :, lo:hi], pt, preferred_element_type=F32))
        m_ref[hh, :, qs] = m_new

    @pl.when(i == 0)
    def _():
        for hh, j in units:
            produce(qc_ref, k_refs[0], 0, hh, j, 0)

    m_ref[...] = jnp.full(m_ref.shape, NEG_INF, F32)
    acc_ref[...] = jnp.zeros(acc_ref.shape, F32)
    for d in range(W + 1):
        for hh, j in units:
            if d < W:
                produce(qc_ref, k_refs[d + 1], d + 1, hh, j, jnp.where(i >= d + 1, d + 1, W + 1))
            else:
                produce(qn_ref, kn_ref, 0, hh, j, 0)
            consume(d, hh, j)

    outs = []
    for hh in range(2):
        a = acc_ref[hh]
        outs.append(a[0:V_DIM, :] / a[V_DIM:V_DIM + 1, :])
    o_ref[0] = jnp.concatenate(outs, axis=0).T.astype(o_ref.dtype)


def _band_attention(q, k, vt, T=FLASH_TILE):
    B, HP, S, _ = q.shape
    v_rows = vt.shape[2]
    n_q = S // T
    W = DIL_MAX_WINDOW // T
    bias = jnp.concatenate([_band_bias(T, W), jnp.full((1, T, T), NEG_INF, F32)], axis=0)
    qk_block = lambda f: pl.BlockSpec((1, 1, T, 2 * LANES), lambda b, h, i: (b, h, f(i), 0))
    vt_block = lambda f: pl.BlockSpec((1, 2, v_rows, T), lambda b, h, i: (b, h, 0, f(i)))
    back = lambda d: (lambda i: jnp.maximum(i - d, 0))
    nxt = lambda i: jnp.minimum(i + 1, n_q - 1)
    in_specs = ([qk_block(lambda i: i), qk_block(nxt), qk_block(nxt)]
                + [qk_block(back(d)) for d in range(W + 1)]
                + [vt_block(back(d)) for d in range(W + 1)]
                + [_const_spec((W + 2, T, T), lambda b, h, i: (0, 0, 0))])
    return pl.pallas_call(
        functools.partial(_band_kernel, T=T, W=W),
        grid=(B, HP, n_q),
        in_specs=in_specs,
        out_specs=pl.BlockSpec((1, T, LANES), lambda b, h, i: (b, i, h)),
        out_shape=jax.ShapeDtypeStruct((B, S, 2 * HP * V_DIM), BF16),
        scratch_shapes=[pltpu.VMEM((2, 1, T), F32), pltpu.VMEM((2, v_rows, T), F32),
                        pltpu.VMEM((W + 1, 2, T, T), F32)],
        compiler_params=_cparams(("arbitrary", "arbitrary", "arbitrary"), 58),
        name="flash_band",
    )(q, q, k, *([k] * (W + 1)), *([vt] * (W + 1)), bias)


def _mlstm_kernel(qk_ref, vt_ref, o_ref, if_ref, cw_ref, cb_ref, gb_ref, hg_ref, y_ref,
                  xs_ref, cn_ref, m_ref, *, Lc):
    H, DK, DV = MLSTM_HEADS, MLSTM_QK, MLSTM_V
    HALO = 8

    @pl.when(pl.program_id(1) == 0)
    def _():
        xs_ref[0:HALO, :] = jnp.zeros((HALO, 2 * H * DK), F32)
        cn_ref[...] = jnp.zeros(cn_ref.shape, F32)
        m_ref[...] = jnp.zeros(m_ref.shape, F32)

    x = qk_ref[...]
    xs_ref[HALO:HALO + Lc, :] = x
    y = cb_ref[0]
    for j in range(CONV_K):
        y = y + cw_ref[0, j:j + 1, :] * xs_ref[pl.ds(HALO - (CONV_K - 1) + j, Lc), :]
    xs_ref[0:HALO, :] = x[Lc - HALO:, :]
    qk = y * _sigmoid(y)
    q_t = qk[:, :H * DK].T.astype(BF16)
    k_all = (qk[:, H * DK:] * (DK ** -0.5)).astype(BF16)

    g = if_ref[...] + gb_ref[0]
    logf = jnp.minimum(g, 0.0) - jnp.log(1.0 + jnp.exp(-jnp.abs(g)))
    src = lax.broadcasted_iota(jnp.int32, (Lc, Lc), 0)
    dst = lax.broadcasted_iota(jnp.int32, (Lc, Lc), 1)
    tri = jnp.where(dst <= src, 1.0, 0.0).astype(F32)
    bcum = jnp.dot(tri, logf, preferred_element_type=F32, precision=HIGHEST)
    causal = src <= dst
    g_t = g.T
    bcum_t = bcum.T

    for h in range(H):
        b_row = bcum_t[H + h:H + h + 1, :]
        w_col = g[:, h:h + 1] - bcum[:, H + h:H + h + 1]
        w_row = g_t[h:h + 1, :] - b_row
        m_prev = m_ref[h:h + 1, 0:1]
        dlog = jnp.where(causal, w_col + b_row, NEG_INF)
        inter = b_row + m_prev
        m_t = jnp.maximum(inter, jnp.max(dlog, axis=0, keepdims=True))
        qh_t = q_t[h * DK:(h + 1) * DK, :]
        kh = k_all[:, h * DK:(h + 1) * DK]
        s_t = jnp.dot(kh, qh_t, preferred_element_type=F32)
        sc_t = (s_t * jnp.exp(dlog - m_t)).astype(BF16)
        decay = jnp.exp(inter - m_t)
        cn = cn_ref[h]
        vt = vt_ref[0, h]
        num = (decay * jnp.dot(cn.astype(BF16), qh_t, preferred_element_type=F32)
               + jnp.dot(vt, sc_t, preferred_element_type=F32))
        den = num[DV:DV + 1, :]
        hh = num[0:DV, :] / jnp.maximum(jnp.abs(den), jnp.exp(-m_t))
        hn = hh * lax.rsqrt(jnp.mean(hh * hh, axis=0, keepdims=True) + EPS)
        o_gate = _sigmoid(o_ref[:, h * DV:(h + 1) * DV].astype(F32))
        y_ref[:, h * DV:(h + 1) * DV] = (hn.T * hg_ref[0, :, h * DV:(h + 1) * DV] * o_gate).astype(y_ref.dtype)

        b_end = b_row[:, Lc - 1:Lc]
        wlog = b_end + w_row
        m_new = jnp.maximum(b_end + m_prev, jnp.max(wlog, axis=1, keepdims=True))
        cd = jnp.exp(b_end + m_prev - m_new)
        vw = (vt.astype(F32) * jnp.exp(wlog - m_new)).astype(BF16)
        cn_ref[h] = cd * cn + jnp.dot(vw, kh, preferred_element_type=F32)
        m_ref[h:h + 1, :] = jnp.broadcast_to(m_new, (1, LANES))


def _mlstm(mqk, mvt, mo, mif, B, S, layer, conv_w, conv_b, gate_b, head_gain, Lc=256):
    H, DK, DV = MLSTM_HEADS, MLSTM_QK, MLSTM_V
    nc = S // Lc
    row = lambda b, c: (b * nc + c, 0)
    VR = mvt.shape[2]
    return pl.pallas_call(
        functools.partial(_mlstm_kernel, Lc=Lc),
        grid=(B, nc),
        in_specs=[pl.BlockSpec((Lc, 2 * H * DK), row),
                  pl.BlockSpec((1, H, VR, Lc), lambda b, c: (b, 0, 0, c)),
                  pl.BlockSpec((Lc, H * DV), row),
                  pl.BlockSpec((Lc, LANES), row),
                  _const_spec((1, CONV_K, 2 * H * DK), lambda b, c: (layer, 0, 0)),
                  _const_spec((1, 1, 2 * H * DK), lambda b, c: (layer, 0, 0)),
                  _const_spec((1, 1, LANES), lambda b, c: (layer, 0, 0)),
                  _const_spec((1, 1, H * DV), lambda b, c: (layer, 0, 0))],
        out_specs=pl.BlockSpec((Lc, H * DV), row),
        out_shape=jax.ShapeDtypeStruct((B * S, H * DV), BF16),
        scratch_shapes=[pltpu.VMEM((Lc + 8, 2 * H * DK), F32), pltpu.VMEM((H, VR, DK), F32),
                        pltpu.VMEM((8, LANES), F32)],
        compiler_params=_cparams(("arbitrary", "arbitrary"), 32),
        name="mlstm",
    )(mqk, mvt, mo, mif, conv_w, conv_b, gate_b, head_gain)


def _merge_kernel(ya_ref, yb_ref, yc_ref, g_ref, x_ref, ga_ref, wa_ref, wb_ref, wc_ref, wo_ref, out_ref):
    def branch(y_ref, w_ref, c):
        gate = _sigmoid(g_ref[:, c * D_MODEL:(c + 1) * D_MODEL].astype(F32))
        return gate * jnp.dot(y_ref[...], w_ref[0], preferred_element_type=F32)

    merged = branch(ya_ref, wa_ref, 0) + branch(yb_ref, wb_ref, 1) + branch(yc_ref, wc_ref, 2)
    y = jnp.dot(merged.astype(BF16), wo_ref[0], preferred_element_type=F32)
    out_ref[...] = x_ref[...] + ga_ref[0] * y


def _merge(ya, yb, yc, gates, x2, S, layer, ga, wa, wb, wc, wo, tm=512):
    N, D = x2.shape
    per_b = S // tm
    row = lambda i: (i, 0)
    wsel = lambda i: (layer, 0, 0)
    return pl.pallas_call(
        _merge_kernel,
        grid=(N // tm,),
        in_specs=[pl.BlockSpec((tm, 512), row), pl.BlockSpec((tm, 512), row), pl.BlockSpec((tm, 512), row),
                  pl.BlockSpec((tm, N_BRANCH * D), row), pl.BlockSpec((tm, D), row),
                  pl.BlockSpec((1, 1, D), lambda i: (i // per_b, 0, 0)),
                  _const_spec((1, 512, D), wsel), _const_spec((1, 512, D), wsel), _const_spec((1, 512, D), wsel),
                  _const_spec((1, D, D), wsel)],
        out_specs=pl.BlockSpec((tm, D), row),
        out_shape=jax.ShapeDtypeStruct((N, D), F32),
        compiler_params=_cparams(("parallel",), 48),
        name="merge",
    )(ya, yb, yc, gates, x2, ga, wa, wb, wc, wo)


def _ffn_kernel(x_ref, gain_ref, sc_ref, sh_ref, gf_ref, wr_ref, br_ref, wg_ref, wu_ref, wd_ref, out_ref):
    x = x_ref[...]
    tm = x.shape[0]
    ms = jnp.mean(x * x, axis=-1, keepdims=True)
    hf = x * lax.rsqrt(ms + EPS) * gain_ref[0] * (1.0 + sc_ref[0]) + sh_ref[0]

    hf_hi = hf.astype(BF16)
    hf_lo = (hf - hf_hi.astype(F32)).astype(BF16)
    wr = wr_ref[0]
    wr_hi = wr.astype(BF16)
    wr_split = jnp.concatenate([wr_hi, (wr - wr_hi.astype(F32)).astype(BF16)], axis=1)
    parts = (jnp.dot(hf_hi, wr_split, preferred_element_type=F32)
             + jnp.dot(hf_lo, wr_split, preferred_element_type=F32))
    logits = parts[:, :LANES] + parts[:, LANES:] + br_ref[0]
    lane_i = lax.broadcasted_iota(jnp.int32, (tm, LANES), 1)
    lane = lane_i.astype(F32)
    big = float(LANES)

    def first_argmax(z):
        zmax = jnp.max(z, axis=1, keepdims=True)
        return zmax, jnp.min(jnp.where(z == zmax, lane, big), axis=1, keepdims=True)

    lg = jnp.where(lane_i < N_GROUPS, logits, NEG_INF)
    gmax, g_idx = first_argmax(lg)
    p_sel = 1.0 / jnp.sum(jnp.exp(lg - gmax), axis=1, keepdims=True)
    lane_grp = ((lane_i - N_GROUPS) // EXPERTS_PER_GROUP).astype(F32)
    in_grp = (lane_i >= N_GROUPS) & (lane_i < N_GROUPS + N_EXPERTS) & (lane_grp == g_idx)
    le = jnp.where(in_grp, logits, NEG_INF)
    v1, i1 = first_argmax(le)
    le2 = jnp.where(lane == i1, NEG_INF, le)
    v2, i2 = first_argmax(le2)
    e2 = jnp.exp(v2 - v1)
    den = 1.0 + e2
    comb = jnp.where(lane == i1, (1.0 / den) * p_sel, 0.0) + jnp.where(lane == i2, (e2 / den) * p_sel, 0.0)

    hb = hf_hi
    acc = jnp.zeros((tm, D_MODEL), F32)
    for e in range(N_EXPERTS):
        gate = jnp.dot(hb, wg_ref[0, e], preferred_element_type=F32)
        up = jnp.dot(hb, wu_ref[0, e], preferred_element_type=F32)
        hid = gate * _sigmoid(gate) * up * comb[:, N_GROUPS + e:N_GROUPS + e + 1]
        acc = acc + jnp.dot(hid.astype(BF16), wd_ref[0, e], preferred_element_type=F32)
    out_ref[...] = x + gf_ref[0] * acc


def _ffn(x2, S, layer, gain, sc, sh, gf, wr, br, wg, wu, wd, tm=512):
    N, D = x2.shape
    per_b = S // tm
    row = lambda i: (i, 0)
    bsel = lambda i: (i // per_b, 0, 0)
    wsel = lambda i: (layer, 0, 0, 0)
    return pl.pallas_call(
        _ffn_kernel,
        grid=(N // tm,),
        in_specs=[pl.BlockSpec((tm, D), row),
                  _const_spec((1, 1, D), lambda i: (layer, 0, 0)),
                  pl.BlockSpec((1, 1, D), bsel), pl.BlockSpec((1, 1, D), bsel), pl.BlockSpec((1, 1, D), bsel),
                  _const_spec((1, D, LANES), lambda i: (layer, 0, 0)),
                  _const_spec((1, 1, LANES), lambda i: (layer, 0, 0)),
                  _const_spec((1, N_EXPERTS, D, EXPERT_FF), wsel),
                  _const_spec((1, N_EXPERTS, D, EXPERT_FF), wsel),
                  _const_spec((1, N_EXPERTS, EXPERT_FF, D), wsel)],
        out_specs=pl.BlockSpec((tm, D), row),
        out_shape=jax.ShapeDtypeStruct((N, D), F32),
        compiler_params=_cparams(("parallel",), 56),
        name="ffn",
    )(x2, gain, sc, sh, gf, wr, br, wg, wu, wd)


def _pad_last(a, n):
    return jnp.pad(a, [(0, 0)] * (a.ndim - 1) + [(0, n - a.shape[-1])])


def _prep_w_in(w_in):
    pts = np.cumsum(IN_SIZES)[:-1].tolist()
    cq, ckv, kr, m_qk, m_v, m_o, m_i, m_f, d_q, d_k, d_v, gates = jnp.split(w_in, pts, axis=-1)
    seg_a = _pad_last(jnp.concatenate([cq, ckv, kr], axis=-1), 512)
    seg_if = _pad_last(jnp.concatenate([m_i, m_f], axis=-1), LANES)
    w = jnp.concatenate([seg_a, m_qk, m_v, m_o, seg_if,
                         d_q, d_k,
                         d_v, gates], axis=-1)
    assert w.shape[-1] == COL_END
    return w.astype(BF16)


def kernel(x, c, w_ada, b_ada, attn_norm, w_in, mla_q_norm, mla_kv_norm, mla_w_uq, mla_w_ukv, mla_q_gain, mla_k_gain, mlstm_conv_w, mlstm_conv_b, mlstm_b_i, mlstm_b_f, mlstm_head_gain, dil_q_gain, dil_k_gain, w_branch_a, w_branch_b, w_branch_c, w_out, ffn_norm, w_router_group, b_router_group, w_router_expert, b_router_expert, w_exp_gate, w_exp_up, w_exp_down):
    B, S, D = x.shape
    L = w_ada.shape[0]
    N = B * S
    assert D == D_MODEL and S % DIL_MAX_WINDOW == 0

    mod = _adaln(c, w_ada, b_ada)

    w_pad = _prep_w_in(w_in)
    half = MLA_ROPE // 2
    swap_rope = lambda a: jnp.concatenate(
        [jnp.zeros_like(a[..., :MLA_NOPE]), a[..., MLA_NOPE + half:], a[..., MLA_NOPE:MLA_NOPE + half]], axis=-1)
    uq = mla_w_uq.reshape(L, MLA_Q_LORA, MLA_HEADS, MLA_QK)
    wuq = jnp.concatenate([_pad_last(u, LANES).reshape(L, MLA_Q_LORA, MLA_HEADS * LANES)
                           for u in (uq, swap_rope(uq))], axis=-1).astype(BF16)
    ukv = mla_w_ukv.reshape(L, MLA_KV_LORA, MLA_HEADS, MLA_NOPE + MLA_V)
    wkv = jnp.concatenate([_pad_last(ukv[..., :MLA_NOPE], LANES).reshape(L, MLA_KV_LORA, MLA_HEADS * LANES),
                           ukv[..., MLA_NOPE:].reshape(L, MLA_KV_LORA, MLA_HEADS * MLA_V)], axis=-1).astype(BF16)
    rows = lambda a: a[:, None, :]
    qg = rows(_pad_last(mla_q_gain, LANES))
    qgs = rows(_pad_last(swap_rope(mla_q_gain), LANES))
    kg = rows(_pad_last(mla_k_gain, LANES))
    dqg = rows(_pad_last(dil_q_gain, LANES))
    dkg = rows(_pad_last(dil_k_gain, LANES))
    gate_b = rows(_pad_last(jnp.concatenate([mlstm_b_i, mlstm_b_f], axis=-1), LANES))
    tables = _rope_tables(S)
    wa, wb, wc, wo = (w.astype(BF16) for w in (w_branch_a, w_branch_b, w_branch_c, w_out))
    wr = _pad_last(jnp.concatenate([w_router_group, w_router_expert], axis=-1), LANES)
    br = rows(_pad_last(jnp.concatenate([b_router_group, b_router_expert], axis=-1), LANES))
    wg, wu, wd = (w.astype(BF16) for w in (w_exp_gate, w_exp_up, w_exp_down))
    attn_norm, ffn_norm, mla_q_norm, mla_kv_norm, mlstm_conv_b, mlstm_head_gain = (
        rows(a) for a in (attn_norm, ffn_norm, mla_q_norm, mla_kv_norm, mlstm_conv_b, mlstm_head_gain))

    x2 = x.reshape(N, D)
    for l in range(L):
        sh_a, sc_a, g_a, sh_f, sc_f, g_f = (mod[l, :, i] for i in range(6))
        oa, mqk, mv, mo, mif, qd, kd, vd, gates = _inproj(x2, S, l, attn_norm, sc_a, sh_a, w_pad, dqg, dkg)
        q, k, vt = _mla_prep(oa, S, l, mla_q_norm, mla_kv_norm, wuq, wkv, qg, qgs, kg, tables)
        ya = _causal_attention(q, k, vt)
        yb = _mlstm(mqk, mv, mo, mif, B, S, l, mlstm_conv_w, mlstm_conv_b, gate_b, mlstm_head_gain)
        yc = _band_attention(qd, kd, vd)
        x2 = _merge(ya.reshape(N, -1), yb, yc.reshape(N, -1), gates, x2, S, l, g_a, wa, wb, wc, wo)
        x2 = _ffn(x2, S, l, ffn_norm, sc_f, sh_f, g_f, wr, br, wg, wu, wd)
    return x2.reshape(B, S, D)
```

```python
import functools

import jax
import jax.numpy as jnp
import numpy as np
from jax import lax
from jax.experimental import pallas as pl
from jax.experimental.pallas import tpu as pltpu

F32 = jnp.float32
BF16 = jnp.bfloat16
HIGHEST = lax.Precision.HIGHEST
NEG_INF = float("-inf")

EPS = 1e-6
LANES = 128
MXU_COLS = 256
VMEM_BYTES_V7X = 64 * 1024 * 1024

D_MODEL = 1024
MLA_HEADS = 8
MLA_Q_LORA = 256
MLA_KV_LORA = 128
MLA_NOPE = 64
MLA_ROPE = 32
MLA_V = 64
MLA_QK = MLA_NOPE + MLA_ROPE
ROPE_THETA = 10000.0
MLSTM_HEADS = 4
MLSTM_QK = 64
MLSTM_V = 128
CONV_K = 4
DIL_HEADS = 8
DIL_HEAD_DIM = 64
DIL_PATTERNS = ((128, 1), (512, 4), (2048, 16))
DIL_WIDTH = DIL_HEADS * DIL_HEAD_DIM
N_GROUPS = 4
EXPERTS_PER_GROUP = 4
N_EXPERTS = N_GROUPS * EXPERTS_PER_GROUP
EXPERT_FF = 256
N_BRANCH = 3
IN_SIZES = (MLA_Q_LORA, MLA_KV_LORA, MLA_ROPE,
            2 * MLSTM_HEADS * MLSTM_QK, MLSTM_HEADS * MLSTM_V, MLSTM_HEADS * MLSTM_V, MLSTM_HEADS, MLSTM_HEADS,
            DIL_WIDTH, DIL_WIDTH, DIL_WIDTH, N_BRANCH * D_MODEL)

COL_A = 0
COL_MQK = 512
COL_MV = 1024
COL_MO = 1536
COL_MIF = 2048
COL_DQ = 2176
COL_DK = 2688
COL_DV = 3200
COL_G = 3712
COL_END = 6784

FLASH_TILE = 1024
MLA_HEADS_PER_STEP = 4
DIL_MAX_WINDOW = max(w for w, _ in DIL_PATTERNS)
V_DIM = 64
MLA_V_ROWS = V_DIM + 48
DIL_V_ROWS = V_DIM + 48
MLSTM_V_ROWS = MLSTM_V + 64
LOG2E = 1.4426950408889634
assert MLA_V == V_DIM and DIL_HEAD_DIM == V_DIM


def _store_v_tiles_t(v, vt_ref, dim=V_DIM):
    tm = v.shape[0]
    rows = vt_ref.shape[2]
    v_t = v.astype(BF16).T
    fill = (lax.broadcasted_iota(jnp.int32, (rows - dim, tm), 0) == 0).astype(BF16)
    for h in range(v.shape[1] // dim):
        vt_ref[0, h, 0:dim, :] = v_t[h * dim:(h + 1) * dim, :]
        vt_ref[0, h, dim:rows, :] = fill


def _cparams(sem, vmem_mb):
    limit = vmem_mb * 1024 * 1024
    assert limit < VMEM_BYTES_V7X
    return pltpu.CompilerParams(dimension_semantics=sem, vmem_limit_bytes=limit)


def _sigmoid(x):
    return 1.0 / (1.0 + jnp.exp(-x))


def _const_spec(shape, index_map):
    return pl.BlockSpec(shape, index_map, pipeline_mode=pl.Buffered(1))


def _adaln_kernel(c_ref, w_ref, b_ref, o_ref):
    c = c_ref[...]
    ca = c * _sigmoid(c)
    o_ref[0] = jnp.dot(ca, w_ref[0], preferred_element_type=F32, precision=HIGHEST) + b_ref[0]


def _adaln(c, w_ada, b_ada):
    L, D, D6 = w_ada.shape
    B = c.shape[0]
    rows = 8
    cp = jnp.zeros((rows, D), F32).at[:B].set(c)
    out = pl.pallas_call(
        _adaln_kernel,
        grid=(L, D6 // D),
        in_specs=[pl.BlockSpec((rows, D), lambda l, j: (0, 0)),
                  pl.BlockSpec((1, D, D), lambda l, j: (l, 0, j)),
                  pl.BlockSpec((1, 1, D), lambda l, j: (l, 0, j))],
        out_specs=pl.BlockSpec((1, rows, D), lambda l, j: (l, 0, j)),
        out_shape=jax.ShapeDtypeStruct((L, rows, D6), F32),
        compiler_params=_cparams(("parallel", "parallel"), 32),
        name="adaln",
    )(cp, w_ada, b_ada.reshape(L, 1, D6))
    return out[:, :B].reshape(L, B, D6 // D, 1, D)


def _inproj_kernel(x_ref, gain_ref, sc_ref, sh_ref, w_ref, dqg_ref, dkg_ref,
                   oa_ref, mqk_ref, mv_ref, mo_ref, mif_ref, qd_ref, kd_ref, vd_ref, g_ref):
    x = x_ref[...]
    ms = jnp.mean(x * x, axis=-1, keepdims=True)
    y = x * lax.rsqrt(ms + EPS) * gain_ref[0]
    u = (y * (1.0 + sc_ref[0]) + sh_ref[0]).astype(BF16)

    def proj(a, b):
        return jnp.dot(u, w_ref[0, :, a:b], preferred_element_type=F32)

    oa_ref[...] = proj(COL_A, COL_MQK)
    mqk_ref[...] = proj(COL_MQK, COL_MV)
    _store_v_tiles_t(proj(COL_MV, COL_MO), mv_ref, MLSTM_V)
    mo_ref[...] = proj(COL_MO, COL_MIF).astype(BF16)
    mif_ref[...] = proj(COL_MIF, COL_DQ)

    half = LANES // 2
    low = lax.broadcasted_iota(jnp.int32, (1, LANES), 1) < half

    def head_norm(z, g):
        for h in range(DIL_HEADS):
            pair = z[:, (h // 2) * LANES:(h // 2 + 1) * LANES]
            zh = jnp.where(low, pair if h % 2 == 0 else pltpu.roll(pair, half, 1), 0.0)
            r = lax.rsqrt(jnp.sum(zh * zh, axis=-1, keepdims=True) / DIL_HEAD_DIM + EPS)
            yield h, zh * r * g

    dq = proj(COL_DQ, COL_DK)
    for h, zn in head_norm(dq, dqg_ref[0]):
        qd_ref[0, h // 2, :, (h % 2) * LANES:(h % 2 + 1) * LANES] = (zn * (DIL_HEAD_DIM ** -0.5 * LOG2E)).astype(BF16)
    dk = proj(COL_DK, COL_DV)
    for h, zn in head_norm(dk, dkg_ref[0]):
        kd_ref[0, h // 2, :, (h % 2) * LANES:(h % 2 + 1) * LANES] = zn.astype(BF16)
    _store_v_tiles_t(proj(COL_DV, COL_G), vd_ref)
    for c in range(N_BRANCH):
        g_ref[:, c * D_MODEL:(c + 1) * D_MODEL] = proj(COL_G + c * D_MODEL, COL_G + (c + 1) * D_MODEL).astype(BF16)


def _inproj(x2, S, layer, gain, sc, sh, w_pad, dqg, dkg, tm=512):
    N, D = x2.shape
    per_b = S // tm
    row = lambda i: (i, 0)
    bsel = lambda i: (i // per_b, 0, 0)
    B = N // S
    outs = [(512, F32), (512, F32), "mvt", (512, BF16), (LANES, F32),
            "qk", "qk", "vt", (3072, BF16)]
    head_major = lambda i: (i // per_b, 0, i % per_b, 0)
    time_minor = lambda i: (i // per_b, 0, 0, i % per_b)
    special_spec = {"qk": pl.BlockSpec((1, DIL_HEADS // 2, tm, 2 * LANES), head_major),
                    "vt": pl.BlockSpec((1, DIL_HEADS, DIL_V_ROWS, tm), time_minor),
                    "mvt": pl.BlockSpec((1, MLSTM_HEADS, MLSTM_V_ROWS, tm), time_minor)}
    special_shape = {"qk": jax.ShapeDtypeStruct((B, DIL_HEADS // 2, S, 2 * LANES), BF16),
                     "vt": jax.ShapeDtypeStruct((B, DIL_HEADS, DIL_V_ROWS, S), BF16),
                     "mvt": jax.ShapeDtypeStruct((B, MLSTM_HEADS, MLSTM_V_ROWS, S), BF16)}
    return pl.pallas_call(
        _inproj_kernel,
        grid=(N // tm,),
        in_specs=[pl.BlockSpec((tm, D), row),
                  _const_spec((1, 1, D), lambda i: (layer, 0, 0)),
                  pl.BlockSpec((1, 1, D), bsel),
                  pl.BlockSpec((1, 1, D), bsel),
                  _const_spec((1, D, COL_END), lambda i: (layer, 0, 0)),
                  _const_spec((1, 1, LANES), lambda i: (layer, 0, 0)),
                  _const_spec((1, 1, LANES), lambda i: (layer, 0, 0))],
        out_specs=[special_spec[o] if isinstance(o, str) else pl.BlockSpec((tm, o[0]), row) for o in outs],
        out_shape=[special_shape[o] if isinstance(o, str) else jax.ShapeDtypeStruct((N, o[0]), o[1]) for o in outs],
        compiler_params=_cparams(("parallel",), 52),
        name="inproj",
    )(x2, gain, sc, sh, w_pad, dqg, dkg)


def _mla_prep_kernel(a_ref, qn_ref, kvn_ref, wuq_ref, wkv_ref, qg_ref, qgs_ref, kg_ref, cos_ref, sin_ref,
                     q_ref, k_ref, vt_ref):
    HW = MLA_HEADS * LANES
    a = a_ref[...]
    cq = a[:, :MLA_Q_LORA]
    ckv = a[:, MLA_Q_LORA:MLA_Q_LORA + MLA_KV_LORA]
    kr_tile = a[:, MLA_Q_LORA + MLA_KV_LORA:]

    def rms(z, g):
        return z * lax.rsqrt(jnp.mean(z * z, axis=-1, keepdims=True) + EPS) * g

    cqn = rms(cq, qn_ref[0]).astype(BF16)
    ckvn = rms(ckv, kvn_ref[0]).astype(BF16)
    qr = jnp.dot(cqn, wuq_ref[0], preferred_element_type=F32)
    kvr = jnp.dot(ckvn, wkv_ref[0], preferred_element_type=F32)
    _store_v_tiles_t(kvr[:, HW:], vt_ref)

    cosf, sins = cos_ref[...], sin_ref[...]
    q_scale = MLA_QK ** -0.5 * LOG2E
    gc = cosf * (qg_ref[0] * q_scale)
    gs = sins * (qgs_ref[0] * q_scale)
    kg = kg_ref[0]
    half = MLA_ROPE // 2
    krg = pltpu.roll(kr_tile, MLA_NOPE, 1) * kg
    kr_rot = krg * cosf + (pltpu.roll(krg, LANES - half, 1) + pltpu.roll(krg, half, 1)) * sins
    kr_ss = jnp.sum(kr_tile * kr_tile, axis=-1, keepdims=True)
    for h in range(MLA_HEADS):
        sl = slice(h * LANES, (h + 1) * LANES)
        grp, pair_sl = h // MLA_HEADS_PER_STEP, slice((h % MLA_HEADS_PER_STEP) * LANES,
                                                     (h % MLA_HEADS_PER_STEP + 1) * LANES)
        z = qr[:, sl]
        rq = lax.rsqrt(jnp.sum(z * z, axis=-1, keepdims=True) / MLA_QK + EPS)
        q_ref[0, grp, :, pair_sl] = ((z * gc + qr[:, HW + h * LANES:HW + (h + 1) * LANES] * gs) * rq).astype(BF16)
        zk = kvr[:, sl]
        rk = lax.rsqrt((jnp.sum(zk * zk, axis=-1, keepdims=True) + kr_ss) / MLA_QK + EPS)
        k_ref[0, grp, :, pair_sl] = ((zk * kg + kr_rot) * rk).astype(BF16)


def _rope_tables(S):
    half = MLA_ROPE // 2
    inv = ROPE_THETA ** (-jnp.arange(half, dtype=F32) / half)
    ang = jnp.arange(S).astype(F32)[:, None] * inv[None, :]
    cos, sin = jnp.cos(ang), jnp.sin(ang)
    one = jnp.ones((S, MLA_NOPE), F32)
    z = lambda n: jnp.zeros((S, n), F32)
    tail = LANES - MLA_QK
    cosf = jnp.concatenate([one, cos, cos, z(tail)], axis=1)
    sins = jnp.concatenate([z(MLA_NOPE), -sin, sin, z(tail)], axis=1)
    return cosf, sins


def _mla_prep(oa, S, layer, qn, kvn, wuq, wkv, qg, qgs, kg, tables, tm=512):
    N = oa.shape[0]
    B = N // S
    per_b = S // tm
    row = lambda i: (i, 0)
    pos = lambda i: (i % per_b, 0)
    HW = MLA_HEADS * LANES
    G, GW = MLA_HEADS // MLA_HEADS_PER_STEP, MLA_HEADS_PER_STEP * LANES
    return pl.pallas_call(
        _mla_prep_kernel,
        grid=(N // tm,),
        in_specs=[pl.BlockSpec((tm, 512), row),
                  _const_spec((1, 1, MLA_Q_LORA), lambda i: (layer, 0, 0)),
                  _const_spec((1, 1, MLA_KV_LORA), lambda i: (layer, 0, 0)),
                  _const_spec((1, MLA_Q_LORA, 2 * HW), lambda i: (layer, 0, 0)),
                  _const_spec((1, MLA_KV_LORA, HW + MLA_HEADS * MLA_V), lambda i: (layer, 0, 0)),
                  _const_spec((1, 1, LANES), lambda i: (layer, 0, 0)),
                  _const_spec((1, 1, LANES), lambda i: (layer, 0, 0)),
                  _const_spec((1, 1, LANES), lambda i: (layer, 0, 0)),
                  pl.BlockSpec((tm, LANES), pos),
                  pl.BlockSpec((tm, LANES), pos)],
        out_specs=[pl.BlockSpec((1, G, tm, GW), lambda i: (i // per_b, 0, i % per_b, 0)),
                   pl.BlockSpec((1, G, tm, GW), lambda i: (i // per_b, 0, i % per_b, 0)),
                   pl.BlockSpec((1, MLA_HEADS, MLA_V_ROWS, tm), lambda i: (i // per_b, 0, 0, i % per_b))],
        out_shape=[jax.ShapeDtypeStruct((B, G, S, GW), BF16),
                   jax.ShapeDtypeStruct((B, G, S, GW), BF16),
                   jax.ShapeDtypeStruct((B, MLA_HEADS, MLA_V_ROWS, S), BF16)],
        compiler_params=_cparams(("parallel",), 32),
        name="mla_prep",
    )(oa, qn, kvn, wuq, wkv, qg, qgs, kg, *tables)


def _flash_kernel(qi_ref, kj_ref, qn_ref, kn_ref, q0_ref, k0_ref, vt_ref, o_ref, m_ref, acc_ref, s_ref, *, T, n_pairs):
    p_id = pl.program_id(2)
    qi = qi_ref[p_id]
    kj = kj_ref[p_id]
    first = kj == 0
    last = kj == qi

    n_heads = m_ref.shape[0]
    units = [(hh, slice(j * MXU_COLS, (j + 1) * MXU_COLS)) for hh in range(n_heads) for j in range(T // MXU_COLS)]

    def scores(q_ref, k_ref, hh, qs, rows=T):
        q = q_ref[0, 0, qs, hh * LANES:(hh + 1) * LANES]
        k = k_ref[0, 0, 0:rows, hh * LANES:(hh + 1) * LANES]
        return lax.dot_general(k, q, (((1,), (1,)), ((), ())), preferred_element_type=F32)

    @pl.when(p_id == 0)
    def _():
        for hh, qs in units:
            s_ref[0, hh, :, qs] = scores(q0_ref, k0_ref, hh, qs)

    @pl.when(first)
    def _():
        m_ref[...] = jnp.full(m_ref.shape, NEG_INF, F32)
        acc_ref[...] = jnp.zeros(acc_ref.shape, F32)

    nxt = jnp.minimum(p_id + 1, n_pairs - 1)

    def step(cur_diag, next_diag, slot):
        for hh, qs in units:
            rows_next = qs.stop if next_diag else T
            s_ref[1 - slot, hh, 0:rows_next, qs] = scores(qn_ref, kn_ref, hh, qs, rows_next)
            rows = qs.stop if cur_diag else T
            st = s_ref[slot, hh, 0:rows, qs]
            if cur_diag:
                key = lax.broadcasted_iota(jnp.int32, (rows, MXU_COLS), 0)
                qry = lax.broadcasted_iota(jnp.int32, (rows, MXU_COLS), 1) + qs.start
                st = jnp.where(key <= qry, st, NEG_INF)
            m_prev = m_ref[hh, :, qs]
            m_new = jnp.maximum(m_prev, jnp.max(st, axis=0, keepdims=True))
            alpha = jnp.exp2(m_prev - m_new)
            pt = jnp.exp2(st - m_new).astype(BF16)
            acc_ref[hh, :, qs] = (acc_ref[hh, :, qs] * alpha
                                  + jnp.dot(vt_ref[0, hh, :, 0:rows], pt, preferred_element_type=F32))
            m_ref[hh, :, qs] = m_new

    parity = p_id % 2
    cur_diag = last
    next_diag = jnp.logical_and(jnp.logical_not(cur_diag), qi_ref[nxt] == kj_ref[nxt])
    plain = jnp.logical_not(jnp.logical_or(cur_diag, next_diag))
    for slot in range(2):
        here = parity == slot
        pl.when(jnp.logical_and(here, cur_diag))(functools.partial(step, True, False, slot))
        pl.when(jnp.logical_and(here, next_diag))(functools.partial(step, False, True, slot))
        pl.when(jnp.logical_and(here, plain))(functools.partial(step, False, False, slot))

    @pl.when(last)
    def _():
        outs = []
        for hh in range(n_heads):
            a = acc_ref[hh]
            outs.append(a[0:V_DIM, :] / a[V_DIM:V_DIM + 1, :])
        o_ref[0] = jnp.concatenate(outs, axis=0).T.astype(o_ref.dtype)


def _band_bias(T, W):
    d = (jnp.arange(W + 1)[:, None, None] * T + jnp.arange(T)[None, None, :] - jnp.arange(T)[None, :, None])
    cw = jnp.zeros(d.shape, F32)
    for window, dil in DIL_PATTERNS:
        cw = cw + ((d >= 0) & (d <= window) & (d % dil == 0)).astype(F32)
    return jnp.log2(cw)


def _causal_attention(q, k, vt, T=FLASH_TILE):
    B, G, S, GW = q.shape
    g = GW // LANES
    v_rows = vt.shape[2]
    nq = S // T
    pairs = [(i, j) for i in range(nq) for j in range(i + 1)]
    n = len(pairs)
    qi = jnp.asarray(np.array([p[0] for p in pairs], np.int32))
    kj = jnp.asarray(np.array([p[1] for p in pairs], np.int32))
    nxt = lambda p: jnp.minimum(p + 1, n - 1)
    in_specs = [pl.BlockSpec((1, 1, T, GW), lambda b, h, p, qi, kj: (b, h, qi[nxt(p)], 0)),
                pl.BlockSpec((1, 1, T, GW), lambda b, h, p, qi, kj: (b, h, kj[nxt(p)], 0)),
                pl.BlockSpec((1, 1, T, GW), lambda b, h, p, qi, kj: (b, h, qi[0], 0)),
                pl.BlockSpec((1, 1, T, GW), lambda b, h, p, qi, kj: (b, h, kj[0], 0)),
                pl.BlockSpec((1, g, v_rows, T), lambda b, h, p, qi, kj: (b, h, 0, kj[p]))]
    return pl.pallas_call(
        functools.partial(_flash_kernel, T=T, n_pairs=n),
        grid_spec=pltpu.PrefetchScalarGridSpec(
            num_scalar_prefetch=2,
            grid=(B, G, n),
            in_specs=in_specs,
            out_specs=pl.BlockSpec((1, T, g * V_DIM), lambda b, h, p, qi, kj: (b, qi[p], h)),
            scratch_shapes=[pltpu.VMEM((g, 1, T), F32), pltpu.VMEM((g, v_rows, T), F32),
                            pltpu.VMEM((2, g, T, T), F32)]),
        out_shape=jax.ShapeDtypeStruct((B, S, G * g * V_DIM), BF16),
        compiler_params=_cparams(("arbitrary", "arbitrary", "arbitrary"), 56),
        name="flash_causal",
    )(qi, kj, q, k, q, k, vt)


def _band_kernel(qc_ref, qn_ref, kn_ref, *rest, T, W):
    k_refs, vt_refs = rest[:W + 1], rest[W + 1:2 * W + 2]
    bias_ref, o_ref, m_ref, acc_ref, s_ref = rest[2 * W + 2:]
    i = pl.program_id(2)
    units = [(hh, j) for hh in range(2) for j in range(T // MXU_COLS)]

    def rows_of(d, j):
        if d == 0:
            return 0, (j + 1) * MXU_COLS
        if d == W:
            return j * MXU_COLS, T
        return 0, T

    def produce(q_ref, k_ref, d, hh, j, bias_idx):
        lo, hi = rows_of(d, j)
        qs = slice(j * MXU_COLS, (j + 1) * MXU_COLS)
        q = q_ref[0, 0, qs, hh * LANES:(hh + 1) * LANES]
        k = k_ref[0, 0, lo:hi, hh * LANES:(hh + 1) * LANES]
        s = lax.dot_general(k, q, (((1,), (1,)), ((), ())), preferred_element_type=F32)
        s_ref[d, hh, lo:hi, qs] = s + bias_ref[bias_idx, lo:hi, qs]

    def consume(d, hh, j):
        lo, hi = rows_of(d, j)
        qs = slice(j * MXU_COLS, (j + 1) * MXU_COLS)
        st = s_ref[d, hh, lo:hi, qs]
        m_prev = m_ref[hh, :, qs]
        m_new = jnp.maximum(m_prev, jnp.max(st, axis=0, keepdims=True))
        alpha = jnp.exp2(m_prev - m_new)
        pt = jnp.exp2(st - m_new).astype(BF16)
        acc_ref[hh, :, qs] = (acc_ref[hh, :, qs] * alpha
                              + jnp.dot(vt_refs[d][0, hh, :, lo:hi], pt, preferred_element_type=F32))
        m_ref[hh, :, qs] = m_new

    @pl.when(i == 0)
    def _():
        for hh, j in units:
            produce(qc_ref, k_refs[0], 0, hh, j, 0)

    m_ref[...] = jnp.full(m_ref.shape, NEG_INF, F32)
    acc_ref[...] = jnp.zeros(acc_ref.shape, F32)
    for d in range(W + 1):
        for hh, j in units:
            if d < W:
                produce(qc_ref, k_refs[d + 1], d + 1, hh, j, jnp.where(i >= d + 1, d + 1, W + 1))
            else:
                produce(qn_ref, kn_ref, 0, hh, j, 0)
            consume(d, hh, j)

    outs = []
    for hh in range(2):
        a = acc_ref[hh]
        outs.append(a[0:V_DIM, :] / a[V_DIM:V_DIM + 1, :])
    o_ref[0] = jnp.concatenate(outs, axis=0).T.astype(o_ref.dtype)


def _band_attention(q, k, vt, T=FLASH_TILE):
    B, HP, S, _ = q.shape
    v_rows = vt.shape[2]
    n_q = S // T
    W = DIL_MAX_WINDOW // T
    bias = jnp.concatenate([_band_bias(T, W), jnp.full((1, T, T), NEG_INF, F32)], axis=0)
    qk_block = lambda f: pl.BlockSpec((1, 1, T, 2 * LANES), lambda b, h, i: (b, h, f(i), 0))
    vt_block = lambda f: pl.BlockSpec((1, 2, v_rows, T), lambda b, h, i: (b, h, 0, f(i)))
    back = lambda d: (lambda i: jnp.maximum(i - d, 0))
    nxt = lambda i: jnp.minimum(i + 1, n_q - 1)
    in_specs = ([qk_block(lambda i: i), qk_block(nxt), qk_block(nxt)]
                + [qk_block(back(d)) for d in range(W + 1)]
                + [vt_block(back(d)) for d in range(W + 1)]
                + [_const_spec((W + 2, T, T), lambda b, h, i: (0, 0, 0))])
    return pl.pallas_call(
        functools.partial(_band_kernel, T=T, W=W),
        grid=(B, HP, n_q),
        in_specs=in_specs,
        out_specs=pl.BlockSpec((1, T, LANES), lambda b, h, i: (b, i, h)),
        out_shape=jax.ShapeDtypeStruct((B, S, 2 * HP * V_DIM), BF16),
        scratch_shapes=[pltpu.VMEM((2, 1, T), F32), pltpu.VMEM((2, v_rows, T), F32),
                        pltpu.VMEM((W + 1, 2, T, T), F32)],
        compiler_params=_cparams(("arbitrary", "arbitrary", "arbitrary"), 58),
        name="flash_band",
    )(q, q, k, *([k] * (W + 1)), *([vt] * (W + 1)), bias)


def _mlstm_kernel(qk_ref, vt_ref, o_ref, if_ref, cw_ref, cb_ref, gb_ref, hg_ref, y_ref,
                  xs_ref, cn_ref, m_ref, *, Lc):
    H, DK, DV = MLSTM_HEADS, MLSTM_QK, MLSTM_V
    HALO = 8

    @pl.when(pl.program_id(1) == 0)
    def _():
        xs_ref[0:HALO, :] = jnp.zeros((HALO, 2 * H * DK), F32)
        cn_ref[...] = jnp.zeros(cn_ref.shape, F32)
        m_ref[...] = jnp.zeros(m_ref.shape, F32)

    x = qk_ref[...]
    xs_ref[HALO:HALO + Lc, :] = x
    y = cb_ref[0]
    for j in range(CONV_K):
        y = y + cw_ref[0, j:j + 1, :] * xs_ref[pl.ds(HALO - (CONV_K - 1) + j, Lc), :]
    xs_ref[0:HALO, :] = x[Lc - HALO:, :]
    qk = y * _sigmoid(y)
    q_t = qk[:, :H * DK].T.astype(BF16)
    k_all = (qk[:, H * DK:] * (DK ** -0.5)).astype(BF16)

    g = if_ref[...] + gb_ref[0]
    logf = jnp.minimum(g, 0.0) - jnp.log(1.0 + jnp.exp(-jnp.abs(g)))
    src = lax.broadcasted_iota(jnp.int32, (Lc, Lc), 0)
    dst = lax.broadcasted_iota(jnp.int32, (Lc, Lc), 1)
    tri = jnp.where(dst <= src, 1.0, 0.0).astype(F32)
    bcum = jnp.dot(tri, logf, preferred_element_type=F32, precision=HIGHEST)
    causal = src <= dst
    g_t = g.T
    bcum_t = bcum.T

    for h in range(H):
        b_row = bcum_t[H + h:H + h + 1, :]
        w_col = g[:, h:h + 1] - bcum[:, H + h:H + h + 1]
        w_row = g_t[h:h + 1, :] - b_row
        m_prev = m_ref[h:h + 1, 0:1]
        dlog = jnp.where(causal, w_col + b_row, NEG_INF)
        inter = b_row + m_prev
        m_t = jnp.maximum(inter, jnp.max(dlog, axis=0, keepdims=True))
        qh_t = q_t[h * DK:(h + 1) * DK, :]
        kh = k_all[:, h * DK:(h + 1) * DK]
        s_t = jnp.dot(kh, qh_t, preferred_element_type=F32)
        sc_t = (s_t * jnp.exp(dlog - m_t)).astype(BF16)
        decay = jnp.exp(inter - m_t)
        cn = cn_ref[h]
        vt = vt_ref[0, h]
        num = (decay * jnp.dot(cn.astype(BF16), qh_t, preferred_element_type=F32)
               + jnp.dot(vt, sc_t, preferred_element_type=F32))
        den = num[DV:DV + 1, :]
        hh = num[0:DV, :] / jnp.maximum(jnp.abs(den), jnp.exp(-m_t))
        hn = hh * lax.rsqrt(jnp.mean(hh * hh, axis=0, keepdims=True) + EPS)
        o_gate = _sigmoid(o_ref[:, h * DV:(h + 1) * DV].astype(F32))
        y_ref[:, h * DV:(h + 1) * DV] = (hn.T * hg_ref[0, :, h * DV:(h + 1) * DV] * o_gate).astype(y_ref.dtype)

        b_end = b_row[:, Lc - 1:Lc]
        wlog = b_end + w_row
        m_new = jnp.maximum(b_end + m_prev, jnp.max(wlog, axis=1, keepdims=True))
        cd = jnp.exp(b_end + m_prev - m_new)
        vw = (vt.astype(F32) * jnp.exp(wlog - m_new)).astype(BF16)
        cn_ref[h] = cd * cn + jnp.dot(vw, kh, preferred_element_type=F32)
        m_ref[h:h + 1, :] = jnp.broadcast_to(m_new, (1, LANES))


def _mlstm(mqk, mvt, mo, mif, B, S, layer, conv_w, conv_b, gate_b, head_gain, Lc=256):
    H, DK, DV = MLSTM_HEADS, MLSTM_QK, MLSTM_V
    nc = S // Lc
    row = lambda b, c: (b * nc + c, 0)
    VR = mvt.shape[2]
    return pl.pallas_call(
        functools.partial(_mlstm_kernel, Lc=Lc),
        grid=(B, nc),
        in_specs=[pl.BlockSpec((Lc, 2 * H * DK), row),
                  pl.BlockSpec((1, H, VR, Lc), lambda b, c: (b, 0, 0, c)),
                  pl.BlockSpec((Lc, H * DV), row),
                  pl.BlockSpec((Lc, LANES), row),
                  _const_spec((1, CONV_K, 2 * H * DK), lambda b, c: (layer, 0, 0)),
                  _const_spec((1, 1, 2 * H * DK), lambda b, c: (layer, 0, 0)),
                  _const_spec((1, 1, LANES), lambda b, c: (layer, 0, 0)),
                  _const_spec((1, 1, H * DV), lambda b, c: (layer, 0, 0))],
        out_specs=pl.BlockSpec((Lc, H * DV), row),
        out_shape=jax.ShapeDtypeStruct((B * S, H * DV), BF16),
        scratch_shapes=[pltpu.VMEM((Lc + 8, 2 * H * DK), F32), pltpu.VMEM((H, VR, DK), F32),
                        pltpu.VMEM((8, LANES), F32)],
        compiler_params=_cparams(("arbitrary", "arbitrary"), 32),
        name="mlstm",
    )(mqk, mvt, mo, mif, conv_w, conv_b, gate_b, head_gain)


def _merge_kernel(ya_ref, yb_ref, yc_ref, g_ref, x_ref, ga_ref, wa_ref, wb_ref, wc_ref, wo_ref, out_ref):
    def branch(y_ref, w_ref, c):
        gate = _sigmoid(g_ref[:, c * D_MODEL:(c + 1) * D_MODEL].astype(F32))
        return gate * jnp.dot(y_ref[...], w_ref[0], preferred_element_type=F32)

    merged = branch(ya_ref, wa_ref, 0) + branch(yb_ref, wb_ref, 1) + branch(yc_ref, wc_ref, 2)
    y = jnp.dot(merged.astype(BF16), wo_ref[0], preferred_element_type=F32)
    out_ref[...] = x_ref[...] + ga_ref[0] * y


def _merge(ya, yb, yc, gates, x2, S, layer, ga, wa, wb, wc, wo, tm=512):
    N, D = x2.shape
    per_b = S // tm
    row = lambda i: (i, 0)
    wsel = lambda i: (layer, 0, 0)
    return pl.pallas_call(
        _merge_kernel,
        grid=(N // tm,),
        in_specs=[pl.BlockSpec((tm, 512), row), pl.BlockSpec((tm, 512), row), pl.BlockSpec((tm, 512), row),
                  pl.BlockSpec((tm, N_BRANCH * D), row), pl.BlockSpec((tm, D), row),
                  pl.BlockSpec((1, 1, D), lambda i: (i // per_b, 0, 0)),
                  _const_spec((1, 512, D), wsel), _const_spec((1, 512, D), wsel), _const_spec((1, 512, D), wsel),
                  _const_spec((1, D, D), wsel)],
        out_specs=pl.BlockSpec((tm, D), row),
        out_shape=jax.ShapeDtypeStruct((N, D), F32),
        compiler_params=_cparams(("parallel",), 48),
        name="merge",
    )(ya, yb, yc, gates, x2, ga, wa, wb, wc, wo)


def _ffn_kernel(x_ref, gain_ref, sc_ref, sh_ref, gf_ref, wr_ref, br_ref, wg_ref, wu_ref, wd_ref, out_ref):
    x = x_ref[...]
    tm = x.shape[0]
    ms = jnp.mean(x * x, axis=-1, keepdims=True)
    hf = x * lax.rsqrt(ms + EPS) * gain_ref[0] * (1.0 + sc_ref[0]) + sh_ref[0]

    hf_hi = hf.astype(BF16)
    hf_lo = (hf - hf_hi.astype(F32)).astype(BF16)
    wr = wr_ref[0]
    wr_hi = wr.astype(BF16)
    wr_split = jnp.concatenate([wr_hi, (wr - wr_hi.astype(F32)).astype(BF16)], axis=1)
    parts = (jnp.dot(hf_hi, wr_split, preferred_element_type=F32)
             + jnp.dot(hf_lo, wr_split, preferred_element_type=F32))
    logits = parts[:, :LANES] + parts[:, LANES:] + br_ref[0]
    lane_i = lax.broadcasted_iota(jnp.int32, (tm, LANES), 1)
    lane = lane_i.astype(F32)
    big = float(LANES)

    def first_argmax(z):
        zmax = jnp.max(z, axis=1, keepdims=True)
        return zmax, jnp.min(jnp.where(z == zmax, lane, big), axis=1, keepdims=True)

    lg = jnp.where(lane_i < N_GROUPS, logits, NEG_INF)
    gmax, g_idx = first_argmax(lg)
    p_sel = 1.0 / jnp.sum(jnp.exp(lg - gmax), axis=1, keepdims=True)
    lane_grp = ((lane_i - N_GROUPS) // EXPERTS_PER_GROUP).astype(F32)
    in_grp = (lane_i >= N_GROUPS) & (lane_i < N_GROUPS + N_EXPERTS) & (lane_grp == g_idx)
    le = jnp.where(in_grp, logits, NEG_INF)
    v1, i1 = first_argmax(le)
    le2 = jnp.where(lane == i1, NEG_INF, le)
    v2, i2 = first_argmax(le2)
    e2 = jnp.exp(v2 - v1)
    den = 1.0 + e2
    comb = jnp.where(lane == i1, (1.0 / den) * p_sel, 0.0) + jnp.where(lane == i2, (e2 / den) * p_sel, 0.0)

    hb = hf_hi
    acc = jnp.zeros((tm, D_MODEL), F32)
    for e in range(N_EXPERTS):
        gate = jnp.dot(hb, wg_ref[0, e], preferred_element_type=F32)
        up = jnp.dot(hb, wu_ref[0, e], preferred_element_type=F32)
        hid = gate * _sigmoid(gate) * up * comb[:, N_GROUPS + e:N_GROUPS + e + 1]
        acc = acc + jnp.dot(hid.astype(BF16), wd_ref[0, e], preferred_element_type=F32)
    out_ref[...] = x + gf_ref[0] * acc


def _ffn(x2, S, layer, gain, sc, sh, gf, wr, br, wg, wu, wd, tm=512):
    N, D = x2.shape
    per_b = S // tm
    row = lambda i: (i, 0)
    bsel = lambda i: (i // per_b, 0, 0)
    wsel = lambda i: (layer, 0, 0, 0)
    return pl.pallas_call(
        _ffn_kernel,
        grid=(N // tm,),
        in_specs=[pl.BlockSpec((tm, D), row),
                  _const_spec((1, 1, D), lambda i: (layer, 0, 0)),
                  pl.BlockSpec((1, 1, D), bsel), pl.BlockSpec((1, 1, D), bsel), pl.BlockSpec((1, 1, D), bsel),
                  _const_spec((1, D, LANES), lambda i: (layer, 0, 0)),
                  _const_spec((1, 1, LANES), lambda i: (layer, 0, 0)),
                  _const_spec((1, N_EXPERTS, D, EXPERT_FF), wsel),
                  _const_spec((1, N_EXPERTS, D, EXPERT_FF), wsel),
                  _const_spec((1, N_EXPERTS, EXPERT_FF, D), wsel)],
        out_specs=pl.BlockSpec((tm, D), row),
        out_shape=jax.ShapeDtypeStruct((N, D), F32),
        compiler_params=_cparams(("parallel",), 56),
        name="ffn",
    )(x2, gain, sc, sh, gf, wr, br, wg, wu, wd)


def _pad_last(a, n):
    return jnp.pad(a, [(0, 0)] * (a.ndim - 1) + [(0, n - a.shape[-1])])


def _prep_w_in(w_in):
    pts = np.cumsum(IN_SIZES)[:-1].tolist()
    cq, ckv, kr, m_qk, m_v, m_o, m_i, m_f, d_q, d_k, d_v, gates = jnp.split(w_in, pts, axis=-1)
    seg_a = _pad_last(jnp.concatenate([cq, ckv, kr], axis=-1), 512)
    seg_if = _pad_last(jnp.concatenate([m_i, m_f], axis=-1), LANES)
    w = jnp.concatenate([seg_a, m_qk, m_v, m_o, seg_if,
                         d_q, d_k,
                         d_v, gates], axis=-1)
    assert w.shape[-1] == COL_END
    return w.astype(BF16)


def kernel(x, c, w_ada, b_ada, attn_norm, w_in, mla_q_norm, mla_kv_norm, mla_w_uq, mla_w_ukv, mla_q_gain, mla_k_gain, mlstm_conv_w, mlstm_conv_b, mlstm_b_i, mlstm_b_f, mlstm_head_gain, dil_q_gain, dil_k_gain, w_branch_a, w_branch_b, w_branch_c, w_out, ffn_norm, w_router_group, b_router_group, w_router_expert, b_router_expert, w_exp_gate, w_exp_up, w_exp_down):
    B, S, D = x.shape
    L = w_ada.shape[0]
    N = B * S
    assert D == D_MODEL and S % DIL_MAX_WINDOW == 0

    mod = _adaln(c, w_ada, b_ada)

    w_pad = _prep_w_in(w_in)
    half = MLA_ROPE // 2
    swap_rope = lambda a: jnp.concatenate(
        [jnp.zeros_like(a[..., :MLA_NOPE]), a[..., MLA_NOPE + half:], a[..., MLA_NOPE:MLA_NOPE + half]], axis=-1)
    uq = mla_w_uq.reshape(L, MLA_Q_LORA, MLA_HEADS, MLA_QK)
    wuq = jnp.concatenate([_pad_last(u, LANES).reshape(L, MLA_Q_LORA, MLA_HEADS * LANES)
                           for u in (uq, swap_rope(uq))], axis=-1).astype(BF16)
    ukv = mla_w_ukv.reshape(L, MLA_KV_LORA, MLA_HEADS, MLA_NOPE + MLA_V)
    wkv = jnp.concatenate([_pad_last(ukv[..., :MLA_NOPE], LANES).reshape(L, MLA_KV_LORA, MLA_HEADS * LANES),
                           ukv[..., MLA_NOPE:].reshape(L, MLA_KV_LORA, MLA_HEADS * MLA_V)], axis=-1).astype(BF16)
    rows = lambda a: a[:, None, :]
    qg = rows(_pad_last(mla_q_gain, LANES))
    qgs = rows(_pad_last(swap_rope(mla_q_gain), LANES))
    kg = rows(_pad_last(mla_k_gain, LANES))
    dqg = rows(_pad_last(dil_q_gain, LANES))
    dkg = rows(_pad_last(dil_k_gain, LANES))
    gate_b = rows(_pad_last(jnp.concatenate([mlstm_b_i, mlstm_b_f], axis=-1), LANES))
    tables = _rope_tables(S)
    wa, wb, wc, wo = (w.astype(BF16) for w in (w_branch_a, w_branch_b, w_branch_c, w_out))
    wr = _pad_last(jnp.concatenate([w_router_group, w_router_expert], axis=-1), LANES)
    br = rows(_pad_last(jnp.concatenate([b_router_group, b_router_expert], axis=-1), LANES))
    wg, wu, wd = (w.astype(BF16) for w in (w_exp_gate, w_exp_up, w_exp_down))
    attn_norm, ffn_norm, mla_q_norm, mla_kv_norm, mlstm_conv_b, mlstm_head_gain = (
        rows(a) for a in (attn_norm, ffn_norm, mla_q_norm, mla_kv_norm, mlstm_conv_b, mlstm_head_gain))

    x2 = x.reshape(N, D)
    for l in range(L):
        sh_a, sc_a, g_a, sh_f, sc_f, g_f = (mod[l, :, i] for i in range(6))
        oa, mqk, mv, mo, mif, qd, kd, vd, gates = _inproj(x2, S, l, attn_norm, sc_a, sh_a, w_pad, dqg, dkg)
        q, k, vt = _mla_prep(oa, S, l, mla_q_norm, mla_kv_norm, wuq, wkv, qg, qgs, kg, tables)
        ya = _causal_attention(q, k, vt)
        yb = _mlstm(mqk, mv, mo, mif, B, S, l, mlstm_conv_w, mlstm_conv_b, gate_b, mlstm_head_gain)
        yc = _band_attention(qd, kd, vd)
        x2 = _merge(ya.reshape(N, -1), yb, yc.reshape(N, -1), gates, x2, S, l, g_a, wa, wb, wc, wo)
        x2 = _ffn(x2, S, l, ffn_norm, sc_f, sh_f, g_f, wr, br, wg, wu, wd)
    return x2.reshape(B, S, D)
```

```python
import functools

import jax
import jax.numpy as jnp
import numpy as np
from jax import lax
from jax.experimental import pallas as pl
from jax.experimental.pallas import tpu as pltpu

F32 = jnp.float32
BF16 = jnp.bfloat16
HIGHEST = lax.Precision.HIGHEST
NEG_INF = float("-inf")

EPS = 1e-6
LANES = 128
MXU_COLS = 256
VMEM_BYTES_V7X = 64 * 1024 * 1024

D_MODEL = 1024
MLA_HEADS = 8
MLA_Q_LORA = 256
MLA_KV_LORA = 128
MLA_NOPE = 64
MLA_ROPE = 32
MLA_V = 64
MLA_QK = MLA_NOPE + MLA_ROPE
ROPE_THETA = 10000.0
MLSTM_HEADS = 4
MLSTM_QK = 64
MLSTM_V = 128
CONV_K = 4
DIL_HEADS = 8
DIL_HEAD_DIM = 64
DIL_PATTERNS = ((128, 1), (512, 4), (2048, 16))
DIL_WIDTH = DIL_HEADS * DIL_HEAD_DIM
N_GROUPS = 4
EXPERTS_PER_GROUP = 4
N_EXPERTS = N_GROUPS * EXPERTS_PER_GROUP
EXPERT_FF = 256
N_BRANCH = 3
IN_SIZES = (MLA_Q_LORA, MLA_KV_LORA, MLA_ROPE,
            2 * MLSTM_HEADS * MLSTM_QK, MLSTM_HEADS * MLSTM_V, MLSTM_HEADS * MLSTM_V, MLSTM_HEADS, MLSTM_HEADS,
            DIL_WIDTH, DIL_WIDTH, DIL_WIDTH, N_BRANCH * D_MODEL)

COL_A = 0
COL_MQK = 512
COL_MV = 1024
COL_MO = 1536
COL_MIF = 2048
COL_DQ = 2176
COL_DK = 2688
COL_DV = 3200
COL_G = 3712
COL_END = 6784

FLASH_TILE = 1024
MLA_HEADS_PER_STEP = 4
DIL_MAX_WINDOW = max(w for w, _ in DIL_PATTERNS)
V_DIM = 64
MLA_V_ROWS = V_DIM + 48
DIL_V_ROWS = V_DIM + 48
MLSTM_V_ROWS = MLSTM_V + 64
LOG2E = 1.4426950408889634
assert MLA_V == V_DIM and DIL_HEAD_DIM == V_DIM


def _store_v_tiles_t(v, vt_ref, dim=V_DIM):
    tm = v.shape[0]
    rows = vt_ref.shape[2]
    v_t = v.astype(BF16).T
    fill = (lax.broadcasted_iota(jnp.int32, (rows - dim, tm), 0) == 0).astype(BF16)
    for h in range(v.shape[1] // dim):
        vt_ref[0, h, 0:dim, :] = v_t[h * dim:(h + 1) * dim, :]
        vt_ref[0, h, dim:rows, :] = fill


def _cparams(sem, vmem_mb):
    limit = vmem_mb * 1024 * 1024
    assert limit < VMEM_BYTES_V7X
    return pltpu.CompilerParams(dimension_semantics=sem, vmem_limit_bytes=limit)


def _sigmoid(x):
    return 1.0 / (1.0 + jnp.exp(-x))


def _const_spec(shape, index_map):
    return pl.BlockSpec(shape, index_map, pipeline_mode=pl.Buffered(1))


def _adaln_kernel(c_ref, w_ref, b_ref, o_ref):
    c = c_ref[...]
    ca = c * _sigmoid(c)
    o_ref[0] = jnp.dot(ca, w_ref[0], preferred_element_type=F32, precision=HIGHEST) + b_ref[0]


def _adaln(c, w_ada, b_ada):
    L, D, D6 = w_ada.shape
    B = c.shape[0]
    rows = 8
    cp = jnp.zeros((rows, D), F32).at[:B].set(c)
    out = pl.pallas_call(
        _adaln_kernel,
        grid=(L, D6 // D),
        in_specs=[pl.BlockSpec((rows, D), lambda l, j: (0, 0)),
                  pl.BlockSpec((1, D, D), lambda l, j: (l, 0, j)),
                  pl.BlockSpec((1, 1, D), lambda l, j: (l, 0, j))],
        out_specs=pl.BlockSpec((1, rows, D), lambda l, j: (l, 0, j)),
        out_shape=jax.ShapeDtypeStruct((L, rows, D6), F32),
        compiler_params=_cparams(("parallel", "parallel"), 32),
        name="adaln",
    )(cp, w_ada, b_ada.reshape(L, 1, D6))
    return out[:, :B].reshape(L, B, D6 // D, 1, D)


def _inproj_kernel(x_ref, gain_ref, sc_ref, sh_ref, w_ref, dqg_ref, dkg_ref,
                   oa_ref, mqk_ref, mv_ref, mo_ref, mif_ref, qd_ref, kd_ref, vd_ref, g_ref):
    x = x_ref[...]
    ms = jnp.mean(x * x, axis=-1, keepdims=True)
    y = x * lax.rsqrt(ms + EPS) * gain_ref[0]
    u = (y * (1.0 + sc_ref[0]) + sh_ref[0]).astype(BF16)

    def proj(a, b):
        return jnp.dot(u, w_ref[0, :, a:b], preferred_element_type=F32)

    oa_ref[...] = proj(COL_A, COL_MQK)
    mqk_ref[...] = proj(COL_MQK, COL_MV)
    _store_v_tiles_t(proj(COL_MV, COL_MO), mv_ref, MLSTM_V)
    mo_ref[...] = proj(COL_MO, COL_MIF).astype(BF16)
    mif_ref[...] = proj(COL_MIF, COL_DQ)

    half = LANES // 2
    low = lax.broadcasted_iota(jnp.int32, (1, LANES), 1) < half

    def head_norm(z, g):
        for h in range(DIL_HEADS):
            pair = z[:, (h // 2) * LANES:(h // 2 + 1) * LANES]
            zh = jnp.where(low, pair if h % 2 == 0 else pltpu.roll(pair, half, 1), 0.0)
            r = lax.rsqrt(jnp.sum(zh * zh, axis=-1, keepdims=True) / DIL_HEAD_DIM + EPS)
            yield h, zh * r * g

    dq = proj(COL_DQ, COL_DK)
    for h, zn in head_norm(dq, dqg_ref[0]):
        qd_ref[0, h // 2, :, (h % 2) * LANES:(h % 2 + 1) * LANES] = (zn * (DIL_HEAD_DIM ** -0.5 * LOG2E)).astype(BF16)
    dk = proj(COL_DK, COL_DV)
    for h, zn in head_norm(dk, dkg_ref[0]):
        kd_ref[0, h // 2, :, (h % 2) * LANES:(h % 2 + 1) * LANES] = zn.astype(BF16)
    _store_v_tiles_t(proj(COL_DV, COL_G), vd_ref)
    for c in range(N_BRANCH):
        g_ref[:, c * D_MODEL:(c + 1) * D_MODEL] = proj(COL_G + c * D_MODEL, COL_G + (c + 1) * D_MODEL).astype(BF16)


def _inproj(x2, S, layer, gain, sc, sh, w_pad, dqg, dkg, tm=512):
    N, D = x2.shape
    per_b = S // tm
    row = lambda i: (i, 0)
    bsel = lambda i: (i // per_b, 0, 0)
    B = N // S
    outs = [(512, F32), (512, F32), "mvt", (512, BF16), (LANES, F32),
            "qk", "qk", "vt", (3072, BF16)]
    head_major = lambda i: (i // per_b, 0, i % per_b, 0)
    time_minor = lambda i: (i // per_b, 0, 0, i % per_b)
    special_spec = {"qk": pl.BlockSpec((1, DIL_HEADS // 2, tm, 2 * LANES), head_major),
                    "vt": pl.BlockSpec((1, DIL_HEADS, DIL_V_ROWS, tm), time_minor),
                    "mvt": pl.BlockSpec((1, MLSTM_HEADS, MLSTM_V_ROWS, tm), time_minor)}
    special_shape = {"qk": jax.ShapeDtypeStruct((B, DIL_HEADS // 2, S, 2 * LANES), BF16),
                     "vt": jax.ShapeDtypeStruct((B, DIL_HEADS, DIL_V_ROWS, S), BF16),
                     "mvt": jax.ShapeDtypeStruct((B, MLSTM_HEADS, MLSTM_V_ROWS, S), BF16)}
    return pl.pallas_call(
        _inproj_kernel,
        grid=(N // tm,),
        in_specs=[pl.BlockSpec((tm, D), row),
                  _const_spec((1, 1, D), lambda i: (layer, 0, 0)),
                  pl.BlockSpec((1, 1, D), bsel),
                  pl.BlockSpec((1, 1, D), bsel),
                  _const_spec((1, D, COL_END), lambda i: (layer, 0, 0)),
                  _const_spec((1, 1, LANES), lambda i: (layer, 0, 0)),
                  _const_spec((1, 1, LANES), lambda i: (layer, 0, 0))],
        out_specs=[special_spec[o] if isinstance(o, str) else pl.BlockSpec((tm, o[0]), row) for o in outs],
        out_shape=[special_shape[o] if isinstance(o, str) else jax.ShapeDtypeStruct((N, o[0]), o[1]) for o in outs],
        compiler_params=_cparams(("parallel",), 52),
        name="inproj",
    )(x2, gain, sc, sh, w_pad, dqg, dkg)


def _mla_prep_kernel(a_ref, qn_ref, kvn_ref, wuq_ref, wkv_ref, qg_ref, qgs_ref, kg_ref, cos_ref, sin_ref,
                     q_ref, k_ref, vt_ref):
    HW = MLA_HEADS * LANES
    a = a_ref[...]
    cq = a[:, :MLA_Q_LORA]
    ckv = a[:, MLA_Q_LORA:MLA_Q_LORA + MLA_KV_LORA]
    kr_tile = a[:, MLA_Q_LORA + MLA_KV_LORA:]

    def rms(z, g):
        return z * lax.rsqrt(jnp.mean(z * z, axis=-1, keepdims=True) + EPS) * g

    cqn = rms(cq, qn_ref[0]).astype(BF16)
    ckvn = rms(ckv, kvn_ref[0]).astype(BF16)
    qr = jnp.dot(cqn, wuq_ref[0], preferred_element_type=F32)
    kvr = jnp.dot(ckvn, wkv_ref[0], preferred_element_type=F32)
    _store_v_tiles_t(kvr[:, HW:], vt_ref)

    cosf, sins = cos_ref[...], sin_ref[...]
    q_scale = MLA_QK ** -0.5 * LOG2E
    gc = cosf * (qg_ref[0] * q_scale)
    gs = sins * (qgs_ref[0] * q_scale)
    kg = kg_ref[0]
    half = MLA_ROPE // 2
    krg = pltpu.roll(kr_tile, MLA_NOPE, 1) * kg
    kr_rot = krg * cosf + (pltpu.roll(krg, LANES - half, 1) + pltpu.roll(krg, half, 1)) * sins
    kr_ss = jnp.sum(kr_tile * kr_tile, axis=-1, keepdims=True)
    for h in range(MLA_HEADS):
        sl = slice(h * LANES, (h + 1) * LANES)
        grp, pair_sl = h // MLA_HEADS_PER_STEP, slice((h % MLA_HEADS_PER_STEP) * LANES,
                                                     (h % MLA_HEADS_PER_STEP + 1) * LANES)
        z = qr[:, sl]
        rq = lax.rsqrt(jnp.sum(z * z, axis=-1, keepdims=True) / MLA_QK + EPS)
        q_ref[0, grp, :, pair_sl] = ((z * gc + qr[:, HW + h * LANES:HW + (h + 1) * LANES] * gs) * rq).astype(BF16)
        zk = kvr[:, sl]
        rk = lax.rsqrt((jnp.sum(zk * zk, axis=-1, keepdims=True) + kr_ss) / MLA_QK + EPS)
        k_ref[0, grp, :, pair_sl] = ((zk * kg + kr_rot) * rk).astype(BF16)


def _rope_tables(S):
    half = MLA_ROPE // 2
    inv = ROPE_THETA ** (-jnp.arange(half, dtype=F32) / half)
    ang = jnp.arange(S).astype(F32)[:, None] * inv[None, :]
    cos, sin = jnp.cos(ang), jnp.sin(ang)
    one = jnp.ones((S, MLA_NOPE), F32)
    z = lambda n: jnp.zeros((S, n), F32)
    tail = LANES - MLA_QK
    cosf = jnp.concatenate([one, cos, cos, z(tail)], axis=1)
    sins = jnp.concatenate([z(MLA_NOPE), -sin, sin, z(tail)], axis=1)
    return cosf, sins


def _mla_prep(oa, S, layer, qn, kvn, wuq, wkv, qg, qgs, kg, tables, tm=1024):
    N = oa.shape[0]
    B = N // S
    per_b = S // tm
    row = lambda i: (i, 0)
    pos = lambda i: (i % per_b, 0)
    HW = MLA_HEADS * LANES
    G, GW = MLA_HEADS // MLA_HEADS_PER_STEP, MLA_HEADS_PER_STEP * LANES
    return pl.pallas_call(
        _mla_prep_kernel,
        grid=(N // tm,),
        in_specs=[pl.BlockSpec((tm, 512), row),
                  _const_spec((1, 1, MLA_Q_LORA), lambda i: (layer, 0, 0)),
                  _const_spec((1, 1, MLA_KV_LORA), lambda i: (layer, 0, 0)),
                  _const_spec((1, MLA_Q_LORA, 2 * HW), lambda i: (layer, 0, 0)),
                  _const_spec((1, MLA_KV_LORA, HW + MLA_HEADS * MLA_V), lambda i: (layer, 0, 0)),
                  _const_spec((1, 1, LANES), lambda i: (layer, 0, 0)),
                  _const_spec((1, 1, LANES), lambda i: (layer, 0, 0)),
                  _const_spec((1, 1, LANES), lambda i: (layer, 0, 0)),
                  pl.BlockSpec((tm, LANES), pos),
                  pl.BlockSpec((tm, LANES), pos)],
        out_specs=[pl.BlockSpec((1, G, tm, GW), lambda i: (i // per_b, 0, i % per_b, 0)),
                   pl.BlockSpec((1, G, tm, GW), lambda i: (i // per_b, 0, i % per_b, 0)),
                   pl.BlockSpec((1, MLA_HEADS, MLA_V_ROWS, tm), lambda i: (i // per_b, 0, 0, i % per_b))],
        out_shape=[jax.ShapeDtypeStruct((B, G, S, GW), BF16),
                   jax.ShapeDtypeStruct((B, G, S, GW), BF16),
                   jax.ShapeDtypeStruct((B, MLA_HEADS, MLA_V_ROWS, S), BF16)],
        compiler_params=_cparams(("parallel",), 32),
        name="mla_prep",
    )(oa, qn, kvn, wuq, wkv, qg, qgs, kg, *tables)


def _flash_kernel(qi_ref, kj_ref, qn_ref, kn_ref, q0_ref, k0_ref, vt_ref, o_ref, m_ref, acc_ref, s_ref, *, T, n_pairs):
    p_id = pl.program_id(2)
    qi = qi_ref[p_id]
    kj = kj_ref[p_id]
    first = kj == 0
    last = kj == qi

    n_heads = m_ref.shape[0]
    units = [(hh, slice(j * MXU_COLS, (j + 1) * MXU_COLS)) for hh in range(n_heads) for j in range(T // MXU_COLS)]

    def scores(q_ref, k_ref, hh, qs, rows=T):
        q = q_ref[0, 0, qs, hh * LANES:(hh + 1) * LANES]
        k = k_ref[0, 0, 0:rows, hh * LANES:(hh + 1) * LANES]
        return lax.dot_general(k, q, (((1,), (1,)), ((), ())), preferred_element_type=F32)

    @pl.when(p_id == 0)
    def _():
        for hh, qs in units:
            s_ref[0, hh, :, qs] = scores(q0_ref, k0_ref, hh, qs)

    @pl.when(first)
    def _():
        m_ref[...] = jnp.full(m_ref.shape, NEG_INF, F32)
        acc_ref[...] = jnp.zeros(acc_ref.shape, F32)

    nxt = jnp.minimum(p_id + 1, n_pairs - 1)

    def step(cur_diag, next_diag, slot):
        for hh, qs in units:
            rows_next = qs.stop if next_diag else T
            s_ref[1 - slot, hh, 0:rows_next, qs] = scores(qn_ref, kn_ref, hh, qs, rows_next)
            rows = qs.stop if cur_diag else T
            st = s_ref[slot, hh, 0:rows, qs]
            if cur_diag:
                key = lax.broadcasted_iota(jnp.int32, (rows, MXU_COLS), 0)
                qry = lax.broadcasted_iota(jnp.int32, (rows, MXU_COLS), 1) + qs.start
                st = jnp.where(key <= qry, st, NEG_INF)
            m_prev = m_ref[hh, :, qs]
            m_new = jnp.maximum(m_prev, jnp.max(st, axis=0, keepdims=True))
            alpha = jnp.exp2(m_prev - m_new)
            pt = jnp.exp2(st - m_new).astype(BF16)
            acc_ref[hh, :, qs] = (acc_ref[hh, :, qs] * alpha
                                  + jnp.dot(vt_ref[0, hh, :, 0:rows], pt, preferred_element_type=F32))
            m_ref[hh, :, qs] = m_new

    parity = p_id % 2
    cur_diag = last
    next_diag = jnp.logical_and(jnp.logical_not(cur_diag), qi_ref[nxt] == kj_ref[nxt])
    plain = jnp.logical_not(jnp.logical_or(cur_diag, next_diag))
    for slot in range(2):
        here = parity == slot
        pl.when(jnp.logical_and(here, cur_diag))(functools.partial(step, True, False, slot))
        pl.when(jnp.logical_and(here, next_diag))(functools.partial(step, False, True, slot))
        pl.when(jnp.logical_and(here, plain))(functools.partial(step, False, False, slot))

    @pl.when(last)
    def _():
        outs = []
        for hh in range(n_heads):
            a = acc_ref[hh]
            outs.append(a[0:V_DIM, :] / a[V_DIM:V_DIM + 1, :])
        o_ref[0] = jnp.concatenate(outs, axis=0).T.astype(o_ref.dtype)


def _band_bias(T, W):
    d = (jnp.arange(W + 1)[:, None, None] * T + jnp.arange(T)[None, None, :] - jnp.arange(T)[None, :, None])
    cw = jnp.zeros(d.shape, F32)
    for window, dil in DIL_PATTERNS:
        cw = cw + ((d >= 0) & (d <= window) & (d % dil == 0)).astype(F32)
    return jnp.log2(cw)


def _causal_attention(q, k, vt, T=FLASH_TILE):
    B, G, S, GW = q.shape
    g = GW // LANES
    v_rows = vt.shape[2]
    nq = S // T
    pairs = [(i, j) for i in range(nq) for j in range(i + 1)]
    n = len(pairs)
    qi = jnp.asarray(np.array([p[0] for p in pairs], np.int32))
    kj = jnp.asarray(np.array([p[1] for p in pairs], np.int32))
    nxt = lambda p: jnp.minimum(p + 1, n - 1)
    in_specs = [pl.BlockSpec((1, 1, T, GW), lambda b, h, p, qi, kj: (b, h, qi[nxt(p)], 0)),
                pl.BlockSpec((1, 1, T, GW), lambda b, h, p, qi, kj: (b, h, kj[nxt(p)], 0)),
                pl.BlockSpec((1, 1, T, GW), lambda b, h, p, qi, kj: (b, h, qi[0], 0)),
                pl.BlockSpec((1, 1, T, GW), lambda b, h, p, qi, kj: (b, h, kj[0], 0)),
                pl.BlockSpec((1, g, v_rows, T), lambda b, h, p, qi, kj: (b, h, 0, kj[p]))]
    return pl.pallas_call(
        functools.partial(_flash_kernel, T=T, n_pairs=n),
        grid_spec=pltpu.PrefetchScalarGridSpec(
            num_scalar_prefetch=2,
            grid=(B, G, n),
            in_specs=in_specs,
            out_specs=pl.BlockSpec((1, T, g * V_DIM), lambda b, h, p, qi, kj: (b, qi[p], h)),
            scratch_shapes=[pltpu.VMEM((g, 1, T), F32), pltpu.VMEM((g, v_rows, T), F32),
                            pltpu.VMEM((2, g, T, T), F32)]),
        out_shape=jax.ShapeDtypeStruct((B, S, G * g * V_DIM), BF16),
        compiler_params=_cparams(("arbitrary", "arbitrary", "arbitrary"), 56),
        name="flash_causal",
    )(qi, kj, q, k, q, k, vt)


def _band_kernel(qc_ref, qn_ref, kn_ref, *rest, T, W):
    k_refs, vt_refs = rest[:W + 1], rest[W + 1:2 * W + 2]
    bias_ref, o_ref, m_ref, acc_ref, s_ref = rest[2 * W + 2:]
    i = pl.program_id(2)
    units = [(hh, j) for hh in range(2) for j in range(T // MXU_COLS)]

    def rows_of(d, j):
        if d == 0:
            return 0, (j + 1) * MXU_COLS
        if d == W:
            return j * MXU_COLS, T
        return 0, T

    def produce(q_ref, k_ref, d, hh, j, bias_idx):
        lo, hi = rows_of(d, j)
        qs = slice(j * MXU_COLS, (j + 1) * MXU_COLS)
        q = q_ref[0, 0, qs, hh * LANES:(hh + 1) * LANES]
        k = k_ref[0, 0, lo:hi, hh * LANES:(hh + 1) * LANES]
        s = lax.dot_general(k, q, (((1,), (1,)), ((), ())), preferred_element_type=F32)
        s_ref[d, hh, lo:hi, qs] = s + bias_ref[bias_idx, lo:hi, qs]

    def consume(d, hh, j):
        lo, hi = rows_of(d, j)
        qs = slice(j * MXU_COLS, (j + 1) * MXU_COLS)
        st = s_ref[d, hh, lo:hi, qs]
        m_prev = m_ref[hh, :, qs]
        m_new = jnp.maximum(m_prev, jnp.max(st, axis=0, keepdims=True))
        alpha = jnp.exp2(m_prev - m_new)
        pt = jnp.exp2(st - m_new).astype(BF16)
        acc_ref[hh, :, qs] = (acc_ref[hh, :, qs] * alpha
                              + jnp.dot(vt_refs[d][0, hh, :, lo:hi], pt, preferred_element_type=F32))
        m_ref[hh, :, qs] = m_new

    @pl.when(i == 0)
    def _():
        for hh, j in units:
            produce(qc_ref, k_refs[0], 0, hh, j, 0)

    m_ref[...] = jnp.full(m_ref.shape, NEG_INF, F32)
    acc_ref[...] = jnp.zeros(acc_ref.shape, F32)
    for d in range(W + 1):
        for hh, j in units:
            if d < W:
                produce(qc_ref, k_refs[d + 1], d + 1, hh, j, jnp.where(i >= d + 1, d + 1, W + 1))
            else:
                produce(qn_ref, kn_ref, 0, hh, j, 0)
            consume(d, hh, j)

    outs = []
    for hh in range(2):
        a = acc_ref[hh]
        outs.append(a[0:V_DIM, :] / a[V_DIM:V_DIM + 1, :])
    o_ref[0] = jnp.concatenate(outs, axis=0).T.astype(o_ref.dtype)


def _band_attention(q, k, vt, T=FLASH_TILE):
    B, HP, S, _ = q.shape
    v_rows = vt.shape[2]
    n_q = S // T
    W = DIL_MAX_WINDOW // T
    bias = jnp.concatenate([_band_bias(T, W), jnp.full((1, T, T), NEG_INF, F32)], axis=0)
    qk_block = lambda f: pl.BlockSpec((1, 1, T, 2 * LANES), lambda b, h, i: (b, h, f(i), 0))
    vt_block = lambda f: pl.BlockSpec((1, 2, v_rows, T), lambda b, h, i: (b, h, 0, f(i)))
    back = lambda d: (lambda i: jnp.maximum(i - d, 0))
    nxt = lambda i: jnp.minimum(i + 1, n_q - 1)
    in_specs = ([qk_block(lambda i: i), qk_block(nxt), qk_block(nxt)]
                + [qk_block(back(d)) for d in range(W + 1)]
                + [vt_block(back(d)) for d in range(W + 1)]
                + [_const_spec((W + 2, T, T), lambda b, h, i: (0, 0, 0))])
    return pl.pallas_call(
        functools.partial(_band_kernel, T=T, W=W),
        grid=(B, HP, n_q),
        in_specs=in_specs,
        out_specs=pl.BlockSpec((1, T, LANES), lambda b, h, i: (b, i, h)),
        out_shape=jax.ShapeDtypeStruct((B, S, 2 * HP * V_DIM), BF16),
        scratch_shapes=[pltpu.VMEM((2, 1, T), F32), pltpu.VMEM((2, v_rows, T), F32),
                        pltpu.VMEM((W + 1, 2, T, T), F32)],
        compiler_params=_cparams(("arbitrary", "arbitrary", "arbitrary"), 58),
        name="flash_band",
    )(q, q, k, *([k] * (W + 1)), *([vt] * (W + 1)), bias)


def _mlstm_kernel(qk_ref, vt_ref, o_ref, if_ref, cw_ref, cb_ref, gb_ref, hg_ref, y_ref,
                  xs_ref, cn_ref, m_ref, *, Lc):
    H, DK, DV = MLSTM_HEADS, MLSTM_QK, MLSTM_V
    HALO = 8

    @pl.when(pl.program_id(1) == 0)
    def _():
        xs_ref[0:HALO, :] = jnp.zeros((HALO, 2 * H * DK), F32)
        cn_ref[...] = jnp.zeros(cn_ref.shape, F32)
        m_ref[...] = jnp.zeros(m_ref.shape, F32)

    x = qk_ref[...]
    xs_ref[HALO:HALO + Lc, :] = x
    y = cb_ref[0]
    for j in range(CONV_K):
        y = y + cw_ref[0, j:j + 1, :] * xs_ref[pl.ds(HALO - (CONV_K - 1) + j, Lc), :]
    xs_ref[0:HALO, :] = x[Lc - HALO:, :]
    qk = y * _sigmoid(y)
    q_t = qk[:, :H * DK].T.astype(BF16)
    k_all = (qk[:, H * DK:] * (DK ** -0.5)).astype(BF16)

    g = if_ref[...] + gb_ref[0]
    logf = jnp.minimum(g, 0.0) - jnp.log(1.0 + jnp.exp(-jnp.abs(g)))
    src = lax.broadcasted_iota(jnp.int32, (Lc, Lc), 0)
    dst = lax.broadcasted_iota(jnp.int32, (Lc, Lc), 1)
    tri = jnp.where(dst <= src, 1.0, 0.0).astype(F32)
    bcum = jnp.dot(tri, logf, preferred_element_type=F32, precision=HIGHEST)
    causal = src <= dst
    g_t = g.T
    bcum_t = bcum.T

    for h in range(H):
        b_row = bcum_t[H + h:H + h + 1, :]
        w_col = g[:, h:h + 1] - bcum[:, H + h:H + h + 1]
        w_row = g_t[h:h + 1, :] - b_row
        m_prev = m_ref[h:h + 1, 0:1]
        dlog = jnp.where(causal, w_col + b_row, NEG_INF)
        inter = b_row + m_prev
        m_t = jnp.maximum(inter, jnp.max(dlog, axis=0, keepdims=True))
        qh_t = q_t[h * DK:(h + 1) * DK, :]
        kh = k_all[:, h * DK:(h + 1) * DK]
        s_t = jnp.dot(kh, qh_t, preferred_element_type=F32)
        sc_t = (s_t * jnp.exp(dlog - m_t)).astype(BF16)
        decay = jnp.exp(inter - m_t)
        cn = cn_ref[h]
        vt = vt_ref[0, h]
        num = (decay * jnp.dot(cn.astype(BF16), qh_t, preferred_element_type=F32)
               + jnp.dot(vt, sc_t, preferred_element_type=F32))
        den = num[DV:DV + 1, :]
        hh = num[0:DV, :] / jnp.maximum(jnp.abs(den), jnp.exp(-m_t))
        hn = hh * lax.rsqrt(jnp.mean(hh * hh, axis=0, keepdims=True) + EPS)
        o_gate = _sigmoid(o_ref[:, h * DV:(h + 1) * DV].astype(F32))
        y_ref[:, h * DV:(h + 1) * DV] = (hn.T * hg_ref[0, :, h * DV:(h + 1) * DV] * o_gate).astype(y_ref.dtype)

        b_end = b_row[:, Lc - 1:Lc]
        wlog = b_end + w_row
        m_new = jnp.maximum(b_end + m_prev, jnp.max(wlog, axis=1, keepdims=True))
        cd = jnp.exp(b_end + m_prev - m_new)
        vw = (vt.astype(F32) * jnp.exp(wlog - m_new)).astype(BF16)
        cn_ref[h] = cd * cn + jnp.dot(vw, kh, preferred_element_type=F32)
        m_ref[h:h + 1, :] = jnp.broadcast_to(m_new, (1, LANES))


def _mlstm(mqk, mvt, mo, mif, B, S, layer, conv_w, conv_b, gate_b, head_gain, Lc=256):
    H, DK, DV = MLSTM_HEADS, MLSTM_QK, MLSTM_V
    nc = S // Lc
    row = lambda b, c: (b * nc + c, 0)
    VR = mvt.shape[2]
    return pl.pallas_call(
        functools.partial(_mlstm_kernel, Lc=Lc),
        grid=(B, nc),
        in_specs=[pl.BlockSpec((Lc, 2 * H * DK), row),
                  pl.BlockSpec((1, H, VR, Lc), lambda b, c: (b, 0, 0, c)),
                  pl.BlockSpec((Lc, H * DV), row),
                  pl.BlockSpec((Lc, LANES), row),
                  _const_spec((1, CONV_K, 2 * H * DK), lambda b, c: (layer, 0, 0)),
                  _const_spec((1, 1, 2 * H * DK), lambda b, c: (layer, 0, 0)),
                  _const_spec((1, 1, LANES), lambda b, c: (layer, 0, 0)),
                  _const_spec((1, 1, H * DV), lambda b, c: (layer, 0, 0))],
        out_specs=pl.BlockSpec((Lc, H * DV), row),
        out_shape=jax.ShapeDtypeStruct((B * S, H * DV), BF16),
        scratch_shapes=[pltpu.VMEM((Lc + 8, 2 * H * DK), F32), pltpu.VMEM((H, VR, DK), F32),
                        pltpu.VMEM((8, LANES), F32)],
        compiler_params=_cparams(("arbitrary", "arbitrary"), 32),
        name="mlstm",
    )(mqk, mvt, mo, mif, conv_w, conv_b, gate_b, head_gain)


def _merge_kernel(ya_ref, yb_ref, yc_ref, g_ref, x_ref, ga_ref, wa_ref, wb_ref, wc_ref, wo_ref, out_ref):
    def branch(y_ref, w_ref, c):
        gate = _sigmoid(g_ref[:, c * D_MODEL:(c + 1) * D_MODEL].astype(F32))
        return gate * jnp.dot(y_ref[...], w_ref[0], preferred_element_type=F32)

    merged = branch(ya_ref, wa_ref, 0) + branch(yb_ref, wb_ref, 1) + branch(yc_ref, wc_ref, 2)
    y = jnp.dot(merged.astype(BF16), wo_ref[0], preferred_element_type=F32)
    out_ref[...] = x_ref[...] + ga_ref[0] * y


def _merge(ya, yb, yc, gates, x2, S, layer, ga, wa, wb, wc, wo, tm=1024):
    N, D = x2.shape
    per_b = S // tm
    row = lambda i: (i, 0)
    wsel = lambda i: (layer, 0, 0)
    return pl.pallas_call(
        _merge_kernel,
        grid=(N // tm,),
        in_specs=[pl.BlockSpec((tm, 512), row), pl.BlockSpec((tm, 512), row), pl.BlockSpec((tm, 512), row),
                  pl.BlockSpec((tm, N_BRANCH * D), row), pl.BlockSpec((tm, D), row),
                  pl.BlockSpec((1, 1, D), lambda i: (i // per_b, 0, 0)),
                  _const_spec((1, 512, D), wsel), _const_spec((1, 512, D), wsel), _const_spec((1, 512, D), wsel),
                  _const_spec((1, D, D), wsel)],
        out_specs=pl.BlockSpec((tm, D), row),
        out_shape=jax.ShapeDtypeStruct((N, D), F32),
        compiler_params=_cparams(("parallel",), 48),
        name="merge",
    )(ya, yb, yc, gates, x2, ga, wa, wb, wc, wo)


def _ffn_kernel(x_ref, gain_ref, sc_ref, sh_ref, gf_ref, wr_ref, br_ref, wg_ref, wu_ref, wd_ref, out_ref):
    x = x_ref[...]
    tm = x.shape[0]
    ms = jnp.mean(x * x, axis=-1, keepdims=True)
    hf = x * lax.rsqrt(ms + EPS) * gain_ref[0] * (1.0 + sc_ref[0]) + sh_ref[0]

    hf_hi = hf.astype(BF16)
    hf_lo = (hf - hf_hi.astype(F32)).astype(BF16)
    wr = wr_ref[0]
    wr_hi = wr.astype(BF16)
    wr_split = jnp.concatenate([wr_hi, (wr - wr_hi.astype(F32)).astype(BF16)], axis=1)
    parts = (jnp.dot(hf_hi, wr_split, preferred_element_type=F32)
             + jnp.dot(hf_lo, wr_split, preferred_element_type=F32))
    logits = parts[:, :LANES] + parts[:, LANES:] + br_ref[0]
    lane_i = lax.broadcasted_iota(jnp.int32, (tm, LANES), 1)
    lane = lane_i.astype(F32)
    big = float(LANES)

    def first_argmax(z):
        zmax = jnp.max(z, axis=1, keepdims=True)
        return zmax, jnp.min(jnp.where(z == zmax, lane, big), axis=1, keepdims=True)

    lg = jnp.where(lane_i < N_GROUPS, logits, NEG_INF)
    gmax, g_idx = first_argmax(lg)
    p_sel = 1.0 / jnp.sum(jnp.exp(lg - gmax), axis=1, keepdims=True)
    lane_grp = ((lane_i - N_GROUPS) // EXPERTS_PER_GROUP).astype(F32)
    in_grp = (lane_i >= N_GROUPS) & (lane_i < N_GROUPS + N_EXPERTS) & (lane_grp == g_idx)
    le = jnp.where(in_grp, logits, NEG_INF)
    v1, i1 = first_argmax(le)
    le2 = jnp.where(lane == i1, NEG_INF, le)
    v2, i2 = first_argmax(le2)
    e2 = jnp.exp(v2 - v1)
    den = 1.0 + e2
    comb = jnp.where(lane == i1, (1.0 / den) * p_sel, 0.0) + jnp.where(lane == i2, (e2 / den) * p_sel, 0.0)

    hb = hf_hi
    acc = jnp.zeros((tm, D_MODEL), F32)
    for e in range(N_EXPERTS):
        gate = jnp.dot(hb, wg_ref[0, e], preferred_element_type=F32)
        up = jnp.dot(hb, wu_ref[0, e], preferred_element_type=F32)
        hid = gate * _sigmoid(gate) * up * comb[:, N_GROUPS + e:N_GROUPS + e + 1]
        acc = acc + jnp.dot(hid.astype(BF16), wd_ref[0, e], preferred_element_type=F32)
    out_ref[...] = x + gf_ref[0] * acc


def _ffn(x2, S, layer, gain, sc, sh, gf, wr, br, wg, wu, wd, tm=1024):
    N, D = x2.shape
    per_b = S // tm
    row = lambda i: (i, 0)
    bsel = lambda i: (i // per_b, 0, 0)
    wsel = lambda i: (layer, 0, 0, 0)
    return pl.pallas_call(
        _ffn_kernel,
        grid=(N // tm,),
        in_specs=[pl.BlockSpec((tm, D), row),
                  _const_spec((1, 1, D), lambda i: (layer, 0, 0)),
                  pl.BlockSpec((1, 1, D), bsel), pl.BlockSpec((1, 1, D), bsel), pl.BlockSpec((1, 1, D), bsel),
                  _const_spec((1, D, LANES), lambda i: (layer, 0, 0)),
                  _const_spec((1, 1, LANES), lambda i: (layer, 0, 0)),
                  _const_spec((1, N_EXPERTS, D, EXPERT_FF), wsel),
                  _const_spec((1, N_EXPERTS, D, EXPERT_FF), wsel),
                  _const_spec((1, N_EXPERTS, EXPERT_FF, D), wsel)],
        out_specs=pl.BlockSpec((tm, D), row),
        out_shape=jax.ShapeDtypeStruct((N, D), F32),
        compiler_params=_cparams(("parallel",), 56),
        name="ffn",
    )(x2, gain, sc, sh, gf, wr, br, wg, wu, wd)


def _pad_last(a, n):
    return jnp.pad(a, [(0, 0)] * (a.ndim - 1) + [(0, n - a.shape[-1])])


def _prep_w_in(w_in):
    pts = np.cumsum(IN_SIZES)[:-1].tolist()
    cq, ckv, kr, m_qk, m_v, m_o, m_i, m_f, d_q, d_k, d_v, gates = jnp.split(w_in, pts, axis=-1)
    seg_a = _pad_last(jnp.concatenate([cq, ckv, kr], axis=-1), 512)
    seg_if = _pad_last(jnp.concatenate([m_i, m_f], axis=-1), LANES)
    w = jnp.concatenate([seg_a, m_qk, m_v, m_o, seg_if,
                         d_q, d_k,
                         d_v, gates], axis=-1)
    assert w.shape[-1] == COL_END
    return w.astype(BF16)


def kernel(x, c, w_ada, b_ada, attn_norm, w_in, mla_q_norm, mla_kv_norm, mla_w_uq, mla_w_ukv, mla_q_gain, mla_k_gain, mlstm_conv_w, mlstm_conv_b, mlstm_b_i, mlstm_b_f, mlstm_head_gain, dil_q_gain, dil_k_gain, w_branch_a, w_branch_b, w_branch_c, w_out, ffn_norm, w_router_group, b_router_group, w_router_expert, b_router_expert, w_exp_gate, w_exp_up, w_exp_down):
    B, S, D = x.shape
    L = w_ada.shape[0]
    N = B * S
    assert D == D_MODEL and S % DIL_MAX_WINDOW == 0

    mod = _adaln(c, w_ada, b_ada)

    w_pad = _prep_w_in(w_in)
    half = MLA_ROPE // 2
    swap_rope = lambda a: jnp.concatenate(
        [jnp.zeros_like(a[..., :MLA_NOPE]), a[..., MLA_NOPE + half:], a[..., MLA_NOPE:MLA_NOPE + half]], axis=-1)
    uq = mla_w_uq.reshape(L, MLA_Q_LORA, MLA_HEADS, MLA_QK)
    wuq = jnp.concatenate([_pad_last(u, LANES).reshape(L, MLA_Q_LORA, MLA_HEADS * LANES)
                           for u in (uq, swap_rope(uq))], axis=-1).astype(BF16)
    ukv = mla_w_ukv.reshape(L, MLA_KV_LORA, MLA_HEADS, MLA_NOPE + MLA_V)
    wkv = jnp.concatenate([_pad_last(ukv[..., :MLA_NOPE], LANES).reshape(L, MLA_KV_LORA, MLA_HEADS * LANES),
                           ukv[..., MLA_NOPE:].reshape(L, MLA_KV_LORA, MLA_HEADS * MLA_V)], axis=-1).astype(BF16)
    rows = lambda a: a[:, None, :]
    qg = rows(_pad_last(mla_q_gain, LANES))
    qgs = rows(_pad_last(swap_rope(mla_q_gain), LANES))
    kg = rows(_pad_last(mla_k_gain, LANES))
    dqg = rows(_pad_last(dil_q_gain, LANES))
    dkg = rows(_pad_last(dil_k_gain, LANES))
    gate_b = rows(_pad_last(jnp.concatenate([mlstm_b_i, mlstm_b_f], axis=-1), LANES))
    tables = _rope_tables(S)
    wa, wb, wc, wo = (w.astype(BF16) for w in (w_branch_a, w_branch_b, w_branch_c, w_out))
    wr = _pad_last(jnp.concatenate([w_router_group, w_router_expert], axis=-1), LANES)
    br = rows(_pad_last(jnp.concatenate([b_router_group, b_router_expert], axis=-1), LANES))
    wg, wu, wd = (w.astype(BF16) for w in (w_exp_gate, w_exp_up, w_exp_down))
    attn_norm, ffn_norm, mla_q_norm, mla_kv_norm, mlstm_conv_b, mlstm_head_gain = (
        rows(a) for a in (attn_norm, ffn_norm, mla_q_norm, mla_kv_norm, mlstm_conv_b, mlstm_head_gain))

    x2 = x.reshape(N, D)
    for l in range(L):
        sh_a, sc_a, g_a, sh_f, sc_f, g_f = (mod[l, :, i] for i in range(6))
        oa, mqk, mv, mo, mif, qd, kd, vd, gates = _inproj(x2, S, l, attn_norm, sc_a, sh_a, w_pad, dqg, dkg)
        q, k, vt = _mla_prep(oa, S, l, mla_q_norm, mla_kv_norm, wuq, wkv, qg, qgs, kg, tables)
        ya = _causal_attention(q, k, vt)
        yb = _mlstm(mqk, mv, mo, mif, B, S, l, mlstm_conv_w, mlstm_conv_b, gate_b, mlstm_head_gain)
        yc = _band_attention(qd, kd, vd)
        x2 = _merge(ya.reshape(N, -1), yb, yc.reshape(N, -1), gates, x2, S, l, g_a, wa, wb, wc, wo)
        x2 = _ffn(x2, S, l, ffn_norm, sc_f, sh_f, g_f, wr, br, wg, wu, wd)
    return x2.reshape(B, S, D)
```

```python
import functools

import jax
import jax.numpy as jnp
import numpy as np
from jax import lax
from jax.experimental import pallas as pl
from jax.experimental.pallas import tpu as pltpu

F32 = jnp.float32
BF16 = jnp.bfloat16
HIGHEST = lax.Precision.HIGHEST
NEG_INF = float("-inf")

EPS = 1e-6
LANES = 128
MXU_COLS = 256
VMEM_BYTES_V7X = 64 * 1024 * 1024

D_MODEL = 1024
MLA_HEADS = 8
MLA_Q_LORA = 256
MLA_KV_LORA = 128
MLA_NOPE = 64
MLA_ROPE = 32
MLA_V = 64
MLA_QK = MLA_NOPE + MLA_ROPE
ROPE_THETA = 10000.0
MLSTM_HEADS = 4
MLSTM_QK = 64
MLSTM_V = 128
CONV_K = 4
DIL_HEADS = 8
DIL_HEAD_DIM = 64
DIL_PATTERNS = ((128, 1), (512, 4), (2048, 16))
DIL_WIDTH = DIL_HEADS * DIL_HEAD_DIM
N_GROUPS = 4
EXPERTS_PER_GROUP = 4
N_EXPERTS = N_GROUPS * EXPERTS_PER_GROUP
EXPERT_FF = 256
N_BRANCH = 3
IN_SIZES = (MLA_Q_LORA, MLA_KV_LORA, MLA_ROPE,
            2 * MLSTM_HEADS * MLSTM_QK, MLSTM_HEADS * MLSTM_V, MLSTM_HEADS * MLSTM_V, MLSTM_HEADS, MLSTM_HEADS,
            DIL_WIDTH, DIL_WIDTH, DIL_WIDTH, N_BRANCH * D_MODEL)

COL_A = 0
COL_MQK = 512
COL_MV = 1024
COL_MO = 1536
COL_MIF = 2048
COL_DQ = 2176
COL_DK = 2688
COL_DV = 3200
COL_G = 3712
COL_END = 6784

FLASH_TILE = 1024
MLA_HEADS_PER_STEP = 4
DIL_MAX_WINDOW = max(w for w, _ in DIL_PATTERNS)
V_DIM = 64
MLA_V_ROWS = V_DIM + 48
DIL_V_ROWS = V_DIM + 48
MLSTM_V_ROWS = MLSTM_V + 64
LOG2E = 1.4426950408889634
assert MLA_V == V_DIM and DIL_HEAD_DIM == V_DIM


def _store_v_tiles_t(v, vt_ref, dim=V_DIM):
    tm = v.shape[0]
    rows = vt_ref.shape[2]
    v_t = v.astype(BF16).T
    fill = (lax.broadcasted_iota(jnp.int32, (rows - dim, tm), 0) == 0).astype(BF16)
    for h in range(v.shape[1] // dim):
        vt_ref[0, h, 0:dim, :] = v_t[h * dim:(h + 1) * dim, :]
        vt_ref[0, h, dim:rows, :] = fill


def _cparams(sem, vmem_mb):
    limit = vmem_mb * 1024 * 1024
    assert limit < VMEM_BYTES_V7X
    return pltpu.CompilerParams(dimension_semantics=sem, vmem_limit_bytes=limit)


def _sigmoid(x):
    return 1.0 / (1.0 + jnp.exp(-x))


def _const_spec(shape, index_map):
    return pl.BlockSpec(shape, index_map, pipeline_mode=pl.Buffered(1))


def _adaln_kernel(c_ref, w_ref, b_ref, o_ref):
    c = c_ref[...]
    ca = c * _sigmoid(c)
    o_ref[0] = jnp.dot(ca, w_ref[0], preferred_element_type=F32, precision=HIGHEST) + b_ref[0]


def _adaln(c, w_ada, b_ada):
    L, D, D6 = w_ada.shape
    B = c.shape[0]
    rows = 8
    cp = jnp.zeros((rows, D), F32).at[:B].set(c)
    out = pl.pallas_call(
        _adaln_kernel,
        grid=(L, D6 // D),
        in_specs=[pl.BlockSpec((rows, D), lambda l, j: (0, 0)),
                  pl.BlockSpec((1, D, D), lambda l, j: (l, 0, j)),
                  pl.BlockSpec((1, 1, D), lambda l, j: (l, 0, j))],
        out_specs=pl.BlockSpec((1, rows, D), lambda l, j: (l, 0, j)),
        out_shape=jax.ShapeDtypeStruct((L, rows, D6), F32),
        compiler_params=_cparams(("parallel", "parallel"), 32),
        name="adaln",
    )(cp, w_ada, b_ada.reshape(L, 1, D6))
    return out[:, :B].reshape(L, B, D6 // D, 1, D)


def _inproj_kernel(x_ref, gain_ref, sc_ref, sh_ref, w_ref, dqg_ref, dkg_ref,
                   oa_ref, mqk_ref, mv_ref, mo_ref, mif_ref, qd_ref, kd_ref, vd_ref, g_ref):
    x = x_ref[...]
    ms = jnp.mean(x * x, axis=-1, keepdims=True)
    y = x * lax.rsqrt(ms + EPS) * gain_ref[0]
    u = (y * (1.0 + sc_ref[0]) + sh_ref[0]).astype(BF16)

    def proj(a, b):
        return jnp.dot(u, w_ref[0, :, a:b], preferred_element_type=F32)

    oa_ref[...] = proj(COL_A, COL_MQK)
    mqk_ref[...] = proj(COL_MQK, COL_MV)
    _store_v_tiles_t(proj(COL_MV, COL_MO), mv_ref, MLSTM_V)
    mo_ref[...] = proj(COL_MO, COL_MIF).astype(BF16)
    mif_ref[...] = proj(COL_MIF, COL_DQ)

    half = LANES // 2
    low = lax.broadcasted_iota(jnp.int32, (1, LANES), 1) < half

    def head_norm(z, g):
        for h in range(DIL_HEADS):
            pair = z[:, (h // 2) * LANES:(h // 2 + 1) * LANES]
            zh = jnp.where(low, pair if h % 2 == 0 else pltpu.roll(pair, half, 1), 0.0)
            r = lax.rsqrt(jnp.sum(zh * zh, axis=-1, keepdims=True) / DIL_HEAD_DIM + EPS)
            yield h, zh * r * g

    dq = proj(COL_DQ, COL_DK)
    for h, zn in head_norm(dq, dqg_ref[0]):
        qd_ref[0, h // 2, :, (h % 2) * LANES:(h % 2 + 1) * LANES] = (zn * (DIL_HEAD_DIM ** -0.5 * LOG2E)).astype(BF16)
    dk = proj(COL_DK, COL_DV)
    for h, zn in head_norm(dk, dkg_ref[0]):
        kd_ref[0, h // 2, :, (h % 2) * LANES:(h % 2 + 1) * LANES] = zn.astype(BF16)
    _store_v_tiles_t(proj(COL_DV, COL_G), vd_ref)
    for c in range(N_BRANCH):
        g_ref[:, c * D_MODEL:(c + 1) * D_MODEL] = proj(COL_G + c * D_MODEL, COL_G + (c + 1) * D_MODEL).astype(BF16)


def _inproj(x2, S, layer, gain, sc, sh, w_pad, dqg, dkg, tm=512):
    N, D = x2.shape
    per_b = S // tm
    row = lambda i: (i, 0)
    bsel = lambda i: (i // per_b, 0, 0)
    B = N // S
    outs = [(512, F32), (512, F32), "mvt", (512, BF16), (LANES, F32),
            "qk", "qk", "vt", (3072, BF16)]
    head_major = lambda i: (i // per_b, 0, i % per_b, 0)
    time_minor = lambda i: (i // per_b, 0, 0, i % per_b)
    special_spec = {"qk": pl.BlockSpec((1, DIL_HEADS // 2, tm, 2 * LANES), head_major),
                    "vt": pl.BlockSpec((1, DIL_HEADS, DIL_V_ROWS, tm), time_minor),
                    "mvt": pl.BlockSpec((1, MLSTM_HEADS, MLSTM_V_ROWS, tm), time_minor)}
    special_shape = {"qk": jax.ShapeDtypeStruct((B, DIL_HEADS // 2, S, 2 * LANES), BF16),
                     "vt": jax.ShapeDtypeStruct((B, DIL_HEADS, DIL_V_ROWS, S), BF16),
                     "mvt": jax.ShapeDtypeStruct((B, MLSTM_HEADS, MLSTM_V_ROWS, S), BF16)}
    return pl.pallas_call(
        _inproj_kernel,
        grid=(N // tm,),
        in_specs=[pl.BlockSpec((tm, D), row),
                  _const_spec((1, 1, D), lambda i: (layer, 0, 0)),
                  pl.BlockSpec((1, 1, D), bsel),
                  pl.BlockSpec((1, 1, D), bsel),
                  _const_spec((1, D, COL_END), lambda i: (layer, 0, 0)),
                  _const_spec((1, 1, LANES), lambda i: (layer, 0, 0)),
                  _const_spec((1, 1, LANES), lambda i: (layer, 0, 0))],
        out_specs=[special_spec[o] if isinstance(o, str) else pl.BlockSpec((tm, o[0]), row) for o in outs],
        out_shape=[special_shape[o] if isinstance(o, str) else jax.ShapeDtypeStruct((N, o[0]), o[1]) for o in outs],
        compiler_params=_cparams(("parallel",), 52),
        name="inproj",
    )(x2, gain, sc, sh, w_pad, dqg, dkg)


def _mla_prep_kernel(a_ref, qn_ref, kvn_ref, wuq_ref, wkv_ref, qg_ref, qgs_ref, kg_ref, cos_ref, sin_ref,
                     q_ref, k_ref, vt_ref):
    HW = MLA_HEADS * LANES
    a = a_ref[...]
    cq = a[:, :MLA_Q_LORA]
    ckv = a[:, MLA_Q_LORA:MLA_Q_LORA + MLA_KV_LORA]
    kr_tile = a[:, MLA_Q_LORA + MLA_KV_LORA:]

    def rms(z, g):
        return z * lax.rsqrt(jnp.mean(z * z, axis=-1, keepdims=True) + EPS) * g

    cqn = rms(cq, qn_ref[0]).astype(BF16)
    ckvn = rms(ckv, kvn_ref[0]).astype(BF16)
    qr = jnp.dot(cqn, wuq_ref[0], preferred_element_type=F32)
    kvr = jnp.dot(ckvn, wkv_ref[0], preferred_element_type=F32)
    _store_v_tiles_t(kvr[:, HW:], vt_ref)

    cosf, sins = cos_ref[...], sin_ref[...]
    q_scale = MLA_QK ** -0.5 * LOG2E
    gc = cosf * (qg_ref[0] * q_scale)
    gs = sins * (qgs_ref[0] * q_scale)
    kg = kg_ref[0]
    half = MLA_ROPE // 2
    krg = pltpu.roll(kr_tile, MLA_NOPE, 1) * kg
    kr_rot = krg * cosf + (pltpu.roll(krg, LANES - half, 1) + pltpu.roll(krg, half, 1)) * sins
    kr_ss = jnp.sum(kr_tile * kr_tile, axis=-1, keepdims=True)
    for h in range(MLA_HEADS):
        sl = slice(h * LANES, (h + 1) * LANES)
        grp, pair_sl = h // MLA_HEADS_PER_STEP, slice((h % MLA_HEADS_PER_STEP) * LANES,
                                                     (h % MLA_HEADS_PER_STEP + 1) * LANES)
        z = qr[:, sl]
        rq = lax.rsqrt(jnp.sum(z * z, axis=-1, keepdims=True) / MLA_QK + EPS)
        q_ref[0, grp, :, pair_sl] = ((z * gc + qr[:, HW + h * LANES:HW + (h + 1) * LANES] * gs) * rq).astype(BF16)
        zk = kvr[:, sl]
        rk = lax.rsqrt((jnp.sum(zk * zk, axis=-1, keepdims=True) + kr_ss) / MLA_QK + EPS)
        k_ref[0, grp, :, pair_sl] = ((zk * kg + kr_rot) * rk).astype(BF16)


def _rope_tables(S):
    half = MLA_ROPE // 2
    inv = ROPE_THETA ** (-jnp.arange(half, dtype=F32) / half)
    ang = jnp.arange(S).astype(F32)[:, None] * inv[None, :]
    cos, sin = jnp.cos(ang), jnp.sin(ang)
    one = jnp.ones((S, MLA_NOPE), F32)
    z = lambda n: jnp.zeros((S, n), F32)
    tail = LANES - MLA_QK
    cosf = jnp.concatenate([one, cos, cos, z(tail)], axis=1)
    sins = jnp.concatenate([z(MLA_NOPE), -sin, sin, z(tail)], axis=1)
    return cosf, sins


def _mla_prep(oa, S, layer, qn, kvn, wuq, wkv, qg, qgs, kg, tables, tm=1024):
    N = oa.shape[0]
    B = N // S
    per_b = S // tm
    row = lambda i: (i, 0)
    pos = lambda i: (i % per_b, 0)
    HW = MLA_HEADS * LANES
    G, GW = MLA_HEADS // MLA_HEADS_PER_STEP, MLA_HEADS_PER_STEP * LANES
    return pl.pallas_call(
        _mla_prep_kernel,
        grid=(N // tm,),
        in_specs=[pl.BlockSpec((tm, 512), row),
                  _const_spec((1, 1, MLA_Q_LORA), lambda i: (layer, 0, 0)),
                  _const_spec((1, 1, MLA_KV_LORA), lambda i: (layer, 0, 0)),
                  _const_spec((1, MLA_Q_LORA, 2 * HW), lambda i: (layer, 0, 0)),
                  _const_spec((1, MLA_KV_LORA, HW + MLA_HEADS * MLA_V), lambda i: (layer, 0, 0)),
                  _const_spec((1, 1, LANES), lambda i: (layer, 0, 0)),
                  _const_spec((1, 1, LANES), lambda i: (layer, 0, 0)),
                  _const_spec((1, 1, LANES), lambda i: (layer, 0, 0)),
                  pl.BlockSpec((tm, LANES), pos),
                  pl.BlockSpec((tm, LANES), pos)],
        out_specs=[pl.BlockSpec((1, G, tm, GW), lambda i: (i // per_b, 0, i % per_b, 0)),
                   pl.BlockSpec((1, G, tm, GW), lambda i: (i // per_b, 0, i % per_b, 0)),
                   pl.BlockSpec((1, MLA_HEADS, MLA_V_ROWS, tm), lambda i: (i // per_b, 0, 0, i % per_b))],
        out_shape=[jax.ShapeDtypeStruct((B, G, S, GW), BF16),
                   jax.ShapeDtypeStruct((B, G, S, GW), BF16),
                   jax.ShapeDtypeStruct((B, MLA_HEADS, MLA_V_ROWS, S), BF16)],
        compiler_params=_cparams(("parallel",), 32),
        name="mla_prep",
    )(oa, qn, kvn, wuq, wkv, qg, qgs, kg, *tables)


def _flash_kernel(qi_ref, kj_ref, qn_ref, kn_ref, q0_ref, k0_ref, vt_ref, o_ref, m_ref, acc_ref, s_ref, *, T, n_pairs):
    p_id = pl.program_id(2)
    qi = qi_ref[p_id]
    kj = kj_ref[p_id]
    first = kj == 0
    last = kj == qi

    n_heads = m_ref.shape[0]
    units = [(hh, slice(j * MXU_COLS, (j + 1) * MXU_COLS)) for hh in range(n_heads) for j in range(T // MXU_COLS)]

    def scores(q_ref, k_ref, hh, qs, rows=T):
        q = q_ref[0, 0, qs, hh * LANES:(hh + 1) * LANES]
        k = k_ref[0, 0, 0:rows, hh * LANES:(hh + 1) * LANES]
        return lax.dot_general(k, q, (((1,), (1,)), ((), ())), preferred_element_type=F32)

    @pl.when(p_id == 0)
    def _():
        for hh, qs in units:
            s_ref[0, hh, :, qs] = scores(q0_ref, k0_ref, hh, qs)

    @pl.when(first)
    def _():
        m_ref[...] = jnp.full(m_ref.shape, NEG_INF, F32)
        acc_ref[...] = jnp.zeros(acc_ref.shape, F32)

    nxt = jnp.minimum(p_id + 1, n_pairs - 1)

    def step(cur_diag, next_diag, slot):
        for hh, qs in units:
            rows_next = qs.stop if next_diag else T
            s_ref[1 - slot, hh, 0:rows_next, qs] = scores(qn_ref, kn_ref, hh, qs, rows_next)
            rows = qs.stop if cur_diag else T
            st = s_ref[slot, hh, 0:rows, qs]
            if cur_diag:
                key = lax.broadcasted_iota(jnp.int32, (rows, MXU_COLS), 0)
                qry = lax.broadcasted_iota(jnp.int32, (rows, MXU_COLS), 1) + qs.start
                st = jnp.where(key <= qry, st, NEG_INF)
            m_prev = m_ref[hh, :, qs]
            m_new = jnp.maximum(m_prev, jnp.max(st, axis=0, keepdims=True))
            alpha = jnp.exp2(m_prev - m_new)
            pt = jnp.exp2(st - m_new).astype(BF16)
            acc_ref[hh, :, qs] = (acc_ref[hh, :, qs] * alpha
                                  + jnp.dot(vt_ref[0, hh, :, 0:rows], pt, preferred_element_type=F32))
            m_ref[hh, :, qs] = m_new

    parity = p_id % 2
    cur_diag = last
    next_diag = jnp.logical_and(jnp.logical_not(cur_diag), qi_ref[nxt] == kj_ref[nxt])
    plain = jnp.logical_not(jnp.logical_or(cur_diag, next_diag))
    for slot in range(2):
        here = parity == slot
        pl.when(jnp.logical_and(here, cur_diag))(functools.partial(step, True, False, slot))
        pl.when(jnp.logical_and(here, next_diag))(functools.partial(step, False, True, slot))
        pl.when(jnp.logical_and(here, plain))(functools.partial(step, False, False, slot))

    @pl.when(last)
    def _():
        outs = []
        for hh in range(n_heads):
            a = acc_ref[hh]
            outs.append(a[0:V_DIM, :] / a[V_DIM:V_DIM + 1, :])
        o_ref[0] = jnp.concatenate(outs, axis=0).T.astype(o_ref.dtype)


def _band_bias(T, W):
    d = (jnp.arange(W + 1)[:, None, None] * T + jnp.arange(T)[None, None, :] - jnp.arange(T)[None, :, None])
    cw = jnp.zeros(d.shape, F32)
    for window, dil in DIL_PATTERNS:
        cw = cw + ((d >= 0) & (d <= window) & (d % dil == 0)).astype(F32)
    return jnp.log2(cw)


def _causal_attention(q, k, vt, T=FLASH_TILE):
    B, G, S, GW = q.shape
    g = GW // LANES
    v_rows = vt.shape[2]
    nq = S // T
    pairs = [(i, j) for i in range(nq) for j in range(i + 1)]
    n = len(pairs)
    qi = jnp.asarray(np.array([p[0] for p in pairs], np.int32))
    kj = jnp.asarray(np.array([p[1] for p in pairs], np.int32))
    nxt = lambda p: jnp.minimum(p + 1, n - 1)
    in_specs = [pl.BlockSpec((1, 1, T, GW), lambda b, h, p, qi, kj: (b, h, qi[nxt(p)], 0)),
                pl.BlockSpec((1, 1, T, GW), lambda b, h, p, qi, kj: (b, h, kj[nxt(p)], 0)),
                pl.BlockSpec((1, 1, T, GW), lambda b, h, p, qi, kj: (b, h, qi[0], 0)),
                pl.BlockSpec((1, 1, T, GW), lambda b, h, p, qi, kj: (b, h, kj[0], 0)),
                pl.BlockSpec((1, g, v_rows, T), lambda b, h, p, qi, kj: (b, h, 0, kj[p]))]
    return pl.pallas_call(
        functools.partial(_flash_kernel, T=T, n_pairs=n),
        grid_spec=pltpu.PrefetchScalarGridSpec(
            num_scalar_prefetch=2,
            grid=(B, G, n),
            in_specs=in_specs,
            out_specs=pl.BlockSpec((1, T, g * V_DIM), lambda b, h, p, qi, kj: (b, qi[p], h)),
            scratch_shapes=[pltpu.VMEM((g, 1, T), F32), pltpu.VMEM((g, v_rows, T), F32),
                            pltpu.VMEM((2, g, T, T), F32)]),
        out_shape=jax.ShapeDtypeStruct((B, S, G * g * V_DIM), BF16),
        compiler_params=_cparams(("arbitrary", "arbitrary", "arbitrary"), 56),
        name="flash_causal",
    )(qi, kj, q, k, q, k, vt)


def _band_kernel(qc_ref, qn_ref, kn_ref, *rest, T, W):
    k_refs, vt_refs = rest[:W + 1], rest[W + 1:2 * W + 2]
    bias_ref, o_ref, m_ref, acc_ref, s_ref = rest[2 * W + 2:]
    i = pl.program_id(2)
    units = [(hh, j) for hh in range(2) for j in range(T // MXU_COLS)]

    def rows_of(d, j):
        if d == 0:
            return 0, (j + 1) * MXU_COLS
        if d == W:
            return j * MXU_COLS, T
        return 0, T

    def produce(q_ref, k_ref, d, hh, j, bias_idx):
        lo, hi = rows_of(d, j)
        qs = slice(j * MXU_COLS, (j + 1) * MXU_COLS)
        q = q_ref[0, 0, qs, hh * LANES:(hh + 1) * LANES]
        k = k_ref[0, 0, lo:hi, hh * LANES:(hh + 1) * LANES]
        s = lax.dot_general(k, q, (((1,), (1,)), ((), ())), preferred_element_type=F32)
        s_ref[d, hh, lo:hi, qs] = s + bias_ref[bias_idx, lo:hi, qs]

    def consume(d, hh, j):
        lo, hi = rows_of(d, j)
        qs = slice(j * MXU_COLS, (j + 1) * MXU_COLS)
        st = s_ref[d, hh, lo:hi, qs]
        m_prev = m_ref[hh, :, qs]
        m_new = jnp.maximum(m_prev, jnp.max(st, axis=0, keepdims=True))
        alpha = jnp.exp2(m_prev - m_new)
        pt = jnp.exp2(st - m_new).astype(BF16)
        acc_ref[hh, :, qs] = (acc_ref[hh, :, qs] * alpha
                              + jnp.dot(vt_refs[d][0, hh, :, lo:hi], pt, preferred_element_type=F32))
        m_ref[hh, :, qs] = m_new

    @pl.when(i == 0)
    def _():
        for hh, j in units:
            produce(qc_ref, k_refs[0], 0, hh, j, 0)

    m_ref[...] = jnp.full(m_ref.shape, NEG_INF, F32)
    acc_ref[...] = jnp.zeros(acc_ref.shape, F32)
    for d in range(W + 1):
        for hh, j in units:
            if d < W:
                produce(qc_ref, k_refs[d + 1], d + 1, hh, j, jnp.where(i >= d + 1, d + 1, W + 1))
            else:
                produce(qn_ref, kn_ref, 0, hh, j, 0)
            consume(d, hh, j)

    outs = []
    for hh in range(2):
        a = acc_ref[hh]
        outs.append(a[0:V_DIM, :] / a[V_DIM:V_DIM + 1, :])
    o_ref[0] = jnp.concatenate(outs, axis=0).T.astype(o_ref.dtype)


def _band_attention(q, k, vt, T=FLASH_TILE):
    B, HP, S, _ = q.shape
    v_rows = vt.shape[2]
    n_q = S // T
    W = DIL_MAX_WINDOW // T
    bias = jnp.concatenate([_band_bias(T, W), jnp.full((1, T, T), NEG_INF, F32)], axis=0)
    qk_block = lambda f: pl.BlockSpec((1, 1, T, 2 * LANES), lambda b, h, i: (b, h, f(i), 0))
    vt_block = lambda f: pl.BlockSpec((1, 2, v_rows, T), lambda b, h, i: (b, h, 0, f(i)))
    back = lambda d: (lambda i: jnp.maximum(i - d, 0))
    nxt = lambda i: jnp.minimum(i + 1, n_q - 1)
    in_specs = ([qk_block(lambda i: i), qk_block(nxt), qk_block(nxt)]
                + [qk_block(back(d)) for d in range(W + 1)]
                + [vt_block(back(d)) for d in range(W + 1)]
                + [_const_spec((W + 2, T, T), lambda b, h, i: (0, 0, 0))])
    return pl.pallas_call(
        functools.partial(_band_kernel, T=T, W=W),
        grid=(B, HP, n_q),
        in_specs=in_specs,
        out_specs=pl.BlockSpec((1, T, LANES), lambda b, h, i: (b, i, h)),
        out_shape=jax.ShapeDtypeStruct((B, S, 2 * HP * V_DIM), BF16),
        scratch_shapes=[pltpu.VMEM((2, 1, T), F32), pltpu.VMEM((2, v_rows, T), F32),
                        pltpu.VMEM((W + 1, 2, T, T), F32)],
        compiler_params=_cparams(("arbitrary", "arbitrary", "arbitrary"), 58),
        name="flash_band",
    )(q, q, k, *([k] * (W + 1)), *([vt] * (W + 1)), bias)


def _mlstm_kernel(qk_ref, vt_ref, o_ref, if_ref, cw_ref, cb_ref, gb_ref, hg_ref, y_ref,
                  xs_ref, cn_ref, m_ref, *, Lc):
    H, DK, DV = MLSTM_HEADS, MLSTM_QK, MLSTM_V
    HALO = 8

    @pl.when(pl.program_id(1) == 0)
    def _():
        xs_ref[0:HALO, :] = jnp.zeros((HALO, 2 * H * DK), F32)
        cn_ref[...] = jnp.zeros(cn_ref.shape, F32)
        m_ref[...] = jnp.zeros(m_ref.shape, F32)

    x = qk_ref[...]
    xs_ref[HALO:HALO + Lc, :] = x
    y = cb_ref[0]
    for j in range(CONV_K):
        y = y + cw_ref[0, j:j + 1, :] * xs_ref[pl.ds(HALO - (CONV_K - 1) + j, Lc), :]
    xs_ref[0:HALO, :] = x[Lc - HALO:, :]
    qk = y * _sigmoid(y)
    q_t = qk[:, :H * DK].T.astype(BF16)
    k_all = (qk[:, H * DK:] * (DK ** -0.5)).astype(BF16)

    g = if_ref[...] + gb_ref[0]
    logf = jnp.minimum(g, 0.0) - jnp.log(1.0 + jnp.exp(-jnp.abs(g)))
    src = lax.broadcasted_iota(jnp.int32, (Lc, Lc), 0)
    dst = lax.broadcasted_iota(jnp.int32, (Lc, Lc), 1)
    tri = jnp.where(dst <= src, 1.0, 0.0).astype(F32)
    bcum = jnp.dot(tri, logf, preferred_element_type=F32, precision=HIGHEST)
    causal = src <= dst
    g_t = g.T
    bcum_t = bcum.T

    for h in range(H):
        b_row = bcum_t[H + h:H + h + 1, :]
        w_col = g[:, h:h + 1] - bcum[:, H + h:H + h + 1]
        w_row = g_t[h:h + 1, :] - b_row
        m_prev = m_ref[h:h + 1, 0:1]
        dlog = jnp.where(causal, w_col + b_row, NEG_INF)
        inter = b_row + m_prev
        m_t = jnp.maximum(inter, jnp.max(dlog, axis=0, keepdims=True))
        qh_t = q_t[h * DK:(h + 1) * DK, :]
        kh = k_all[:, h * DK:(h + 1) * DK]
        s_t = jnp.dot(kh, qh_t, preferred_element_type=F32)
        sc_t = (s_t * jnp.exp(dlog - m_t)).astype(BF16)
        decay = jnp.exp(inter - m_t)
        cn = cn_ref[h]
        vt = vt_ref[0, h]
        num = (decay * jnp.dot(cn.astype(BF16), qh_t, preferred_element_type=F32)
               + jnp.dot(vt, sc_t, preferred_element_type=F32))
        den = num[DV:DV + 1, :]
        hh = num[0:DV, :] / jnp.maximum(jnp.abs(den), jnp.exp(-m_t))
        hn = hh * lax.rsqrt(jnp.mean(hh * hh, axis=0, keepdims=True) + EPS)
        o_gate = _sigmoid(o_ref[:, h * DV:(h + 1) * DV].astype(F32))
        y_ref[:, h * DV:(h + 1) * DV] = (hn.T * hg_ref[0, :, h * DV:(h + 1) * DV] * o_gate).astype(y_ref.dtype)

        b_end = b_row[:, Lc - 1:Lc]
        wlog = b_end + w_row
        m_new = jnp.maximum(b_end + m_prev, jnp.max(wlog, axis=1, keepdims=True))
        cd = jnp.exp(b_end + m_prev - m_new)
        vw = (vt.astype(F32) * jnp.exp(wlog - m_new)).astype(BF16)
        cn_ref[h] = cd * cn + jnp.dot(vw, kh, preferred_element_type=F32)
        m_ref[h:h + 1, :] = jnp.broadcast_to(m_new, (1, LANES))


def _mlstm(mqk, mvt, mo, mif, B, S, layer, conv_w, conv_b, gate_b, head_gain, Lc=256):
    H, DK, DV = MLSTM_HEADS, MLSTM_QK, MLSTM_V
    nc = S // Lc
    row = lambda b, c: (b * nc + c, 0)
    VR = mvt.shape[2]
    return pl.pallas_call(
        functools.partial(_mlstm_kernel, Lc=Lc),
        grid=(B, nc),
        in_specs=[pl.BlockSpec((Lc, 2 * H * DK), row),
                  pl.BlockSpec((1, H, VR, Lc), lambda b, c: (b, 0, 0, c)),
                  pl.BlockSpec((Lc, H * DV), row),
                  pl.BlockSpec((Lc, LANES), row),
                  _const_spec((1, CONV_K, 2 * H * DK), lambda b, c: (layer, 0, 0)),
                  _const_spec((1, 1, 2 * H * DK), lambda b, c: (layer, 0, 0)),
                  _const_spec((1, 1, LANES), lambda b, c: (layer, 0, 0)),
                  _const_spec((1, 1, H * DV), lambda b, c: (layer, 0, 0))],
        out_specs=pl.BlockSpec((Lc, H * DV), row),
        out_shape=jax.ShapeDtypeStruct((B * S, H * DV), BF16),
        scratch_shapes=[pltpu.VMEM((Lc + 8, 2 * H * DK), F32), pltpu.VMEM((H, VR, DK), F32),
                        pltpu.VMEM((8, LANES), F32)],
        compiler_params=_cparams(("arbitrary", "arbitrary"), 32),
        name="mlstm",
    )(mqk, mvt, mo, mif, conv_w, conv_b, gate_b, head_gain)


def _merged_residual(ya_ref, yb_ref, yc_ref, g_ref, x_ref, ga_ref, wa_ref, wb_ref, wc_ref, wo_ref):
    def branch(y_ref, w_ref, c):
        gate = _sigmoid(g_ref[:, c * D_MODEL:(c + 1) * D_MODEL].astype(F32))
        return gate * jnp.dot(y_ref[...], w_ref[0], preferred_element_type=F32)

    merged = branch(ya_ref, wa_ref, 0) + branch(yb_ref, wb_ref, 1) + branch(yc_ref, wc_ref, 2)
    y = jnp.dot(merged.astype(BF16), wo_ref[0], preferred_element_type=F32)
    return x_ref[...] + ga_ref[0] * y


def _merge_ffn_kernel(ya_ref, yb_ref, yc_ref, g_ref, x_ref, ga_ref, wa_ref, wb_ref, wc_ref, wo_ref, *ffn_refs):
    x = _merged_residual(ya_ref, yb_ref, yc_ref, g_ref, x_ref, ga_ref, wa_ref, wb_ref, wc_ref, wo_ref)
    _ffn_body(x, *ffn_refs)


def _ffn_body(x, gain_ref, sc_ref, sh_ref, gf_ref, wr_ref, br_ref, wg_ref, wu_ref, wd_ref, out_ref):
    tm = x.shape[0]
    ms = jnp.mean(x * x, axis=-1, keepdims=True)
    hf = x * lax.rsqrt(ms + EPS) * gain_ref[0] * (1.0 + sc_ref[0]) + sh_ref[0]

    hf_hi = hf.astype(BF16)
    hf_lo = (hf - hf_hi.astype(F32)).astype(BF16)
    wr = wr_ref[0]
    wr_hi = wr.astype(BF16)
    wr_split = jnp.concatenate([wr_hi, (wr - wr_hi.astype(F32)).astype(BF16)], axis=1)
    parts = (jnp.dot(hf_hi, wr_split, preferred_element_type=F32)
             + jnp.dot(hf_lo, wr_split, preferred_element_type=F32))
    logits = parts[:, :LANES] + parts[:, LANES:] + br_ref[0]
    lane_i = lax.broadcasted_iota(jnp.int32, (tm, LANES), 1)
    lane = lane_i.astype(F32)
    big = float(LANES)

    def first_argmax(z):
        zmax = jnp.max(z, axis=1, keepdims=True)
        return zmax, jnp.min(jnp.where(z == zmax, lane, big), axis=1, keepdims=True)

    lg = jnp.where(lane_i < N_GROUPS, logits, NEG_INF)
    gmax, g_idx = first_argmax(lg)
    p_sel = 1.0 / jnp.sum(jnp.exp(lg - gmax), axis=1, keepdims=True)
    lane_grp = ((lane_i - N_GROUPS) // EXPERTS_PER_GROUP).astype(F32)
    in_grp = (lane_i >= N_GROUPS) & (lane_i < N_GROUPS + N_EXPERTS) & (lane_grp == g_idx)
    le = jnp.where(in_grp, logits, NEG_INF)
    v1, i1 = first_argmax(le)
    le2 = jnp.where(lane == i1, NEG_INF, le)
    v2, i2 = first_argmax(le2)
    e2 = jnp.exp(v2 - v1)
    den = 1.0 + e2
    comb = jnp.where(lane == i1, (1.0 / den) * p_sel, 0.0) + jnp.where(lane == i2, (e2 / den) * p_sel, 0.0)

    hb = hf_hi
    acc = jnp.zeros((tm, D_MODEL), F32)
    for e in range(N_EXPERTS):
        gate = jnp.dot(hb, wg_ref[0, e], preferred_element_type=F32)
        up = jnp.dot(hb, wu_ref[0, e], preferred_element_type=F32)
        hid = gate * _sigmoid(gate) * up * comb[:, N_GROUPS + e:N_GROUPS + e + 1]
        acc = acc + jnp.dot(hid.astype(BF16), wd_ref[0, e], preferred_element_type=F32)
    out_ref[...] = x + gf_ref[0] * acc


def _merge_ffn(ya, yb, yc, gates, x2, S, layer, ga, wa, wb, wc, wo, gain, sc, sh, gf, wr, br, wg, wu, wd, tm=512):
    N, D = x2.shape
    per_b = S // tm
    row = lambda i: (i, 0)
    bsel = lambda i: (i // per_b, 0, 0)
    wsel3 = lambda i: (layer, 0, 0)
    wsel = lambda i: (layer, 0, 0, 0)
    return pl.pallas_call(
        _merge_ffn_kernel,
        grid=(N // tm,),
        in_specs=[pl.BlockSpec((tm, 512), row), pl.BlockSpec((tm, 512), row), pl.BlockSpec((tm, 512), row),
                  pl.BlockSpec((tm, N_BRANCH * D), row), pl.BlockSpec((tm, D), row),
                  pl.BlockSpec((1, 1, D), bsel),
                  _const_spec((1, 512, D), wsel3), _const_spec((1, 512, D), wsel3), _const_spec((1, 512, D), wsel3),
                  _const_spec((1, D, D), wsel3),
                  _const_spec((1, 1, D), lambda i: (layer, 0, 0)),
                  pl.BlockSpec((1, 1, D), bsel), pl.BlockSpec((1, 1, D), bsel), pl.BlockSpec((1, 1, D), bsel),
                  _const_spec((1, D, LANES), lambda i: (layer, 0, 0)),
                  _const_spec((1, 1, LANES), lambda i: (layer, 0, 0)),
                  _const_spec((1, N_EXPERTS, D, EXPERT_FF), wsel),
                  _const_spec((1, N_EXPERTS, D, EXPERT_FF), wsel),
                  _const_spec((1, N_EXPERTS, EXPERT_FF, D), wsel)],
        out_specs=pl.BlockSpec((tm, D), row),
        out_shape=jax.ShapeDtypeStruct((N, D), F32),
        compiler_params=_cparams(("parallel",), 60),
        name="merge_ffn",
    )(ya, yb, yc, gates, x2, ga, wa, wb, wc, wo, gain, sc, sh, gf, wr, br, wg, wu, wd)


def _pad_last(a, n):
    return jnp.pad(a, [(0, 0)] * (a.ndim - 1) + [(0, n - a.shape[-1])])


def _prep_w_in(w_in):
    pts = np.cumsum(IN_SIZES)[:-1].tolist()
    cq, ckv, kr, m_qk, m_v, m_o, m_i, m_f, d_q, d_k, d_v, gates = jnp.split(w_in, pts, axis=-1)
    seg_a = _pad_last(jnp.concatenate([cq, ckv, kr], axis=-1), 512)
    seg_if = _pad_last(jnp.concatenate([m_i, m_f], axis=-1), LANES)
    w = jnp.concatenate([seg_a, m_qk, m_v, m_o, seg_if,
                         d_q, d_k,
                         d_v, gates], axis=-1)
    assert w.shape[-1] == COL_END
    return w.astype(BF16)


def kernel(x, c, w_ada, b_ada, attn_norm, w_in, mla_q_norm, mla_kv_norm, mla_w_uq, mla_w_ukv, mla_q_gain, mla_k_gain, mlstm_conv_w, mlstm_conv_b, mlstm_b_i, mlstm_b_f, mlstm_head_gain, dil_q_gain, dil_k_gain, w_branch_a, w_branch_b, w_branch_c, w_out, ffn_norm, w_router_group, b_router_group, w_router_expert, b_router_expert, w_exp_gate, w_exp_up, w_exp_down):
    B, S, D = x.shape
    L = w_ada.shape[0]
    N = B * S
    assert D == D_MODEL and S % DIL_MAX_WINDOW == 0

    mod = _adaln(c, w_ada, b_ada)

    w_pad = _prep_w_in(w_in)
    half = MLA_ROPE // 2
    swap_rope = lambda a: jnp.concatenate(
        [jnp.zeros_like(a[..., :MLA_NOPE]), a[..., MLA_NOPE + half:], a[..., MLA_NOPE:MLA_NOPE + half]], axis=-1)
    uq = mla_w_uq.reshape(L, MLA_Q_LORA, MLA_HEADS, MLA_QK)
    wuq = jnp.concatenate([_pad_last(u, LANES).reshape(L, MLA_Q_LORA, MLA_HEADS * LANES)
                           for u in (uq, swap_rope(uq))], axis=-1).astype(BF16)
    ukv = mla_w_ukv.reshape(L, MLA_KV_LORA, MLA_HEADS, MLA_NOPE + MLA_V)
    wkv = jnp.concatenate([_pad_last(ukv[..., :MLA_NOPE], LANES).reshape(L, MLA_KV_LORA, MLA_HEADS * LANES),
                           ukv[..., MLA_NOPE:].reshape(L, MLA_KV_LORA, MLA_HEADS * MLA_V)], axis=-1).astype(BF16)
    rows = lambda a: a[:, None, :]
    qg = rows(_pad_last(mla_q_gain, LANES))
    qgs = rows(_pad_last(swap_rope(mla_q_gain), LANES))
    kg = rows(_pad_last(mla_k_gain, LANES))
    dqg = rows(_pad_last(dil_q_gain, LANES))
    dkg = rows(_pad_last(dil_k_gain, LANES))
    gate_b = rows(_pad_last(jnp.concatenate([mlstm_b_i, mlstm_b_f], axis=-1), LANES))
    tables = _rope_tables(S)
    wa, wb, wc, wo = (w.astype(BF16) for w in (w_branch_a, w_branch_b, w_branch_c, w_out))
    wr = _pad_last(jnp.concatenate([w_router_group, w_router_expert], axis=-1), LANES)
    br = rows(_pad_last(jnp.concatenate([b_router_group, b_router_expert], axis=-1), LANES))
    wg, wu, wd = (w.astype(BF16) for w in (w_exp_gate, w_exp_up, w_exp_down))
    attn_norm, ffn_norm, mla_q_norm, mla_kv_norm, mlstm_conv_b, mlstm_head_gain = (
        rows(a) for a in (attn_norm, ffn_norm, mla_q_norm, mla_kv_norm, mlstm_conv_b, mlstm_head_gain))

    x2 = x.reshape(N, D)
    for l in range(L):
        sh_a, sc_a, g_a, sh_f, sc_f, g_f = (mod[l, :, i] for i in range(6))
        oa, mqk, mv, mo, mif, qd, kd, vd, gates = _inproj(x2, S, l, attn_norm, sc_a, sh_a, w_pad, dqg, dkg)
        q, k, vt = _mla_prep(oa, S, l, mla_q_norm, mla_kv_norm, wuq, wkv, qg, qgs, kg, tables)
        ya = _causal_attention(q, k, vt)
        yb = _mlstm(mqk, mv, mo, mif, B, S, l, mlstm_conv_w, mlstm_conv_b, gate_b, mlstm_head_gain)
        yc = _band_attention(qd, kd, vd)
        x2 = _merge_ffn(ya.reshape(N, -1), yb, yc.reshape(N, -1), gates, x2, S, l, g_a, wa, wb, wc, wo,
                        ffn_norm, sc_f, sh_f, g_f, wr, br, wg, wu, wd)
    return x2.reshape(B, S, D)
```
